```python
import math
import jax, jax.numpy as jnp
from jax import lax
import numpy as np

D_MODEL = 2048
BATCH = 4
SEQ = 4096
DEPTH = 1

NSA_HEADS = 16
NSA_GROUPS = 4
NSA_HEAD_DIM = 64
NSA_REP = NSA_HEADS // NSA_GROUPS
NSA_WIDTH = NSA_HEADS * NSA_HEAD_DIM
KV_WIDTH = NSA_GROUPS * NSA_HEAD_DIM
CMP_BLOCK = 32
CMP_STRIDE = 16
CMP_HIDDEN = 128
SEL_BLOCK = 64
SEL_TOPK = 8
WINDOW = 512
Q_BLOCK = 128

SSM_WIDTH = 1024
SSM_GROUP = 16
SSM_GROUPS = SSM_WIDTH // SSM_GROUP
SSM_STATE = 64
DT_MIN = 1e-3
DT_MAX = 1e-1

REL_BUCKETS = 32
REL_MAX_DIST = 128

DEEPNORM_ALPHA = (2 * DEPTH) ** 0.25
DEEPNORM_BETA = (8 * DEPTH) ** -0.25
LN_EPS = 1e-5
MASK_VALUE = -1e30
FORCE_VALUE = 1e4

IN_SPLITS = (NSA_WIDTH, 6 * KV_WIDTH, 3 * NSA_HEADS, NSA_WIDTH, SSM_WIDTH, SSM_WIDTH, D_MODEL, D_MODEL)
IN_WIDTH = sum(IN_SPLITS)

kernel_name = 'hybrid_nsa_s5_gated_block'


def _ln(x):
    xf = x.astype(jnp.float32)
    mu = jnp.mean(xf, axis=-1, keepdims=True)
    var = jnp.mean(jnp.square(xf - mu), axis=-1, keepdims=True)
    return ((xf - mu) * lax.rsqrt(var + LN_EPS)).astype(x.dtype)


def _t5_bucket(dist):
    n = jnp.maximum(dist, 0)
    max_exact = REL_BUCKETS // 2
    nf = jnp.maximum(n, max_exact).astype(jnp.float32)
    large = max_exact + (jnp.log(nf / max_exact) / math.log(REL_MAX_DIST / max_exact)
                         * (REL_BUCKETS - max_exact)).astype(jnp.int32)
    large = jnp.minimum(large, REL_BUCKETS - 1)
    return jnp.where(n < max_exact, n, large)


def _masked_softmax(s, valid):
    p = jax.nn.softmax(jnp.where(valid, s, MASK_VALUE), axis=-1)
    return jnp.where(valid, p, 0.0)


def _nsa(q, kv, gates, rel_bias, pos_k, pos_v, wk1, wk2, wv1, wv2):
    B, L, _ = q.shape
    G, R, dh = NSA_GROUPS, NSA_REP, NSA_HEAD_DIM
    dtype = q.dtype
    kc_in, vc_in, ks, vs, kw, vw = [t.reshape(B, L, G, dh) for t in jnp.split(kv, 6, axis=-1)]

    n_cmp = (L - CMP_BLOCK) // CMP_STRIDE + 1
    tok = jnp.arange(n_cmp)[:, None] * CMP_STRIDE + jnp.arange(CMP_BLOCK)[None, :]
    cmp_end = jnp.arange(n_cmp) * CMP_STRIDE + CMP_BLOCK - 1

    def compress(t, pos, w1, w2):
        blk = t[:, tok] + pos[:, None, :]
        blk = blk.transpose(0, 1, 3, 2, 4).reshape(B, n_cmp, G, CMP_BLOCK * dh)
        out = jax.nn.gelu(blk @ w1) @ w2
        return out.transpose(0, 2, 1, 3)

    kc = compress(kc_in, pos_k, wk1, wk2)
    vc = compress(vc_in, pos_v, wv1, wv2)

    n_sel = L // SEL_BLOCK
    n_top = min(SEL_TOPK, n_sel)
    ks_blk = ks.reshape(B, n_sel, SEL_BLOCK, G, dh).transpose(0, 3, 1, 2, 4)
    vs_blk = vs.reshape(B, n_sel, SEL_BLOCK, G, dh).transpose(0, 3, 1, 2, 4)
    ci = jnp.arange(n_cmp)[:, None] * CMP_STRIDE
    sj = jnp.arange(n_sel)[None, :] * SEL_BLOCK
    overlap = ((ci < sj + SEL_BLOCK) & (ci + CMP_BLOCK > sj)).astype(jnp.float32)
    blk_id = jnp.arange(n_sel)

    kw_pad = jnp.pad(kw, ((0, 0), (WINDOW, 0), (0, 0), (0, 0)))
    vw_pad = jnp.pad(vw, ((0, 0), (WINDOW, 0), (0, 0), (0, 0)))

    tbl = rel_bias.astype(jnp.float32)
    tbl_g = tbl.reshape(REL_BUCKETS, G, R).transpose(1, 0, 2)
    bi = jnp.arange(B)[:, None, None, None]
    gi = jnp.arange(G)[None, :, None, None]
    gi5 = jnp.arange(G)[None, :, None, None, None]
    scale = dh ** -0.5

    def head_bias(dist):
        b = tbl[_t5_bucket(dist)]
        return jnp.moveaxis(b, -1, 0).reshape(G, R, *dist.shape)

    n_qb = L // Q_BLOCK
    q_blocks = q.reshape(B, n_qb, Q_BLOCK, G, R, dh).transpose(1, 0, 3, 4, 2, 5)
    g_blocks = gates.reshape(B, n_qb, Q_BLOCK, NSA_HEADS, 3).transpose(1, 0, 2, 3, 4)

    def block_fn(args):
        cidx, qb, gb = args
        t = cidx * Q_BLOCK + jnp.arange(Q_BLOCK)
        qs = qb * scale

        s = jnp.einsum('bgrqd,bgnd->bgrqn', qs, kc).astype(jnp.float32)
        s = s + head_bias(t[:, None] - cmp_end[None, :])
        p_c = _masked_softmax(s, cmp_end[None, :] <= t[:, None])
        o_c = jnp.einsum('bgrqn,bgnd->bgrqd', p_c.astype(dtype), vc)

        imp = jnp.sum(p_c, axis=2) @ overlap
        cur = t // SEL_BLOCK
        forced = (blk_id[None, :] == 0) | (blk_id[None, :] == cur[:, None]) | (blk_id[None, :] == cur[:, None] - 1)
        imp = jnp.where(forced, FORCE_VALUE, imp)
        imp = jnp.where(blk_id[None, :] <= cur[:, None], imp, MASK_VALUE)
        _, sel = lax.top_k(imp, n_top)

        k_g = ks_blk[bi, gi, sel]
        v_g = vs_blk[bi, gi, sel]
        pos = sel[..., None] * SEL_BLOCK + jnp.arange(SEL_BLOCK)
        dist = t[None, None, :, None, None] - pos
        sb = jnp.moveaxis(tbl_g[gi5, _t5_bucket(dist)], -1, 2)
        s = jnp.einsum('bgrqd,bgqnkd->bgrqnk', qs, k_g).astype(jnp.float32) + sb
        n_tok = n_top * SEL_BLOCK
        s = s.reshape(B, G, R, Q_BLOCK, n_tok)
        valid_s = (dist >= 0).reshape(B, G, 1, Q_BLOCK, n_tok)
        p_s = _masked_softmax(s, valid_s)
        o_s = jnp.einsum('bgrqm,bgqmd->bgrqd', p_s.astype(dtype), v_g.reshape(B, G, Q_BLOCK, n_tok, dh))

        kwb = lax.dynamic_slice_in_dim(kw_pad, cidx * Q_BLOCK, WINDOW + Q_BLOCK, axis=1)
        vwb = lax.dynamic_slice_in_dim(vw_pad, cidx * Q_BLOCK, WINDOW + Q_BLOCK, axis=1)
        wpos = cidx * Q_BLOCK - WINDOW + jnp.arange(WINDOW + Q_BLOCK)
        wdist = t[:, None] - wpos[None, :]
        valid_w = (wpos[None, :] >= 0) & (wdist >= 0) & (wdist < WINDOW)
        s = jnp.einsum('bgrqd,bkgd->bgrqk', qs, kwb).astype(jnp.float32) + head_bias(wdist)
        p_w = _masked_softmax(s, valid_w)
        o_w = jnp.einsum('bgrqk,bkgd->bgrqd', p_w.astype(dtype), vwb)

        g = gb.transpose(0, 2, 1, 3).reshape(B, G, R, Q_BLOCK, 3)
        o = o_c * g[..., 0:1] + o_s * g[..., 1:2] + o_w * g[..., 2:3]
        return o.transpose(0, 3, 1, 2, 4).reshape(B, Q_BLOCK, NSA_WIDTH)

    out = lax.map(block_fn, (jnp.arange(n_qb), q_blocks, g_blocks))
    return out.transpose(1, 0, 2, 3).reshape(B, L, NSA_WIDTH)


def _cmul_combine(e1, e2):
    a1r, a1i, b1r, b1i = e1
    a2r, a2i, b2r, b2i = e2
    return (a2r * a1r - a2i * a1i,
            a2r * a1i + a2i * a1r,
            a2r * b1r - a2i * b1i + b2r,
            a2r * b1i + a2i * b1r + b2i)


def _s5(u, a_re, a_im, log_dt, b_re, b_im, c_re, c_im, d_skip, w_glu, b_glu):
    B, L, _ = u.shape
    f32 = jnp.float32
    dt = jnp.exp(log_dt.astype(f32))[:, None]
    ar, ai = a_re.astype(f32), a_im.astype(f32)
    mag = jnp.exp(ar * dt)
    lr, li = mag * jnp.cos(ai * dt), mag * jnp.sin(ai * dt)
    den = ar * ar + ai * ai
    nr, ni = lr - 1.0, li
    fr, fi = (nr * ar + ni * ai) / den, (ni * ar - nr * ai) / den
    br, bim = b_re.astype(f32), b_im.astype(f32)
    bbr = fr[..., None] * br - fi[..., None] * bim
    bbi = fr[..., None] * bim + fi[..., None] * br
    ug = u.astype(f32).reshape(B, L, SSM_GROUPS, SSM_GROUP).transpose(1, 0, 2, 3)
    xr = jnp.einsum('lbgc,gpc->lbgp', ug, bbr)
    xi = jnp.einsum('lbgc,gpc->lbgp', ug, bbi)
    lam_r = jnp.broadcast_to(lr[None, None], xr.shape)
    lam_i = jnp.broadcast_to(li[None, None], xr.shape)
    _, _, hr, hi = lax.associative_scan(_cmul_combine, (lam_r, lam_i, xr, xi), axis=0)
    y = (jnp.einsum('lbgp,gcp->lbgc', hr, c_re.astype(f32))
         - jnp.einsum('lbgp,gcp->lbgc', hi, c_im.astype(f32)))
    y = y.transpose(1, 0, 2, 3).reshape(B, L, SSM_WIDTH) + d_skip.astype(f32) * u.astype(f32)
    y = jax.nn.gelu(y).astype(u.dtype)
    return y * jax.nn.sigmoid(y @ w_glu + b_glu)


def setup_inputs(seed: int = 0) -> dict:
    key = jax.random.key(seed)
    k = jax.random.split(key, 32)
    f32 = jnp.float32
    D, dh = D_MODEL, NSA_HEAD_DIM

    def nrm(i, shape, std):
        return jax.random.normal(k[i], shape, f32) * std

    return {
        'x': nrm(0, (BATCH, SEQ, D), 1.0),
        'c': nrm(1, (BATCH, D), 1.0),
        'w_ada': nrm(2, (DEPTH, D, 3 * D), 0.5 * D ** -0.5),
        'b_ada': nrm(3, (DEPTH, 3 * D), 0.02),
        'w_in': nrm(4, (DEPTH, D, IN_WIDTH), D ** -0.5),
        'rel_bias': nrm(5, (REL_BUCKETS, NSA_HEADS), 0.2),
        'cmp_pos_k': nrm(6, (DEPTH, CMP_BLOCK, dh), 0.1),
        'cmp_pos_v': nrm(7, (DEPTH, CMP_BLOCK, dh), 0.1),
        'w_cmp_k1': nrm(8, (DEPTH, CMP_BLOCK * dh, CMP_HIDDEN), (CMP_BLOCK * dh) ** -0.5),
        'w_cmp_k2': nrm(9, (DEPTH, CMP_HIDDEN, dh), CMP_HIDDEN ** -0.5),
        'w_cmp_v1': nrm(10, (DEPTH, CMP_BLOCK * dh, CMP_HIDDEN), (CMP_BLOCK * dh) ** -0.5),
        'w_cmp_v2': nrm(11, (DEPTH, CMP_HIDDEN, dh), CMP_HIDDEN ** -0.5),
        'ssm_a_re': -0.5 + nrm(12, (DEPTH, SSM_GROUPS, SSM_STATE), 0.01),
        'ssm_a_im': jnp.pi * jnp.arange(SSM_STATE, dtype=f32)[None, None, :] + nrm(13, (DEPTH, SSM_GROUPS, SSM_STATE), 0.01),
        'ssm_log_dt': jax.random.uniform(k[14], (DEPTH, SSM_GROUPS), f32, math.log(DT_MIN), math.log(DT_MAX)),
        'ssm_b_re': nrm(15, (DEPTH, SSM_GROUPS, SSM_STATE, SSM_GROUP), (2 * SSM_GROUP) ** -0.5),
        'ssm_b_im': nrm(16, (DEPTH, SSM_GROUPS, SSM_STATE, SSM_GROUP), (2 * SSM_GROUP) ** -0.5),
        'ssm_c_re': nrm(17, (DEPTH, SSM_GROUPS, SSM_GROUP, SSM_STATE), 0.25),
        'ssm_c_im': nrm(18, (DEPTH, SSM_GROUPS, SSM_GROUP, SSM_STATE), 0.25),
        'ssm_d': nrm(19, (DEPTH, SSM_WIDTH), 1.0),
        'w_glu': nrm(20, (DEPTH, SSM_WIDTH, SSM_WIDTH), SSM_WIDTH ** -0.5),
        'b_glu': nrm(21, (DEPTH, SSM_WIDTH), 0.02),
        'w_branch_nsa': nrm(22, (DEPTH, NSA_WIDTH, D), NSA_WIDTH ** -0.5 * DEEPNORM_BETA),
        'w_branch_ssm': nrm(23, (DEPTH, SSM_WIDTH, D), SSM_WIDTH ** -0.5 * DEEPNORM_BETA),
        'w_out': nrm(24, (DEPTH, D, D), D ** -0.5 * DEEPNORM_BETA),
        'ln_g': 1.0 + nrm(25, (DEPTH, D), 0.02),
        'ln_b': nrm(26, (DEPTH, D), 0.02),
    }


def reference(x, c, w_ada, b_ada, w_in, rel_bias, cmp_pos_k, cmp_pos_v, w_cmp_k1, w_cmp_k2,
              w_cmp_v1, w_cmp_v2, ssm_a_re, ssm_a_im, ssm_log_dt, ssm_b_re, ssm_b_im, ssm_c_re,
              ssm_c_im, ssm_d, w_glu, b_glu, w_branch_nsa, w_branch_ssm, w_out, ln_g, ln_b):
    offs = np.cumsum(IN_SPLITS)[:-1].tolist()
    for i in range(DEPTH):
        mod = c @ w_ada[i] + b_ada[i]
        shift, scale, gate = jnp.split(mod, 3, axis=-1)
        h = _ln(x) * (1.0 + scale[:, None, :]) + shift[:, None, :]

        proj = h @ w_in[i]
        q, kv, ng, za, us, zb, ga, gb = jnp.split(proj, offs, axis=-1)

        o_a = _nsa(q, kv, jax.nn.sigmoid(ng), rel_bias, cmp_pos_k[i], cmp_pos_v[i],
                   w_cmp_k1[i], w_cmp_k2[i], w_cmp_v1[i], w_cmp_v2[i]) * jax.nn.silu(za)
        o_b = _s5(us, ssm_a_re[i], ssm_a_im[i], ssm_log_dt[i], ssm_b_re[i], ssm_b_im[i],
                  ssm_c_re[i], ssm_c_im[i], ssm_d[i], w_glu[i], b_glu[i]) * jax.nn.silu(zb)

        m = jax.nn.sigmoid(ga) * (o_a @ w_branch_nsa[i]) + jax.nn.sigmoid(gb) * (o_b @ w_branch_ssm[i])
        y = m @ w_out[i]

        x = _ln(DEEPNORM_ALPHA * x + gate[:, None, :] * y) * ln_g[i] + ln_b[i]
    return x
```

```python
import functools
import math

import numpy as np
import jax
import jax.numpy as jnp
from jax import lax
from jax.experimental import pallas as pl
from jax.experimental.pallas import tpu as pltpu

F32 = jnp.float32
BF16 = jnp.bfloat16
HIGHEST = lax.Precision.HIGHEST

D_MODEL = 2048
NSA_HEADS = 16
NSA_GROUPS = 4
NSA_HEAD_DIM = 64
NSA_REP = NSA_HEADS // NSA_GROUPS
NSA_WIDTH = NSA_HEADS * NSA_HEAD_DIM
KV_WIDTH = NSA_GROUPS * NSA_HEAD_DIM
CMP_BLOCK = 32
CMP_STRIDE = 16
CMP_HIDDEN = 128
SEL_BLOCK = 64
SEL_TOPK = 8
WINDOW = 512
Q_BLOCK = 128
SSM_WIDTH = 1024
SSM_GROUP = 16
SSM_GROUPS = SSM_WIDTH // SSM_GROUP
SSM_STATE = 64
REL_BUCKETS = 32
REL_MAX_DIST = 128
DEEPNORM_ALPHA = 2.0 ** 0.25
LN_EPS = 1e-5
MASK_VALUE = -1e30
FORCE_VALUE = 1e4
NEVER_VALUE = -3e38
SSM_CHUNK = 16

COL_Q, COL_ZA, COL_US, COL_ZB, COL_GA, COL_GB, COL_KV, COL_NG = 0, 1024, 2048, 3072, 4096, 6144, 8192, 9728
PROJ_COLS = 9856
VMEM_LIMIT = 56 * 1024 * 1024


def _cparams(sem):
    return pltpu.CompilerParams(dimension_semantics=sem, vmem_limit_bytes=VMEM_LIMIT)


def _ada_kernel(c_ref, w_ref, b_ref, o_ref):
    o_ref[...] = jnp.dot(c_ref[...], w_ref[...], preferred_element_type=F32, precision=HIGHEST) + b_ref[...]


def _ada_mod(c, w_ada, b_ada):
    B, D = c.shape
    n = w_ada.shape[1]
    tn = 1536
    return pl.pallas_call(
        _ada_kernel,
        grid=(n // tn,),
        in_specs=[pl.BlockSpec((B, D), lambda j: (0, 0)),
                  pl.BlockSpec((D, tn), lambda j: (0, j)),
                  pl.BlockSpec((1, tn), lambda j: (0, j))],
        out_specs=pl.BlockSpec((B, tn), lambda j: (0, j)),
        out_shape=jax.ShapeDtypeStruct((B, n), F32),
        compiler_params=_cparams(("parallel",)),
        name="ada_mod",
    )(c, w_ada, b_ada.reshape(1, n))


def _inproj_kernel(x_ref, scale_ref, shift_ref, w_ref, o_ref, h_ref):
    @pl.when(pl.program_id(1) == 0)
    def _():
        x = x_ref[...]
        mu = jnp.mean(x, axis=-1, keepdims=True)
        xc = x - mu
        var = jnp.mean(xc * xc, axis=-1, keepdims=True)
        hn = xc * lax.rsqrt(var + LN_EPS)
        h_ref[...] = (hn * (1.0 + scale_ref[...]) + shift_ref[...]).astype(BF16)

    o_ref[...] = jnp.dot(h_ref[...], w_ref[...], preferred_element_type=F32).astype(o_ref.dtype)


def _in_proj(x2, scale, shift, w, L):
    N, D = x2.shape
    ncol = w.shape[1]
    tm = min(1024, L)
    tn = 1408
    assert N % tm == 0 and L % tm == 0 and ncol % tn == 0
    B = scale.shape[0]
    return pl.pallas_call(
        _inproj_kernel,
        grid=(N // tm, ncol // tn),
        in_specs=[pl.BlockSpec((tm, D), lambda i, j: (i, 0)),
                  pl.BlockSpec((None, 1, D), lambda i, j: ((i * tm) // L, 0, 0)),
                  pl.BlockSpec((None, 1, D), lambda i, j: ((i * tm) // L, 0, 0)),
                  pl.BlockSpec((D, tn), lambda i, j: (0, j))],
        out_specs=pl.BlockSpec((tm, tn), lambda i, j: (i, j)),
        out_shape=jax.ShapeDtypeStruct((N, ncol), BF16),
        scratch_shapes=[pltpu.VMEM((tm, D), BF16)],
        compiler_params=_cparams(("parallel", "arbitrary")),
        name="in_proj",
    )(x2, scale.reshape(B, 1, D), shift.reshape(B, 1, D), w)


def _cmp_kernel(x_ref, w1_ref, w2_ref, pos_ref, o_ref, *, npad):
    half = CMP_STRIDE * NSA_HEAD_DIM
    x = x_ref[...]
    w1 = w1_ref[...]
    a = jnp.dot(x, w1[:half].astype(BF16), preferred_element_type=F32)
    b = jnp.dot(x, w1[half:].astype(BF16), preferred_element_type=F32)
    pw = jnp.dot(pos_ref[...], w1, preferred_element_type=F32, precision=HIGHEST)[0:1]
    ncp = x.shape[0]
    h = a + pltpu.roll(b, ncp - 1, 0) + pw
    y = jnp.dot(jax.nn.gelu(h).astype(BF16), w2_ref[...].astype(BF16), preferred_element_type=F32)
    o_ref[0:npad, :] = jnp.zeros((npad, NSA_HEAD_DIM), F32)
    o_ref[npad:, :] = y


def _nsa_compress(xc, w1s, w2s, poss, npad):
    _, B, G, ncp, width = xc.shape
    return pl.pallas_call(
        functools.partial(_cmp_kernel, npad=npad),
        grid=(2, B, G),
        in_specs=[pl.BlockSpec((None, None, None, ncp, width), lambda s, b, g: (s, b, g, 0, 0)),
                  pl.BlockSpec((None,) + w1s.shape[1:], lambda s, b, g: (s, 0, 0)),
                  pl.BlockSpec((None,) + w2s.shape[1:], lambda s, b, g: (s, 0, 0)),
                  pl.BlockSpec((None,) + poss.shape[1:], lambda s, b, g: (s, 0, 0))],
        out_specs=pl.BlockSpec((None, None, None, npad + ncp, NSA_HEAD_DIM), lambda s, b, g: (s, b, g, 0, 0)),
        out_shape=jax.ShapeDtypeStruct((2, B, G, npad + ncp, NSA_HEAD_DIM), F32),
        compiler_params=_cparams(("parallel", "parallel", "parallel")),
        name="nsa_compress",
    )(xc, w1s, w2s, poss)


def _softmax_rows(s, valid):
    sm = jnp.where(valid, s, MASK_VALUE)
    m = jnp.max(sm, axis=-1, keepdims=True)
    e = jnp.where(valid, jnp.exp(sm - m), 0.0)
    return e, jnp.sum(e, axis=-1, keepdims=True)


def _safe_inv(l):
    return jnp.where(l > 0.0, 1.0 / jnp.where(l > 0.0, l, 1.0), 0.0)


def _nsa_kernel(q_ref, za_ref, ng_ref, kc_ref, vc_ref, ks_ref, vs_ref, kw_ref, vw_ref,
                tc_ref, tw_ref, cfar_ref, ov_ref, gt_ref, o_ref, *, L):
    R, dh, QB = NSA_REP, NSA_HEAD_DIM, Q_BLOCK
    ncp = L // CMP_STRIDE
    nsel = L // SEL_BLOCK
    qb = pl.program_id(2)
    NT = (((1,), (1,)), ((), ()))

    qblk = q_ref[...] * (dh ** -0.5)
    Q = jnp.concatenate([qblk[:, r * dh:(r + 1) * dh] for r in range(R)], axis=0)

    c0 = pl.multiple_of(qb * (QB // CMP_STRIDE), 8)
    kc = kc_ref[pl.ds(c0, ncp), :].astype(BF16)
    vc = vc_ref[pl.ds(c0, ncp), :].astype(BF16)
    s = lax.dot_general(Q, kc, NT, preferred_element_type=F32)
    s = s.reshape(R, QB, ncp) + tc_ref[...]
    ii = lax.broadcasted_iota(jnp.int32, (QB, ncp), 0)
    ww = lax.broadcasted_iota(jnp.int32, (QB, ncp), 1)
    dist_c = ii + CMP_STRIDE * (ncp - ww) - (QB + CMP_BLOCK - 1)
    valid_c = (dist_c >= 0) & (ww >= ncp - 8 - 8 * qb)
    e, l = _softmax_rows(s, valid_c[None])
    p_c = e * _safe_inv(l)
    o_c = jnp.dot(p_c.reshape(R * QB, ncp).astype(BF16), vc, preferred_element_type=F32)

    psum = jnp.sum(p_c, axis=0)
    p_hi = psum.astype(BF16)
    p_lo = (psum - p_hi.astype(F32)).astype(BF16)
    ov = ov_ref[...]
    imp = (jnp.dot(p_hi, ov, preferred_element_type=F32) + jnp.dot(p_lo, ov, preferred_element_type=F32))
    i2 = lax.broadcasted_iota(jnp.int32, (QB, nsel), 0)
    jr = lax.broadcasted_iota(jnp.int32, (QB, nsel), 1)
    cur = (nsel - 2) + (i2 >= SEL_BLOCK).astype(jnp.int32)
    first = (nsel - 2) - 2 * qb
    forced = (jr == first) | (jr == cur) | (jr == cur - 1)
    imp = jnp.where(forced, FORCE_VALUE, imp)
    imp = jnp.where(jr <= cur, imp, MASK_VALUE)
    imp = jnp.where(jr >= first, imp, NEVER_VALUE)
    jf = jr.astype(F32)
    sel = jnp.zeros((QB, nsel), F32)
    for _ in range(SEL_TOPK):
        mx = jnp.max(imp, axis=-1, keepdims=True)
        idx = jnp.min(jnp.where(imp == mx, jf, 1e9), axis=-1, keepdims=True)
        hit = jf == idx
        sel = jnp.where(hit, 1.0, sel)
        imp = jnp.where(hit, -jnp.inf, imp)
    sel_b = sel.astype(BF16)

    tw = tw_ref[...]
    b_diag = tw[:, :, WINDOW:]
    b_prev = tw[:, :, WINDOW - QB:WINDOW]
    b_far = cfar_ref[...]
    rows = lax.broadcasted_iota(jnp.int32, (QB, QB), 0)
    cols = lax.broadcasted_iota(jnp.int32, (QB, QB), 1)
    causal = cols <= rows

    def sel_body(kb, carry):
        m_i, l_i, acc = carry
        k0 = pl.multiple_of(kb * QB, QB)
        kt = ks_ref[pl.ds(k0, QB), :]
        vt = vs_ref[pl.ds(k0, QB), :]
        g0 = pl.multiple_of((kb - qb) * QB + (L - QB), QB)
        gt = gt_ref[pl.ds(g0, QB), :]
        msk = lax.dot_general(sel_b, gt, NT, preferred_element_type=F32) > 0.5
        msk = msk & (causal | (kb < qb))
        bias = jnp.where(kb == qb, b_diag, jnp.where(kb == qb - 1, b_prev, b_far))
        sc = lax.dot_general(Q, kt, NT, preferred_element_type=F32).reshape(R, QB, QB) + bias
        sc = jnp.where(msk[None], sc, MASK_VALUE)
        m_new = jnp.maximum(m_i, jnp.max(sc, axis=-1, keepdims=True))
        alpha = jnp.exp(m_i - m_new)
        p = jnp.where(msk[None], jnp.exp(sc - m_new), 0.0)
        l_new = alpha * l_i + jnp.sum(p, axis=-1, keepdims=True)
        pv = jnp.dot(p.reshape(R * QB, QB).astype(BF16), vt, preferred_element_type=F32).reshape(R, QB, dh)
        return m_new, l_new, alpha * acc + pv

    init = (jnp.full((R, QB, 1), MASK_VALUE, F32), jnp.zeros((R, QB, 1), F32), jnp.zeros((R, QB, dh), F32))
    _, l_s, acc_s = lax.fori_loop(0, qb + 1, sel_body, init)
    o_s = acc_s * _safe_inv(l_s)

    w0 = pl.multiple_of(qb * QB, QB)
    kwt = kw_ref[pl.ds(w0, WINDOW + QB), :]
    vwt = vw_ref[pl.ds(w0, WINDOW + QB), :]
    sw = lax.dot_general(Q, kwt, NT, preferred_element_type=F32).reshape(R, QB, WINDOW + QB) + tw
    iw = lax.broadcasted_iota(jnp.int32, (QB, WINDOW + QB), 0)
    jw = lax.broadcasted_iota(jnp.int32, (QB, WINDOW + QB), 1)
    wdist = iw + WINDOW - jw
    valid_w = (jw >= WINDOW - QB * qb) & (wdist >= 0) & (wdist < WINDOW)
    e, l = _softmax_rows(sw, valid_w[None])
    o_w = jnp.dot(e.reshape(R * QB, WINDOW + QB).astype(BF16), vwt, preferred_element_type=F32)
    o_w = o_w.reshape(R, QB, dh) * _safe_inv(l)

    gates = jax.nn.sigmoid(ng_ref[...].astype(F32))
    za = za_ref[...].astype(F32)
    o_c = o_c.reshape(R, QB, dh)
    g = pl.program_id(1)
    lane = lax.broadcasted_iota(jnp.int32, gates.shape, 1)
    outs = []
    for r in range(R):
        col = 3 * (g * R + r)
        gc = jnp.sum(jnp.where(lane == col, gates, 0.0), axis=-1, keepdims=True)
        gs = jnp.sum(jnp.where(lane == col + 1, gates, 0.0), axis=-1, keepdims=True)
        gw = jnp.sum(jnp.where(lane == col + 2, gates, 0.0), axis=-1, keepdims=True)
        outs.append(o_c[r] * gc + o_s[r] * gs + o_w[r] * gw)
    o = jnp.concatenate(outs, axis=-1)
    o_ref[...] = (o * (za * jax.nn.sigmoid(za))).astype(o_ref.dtype)


def _nsa_attend(proj, kcv, ks, vs, kw, vw, tc, tw, cfar, ov, gt, B, L):
    R, dh, G, QB = NSA_REP, NSA_HEAD_DIM, NSA_GROUPS, Q_BLOCK
    nqb = L // QB
    gw = R * dh
    ncp_pad = kcv.shape[3]
    row = lambda b, g, i: b * nqb + i
    return pl.pallas_call(
        functools.partial(_nsa_kernel, L=L),
        grid=(B, G, nqb),
        in_specs=[
            pl.BlockSpec((QB, gw), lambda b, g, i: (row(b, g, i), COL_Q // gw + g)),
            pl.BlockSpec((QB, gw), lambda b, g, i: (row(b, g, i), COL_ZA // gw + g)),
            pl.BlockSpec((QB, 128), lambda b, g, i: (row(b, g, i), COL_NG // 128)),
            pl.BlockSpec((None, None, None, ncp_pad, dh), lambda b, g, i: (0, b, g, 0, 0)),
            pl.BlockSpec((None, None, None, ncp_pad, dh), lambda b, g, i: (1, b, g, 0, 0)),
            pl.BlockSpec((None, None, L, dh), lambda b, g, i: (b, g, 0, 0)),
            pl.BlockSpec((None, None, L, dh), lambda b, g, i: (b, g, 0, 0)),
            pl.BlockSpec((None, None, L + WINDOW, dh), lambda b, g, i: (b, g, 0, 0)),
            pl.BlockSpec((None, None, L + WINDOW, dh), lambda b, g, i: (b, g, 0, 0)),
            pl.BlockSpec((R,) + tc.shape[1:], lambda b, g, i: (g, 0, 0)),
            pl.BlockSpec((R,) + tw.shape[1:], lambda b, g, i: (g, 0, 0)),
            pl.BlockSpec((R,) + cfar.shape[1:], lambda b, g, i: (g, 0, 0)),
            pl.BlockSpec(ov.shape, lambda b, g, i: (0, 0)),
            pl.BlockSpec(gt.shape, lambda b, g, i: (0, 0)),
        ],
        out_specs=pl.BlockSpec((QB, gw), lambda b, g, i: (row(b, g, i), g)),
        out_shape=jax.ShapeDtypeStruct((B * L, NSA_WIDTH), BF16),
        compiler_params=_cparams(("parallel", "parallel", "arbitrary")),
        name="nsa_attend",
    )(proj, proj, proj, kcv, kcv, ks, vs, kw, vw, tc, tw, cfar, ov, gt)


def _t5_bucket(dist):
    n = jnp.maximum(dist, 0)
    max_exact = REL_BUCKETS // 2
    nf = jnp.maximum(n, max_exact).astype(F32)
    large = max_exact + (jnp.log(nf / max_exact) / math.log(REL_MAX_DIST / max_exact)
                         * (REL_BUCKETS - max_exact)).astype(jnp.int32)
    large = jnp.minimum(large, REL_BUCKETS - 1)
    return jnp.where(n < max_exact, n, large)


def _nsa_tables(rel_bias, L):
    QB = Q_BLOCK
    ncp = L // CMP_STRIDE
    nsel = L // SEL_BLOCK
    bvec = rel_bias.astype(F32)[_t5_bucket(jnp.arange(L, dtype=jnp.int32))].T
    i = np.arange(QB)[:, None]
    dist_c = np.clip(i + CMP_STRIDE * (ncp - np.arange(ncp)[None, :]) - (QB + CMP_BLOCK - 1), 0, L - 1)
    dist_w = np.clip(i + WINDOW - np.arange(WINDOW + QB)[None, :], 0, L - 1)
    tc = bvec[:, dist_c]
    tw = bvec[:, dist_w]
    cfar = jnp.broadcast_to(bvec[:, L - 1][:, None, None], (NSA_HEADS, 1, QB))
    w = np.arange(ncp)[:, None]
    jrel = np.arange(nsel)[None, :]
    ov = ((w - 4 * jrel >= -1) & (w - 4 * jrel <= 3)).astype(np.float32)
    x = np.arange(L)[:, None] - (L - QB)
    gt = (np.floor_divide(x, SEL_BLOCK) == jrel - (nsel - 2)).astype(np.float32)
    return tc, tw, cfar, jnp.asarray(ov, BF16), jnp.asarray(gt, BF16)


def _s5_kernel(u_ref, mt_ref, ws_ref, wo_ref, lam_ref, y_ref, *, nk):
    u0, u1 = u_ref[0], u_ref[1]
    rows = u0.shape[0]
    half = 2 * SSM_STATE
    s = (jnp.dot(u0, ws_ref[0], preferred_element_type=F32)
         + jnp.dot(u1, ws_ref[1], preferred_element_type=F32))
    hr, hi = s[:, :half], s[:, half:]
    kidx = lax.broadcasted_iota(jnp.int32, (rows, half), 0) % nk
    d, step = 1, 0
    while d < nk:
        lr = lam_ref[step:step + 1, :half]
        li = lam_ref[step:step + 1, half:]
        keep = kidx >= d
        sr = jnp.where(keep, pltpu.roll(hr, d, 0), 0.0)
        si = jnp.where(keep, pltpu.roll(hi, d, 0), 0.0)
        hr, hi = hr + lr * sr - li * si, hi + lr * si + li * sr
        d, step = 2 * d, step + 1
    keep = kidx >= 1
    pr = jnp.where(keep, pltpu.roll(hr, 1, 0), 0.0)
    pi = jnp.where(keep, pltpu.roll(hi, 1, 0), 0.0)
    hcat = jnp.concatenate([pr, pi], axis=-1).astype(BF16)
    y_ref[0] = jnp.dot(u0, mt_ref[0], preferred_element_type=F32) + jnp.dot(hcat, wo_ref[0], preferred_element_type=F32)
    y_ref[1] = jnp.dot(u1, mt_ref[1], preferred_element_type=F32) + jnp.dot(hcat, wo_ref[1], preferred_element_type=F32)


def _s5_scan(ug, mt, ws, wo, lamp, nk):
    gs, rows, cw = ug.shape
    nstep = lamp.shape[1]
    return pl.pallas_call(
        functools.partial(_s5_kernel, nk=nk),
        grid=(gs // 2,),
        in_specs=[pl.BlockSpec((2, rows, cw), lambda p: (p, 0, 0)),
                  pl.BlockSpec((2, cw, cw), lambda p: (p, 0, 0)),
                  pl.BlockSpec((2, cw, cw), lambda p: (p, 0, 0)),
                  pl.BlockSpec((2, cw, cw), lambda p: (p, 0, 0)),
                  pl.BlockSpec((None, nstep, cw), lambda p: (p, 0, 0))],
        out_specs=pl.BlockSpec((2, rows, cw), lambda p: (p, 0, 0)),
        out_shape=jax.ShapeDtypeStruct((gs, rows, cw), F32),
        compiler_params=_cparams(("parallel",)),
        name="s5_scan",
    )(ug, mt, ws, wo, lamp)


def _cmul(ar, ai, br, bi):
    return ar * br - ai * bi, ar * bi + ai * br


def _s5_tables(a_re, a_im, log_dt, b_re, b_im, c_re, c_im, nk):
    T, P, C, Gs = SSM_CHUNK, SSM_STATE, SSM_GROUP, SSM_GROUPS
    dt = jnp.exp(log_dt.astype(F32))[:, None]
    ar, ai = a_re.astype(F32), a_im.astype(F32)
    mag = jnp.exp(ar * dt)
    lr, li = mag * jnp.cos(ai * dt), mag * jnp.sin(ai * dt)
    den = ar * ar + ai * ai
    nr, ni = lr - 1.0, li
    fr, fi = (nr * ar + ni * ai) / den, (ni * ar - nr * ai) / den
    br, bim = b_re.astype(F32), b_im.astype(F32)
    bbr = fr[..., None] * br - fi[..., None] * bim
    bbi = fr[..., None] * bim + fi[..., None] * br
    pr, pi = [jnp.ones_like(lr)], [jnp.zeros_like(lr)]
    for _ in range(T):
        nr_, ni_ = _cmul(pr[-1], pi[-1], lr, li)
        pr.append(nr_)
        pi.append(ni_)
    pwr, pwi = jnp.stack(pr, 0), jnp.stack(pi, 0)
    zr = pwr[..., None] * bbr[None] - pwi[..., None] * bbi[None]
    zi = pwr[..., None] * bbi[None] + pwi[..., None] * bbr[None]
    cr, ci = c_re.astype(F32), c_im.astype(F32)
    kj = (jnp.einsum('gcp,jgpd->gjcd', cr, zr[:T], precision=HIGHEST)
          - jnp.einsum('gcp,jgpd->gjcd', ci, zi[:T], precision=HIGHEST))
    a = np.arange(T)
    lag = a[None, :] - a[:, None]
    mt = kj[:, np.clip(lag, 0, T - 1)]
    mt = mt * jnp.asarray((lag >= 0).astype(np.float32))[None, :, :, None, None]
    mt = mt.transpose(0, 1, 4, 2, 3).reshape(Gs, T * C, T * C)
    wsr = zr[:T][::-1].transpose(1, 0, 3, 2).reshape(Gs, T * C, P)
    wsi = zi[:T][::-1].transpose(1, 0, 3, 2).reshape(Gs, T * C, P)
    clr = cr[:, None] * pwr[1:].transpose(1, 0, 2)[:, :, None, :] - ci[:, None] * pwi[1:].transpose(1, 0, 2)[:, :, None, :]
    cli = cr[:, None] * pwi[1:].transpose(1, 0, 2)[:, :, None, :] + ci[:, None] * pwr[1:].transpose(1, 0, 2)[:, :, None, :]
    wor = clr.transpose(0, 3, 1, 2).reshape(Gs, P, T * C)
    woi = (-cli).transpose(0, 3, 1, 2).reshape(Gs, P, T * C)
    z = jnp.zeros((Gs // 2, T * C, P), F32)
    wsr2, wsi2 = wsr.reshape(Gs // 2, 2, T * C, P), wsi.reshape(Gs // 2, 2, T * C, P)
    ws = jnp.stack([jnp.concatenate([wsr2[:, 0], z, wsi2[:, 0], z], -1),
                    jnp.concatenate([z, wsr2[:, 1], z, wsi2[:, 1]], -1)], 1).reshape(Gs, T * C, 4 * P)
    zt = jnp.zeros((Gs // 2, P, T * C), F32)
    wor2, woi2 = wor.reshape(Gs // 2, 2, P, T * C), woi.reshape(Gs // 2, 2, P, T * C)
    wo = jnp.stack([jnp.concatenate([wor2[:, 0], zt, woi2[:, 0], zt], 1),
                    jnp.concatenate([zt, wor2[:, 1], zt, woi2[:, 1]], 1)], 1).reshape(Gs, 4 * P, T * C)
    qr, qi = pwr[T], pwi[T]
    steps = []
    d = 1
    while d < nk:
        q2r, q2i = qr.reshape(Gs // 2, 2, P), qi.reshape(Gs // 2, 2, P)
        steps.append(jnp.concatenate([q2r[:, 0], q2r[:, 1], q2i[:, 0], q2i[:, 1]], -1))
        qr, qi = _cmul(qr, qi, qr, qi)
        d *= 2
    lamp = jnp.stack(steps, 1)
    return mt.astype(BF16), ws.astype(BF16), wo.astype(BF16), lamp


def _glu_kernel(y_ref, u_ref, zb_ref, d_ref, w_ref, b_ref, o_ref):
    y = y_ref[...] + d_ref[...] * u_ref[...].astype(F32)
    yg = jax.nn.gelu(y).astype(BF16)
    z = jnp.dot(yg, w_ref[...], preferred_element_type=F32) + b_ref[...]
    zb = zb_ref[...].astype(F32)
    o_ref[...] = (yg.astype(F32) * jax.nn.sigmoid(z) * (zb * jax.nn.sigmoid(zb))).astype(o_ref.dtype)


def _s5_glu(y, proj, d_skip, w_glu, b_glu):
    N, W = y.shape
    tm = min(1024, N)
    return pl.pallas_call(
        _glu_kernel,
        grid=(N // tm,),
        in_specs=[pl.BlockSpec((tm, W), lambda i: (i, 0)),
                  pl.BlockSpec((tm, W), lambda i: (i, COL_US // W)),
                  pl.BlockSpec((tm, W), lambda i: (i, COL_ZB // W)),
                  pl.BlockSpec((1, W), lambda i: (0, 0)),
                  pl.BlockSpec((W, W), lambda i: (0, 0)),
                  pl.BlockSpec((1, W), lambda i: (0, 0))],
        out_specs=pl.BlockSpec((tm, W), lambda i: (i, 0)),
        out_shape=jax.ShapeDtypeStruct((N, W), BF16),
        compiler_params=_cparams(("parallel",)),
        name="s5_glu",
    )(y, proj, proj, d_skip.reshape(1, W), w_glu, b_glu.reshape(1, W))


def _merge_kernel(oa_ref, ob_ref, ga_ref, gb_ref, x_ref, gate_ref, wa_ref, wb_ref, wo_ref, lg_ref, lb_ref, o_ref):
    pa = jnp.dot(oa_ref[...], wa_ref[...], preferred_element_type=F32)
    pb = jnp.dot(ob_ref[...], wb_ref[...], preferred_element_type=F32)
    m = jax.nn.sigmoid(ga_ref[...].astype(F32)) * pa + jax.nn.sigmoid(gb_ref[...].astype(F32)) * pb
    y = jnp.dot(m.astype(BF16), wo_ref[...], preferred_element_type=F32)
    r = DEEPNORM_ALPHA * x_ref[...] + gate_ref[...] * y
    mu = jnp.mean(r, axis=-1, keepdims=True)
    rc = r - mu
    var = jnp.mean(rc * rc, axis=-1, keepdims=True)
    o_ref[...] = rc * lax.rsqrt(var + LN_EPS) * lg_ref[...] + lb_ref[...]


def _merge_out(oa, ob, proj, x2, gate, wa, wb, wo, ln_g, ln_b, L):
    N, D = x2.shape
    W = oa.shape[1]
    B = gate.shape[0]
    tm = min(256, L)
    const = lambda i: (0, 0)
    return pl.pallas_call(
        _merge_kernel,
        grid=(N // tm,),
        in_specs=[pl.BlockSpec((tm, W), lambda i: (i, 0)),
                  pl.BlockSpec((tm, W), lambda i: (i, 0)),
                  pl.BlockSpec((tm, D), lambda i: (i, COL_GA // D)),
                  pl.BlockSpec((tm, D), lambda i: (i, COL_GB // D)),
                  pl.BlockSpec((tm, D), lambda i: (i, 0)),
                  pl.BlockSpec((None, 1, D), lambda i: ((i * tm) // L, 0, 0)),
                  pl.BlockSpec((W, D), const),
                  pl.BlockSpec((W, D), const),
                  pl.BlockSpec((D, D), const),
                  pl.BlockSpec((1, D), const),
                  pl.BlockSpec((1, D), const)],
        out_specs=pl.BlockSpec((tm, D), lambda i: (i, 0)),
        out_shape=jax.ShapeDtypeStruct((N, D), F32),
        compiler_params=_cparams(("parallel",)),
        name="merge_out",
    )(oa, ob, proj, proj, x2, gate.reshape(B, 1, D), wa, wb, wo, ln_g.reshape(1, D), ln_b.reshape(1, D))


def _layer(x, c, w_ada, b_ada, w_in, rel_bias, cmp_pos_k, cmp_pos_v, w_cmp_k1, w_cmp_k2, w_cmp_v1, w_cmp_v2,
           ssm_a_re, ssm_a_im, ssm_log_dt, ssm_b_re, ssm_b_im, ssm_c_re, ssm_c_im, ssm_d, w_glu, b_glu,
           w_branch_nsa, w_branch_ssm, w_out, ln_g, ln_b):
    B, L, D = x.shape
    N = B * L
    G, dh = NSA_GROUPS, NSA_HEAD_DIM
    x2 = x.reshape(N, D)

    mod = _ada_mod(c, w_ada, b_ada)
    shift, scale, gate = mod[:, :D], mod[:, D:2 * D], mod[:, 2 * D:]

    o_q, o_kv, o_ng, o_za, o_us, o_zb, o_ga, o_gb = np.cumsum(
        [0, NSA_WIDTH, 6 * KV_WIDTH, 3 * NSA_HEADS, NSA_WIDTH, SSM_WIDTH, SSM_WIDTH, D_MODEL]).tolist()
    w_re = jnp.concatenate([
        w_in[:, o_q:o_kv], w_in[:, o_za:o_us], w_in[:, o_us:o_zb], w_in[:, o_zb:o_ga], w_in[:, o_ga:o_gb],
        w_in[:, o_gb:], w_in[:, o_kv:o_ng], w_in[:, o_ng:o_za],
        jnp.zeros((D, PROJ_COLS - COL_NG - 3 * NSA_HEADS), w_in.dtype)], axis=1).astype(BF16)
    proj = _in_proj(x2, scale, shift, w_re, L)

    kv = proj[:, COL_KV:COL_KV + 6 * KV_WIDTH].reshape(B, L, 6, G, dh)
    ncp = L // CMP_STRIDE
    xc = kv[:, :, 0:2].reshape(B, ncp, CMP_STRIDE, 2, G, dh).transpose(3, 0, 4, 1, 2, 5)
    xc = xc.reshape(2, B, G, ncp, CMP_STRIDE * dh)
    w1s = jnp.stack([w_cmp_k1, w_cmp_v1])
    w2s = jnp.stack([w_cmp_k2, w_cmp_v2])
    poss = jnp.broadcast_to(jnp.stack([cmp_pos_k, cmp_pos_v]).reshape(2, 1, CMP_BLOCK * dh), (2, 8, CMP_BLOCK * dh))
    kcv = _nsa_compress(xc, w1s, w2s, poss, ncp - 8)
    kvt = kv.transpose(2, 0, 3, 1, 4)
    ks, vs = kvt[2], kvt[3]
    kw = jnp.pad(kvt[4], ((0, 0), (0, 0), (WINDOW, 0), (0, 0)))
    vw = jnp.pad(kvt[5], ((0, 0), (0, 0), (WINDOW, 0), (0, 0)))
    tc, tw, cfar, ov, gt = _nsa_tables(rel_bias, L)
    o_a = _nsa_attend(proj, kcv, ks, vs, kw, vw, tc, tw, cfar, ov, gt, B, L)

    T, C, Gs = SSM_CHUNK, SSM_GROUP, SSM_GROUPS
    nk = L // T
    us = proj[:, COL_US:COL_US + SSM_WIDTH]
    ug = us.reshape(B * nk, T, Gs, C).transpose(2, 0, 1, 3).reshape(Gs, B * nk, T * C)
    mt, ws, wo, lamp = _s5_tables(ssm_a_re, ssm_a_im, ssm_log_dt, ssm_b_re, ssm_b_im, ssm_c_re, ssm_c_im, nk)
    yg = _s5_scan(ug, mt, ws, wo, lamp, nk)
    y = yg.reshape(Gs, B * nk, T, C).transpose(1, 2, 0, 3).reshape(N, SSM_WIDTH)
    o_b = _s5_glu(y, proj, ssm_d, w_glu.astype(BF16), b_glu)

    out = _merge_out(o_a, o_b, proj, x2, gate, w_branch_nsa.astype(BF16), w_branch_ssm.astype(BF16),
                     w_out.astype(BF16), ln_g, ln_b, L)
    return out.reshape(B, L, D)


def kernel(x, c, w_ada, b_ada, w_in, rel_bias, cmp_pos_k, cmp_pos_v, w_cmp_k1, w_cmp_k2, w_cmp_v1, w_cmp_v2,
           ssm_a_re, ssm_a_im, ssm_log_dt, ssm_b_re, ssm_b_im, ssm_c_re, ssm_c_im, ssm_d, w_glu, b_glu,
           w_branch_nsa, w_branch_ssm, w_out, ln_g, ln_b):
    for i in range(w_ada.shape[0]):
        x = _layer(x, c, w_ada[i], b_ada[i], w_in[i], rel_bias, cmp_pos_k[i], cmp_pos_v[i], w_cmp_k1[i],
                   w_cmp_k2[i], w_cmp_v1[i], w_cmp_v2[i], ssm_a_re[i], ssm_a_im[i], ssm_log_dt[i], ssm_b_re[i],
                   ssm_b_im[i], ssm_c_re[i], ssm_c_im[i], ssm_d[i], w_glu[i], b_glu[i], w_branch_nsa[i],
                   w_branch_ssm[i], w_out[i], ln_g[i], ln_b[i])
    return x
```

```python
import functools
import math

import numpy as np
import jax
import jax.numpy as jnp
from jax import lax
from jax.experimental import pallas as pl
from jax.experimental.pallas import tpu as pltpu

F32 = jnp.float32
BF16 = jnp.bfloat16
HIGHEST = lax.Precision.HIGHEST

D_MODEL = 2048
NSA_HEADS = 16
NSA_GROUPS = 4
NSA_HEAD_DIM = 64
NSA_REP = NSA_HEADS // NSA_GROUPS
NSA_WIDTH = NSA_HEADS * NSA_HEAD_DIM
KV_WIDTH = NSA_GROUPS * NSA_HEAD_DIM
CMP_BLOCK = 32
CMP_STRIDE = 16
CMP_HIDDEN = 128
SEL_BLOCK = 64
SEL_TOPK = 8
WINDOW = 512
Q_BLOCK = 128
SSM_WIDTH = 1024
SSM_GROUP = 16
SSM_GROUPS = SSM_WIDTH // SSM_GROUP
SSM_STATE = 64
REL_BUCKETS = 32
REL_MAX_DIST = 128
DEEPNORM_ALPHA = 2.0 ** 0.25
LN_EPS = 1e-5
MASK_VALUE = -1e30
FORCE_VALUE = 1e4
NEVER_VALUE = -3e38
SSM_CHUNK = 16

COL_Q, COL_ZA, COL_US, COL_ZB, COL_GA, COL_GB, COL_KV, COL_NG = 0, 1024, 2048, 3072, 4096, 6144, 8192, 9728
PROJ_COLS = 9856
VMEM_LIMIT = 56 * 1024 * 1024


def _cparams(sem):
    return pltpu.CompilerParams(dimension_semantics=sem, vmem_limit_bytes=VMEM_LIMIT)


def _ada_kernel(c_ref, w_ref, b_ref, o_ref):
    o_ref[...] = jnp.dot(c_ref[...], w_ref[...], preferred_element_type=F32, precision=HIGHEST) + b_ref[...]


def _ada_mod(c, w_ada, b_ada):
    B, D = c.shape
    n = w_ada.shape[1]
    tn = 1536
    return pl.pallas_call(
        _ada_kernel,
        grid=(n // tn,),
        in_specs=[pl.BlockSpec((B, D), lambda j: (0, 0)),
                  pl.BlockSpec((D, tn), lambda j: (0, j)),
                  pl.BlockSpec((1, tn), lambda j: (0, j))],
        out_specs=pl.BlockSpec((B, tn), lambda j: (0, j)),
        out_shape=jax.ShapeDtypeStruct((B, n), F32),
        compiler_params=_cparams(("parallel",)),
        name="ada_mod",
    )(c, w_ada, b_ada.reshape(1, n))


def _inproj_kernel(x_ref, scale_ref, shift_ref, w_ref, o_ref, h_ref):
    @pl.when(pl.program_id(1) == 0)
    def _():
        x = x_ref[...]
        mu = jnp.mean(x, axis=-1, keepdims=True)
        xc = x - mu
        var = jnp.mean(xc * xc, axis=-1, keepdims=True)
        hn = xc * lax.rsqrt(var + LN_EPS)
        h_ref[...] = (hn * (1.0 + scale_ref[...]) + shift_ref[...]).astype(BF16)

    o_ref[...] = jnp.dot(h_ref[...], w_ref[...], preferred_element_type=F32).astype(o_ref.dtype)


def _in_proj(x2, scale, shift, w, L):
    N, D = x2.shape
    ncol = w.shape[1]
    tm = min(512, L)
    tn = 1408
    assert N % tm == 0 and L % tm == 0 and ncol % tn == 0
    B = scale.shape[0]
    return pl.pallas_call(
        _inproj_kernel,
        grid=(N // tm, ncol // tn),
        in_specs=[pl.BlockSpec((tm, D), lambda i, j: (i, 0)),
                  pl.BlockSpec((None, 1, D), lambda i, j: ((i * tm) // L, 0, 0)),
                  pl.BlockSpec((None, 1, D), lambda i, j: ((i * tm) // L, 0, 0)),
                  pl.BlockSpec((D, tn), lambda i, j: (0, j))],
        out_specs=pl.BlockSpec((tm, tn), lambda i, j: (i, j)),
        out_shape=jax.ShapeDtypeStruct((N, ncol), BF16),
        scratch_shapes=[pltpu.VMEM((tm, D), BF16)],
        compiler_params=_cparams(("parallel", "arbitrary")),
        name="in_proj",
    )(x2, scale.reshape(B, 1, D), shift.reshape(B, 1, D), w)


def _cmp_kernel(x_ref, w1_ref, w2_ref, pos_ref, o_ref):
    half = CMP_STRIDE * NSA_HEAD_DIM
    x = x_ref[...]
    w1 = w1_ref[...]
    a = jnp.dot(x, w1[:half].astype(BF16), preferred_element_type=F32)
    b = jnp.dot(x, w1[half:].astype(BF16), preferred_element_type=F32)
    pw = jnp.dot(pos_ref[...], w1, preferred_element_type=F32, precision=HIGHEST)[0:1]
    ncp = x.shape[0]
    h = a + pltpu.roll(b, ncp - 1, 0) + pw
    y = jnp.dot(jax.nn.gelu(h).astype(BF16), w2_ref[...].astype(BF16), preferred_element_type=F32)
    o_ref[...] = y


def _nsa_compress(xc, w1s, w2s, poss):
    _, B, G, ncp, width = xc.shape
    return pl.pallas_call(
        _cmp_kernel,
        grid=(2, B, G),
        in_specs=[pl.BlockSpec((None, None, None, ncp, width), lambda s, b, g: (s, b, g, 0, 0)),
                  pl.BlockSpec((None,) + w1s.shape[1:], lambda s, b, g: (s, 0, 0)),
                  pl.BlockSpec((None,) + w2s.shape[1:], lambda s, b, g: (s, 0, 0)),
                  pl.BlockSpec((None,) + poss.shape[1:], lambda s, b, g: (s, 0, 0))],
        out_specs=pl.BlockSpec((None, None, None, ncp, NSA_HEAD_DIM), lambda s, b, g: (s, b, g, 0, 0)),
        out_shape=jax.ShapeDtypeStruct((2, B, G, ncp, NSA_HEAD_DIM), F32),
        compiler_params=_cparams(("parallel", "parallel", "parallel")),
        name="nsa_compress",
    )(xc, w1s, w2s, poss)


SEL_TILE = 512
SEL_TABLE_FAR = 640


def _tree(op, parts):
    while len(parts) > 1:
        parts = [op(parts[i], parts[i + 1]) if i + 1 < len(parts) else parts[i] for i in range(0, len(parts), 2)]
    return parts[0]


def _fold8(op, x):
    return _tree(op, [x[k:k + 8] for k in range(0, x.shape[0], 8)])


def _safe_inv(l):
    return jnp.where(l > 0.0, 1.0 / jnp.where(l > 0.0, l, 1.0), 0.0)


def _nsa_kernel(q_ref, za_ref, ng_ref, kc_ref, vct_ref, ks_ref, vst_ref, kw_ref, vwt_ref,
                tc_ref, tw_ref, ts_ref, ovt_ref, o_ref, s_ref, acc_ref, gt_ref, *, L):
    R, dh, QB = NSA_REP, NSA_HEAD_DIM, Q_BLOCK
    ncp = L // CMP_STRIDE
    nsel = L // SEL_BLOCK
    W = R * QB
    g = pl.program_id(1)
    qb = pl.program_id(2)

    qt = (q_ref[...].astype(F32) * (dh ** -0.5)).T
    qT = jnp.concatenate([qt[r * dh:(r + 1) * dh] for r in range(R)], axis=1).astype(BF16)

    c_off = pl.multiple_of((ncp - 8) - 8 * qb, 8)
    sc = jnp.dot(kc_ref[...], qT, preferred_element_type=F32) + tc_ref[pl.ds(c_off, ncp), :]
    m = jnp.maximum(jnp.max(_fold8(jnp.maximum, sc), axis=0, keepdims=True), 0.1 * MASK_VALUE)
    e = jnp.exp(sc - m)
    l = jnp.sum(_fold8(jnp.add, e), axis=0, keepdims=True)
    p_c = e * _safe_inv(l)
    oc = jnp.dot(vct_ref[...], p_c.astype(BF16), preferred_element_type=F32)

    psum = _tree(jnp.add, [p_c[:, r * QB:(r + 1) * QB] for r in range(R)])
    p_hi = psum.astype(BF16)
    p_lo = (psum - p_hi.astype(F32)).astype(BF16)
    ovt = ovt_ref[...]
    imp = jnp.dot(ovt, p_hi, preferred_element_type=F32) + jnp.dot(ovt, p_lo, preferred_element_type=F32)
    jj = lax.broadcasted_iota(jnp.int32, (nsel, QB), 0)
    ii = lax.broadcasted_iota(jnp.int32, (nsel, QB), 1)
    cur = 2 * qb + (ii >= SEL_BLOCK).astype(jnp.int32)
    forced = (jj == 0) | (jj == cur) | (jj == cur - 1)
    imp = jnp.where(forced, FORCE_VALUE, imp)
    imp = jnp.where(jj <= cur, imp, MASK_VALUE)
    jf = jj.astype(F32)
    sel = jnp.zeros((nsel, QB), F32)
    for _ in range(min(SEL_TOPK, nsel)):
        mx = jnp.max(_fold8(jnp.maximum, imp), axis=0, keepdims=True)
        idx = jnp.min(_fold8(jnp.minimum, jnp.where(imp == mx, jf, 1e9)), axis=0, keepdims=True)
        hit = jf == idx
        sel = jnp.where(hit & (mx > 0.1 * MASK_VALUE), 1.0, sel)
        imp = jnp.where(hit, -jnp.inf, imp)
    selbias = jnp.where(sel > 0.5, 0.0, MASK_VALUE).astype(BF16)
    q_sel = jnp.concatenate([qT, jnp.concatenate([selbias] * R, axis=1)], axis=0)

    w0 = pl.multiple_of(qb * QB, QB)
    flag = jnp.where(lax.broadcasted_iota(jnp.int32, (dh, W), 0) == 0, MASK_VALUE, 0.0).astype(BF16)
    q_win = jnp.concatenate([qT, flag], axis=0)
    sw = jnp.dot(kw_ref[pl.ds(w0, WINDOW + QB), :], q_win, preferred_element_type=F32) + tw_ref[...]
    m_w = jnp.max(_fold8(jnp.maximum, sw), axis=0, keepdims=True)
    pw = jnp.exp(sw - m_w).astype(BF16)
    accw = jnp.dot(vwt_ref[:, pl.ds(w0, WINDOW + QB)], pw, preferred_element_type=F32)
    owin = accw[:dh] * _safe_inv(accw[dh:dh + 1])

    gt_ref[...] = jax.nn.sigmoid(ng_ref[...].astype(F32)).T

    def gate_row(branch):
        return jnp.concatenate([gt_ref[pl.ds(3 * (g * R + r) + branch, 1), :] for r in range(R)], axis=1)

    o_cw = oc * gate_row(0) + owin * gate_row(2)
    gate_s = gate_row(1)

    npair = (qb // (SEL_TILE // QB) + 2) // 2

    def pass1(u, macc):
        for h in range(2):
            t = 2 * u + h
            k0 = pl.multiple_of(t * SEL_TILE, SEL_TILE)
            x0 = pl.multiple_of(jnp.maximum(t * SEL_TILE - qb * QB + SEL_TABLE_FAR, 0), QB)
            s = (jnp.dot(ks_ref[pl.ds(k0, SEL_TILE), :], q_sel, preferred_element_type=F32)
                 + ts_ref[pl.ds(x0, SEL_TILE), :])
            s_ref[pl.ds(k0, SEL_TILE), :] = s
            macc = jnp.maximum(macc, _fold8(jnp.maximum, s))
        return macc

    macc = lax.fori_loop(0, npair, pass1, jnp.full((8, W), NEVER_VALUE, F32))
    m_s = jnp.max(macc, axis=0, keepdims=True)

    acc_ref[...] = jnp.zeros_like(acc_ref)

    def pass2(u, carry):
        part = []
        for h in range(2):
            k0 = pl.multiple_of((2 * u + h) * SEL_TILE, SEL_TILE)
            p = jnp.exp(s_ref[pl.ds(k0, SEL_TILE), :] - m_s).astype(BF16)
            part.append(jnp.dot(vst_ref[:, pl.ds(k0, SEL_TILE)], p, preferred_element_type=F32))
        acc_ref[...] += part[0] + part[1]
        return carry

    lax.fori_loop(0, npair, pass2, 0)
    acc = acc_ref[...]
    osel = acc[:dh] * _safe_inv(acc[dh:dh + 1])

    ot = o_cw + osel * gate_s
    o = jnp.concatenate([ot[:, r * QB:(r + 1) * QB] for r in range(R)], axis=0).T
    za = za_ref[...].astype(F32)
    o_ref[...] = (o * (za * jax.nn.sigmoid(za))).astype(o_ref.dtype)


def _nsa_attend(proj, kc, vct, ks, vst, kw, vwt, tc, tw, ts, ovt, B, L):
    R, dh, G, QB = NSA_REP, NSA_HEAD_DIM, NSA_GROUPS, Q_BLOCK
    assert L % (2 * SEL_TILE) == 0
    nqb = L // QB
    gw = R * dh
    W = R * QB
    row = lambda b, g, i: b * nqb + i
    bg = lambda b, g, i: (b, g, 0, 0)
    grp = lambda b, g, i: (g, 0, 0)
    return pl.pallas_call(
        functools.partial(_nsa_kernel, L=L),
        grid=(B, G, nqb),
        in_specs=[
            pl.BlockSpec((QB, gw), lambda b, g, i: (row(b, g, i), COL_Q // gw + g)),
            pl.BlockSpec((QB, gw), lambda b, g, i: (row(b, g, i), COL_ZA // gw + g)),
            pl.BlockSpec((QB, 128), lambda b, g, i: (row(b, g, i), COL_NG // 128)),
            pl.BlockSpec((None, None) + kc.shape[2:], bg),
            pl.BlockSpec((None, None) + vct.shape[2:], bg),
            pl.BlockSpec((None, None) + ks.shape[2:], bg),
            pl.BlockSpec((None, None) + vst.shape[2:], bg),
            pl.BlockSpec((None, None) + kw.shape[2:], bg),
            pl.BlockSpec((None, None) + vwt.shape[2:], bg),
            pl.BlockSpec((None,) + tc.shape[1:], grp),
            pl.BlockSpec((None,) + tw.shape[1:], grp),
            pl.BlockSpec((None,) + ts.shape[1:], grp),
            pl.BlockSpec(ovt.shape, lambda b, g, i: (0, 0)),
        ],
        out_specs=pl.BlockSpec((QB, gw), lambda b, g, i: (row(b, g, i), g)),
        out_shape=jax.ShapeDtypeStruct((B * L, NSA_WIDTH), BF16),
        scratch_shapes=[pltpu.VMEM((L, W), F32), pltpu.VMEM((2 * dh, W), F32), pltpu.VMEM((128, QB), F32)],
        compiler_params=_cparams(("parallel", "parallel", "arbitrary")),
        name="nsa_attend",
    )(proj, proj, proj, kc, vct, ks, vst, kw, vwt, tc, tw, ts, ovt)


def _t5_bucket(dist):
    n = jnp.maximum(dist, 0)
    max_exact = REL_BUCKETS // 2
    nf = jnp.maximum(n, max_exact).astype(F32)
    large = max_exact + (jnp.log(nf / max_exact) / math.log(REL_MAX_DIST / max_exact)
                         * (REL_BUCKETS - max_exact)).astype(jnp.int32)
    large = jnp.minimum(large, REL_BUCKETS - 1)
    return jnp.where(n < max_exact, n, large)


def _toeplitz(f, d0, base, step, n_rows, width):
    f2 = f[..., base - step * (n_rows - 1) - d0:]
    if step == 1:
        w = jnp.stack([f2[..., i:i + n_rows] for i in range(width)], axis=-1)
    else:
        ch = f2[..., :(n_rows + width // step) * step].reshape(f2.shape[:-1] + (-1, step))
        w = jnp.concatenate([ch[..., k:k + n_rows, :] for k in range(width // step)], axis=-1)
    return w[..., ::-1, :]


def _nsa_tables(rel_bias, L):
    QB, G, R = Q_BLOCK, NSA_GROUPS, NSA_REP
    ncp = L // CMP_STRIDE
    nsel = L // SEL_BLOCK
    npos = L + QB
    tbl = rel_bias.astype(F32)
    bpos = (tbl[_t5_bucket(jnp.arange(npos, dtype=jnp.int32))] - tbl[REL_BUCKETS - 1][None, :]).T
    f = jnp.concatenate([jnp.full((NSA_HEADS, npos), MASK_VALUE, F32), bpos], axis=-1)
    d = np.arange(-npos, npos)
    fw = jnp.where(jnp.asarray(d < WINDOW), f, MASK_VALUE)
    tc = _toeplitz(f, -npos, L - QB - CMP_BLOCK + 1, CMP_STRIDE, 2 * ncp - 8, QB)
    tw = _toeplitz(fw, -npos, WINDOW, 1, WINDOW + QB, QB)
    ts = _toeplitz(f, -npos, SEL_TABLE_FAR, 1, SEL_TABLE_FAR + 2 * SEL_TILE, QB)
    lanes = lambda t: t.reshape(G, R, t.shape[1], QB).transpose(0, 2, 1, 3).reshape(G, t.shape[1], R * QB)
    c = np.arange(ncp)[None, :]
    j = np.arange(nsel)[:, None]
    ovt = ((c - 4 * j >= -1) & (c - 4 * j <= 3)).astype(np.float32)
    return lanes(tc), lanes(tw), lanes(ts), jnp.asarray(ovt, BF16)


def _s5_kernel(u_ref, mt_ref, ws_ref, wo_ref, lam_ref, y_ref, *, nk):
    u0, u1 = u_ref[0], u_ref[1]
    rows = u0.shape[0]
    half = 2 * SSM_STATE
    s = (jnp.dot(u0, ws_ref[0], preferred_element_type=F32)
         + jnp.dot(u1, ws_ref[1], preferred_element_type=F32))
    hr, hi = s[:, :half], s[:, half:]
    kidx = lax.broadcasted_iota(jnp.int32, (rows, half), 0) % nk
    d, step = 1, 0
    while d < nk:
        lr = lam_ref[step:step + 1, :half]
        li = lam_ref[step:step + 1, half:]
        keep = kidx >= d
        sr = jnp.where(keep, pltpu.roll(hr, d, 0), 0.0)
        si = jnp.where(keep, pltpu.roll(hi, d, 0), 0.0)
        hr, hi = hr + lr * sr - li * si, hi + lr * si + li * sr
        d, step = 2 * d, step + 1
    keep = kidx >= 1
    pr = jnp.where(keep, pltpu.roll(hr, 1, 0), 0.0)
    pi = jnp.where(keep, pltpu.roll(hi, 1, 0), 0.0)
    hcat = jnp.concatenate([pr, pi], axis=-1).astype(BF16)
    y_ref[0] = jnp.dot(u0, mt_ref[0], preferred_element_type=F32) + jnp.dot(hcat, wo_ref[0], preferred_element_type=F32)
    y_ref[1] = jnp.dot(u1, mt_ref[1], preferred_element_type=F32) + jnp.dot(hcat, wo_ref[1], preferred_element_type=F32)


def _s5_scan(ug, mt, ws, wo, lamp, nk):
    gs, rows, cw = ug.shape
    nstep = lamp.shape[1]
    return pl.pallas_call(
        functools.partial(_s5_kernel, nk=nk),
        grid=(gs // 2,),
        in_specs=[pl.BlockSpec((2, rows, cw), lambda p: (p, 0, 0)),
                  pl.BlockSpec((2, cw, cw), lambda p: (p, 0, 0)),
                  pl.BlockSpec((2, cw, cw), lambda p: (p, 0, 0)),
                  pl.BlockSpec((2, cw, cw), lambda p: (p, 0, 0)),
                  pl.BlockSpec((None, nstep, cw), lambda p: (p, 0, 0))],
        out_specs=pl.BlockSpec((2, rows, cw), lambda p: (p, 0, 0)),
        out_shape=jax.ShapeDtypeStruct((gs, rows, cw), F32),
        compiler_params=_cparams(("parallel",)),
        name="s5_scan",
    )(ug, mt, ws, wo, lamp)


def _cmul(ar, ai, br, bi):
    return ar * br - ai * bi, ar * bi + ai * br


def _s5_tables(a_re, a_im, log_dt, b_re, b_im, c_re, c_im, nk):
    T, P, C, Gs = SSM_CHUNK, SSM_STATE, SSM_GROUP, SSM_GROUPS
    dt = jnp.exp(log_dt.astype(F32))[:, None]
    ar, ai = a_re.astype(F32), a_im.astype(F32)
    mag = jnp.exp(ar * dt)
    lr, li = mag * jnp.cos(ai * dt), mag * jnp.sin(ai * dt)
    den = ar * ar + ai * ai
    nr, ni = lr - 1.0, li
    fr, fi = (nr * ar + ni * ai) / den, (ni * ar - nr * ai) / den
    br, bim = b_re.astype(F32), b_im.astype(F32)
    bbr = fr[..., None] * br - fi[..., None] * bim
    bbi = fr[..., None] * bim + fi[..., None] * br
    pr, pi = [jnp.ones_like(lr)], [jnp.zeros_like(lr)]
    for _ in range(T):
        nr_, ni_ = _cmul(pr[-1], pi[-1], lr, li)
        pr.append(nr_)
        pi.append(ni_)
    pwr, pwi = jnp.stack(pr, 0), jnp.stack(pi, 0)
    zr = pwr[..., None] * bbr[None] - pwi[..., None] * bbi[None]
    zi = pwr[..., None] * bbi[None] + pwi[..., None] * bbr[None]
    cr, ci = c_re.astype(F32), c_im.astype(F32)
    kj = (jnp.einsum('gcp,jgpd->gjcd', cr, zr[:T], precision=HIGHEST)
          - jnp.einsum('gcp,jgpd->gjcd', ci, zi[:T], precision=HIGHEST))
    wj = jnp.concatenate([kj.transpose(0, 3, 2, 1), jnp.zeros((Gs, C, C, T + 1), F32)], axis=-1)
    mt = jnp.tile(wj, (1, 1, 1, T))[..., :2 * T * T].reshape(Gs, C, C, T, 2 * T)[..., :T]
    mt = mt.transpose(0, 3, 1, 4, 2).reshape(Gs, T * C, T * C)
    wsr = zr[:T][::-1].transpose(1, 0, 3, 2).reshape(Gs, T * C, P)
    wsi = zi[:T][::-1].transpose(1, 0, 3, 2).reshape(Gs, T * C, P)
    clr = cr[:, None] * pwr[1:].transpose(1, 0, 2)[:, :, None, :] - ci[:, None] * pwi[1:].transpose(1, 0, 2)[:, :, None, :]
    cli = cr[:, None] * pwi[1:].transpose(1, 0, 2)[:, :, None, :] + ci[:, None] * pwr[1:].transpose(1, 0, 2)[:, :, None, :]
    wor = clr.transpose(0, 3, 1, 2).reshape(Gs, P, T * C)
    woi = (-cli).transpose(0, 3, 1, 2).reshape(Gs, P, T * C)
    z = jnp.zeros((Gs // 2, T * C, P), F32)
    wsr2, wsi2 = wsr.reshape(Gs // 2, 2, T * C, P), wsi.reshape(Gs // 2, 2, T * C, P)
    ws = jnp.stack([jnp.concatenate([wsr2[:, 0], z, wsi2[:, 0], z], -1),
                    jnp.concatenate([z, wsr2[:, 1], z, wsi2[:, 1]], -1)], 1).reshape(Gs, T * C, 4 * P)
    zt = jnp.zeros((Gs // 2, P, T * C), F32)
    wor2, woi2 = wor.reshape(Gs // 2, 2, P, T * C), woi.reshape(Gs // 2, 2, P, T * C)
    wo = jnp.stack([jnp.concatenate([wor2[:, 0], zt, woi2[:, 0], zt], 1),
                    jnp.concatenate([zt, wor2[:, 1], zt, woi2[:, 1]], 1)], 1).reshape(Gs, 4 * P, T * C)
    qr, qi = pwr[T], pwi[T]
    steps = []
    d = 1
    while d < nk:
        q2r, q2i = qr.reshape(Gs // 2, 2, P), qi.reshape(Gs // 2, 2, P)
        steps.append(jnp.concatenate([q2r[:, 0], q2r[:, 1], q2i[:, 0], q2i[:, 1]], -1))
        qr, qi = _cmul(qr, qi, qr, qi)
        d *= 2
    lamp = jnp.stack(steps, 1)
    return mt.astype(BF16), ws.astype(BF16), wo.astype(BF16), lamp


def _glu_kernel(y_ref, u_ref, zb_ref, d_ref, w_ref, b_ref, o_ref):
    y = y_ref[...] + d_ref[...] * u_ref[...].astype(F32)
    yg = jax.nn.gelu(y).astype(BF16)
    z = jnp.dot(yg, w_ref[...], preferred_element_type=F32) + b_ref[...]
    zb = zb_ref[...].astype(F32)
    o_ref[...] = (yg.astype(F32) * jax.nn.sigmoid(z) * (zb * jax.nn.sigmoid(zb))).astype(o_ref.dtype)


def _s5_glu(y, proj, d_skip, w_glu, b_glu):
    N, W = y.shape
    tm = min(1024, N)
    return pl.pallas_call(
        _glu_kernel,
        grid=(N // tm,),
        in_specs=[pl.BlockSpec((tm, W), lambda i: (i, 0)),
                  pl.BlockSpec((tm, W), lambda i: (i, COL_US // W)),
                  pl.BlockSpec((tm, W), lambda i: (i, COL_ZB // W)),
                  pl.BlockSpec((1, W), lambda i: (0, 0)),
                  pl.BlockSpec((W, W), lambda i: (0, 0)),
                  pl.BlockSpec((1, W), lambda i: (0, 0))],
        out_specs=pl.BlockSpec((tm, W), lambda i: (i, 0)),
        out_shape=jax.ShapeDtypeStruct((N, W), BF16),
        compiler_params=_cparams(("parallel",)),
        name="s5_glu",
    )(y, proj, proj, d_skip.reshape(1, W), w_glu, b_glu.reshape(1, W))


def _merge_kernel(oa_ref, ob_ref, ga_ref, gb_ref, x_ref, gate_ref, wa_ref, wb_ref, wo_ref, lg_ref, lb_ref, o_ref):
    pa = jnp.dot(oa_ref[...], wa_ref[...], preferred_element_type=F32)
    pb = jnp.dot(ob_ref[...], wb_ref[...], preferred_element_type=F32)
    m = jax.nn.sigmoid(ga_ref[...].astype(F32)) * pa + jax.nn.sigmoid(gb_ref[...].astype(F32)) * pb
    y = jnp.dot(m.astype(BF16), wo_ref[...], preferred_element_type=F32)
    r = DEEPNORM_ALPHA * x_ref[...] + gate_ref[...] * y
    mu = jnp.mean(r, axis=-1, keepdims=True)
    rc = r - mu
    var = jnp.mean(rc * rc, axis=-1, keepdims=True)
    o_ref[...] = rc * lax.rsqrt(var + LN_EPS) * lg_ref[...] + lb_ref[...]


def _merge_out(oa, ob, proj, x2, gate, wa, wb, wo, ln_g, ln_b, L):
    N, D = x2.shape
    W = oa.shape[1]
    B = gate.shape[0]
    tm = min(256, L)
    const = lambda i: (0, 0)
    return pl.pallas_call(
        _merge_kernel,
        grid=(N // tm,),
        in_specs=[pl.BlockSpec((tm, W), lambda i: (i, 0)),
                  pl.BlockSpec((tm, W), lambda i: (i, 0)),
                  pl.BlockSpec((tm, D), lambda i: (i, COL_GA // D)),
                  pl.BlockSpec((tm, D), lambda i: (i, COL_GB // D)),
                  pl.BlockSpec((tm, D), lambda i: (i, 0)),
                  pl.BlockSpec((None, 1, D), lambda i: ((i * tm) // L, 0, 0)),
                  pl.BlockSpec((W, D), const),
                  pl.BlockSpec((W, D), const),
                  pl.BlockSpec((D, D), const),
                  pl.BlockSpec((1, D), const),
                  pl.BlockSpec((1, D), const)],
        out_specs=pl.BlockSpec((tm, D), lambda i: (i, 0)),
        out_shape=jax.ShapeDtypeStruct((N, D), F32),
        compiler_params=_cparams(("parallel",)),
        name="merge_out",
    )(oa, ob, proj, proj, x2, gate.reshape(B, 1, D), wa, wb, wo, ln_g.reshape(1, D), ln_b.reshape(1, D))


def _layer(x, c, w_ada, b_ada, w_in, rel_bias, cmp_pos_k, cmp_pos_v, w_cmp_k1, w_cmp_k2, w_cmp_v1, w_cmp_v2,
           ssm_a_re, ssm_a_im, ssm_log_dt, ssm_b_re, ssm_b_im, ssm_c_re, ssm_c_im, ssm_d, w_glu, b_glu,
           w_branch_nsa, w_branch_ssm, w_out, ln_g, ln_b):
    B, L, D = x.shape
    N = B * L
    G, dh = NSA_GROUPS, NSA_HEAD_DIM
    x2 = x.reshape(N, D)

    mod = _ada_mod(c, w_ada, b_ada)
    shift, scale, gate = mod[:, :D], mod[:, D:2 * D], mod[:, 2 * D:]

    o_q, o_kv, o_ng, o_za, o_us, o_zb, o_ga, o_gb = np.cumsum(
        [0, NSA_WIDTH, 6 * KV_WIDTH, 3 * NSA_HEADS, NSA_WIDTH, SSM_WIDTH, SSM_WIDTH, D_MODEL]).tolist()
    w_re = jnp.concatenate([
        w_in[:, o_q:o_kv], w_in[:, o_za:o_us], w_in[:, o_us:o_zb], w_in[:, o_zb:o_ga], w_in[:, o_ga:o_gb],
        w_in[:, o_gb:], w_in[:, o_kv:o_ng], w_in[:, o_ng:o_za],
        jnp.zeros((D, PROJ_COLS - COL_NG - 3 * NSA_HEADS), w_in.dtype)], axis=1).astype(BF16)
    proj = _in_proj(x2, scale, shift, w_re, L)

    kv = proj[:, COL_KV:COL_KV + 6 * KV_WIDTH].reshape(B, L, 6, G, dh)
    ncp = L // CMP_STRIDE
    xc = kv[:, :, 0:2].reshape(B, ncp, CMP_STRIDE, 2, G, dh).transpose(3, 0, 4, 1, 2, 5)
    xc = xc.reshape(2, B, G, ncp, CMP_STRIDE * dh)
    w1s = jnp.stack([w_cmp_k1, w_cmp_v1])
    w2s = jnp.stack([w_cmp_k2, w_cmp_v2])
    poss = jnp.broadcast_to(jnp.stack([cmp_pos_k, cmp_pos_v]).reshape(2, 1, CMP_BLOCK * dh), (2, 8, CMP_BLOCK * dh))
    kcv = _nsa_compress(xc, w1s, w2s, poss)
    kc = kcv[0].astype(BF16)
    vct = kcv[1].swapaxes(-1, -2).astype(BF16)
    nsel = L // SEL_BLOCK
    e_abs = jnp.asarray(np.arange(L)[:, None] // SEL_BLOCK == np.arange(nsel)[None, :], BF16)
    ks = jnp.concatenate([kv[:, :, 2].transpose(0, 2, 1, 3), jnp.broadcast_to(e_abs, (B, G, L, nsel))], axis=-1)
    vrows = lambda n: [jnp.ones((B, G, 1, n), BF16), jnp.zeros((B, G, dh - 1, n), BF16)]
    vst = jnp.concatenate([kv[:, :, 3].transpose(0, 2, 3, 1)] + vrows(L), axis=2)
    padflag = jnp.asarray((np.arange(L + WINDOW) < WINDOW)[:, None] & (np.arange(dh) == 0)[None, :], BF16)
    kw = jnp.pad(kv[:, :, 4].transpose(0, 2, 1, 3), ((0, 0), (0, 0), (WINDOW, 0), (0, 0)))
    kw = jnp.concatenate([kw, jnp.broadcast_to(padflag, (B, G, L + WINDOW, dh))], axis=-1)
    vwt = jnp.pad(kv[:, :, 5].transpose(0, 2, 3, 1), ((0, 0), (0, 0), (0, 0), (WINDOW, 0)))
    vwt = jnp.concatenate([vwt] + vrows(L + WINDOW), axis=2)
    tc, tw, ts, ovt = _nsa_tables(rel_bias, L)
    o_a = _nsa_attend(proj, kc, vct, ks, vst, kw, vwt, tc, tw, ts, ovt, B, L)

    T, C, Gs = SSM_CHUNK, SSM_GROUP, SSM_GROUPS
    nk = L // T
    us = proj[:, COL_US:COL_US + SSM_WIDTH]
    ug = us.reshape(B * nk, T, Gs, C).transpose(2, 0, 1, 3).reshape(Gs, B * nk, T * C)
    mt, ws, wo, lamp = _s5_tables(ssm_a_re, ssm_a_im, ssm_log_dt, ssm_b_re, ssm_b_im, ssm_c_re, ssm_c_im, nk)
    yg = _s5_scan(ug, mt, ws, wo, lamp, nk)
    y = yg.reshape(Gs, B * nk, T, C).transpose(1, 2, 0, 3).reshape(N, SSM_WIDTH)
    o_b = _s5_glu(y, proj, ssm_d, w_glu.astype(BF16), b_glu)

    out = _merge_out(o_a, o_b, proj, x2, gate, w_branch_nsa.astype(BF16), w_branch_ssm.astype(BF16),
                     w_out.astype(BF16), ln_g, ln_b, L)
    return out.reshape(B, L, D)


def kernel(x, c, w_ada, b_ada, w_in, rel_bias, cmp_pos_k, cmp_pos_v, w_cmp_k1, w_cmp_k2, w_cmp_v1, w_cmp_v2,
           ssm_a_re, ssm_a_im, ssm_log_dt, ssm_b_re, ssm_b_im, ssm_c_re, ssm_c_im, ssm_d, w_glu, b_glu,
           w_branch_nsa, w_branch_ssm, w_out, ln_g, ln_b):
    for i in range(w_ada.shape[0]):
        x = _layer(x, c, w_ada[i], b_ada[i], w_in[i], rel_bias, cmp_pos_k[i], cmp_pos_v[i], w_cmp_k1[i],
                   w_cmp_k2[i], w_cmp_v1[i], w_cmp_v2[i], ssm_a_re[i], ssm_a_im[i], ssm_log_dt[i], ssm_b_re[i],
                   ssm_b_im[i], ssm_c_re[i], ssm_c_im[i], ssm_d[i], w_glu[i], b_glu[i], w_branch_nsa[i],
                   w_branch_ssm[i], w_out[i], ln_g[i], ln_b[i])
    return x
```

```python
import functools
import math

import numpy as np
import jax
import jax.numpy as jnp
from jax import lax
from jax.experimental import pallas as pl
from jax.experimental.pallas import tpu as pltpu

F32 = jnp.float32
BF16 = jnp.bfloat16
HIGHEST = lax.Precision.HIGHEST

D_MODEL = 2048
NSA_HEADS = 16
NSA_GROUPS = 4
NSA_HEAD_DIM = 64
NSA_REP = NSA_HEADS // NSA_GROUPS
NSA_WIDTH = NSA_HEADS * NSA_HEAD_DIM
KV_WIDTH = NSA_GROUPS * NSA_HEAD_DIM
CMP_BLOCK = 32
CMP_STRIDE = 16
CMP_HIDDEN = 128
SEL_BLOCK = 64
SEL_TOPK = 8
WINDOW = 512
Q_BLOCK = 128
SSM_WIDTH = 1024
SSM_GROUP = 16
SSM_GROUPS = SSM_WIDTH // SSM_GROUP
SSM_STATE = 64
REL_BUCKETS = 32
REL_MAX_DIST = 128
DEEPNORM_ALPHA = 2.0 ** 0.25
LN_EPS = 1e-5
MASK_VALUE = -1e30
FORCE_VALUE = 1e4
NEVER_VALUE = -3e38
SSM_CHUNK = 16

COL_Q, COL_ZA, COL_US, COL_ZB, COL_GA, COL_GB, COL_KV, COL_NG = 0, 1024, 2048, 3072, 4096, 6144, 8192, 9728
PROJ_COLS = 9856
VMEM_LIMIT = 56 * 1024 * 1024


def _cparams(sem):
    return pltpu.CompilerParams(dimension_semantics=sem, vmem_limit_bytes=VMEM_LIMIT)


def _ada_kernel(c_ref, w_ref, b_ref, o_ref):
    o_ref[...] = jnp.dot(c_ref[...], w_ref[...], preferred_element_type=F32, precision=HIGHEST) + b_ref[...]


def _ada_mod(c, w_ada, b_ada):
    B, D = c.shape
    n = w_ada.shape[1]
    tn = 1536
    return pl.pallas_call(
        _ada_kernel,
        grid=(n // tn,),
        in_specs=[pl.BlockSpec((B, D), lambda j: (0, 0)),
                  pl.BlockSpec((D, tn), lambda j: (0, j)),
                  pl.BlockSpec((1, tn), lambda j: (0, j))],
        out_specs=pl.BlockSpec((B, tn), lambda j: (0, j)),
        out_shape=jax.ShapeDtypeStruct((B, n), F32),
        compiler_params=_cparams(("parallel",)),
        name="ada_mod",
    )(c, w_ada, b_ada.reshape(1, n))


def _inproj_kernel(x_ref, scale_ref, shift_ref, w_ref, o_ref, h_ref):
    @pl.when(pl.program_id(1) == 0)
    def _():
        def rows(k, carry):
            r0 = pl.multiple_of(k * LN_ROWS, LN_ROWS)
            x = x_ref[pl.ds(r0, LN_ROWS), :]
            mu = jnp.mean(x, axis=-1, keepdims=True)
            xc = x - mu
            var = jnp.mean(xc * xc, axis=-1, keepdims=True)
            hn = xc * lax.rsqrt(var + LN_EPS)
            h_ref[pl.ds(r0, LN_ROWS), :] = (hn * (1.0 + scale_ref[...]) + shift_ref[...]).astype(BF16)
            return carry

        lax.fori_loop(0, x_ref.shape[0] // LN_ROWS, rows, 0)

    o_ref[...] = jnp.dot(h_ref[...], w_ref[...], preferred_element_type=F32).astype(o_ref.dtype)


LN_ROWS = 256


def _in_proj(x2, scale, shift, w, L):
    N, D = x2.shape
    ncol = w.shape[1]
    tm = min(1024, L)
    tn = 1408
    assert N % tm == 0 and L % tm == 0 and ncol % tn == 0
    B = scale.shape[0]
    return pl.pallas_call(
        _inproj_kernel,
        grid=(N // tm, ncol // tn),
        in_specs=[pl.BlockSpec((tm, D), lambda i, j: (i, 0)),
                  pl.BlockSpec((None, 1, D), lambda i, j: ((i * tm) // L, 0, 0)),
                  pl.BlockSpec((None, 1, D), lambda i, j: ((i * tm) // L, 0, 0)),
                  pl.BlockSpec((D, tn), lambda i, j: (0, j))],
        out_specs=pl.BlockSpec((tm, tn), lambda i, j: (i, j)),
        out_shape=jax.ShapeDtypeStruct((N, ncol), BF16),
        scratch_shapes=[pltpu.VMEM((tm, D), BF16)],
        compiler_params=_cparams(("parallel", "arbitrary")),
        name="in_proj",
    )(x2, scale.reshape(B, 1, D), shift.reshape(B, 1, D), w)


def _cmp_kernel(x_ref, w1_ref, w2_ref, pos_ref, o_ref):
    half = CMP_STRIDE * NSA_HEAD_DIM
    x = x_ref[...]
    w1 = w1_ref[...]
    a = jnp.dot(x, w1[:half].astype(BF16), preferred_element_type=F32)
    b = jnp.dot(x, w1[half:].astype(BF16), preferred_element_type=F32)
    pw = jnp.dot(pos_ref[...], w1, preferred_element_type=F32, precision=HIGHEST)[0:1]
    ncp = x.shape[0]
    h = a + pltpu.roll(b, ncp - 1, 0) + pw
    y = jnp.dot(jax.nn.gelu(h).astype(BF16), w2_ref[...].astype(BF16), preferred_element_type=F32)
    o_ref[...] = y


def _nsa_compress(xc, w1s, w2s, poss):
    _, B, G, ncp, width = xc.shape
    return pl.pallas_call(
        _cmp_kernel,
        grid=(2, B, G),
        in_specs=[pl.BlockSpec((None, None, None, ncp, width), lambda s, b, g: (s, b, g, 0, 0)),
                  pl.BlockSpec((None,) + w1s.shape[1:], lambda s, b, g: (s, 0, 0)),
                  pl.BlockSpec((None,) + w2s.shape[1:], lambda s, b, g: (s, 0, 0)),
                  pl.BlockSpec((None,) + poss.shape[1:], lambda s, b, g: (s, 0, 0))],
        out_specs=pl.BlockSpec((None, None, None, ncp, NSA_HEAD_DIM), lambda s, b, g: (s, b, g, 0, 0)),
        out_shape=jax.ShapeDtypeStruct((2, B, G, ncp, NSA_HEAD_DIM), F32),
        compiler_params=_cparams(("parallel", "parallel", "parallel")),
        name="nsa_compress",
    )(xc, w1s, w2s, poss)


SEL_TILE = 512
SEL_TABLE_FAR = 640


def _tree(op, parts):
    while len(parts) > 1:
        parts = [op(parts[i], parts[i + 1]) if i + 1 < len(parts) else parts[i] for i in range(0, len(parts), 2)]
    return parts[0]


def _fold8(op, x):
    return _tree(op, [x[k:k + 8] for k in range(0, x.shape[0], 8)])


def _safe_inv(l):
    return jnp.where(l > 0.0, 1.0 / jnp.where(l > 0.0, l, 1.0), 0.0)


def _nsa_kernel(q_ref, za_ref, ng_ref, kc_ref, vct_ref, ks_ref, vst_ref, kw_ref, vwt_ref,
                tc_ref, tw_ref, ts_ref, ovt_ref, o_ref, s_ref, acc_ref, gt_ref, *, L):
    R, dh, QB = NSA_REP, NSA_HEAD_DIM, Q_BLOCK
    ncp = L // CMP_STRIDE
    nsel = L // SEL_BLOCK
    W = R * QB
    g = pl.program_id(1)
    qb = pl.program_id(2)

    qt = (q_ref[...].astype(F32) * (dh ** -0.5)).T
    qT = jnp.concatenate([qt[r * dh:(r + 1) * dh] for r in range(R)], axis=1).astype(BF16)

    c_off = pl.multiple_of((ncp - 8) - 8 * qb, 8)
    sc = jnp.dot(kc_ref[...], qT, preferred_element_type=F32) + tc_ref[pl.ds(c_off, ncp), :]
    m = jnp.maximum(jnp.max(_fold8(jnp.maximum, sc), axis=0, keepdims=True), 0.1 * MASK_VALUE)
    e = jnp.exp(sc - m)
    l = jnp.sum(_fold8(jnp.add, e), axis=0, keepdims=True)
    p_c = e * _safe_inv(l)
    oc = jnp.dot(vct_ref[...], p_c.astype(BF16), preferred_element_type=F32)

    psum = _tree(jnp.add, [p_c[:, r * QB:(r + 1) * QB] for r in range(R)])
    p_hi = psum.astype(BF16)
    p_lo = (psum - p_hi.astype(F32)).astype(BF16)
    ovt = ovt_ref[...]
    imp = jnp.dot(ovt, p_hi, preferred_element_type=F32) + jnp.dot(ovt, p_lo, preferred_element_type=F32)
    jj = lax.broadcasted_iota(jnp.int32, (nsel, QB), 0)
    ii = lax.broadcasted_iota(jnp.int32, (nsel, QB), 1)
    cur = 2 * qb + (ii >= SEL_BLOCK).astype(jnp.int32)
    forced = (jj == 0) | (jj == cur) | (jj == cur - 1)
    imp = jnp.where(forced, FORCE_VALUE, imp)
    imp = jnp.where(jj <= cur, imp, MASK_VALUE)
    jf = jj.astype(F32)
    sel = jnp.zeros((nsel, QB), F32)
    for _ in range(min(SEL_TOPK, nsel)):
        mx = jnp.max(_fold8(jnp.maximum, imp), axis=0, keepdims=True)
        idx = jnp.min(_fold8(jnp.minimum, jnp.where(imp == mx, jf, 1e9)), axis=0, keepdims=True)
        hit = jf == idx
        sel = jnp.where(hit & (mx > 0.1 * MASK_VALUE), 1.0, sel)
        imp = jnp.where(hit, -jnp.inf, imp)
    selbias = jnp.where(sel > 0.5, 0.0, MASK_VALUE).astype(BF16)
    q_sel = jnp.concatenate([qT, jnp.concatenate([selbias] * R, axis=1)], axis=0)

    w0 = pl.multiple_of(qb * QB, QB)
    flag = jnp.where(lax.broadcasted_iota(jnp.int32, (dh, W), 0) == 0, MASK_VALUE, 0.0).astype(BF16)
    q_win = jnp.concatenate([qT, flag], axis=0)
    sw = jnp.dot(kw_ref[pl.ds(w0, WINDOW + QB), :], q_win, preferred_element_type=F32) + tw_ref[...]
    m_w = jnp.max(_fold8(jnp.maximum, sw), axis=0, keepdims=True)
    pw = jnp.exp(sw - m_w).astype(BF16)
    accw = jnp.dot(vwt_ref[:, pl.ds(w0, WINDOW + QB)], pw, preferred_element_type=F32)
    owin = accw[:dh] * _safe_inv(accw[dh:dh + 1])

    gt_ref[...] = jax.nn.sigmoid(ng_ref[...].astype(F32)).T

    def gate_row(branch):
        return jnp.concatenate([gt_ref[pl.ds(3 * (g * R + r) + branch, 1), :] for r in range(R)], axis=1)

    o_cw = oc * gate_row(0) + owin * gate_row(2)
    gate_s = gate_row(1)

    npair = (qb // (SEL_TILE // QB) + 2) // 2

    def pass1(u, macc):
        for h in range(2):
            t = 2 * u + h
            k0 = pl.multiple_of(t * SEL_TILE, SEL_TILE)
            x0 = pl.multiple_of(jnp.maximum(t * SEL_TILE - qb * QB + SEL_TABLE_FAR, 0), QB)
            s = (jnp.dot(ks_ref[pl.ds(k0, SEL_TILE), :], q_sel, preferred_element_type=F32)
                 + ts_ref[pl.ds(x0, SEL_TILE), :])
            s_ref[pl.ds(k0, SEL_TILE), :] = s
            macc = jnp.maximum(macc, _fold8(jnp.maximum, s))
        return macc

    macc = lax.fori_loop(0, npair, pass1, jnp.full((8, W), NEVER_VALUE, F32))
    m_s = jnp.max(macc, axis=0, keepdims=True)

    acc_ref[...] = jnp.zeros_like(acc_ref)

    def pass2(u, carry):
        part = []
        for h in range(2):
            k0 = pl.multiple_of((2 * u + h) * SEL_TILE, SEL_TILE)
            p = jnp.exp(s_ref[pl.ds(k0, SEL_TILE), :] - m_s).astype(BF16)
            part.append(jnp.dot(vst_ref[:, pl.ds(k0, SEL_TILE)], p, preferred_element_type=F32))
        acc_ref[...] += part[0] + part[1]
        return carry

    lax.fori_loop(0, npair, pass2, 0)
    acc = acc_ref[...]
    osel = acc[:dh] * _safe_inv(acc[dh:dh + 1])

    ot = o_cw + osel * gate_s
    o = jnp.concatenate([ot[:, r * QB:(r + 1) * QB] for r in range(R)], axis=0).T
    za = za_ref[...].astype(F32)
    o_ref[...] = (o * (za * jax.nn.sigmoid(za))).astype(o_ref.dtype)


def _nsa_attend(proj, kc, vct, ks, vst, kw, vwt, tc, tw, ts, ovt, B, L):
    R, dh, G, QB = NSA_REP, NSA_HEAD_DIM, NSA_GROUPS, Q_BLOCK
    assert L % (2 * SEL_TILE) == 0
    nqb = L // QB
    gw = R * dh
    W = R * QB
    row = lambda b, g, i: b * nqb + i
    bg = lambda b, g, i: (b, g, 0, 0)
    grp = lambda b, g, i: (g, 0, 0)
    return pl.pallas_call(
        functools.partial(_nsa_kernel, L=L),
        grid=(B, G, nqb),
        in_specs=[
            pl.BlockSpec((QB, gw), lambda b, g, i: (row(b, g, i), COL_Q // gw + g)),
            pl.BlockSpec((QB, gw), lambda b, g, i: (row(b, g, i), COL_ZA // gw + g)),
            pl.BlockSpec((QB, 128), lambda b, g, i: (row(b, g, i), COL_NG // 128)),
            pl.BlockSpec((None, None) + kc.shape[2:], bg),
            pl.BlockSpec((None, None) + vct.shape[2:], bg),
            pl.BlockSpec((None, None) + ks.shape[2:], bg),
            pl.BlockSpec((None, None) + vst.shape[2:], bg),
            pl.BlockSpec((None, None) + kw.shape[2:], bg),
            pl.BlockSpec((None, None) + vwt.shape[2:], bg),
            pl.BlockSpec((None,) + tc.shape[1:], grp),
            pl.BlockSpec((None,) + tw.shape[1:], grp),
            pl.BlockSpec((None,) + ts.shape[1:], grp),
            pl.BlockSpec(ovt.shape, lambda b, g, i: (0, 0)),
        ],
        out_specs=pl.BlockSpec((QB, gw), lambda b, g, i: (row(b, g, i), g)),
        out_shape=jax.ShapeDtypeStruct((B * L, NSA_WIDTH), BF16),
        scratch_shapes=[pltpu.VMEM((L, W), F32), pltpu.VMEM((2 * dh, W), F32), pltpu.VMEM((128, QB), F32)],
        compiler_params=_cparams(("parallel", "parallel", "arbitrary")),
        name="nsa_attend",
    )(proj, proj, proj, kc, vct, ks, vst, kw, vwt, tc, tw, ts, ovt)


def _t5_bucket(dist):
    n = jnp.maximum(dist, 0)
    max_exact = REL_BUCKETS // 2
    nf = jnp.maximum(n, max_exact).astype(F32)
    large = max_exact + (jnp.log(nf / max_exact) / math.log(REL_MAX_DIST / max_exact)
                         * (REL_BUCKETS - max_exact)).astype(jnp.int32)
    large = jnp.minimum(large, REL_BUCKETS - 1)
    return jnp.where(n < max_exact, n, large)


def _toeplitz(f, d0, base, step, n_rows, width):
    rpb = width // step
    nblk = n_rows // rpb
    assert rpb * step == width and nblk * rpb == n_rows
    lo = base - width * nblk - d0
    seg = f[..., lo:lo + width * (nblk + 1)].reshape(f.shape[:-1] + (nblk + 1, width))[..., ::-1, :]
    win = jnp.concatenate([seg[..., :-1, :], seg[..., 1:, :]], axis=-1)
    t = jnp.tile(win, rpb)[..., :rpb * (2 * width - step)].reshape(win.shape[:-1] + (rpb, 2 * width - step))
    return t[..., :width].reshape(f.shape[:-1] + (n_rows, width))


def _nsa_tables(rel_bias, L):
    QB, G, R = Q_BLOCK, NSA_GROUPS, NSA_REP
    ncp = L // CMP_STRIDE
    nsel = L // SEL_BLOCK
    npos = L + QB
    tbl = rel_bias.astype(F32)
    bpos = (tbl[_t5_bucket(jnp.arange(npos, dtype=jnp.int32))] - tbl[REL_BUCKETS - 1][None, :]).T
    f = jnp.concatenate([jnp.full((NSA_HEADS, npos), MASK_VALUE, F32), bpos], axis=-1)
    d = np.arange(-npos, npos)
    fw = jnp.where(jnp.asarray(d < WINDOW), f, MASK_VALUE)
    tc = _toeplitz(f, -npos, L - QB - CMP_BLOCK + 1, CMP_STRIDE, 2 * ncp - 8, QB)
    tw = _toeplitz(fw, -npos, WINDOW, 1, WINDOW + QB, QB)
    ts = _toeplitz(f, -npos, SEL_TABLE_FAR, 1, SEL_TABLE_FAR + 2 * SEL_TILE, QB)
    lanes = lambda t: t.reshape(G, R, t.shape[1], QB).transpose(0, 2, 1, 3).reshape(G, t.shape[1], R * QB)
    c = np.arange(ncp)[None, :]
    j = np.arange(nsel)[:, None]
    ovt = ((c - 4 * j >= -1) & (c - 4 * j <= 3)).astype(np.float32)
    return lanes(tc), lanes(tw), lanes(ts), jnp.asarray(ovt, BF16)


def _s5_kernel(u_ref, mt_ref, ws_ref, wo_ref, lam_ref, y_ref, *, nk):
    u0, u1 = u_ref[0], u_ref[1]
    rows = u0.shape[0]
    half = 2 * SSM_STATE
    s = (jnp.dot(u0, ws_ref[0], preferred_element_type=F32)
         + jnp.dot(u1, ws_ref[1], preferred_element_type=F32))
    hr, hi = s[:, :half], s[:, half:]
    kidx = lax.broadcasted_iota(jnp.int32, (rows, half), 0) % nk
    d, step = 1, 0
    while d < nk:
        lr = lam_ref[step:step + 1, :half]
        li = lam_ref[step:step + 1, half:]
        keep = kidx >= d
        sr = jnp.where(keep, pltpu.roll(hr, d, 0), 0.0)
        si = jnp.where(keep, pltpu.roll(hi, d, 0), 0.0)
        hr, hi = hr + lr * sr - li * si, hi + lr * si + li * sr
        d, step = 2 * d, step + 1
    keep = kidx >= 1
    pr = jnp.where(keep, pltpu.roll(hr, 1, 0), 0.0)
    pi = jnp.where(keep, pltpu.roll(hi, 1, 0), 0.0)
    hcat = jnp.concatenate([pr, pi], axis=-1).astype(BF16)
    y_ref[0] = jnp.dot(u0, mt_ref[0], preferred_element_type=F32) + jnp.dot(hcat, wo_ref[0], preferred_element_type=F32)
    y_ref[1] = jnp.dot(u1, mt_ref[1], preferred_element_type=F32) + jnp.dot(hcat, wo_ref[1], preferred_element_type=F32)


def _s5_scan(ug, mt, ws, wo, lamp, nk):
    gs, rows, cw = ug.shape
    nstep = lamp.shape[1]
    return pl.pallas_call(
        functools.partial(_s5_kernel, nk=nk),
        grid=(gs // 2,),
        in_specs=[pl.BlockSpec((2, rows, cw), lambda p: (p, 0, 0)),
                  pl.BlockSpec((2, cw, cw), lambda p: (p, 0, 0)),
                  pl.BlockSpec((2, cw, cw), lambda p: (p, 0, 0)),
                  pl.BlockSpec((2, cw, cw), lambda p: (p, 0, 0)),
                  pl.BlockSpec((None, nstep, cw), lambda p: (p, 0, 0))],
        out_specs=pl.BlockSpec((2, rows, cw), lambda p: (p, 0, 0)),
        out_shape=jax.ShapeDtypeStruct((gs, rows, cw), F32),
        compiler_params=_cparams(("parallel",)),
        name="s5_scan",
    )(ug, mt, ws, wo, lamp)


def _cmul(ar, ai, br, bi):
    return ar * br - ai * bi, ar * bi + ai * br


def _s5_tables(a_re, a_im, log_dt, b_re, b_im, c_re, c_im, nk):
    T, P, C, Gs = SSM_CHUNK, SSM_STATE, SSM_GROUP, SSM_GROUPS
    dt = jnp.exp(log_dt.astype(F32))[:, None]
    ar, ai = a_re.astype(F32), a_im.astype(F32)
    mag = jnp.exp(ar * dt)
    lr, li = mag * jnp.cos(ai * dt), mag * jnp.sin(ai * dt)
    den = ar * ar + ai * ai
    nr, ni = lr - 1.0, li
    fr, fi = (nr * ar + ni * ai) / den, (ni * ar - nr * ai) / den
    br, bim = b_re.astype(F32), b_im.astype(F32)
    bbr = fr[..., None] * br - fi[..., None] * bim
    bbi = fr[..., None] * bim + fi[..., None] * br
    pr, pi = [jnp.ones_like(lr)], [jnp.zeros_like(lr)]
    for _ in range(T):
        nr_, ni_ = _cmul(pr[-1], pi[-1], lr, li)
        pr.append(nr_)
        pi.append(ni_)
    pwr, pwi = jnp.stack(pr, 0), jnp.stack(pi, 0)
    zr = pwr[..., None] * bbr[None] - pwi[..., None] * bbi[None]
    zi = pwr[..., None] * bbi[None] + pwi[..., None] * bbr[None]
    cr, ci = c_re.astype(F32), c_im.astype(F32)
    kj = (jnp.einsum('gcp,jgpd->gjcd', cr, zr[:T], precision=HIGHEST)
          - jnp.einsum('gcp,jgpd->gjcd', ci, zi[:T], precision=HIGHEST))
    wj = jnp.concatenate([kj.transpose(0, 3, 2, 1), jnp.zeros((Gs, C, C, T + 1), F32)], axis=-1)
    mt = jnp.tile(wj, (1, 1, 1, T))[..., :2 * T * T].reshape(Gs, C, C, T, 2 * T)[..., :T]
    mt = mt.transpose(0, 3, 1, 4, 2).reshape(Gs, T * C, T * C)
    wsr = zr[:T][::-1].transpose(1, 0, 3, 2).reshape(Gs, T * C, P)
    wsi = zi[:T][::-1].transpose(1, 0, 3, 2).reshape(Gs, T * C, P)
    clr = cr[:, None] * pwr[1:].transpose(1, 0, 2)[:, :, None, :] - ci[:, None] * pwi[1:].transpose(1, 0, 2)[:, :, None, :]
    cli = cr[:, None] * pwi[1:].transpose(1, 0, 2)[:, :, None, :] + ci[:, None] * pwr[1:].transpose(1, 0, 2)[:, :, None, :]
    wor = clr.transpose(0, 3, 1, 2).reshape(Gs, P, T * C)
    woi = (-cli).transpose(0, 3, 1, 2).reshape(Gs, P, T * C)
    z = jnp.zeros((Gs // 2, T * C, P), F32)
    wsr2, wsi2 = wsr.reshape(Gs // 2, 2, T * C, P), wsi.reshape(Gs // 2, 2, T * C, P)
    ws = jnp.stack([jnp.concatenate([wsr2[:, 0], z, wsi2[:, 0], z], -1),
                    jnp.concatenate([z, wsr2[:, 1], z, wsi2[:, 1]], -1)], 1).reshape(Gs, T * C, 4 * P)
    zt = jnp.zeros((Gs // 2, P, T * C), F32)
    wor2, woi2 = wor.reshape(Gs // 2, 2, P, T * C), woi.reshape(Gs // 2, 2, P, T * C)
    wo = jnp.stack([jnp.concatenate([wor2[:, 0], zt, woi2[:, 0], zt], 1),
                    jnp.concatenate([zt, wor2[:, 1], zt, woi2[:, 1]], 1)], 1).reshape(Gs, 4 * P, T * C)
    qr, qi = pwr[T], pwi[T]
    steps = []
    d = 1
    while d < nk:
        q2r, q2i = qr.reshape(Gs // 2, 2, P), qi.reshape(Gs // 2, 2, P)
        steps.append(jnp.concatenate([q2r[:, 0], q2r[:, 1], q2i[:, 0], q2i[:, 1]], -1))
        qr, qi = _cmul(qr, qi, qr, qi)
        d *= 2
    lamp = jnp.stack(steps, 1)
    return mt.astype(BF16), ws.astype(BF16), wo.astype(BF16), lamp


def _glu_kernel(y_ref, u_ref, zb_ref, d_ref, w_ref, b_ref, o_ref):
    y = y_ref[...] + d_ref[...] * u_ref[...].astype(F32)
    yg = jax.nn.gelu(y).astype(BF16)
    z = jnp.dot(yg, w_ref[...], preferred_element_type=F32) + b_ref[...]
    zb = zb_ref[...].astype(F32)
    o_ref[...] = (yg.astype(F32) * jax.nn.sigmoid(z) * (zb * jax.nn.sigmoid(zb))).astype(o_ref.dtype)


def _s5_glu(y, proj, d_skip, w_glu, b_glu):
    N, W = y.shape
    tm = min(1024, N)
    return pl.pallas_call(
        _glu_kernel,
        grid=(N // tm,),
        in_specs=[pl.BlockSpec((tm, W), lambda i: (i, 0)),
                  pl.BlockSpec((tm, W), lambda i: (i, COL_US // W)),
                  pl.BlockSpec((tm, W), lambda i: (i, COL_ZB // W)),
                  pl.BlockSpec((1, W), lambda i: (0, 0)),
                  pl.BlockSpec((W, W), lambda i: (0, 0)),
                  pl.BlockSpec((1, W), lambda i: (0, 0))],
        out_specs=pl.BlockSpec((tm, W), lambda i: (i, 0)),
        out_shape=jax.ShapeDtypeStruct((N, W), BF16),
        compiler_params=_cparams(("parallel",)),
        name="s5_glu",
    )(y, proj, proj, d_skip.reshape(1, W), w_glu, b_glu.reshape(1, W))


def _merge_kernel(oa_ref, ob_ref, ga_ref, gb_ref, x_ref, gate_ref, wa_ref, wb_ref, wo_ref, lg_ref, lb_ref, o_ref):
    pa = jnp.dot(oa_ref[...], wa_ref[...], preferred_element_type=F32)
    pb = jnp.dot(ob_ref[...], wb_ref[...], preferred_element_type=F32)
    m = jax.nn.sigmoid(ga_ref[...].astype(F32)) * pa + jax.nn.sigmoid(gb_ref[...].astype(F32)) * pb
    y = jnp.dot(m.astype(BF16), wo_ref[...], preferred_element_type=F32)
    r = DEEPNORM_ALPHA * x_ref[...] + gate_ref[...] * y
    mu = jnp.mean(r, axis=-1, keepdims=True)
    rc = r - mu
    var = jnp.mean(rc * rc, axis=-1, keepdims=True)
    o_ref[...] = rc * lax.rsqrt(var + LN_EPS) * lg_ref[...] + lb_ref[...]


def _merge_out(oa, ob, proj, x2, gate, wa, wb, wo, ln_g, ln_b, L):
    N, D = x2.shape
    W = oa.shape[1]
    B = gate.shape[0]
    tm = min(256, L)
    const = lambda i: (0, 0)
    return pl.pallas_call(
        _merge_kernel,
        grid=(N // tm,),
        in_specs=[pl.BlockSpec((tm, W), lambda i: (i, 0)),
                  pl.BlockSpec((tm, W), lambda i: (i, 0)),
                  pl.BlockSpec((tm, D), lambda i: (i, COL_GA // D)),
                  pl.BlockSpec((tm, D), lambda i: (i, COL_GB // D)),
                  pl.BlockSpec((tm, D), lambda i: (i, 0)),
                  pl.BlockSpec((None, 1, D), lambda i: ((i * tm) // L, 0, 0)),
                  pl.BlockSpec((W, D), const),
                  pl.BlockSpec((W, D), const),
                  pl.BlockSpec((D, D), const),
                  pl.BlockSpec((1, D), const),
                  pl.BlockSpec((1, D), const)],
        out_specs=pl.BlockSpec((tm, D), lambda i: (i, 0)),
        out_shape=jax.ShapeDtypeStruct((N, D), F32),
        compiler_params=_cparams(("parallel",)),
        name="merge_out",
    )(oa, ob, proj, proj, x2, gate.reshape(B, 1, D), wa, wb, wo, ln_g.reshape(1, D), ln_b.reshape(1, D))


def _layer(x, c, w_ada, b_ada, w_in, rel_bias, cmp_pos_k, cmp_pos_v, w_cmp_k1, w_cmp_k2, w_cmp_v1, w_cmp_v2,
           ssm_a_re, ssm_a_im, ssm_log_dt, ssm_b_re, ssm_b_im, ssm_c_re, ssm_c_im, ssm_d, w_glu, b_glu,
           w_branch_nsa, w_branch_ssm, w_out, ln_g, ln_b):
    B, L, D = x.shape
    N = B * L
    G, dh = NSA_GROUPS, NSA_HEAD_DIM
    x2 = x.reshape(N, D)

    mod = _ada_mod(c, w_ada, b_ada)
    shift, scale, gate = mod[:, :D], mod[:, D:2 * D], mod[:, 2 * D:]

    o_q, o_kv, o_ng, o_za, o_us, o_zb, o_ga, o_gb = np.cumsum(
        [0, NSA_WIDTH, 6 * KV_WIDTH, 3 * NSA_HEADS, NSA_WIDTH, SSM_WIDTH, SSM_WIDTH, D_MODEL]).tolist()
    w_re = jnp.concatenate([
        w_in[:, o_q:o_kv], w_in[:, o_za:o_us], w_in[:, o_us:o_zb], w_in[:, o_zb:o_ga], w_in[:, o_ga:o_gb],
        w_in[:, o_gb:], w_in[:, o_kv:o_ng], w_in[:, o_ng:o_za],
        jnp.zeros((D, PROJ_COLS - COL_NG - 3 * NSA_HEADS), w_in.dtype)], axis=1).astype(BF16)
    proj = _in_proj(x2, scale, shift, w_re, L)

    kv = proj[:, COL_KV:COL_KV + 6 * KV_WIDTH].reshape(B, L, 6, G, dh)
    ncp = L // CMP_STRIDE
    xc = kv[:, :, 0:2].reshape(B, ncp, CMP_STRIDE, 2, G, dh).transpose(3, 0, 4, 1, 2, 5)
    xc = xc.reshape(2, B, G, ncp, CMP_STRIDE * dh)
    w1s = jnp.stack([w_cmp_k1, w_cmp_v1])
    w2s = jnp.stack([w_cmp_k2, w_cmp_v2])
    poss = jnp.broadcast_to(jnp.stack([cmp_pos_k, cmp_pos_v]).reshape(2, 1, CMP_BLOCK * dh), (2, 8, CMP_BLOCK * dh))
    kcv = _nsa_compress(xc, w1s, w2s, poss)
    kc = kcv[0].astype(BF16)
    vct = kcv[1].swapaxes(-1, -2).astype(BF16)
    nsel = L // SEL_BLOCK
    e_abs = jnp.asarray(np.arange(L)[:, None] // SEL_BLOCK == np.arange(nsel)[None, :], BF16)
    ks = jnp.concatenate([kv[:, :, 2].transpose(0, 2, 1, 3), jnp.broadcast_to(e_abs, (B, G, L, nsel))], axis=-1)
    vrows = lambda n: [jnp.ones((B, G, 1, n), BF16), jnp.zeros((B, G, dh - 1, n), BF16)]
    vst = jnp.concatenate([kv[:, :, 3].transpose(0, 2, 3, 1)] + vrows(L), axis=2)
    padflag = jnp.asarray((np.arange(L + WINDOW) < WINDOW)[:, None] & (np.arange(dh) == 0)[None, :], BF16)
    kw = jnp.pad(kv[:, :, 4].transpose(0, 2, 1, 3), ((0, 0), (0, 0), (WINDOW, 0), (0, 0)))
    kw = jnp.concatenate([kw, jnp.broadcast_to(padflag, (B, G, L + WINDOW, dh))], axis=-1)
    vwt = jnp.pad(kv[:, :, 5].transpose(0, 2, 3, 1), ((0, 0), (0, 0), (0, 0), (WINDOW, 0)))
    vwt = jnp.concatenate([vwt] + vrows(L + WINDOW), axis=2)
    tc, tw, ts, ovt = _nsa_tables(rel_bias, L)
    o_a = _nsa_attend(proj, kc, vct, ks, vst, kw, vwt, tc, tw, ts, ovt, B, L)

    T, C, Gs = SSM_CHUNK, SSM_GROUP, SSM_GROUPS
    nk = L // T
    us = proj[:, COL_US:COL_US + SSM_WIDTH]
    ug = us.reshape(B * nk, T, Gs, C).transpose(2, 0, 1, 3).reshape(Gs, B * nk, T * C)
    mt, ws, wo, lamp = _s5_tables(ssm_a_re, ssm_a_im, ssm_log_dt, ssm_b_re, ssm_b_im, ssm_c_re, ssm_c_im, nk)
    yg = _s5_scan(ug, mt, ws, wo, lamp, nk)
    y = yg.reshape(Gs, B * nk, T, C).transpose(1, 2, 0, 3).reshape(N, SSM_WIDTH)
    o_b = _s5_glu(y, proj, ssm_d, w_glu.astype(BF16), b_glu)

    out = _merge_out(o_a, o_b, proj, x2, gate, w_branch_nsa.astype(BF16), w_branch_ssm.astype(BF16),
                     w_out.astype(BF16), ln_g, ln_b, L)
    return out.reshape(B, L, D)


def kernel(x, c, w_ada, b_ada, w_in, rel_bias, cmp_pos_k, cmp_pos_v, w_cmp_k1, w_cmp_k2, w_cmp_v1, w_cmp_v2,
           ssm_a_re, ssm_a_im, ssm_log_dt, ssm_b_re, ssm_b_im, ssm_c_re, ssm_c_im, ssm_d, w_glu, b_glu,
           w_branch_nsa, w_branch_ssm, w_out, ln_g, ln_b):
    for i in range(w_ada.shape[0]):
        x = _layer(x, c, w_ada[i], b_ada[i], w_in[i], rel_bias, cmp_pos_k[i], cmp_pos_v[i], w_cmp_k1[i],
                   w_cmp_k2[i], w_cmp_v1[i], w_cmp_v2[i], ssm_a_re[i], ssm_a_im[i], ssm_log_dt[i], ssm_b_re[i],
                   ssm_b_im[i], ssm_c_re[i], ssm_c_im[i], ssm_d[i], w_glu[i], b_glu[i], w_branch_nsa[i],
                   w_branch_ssm[i], w_out[i], ln_g[i], ln_b[i])
    return x
```

```python
import functools
import math

import numpy as np
import jax
import jax.numpy as jnp
from jax import lax
from jax.experimental import pallas as pl
from jax.experimental.pallas import tpu as pltpu

F32 = jnp.float32
BF16 = jnp.bfloat16
HIGHEST = lax.Precision.HIGHEST

D_MODEL = 2048
NSA_HEADS = 16
NSA_GROUPS = 4
NSA_HEAD_DIM = 64
NSA_REP = NSA_HEADS // NSA_GROUPS
NSA_WIDTH = NSA_HEADS * NSA_HEAD_DIM
KV_WIDTH = NSA_GROUPS * NSA_HEAD_DIM
CMP_BLOCK = 32
CMP_STRIDE = 16
CMP_HIDDEN = 128
SEL_BLOCK = 64
SEL_TOPK = 8
WINDOW = 512
Q_BLOCK = 128
SSM_WIDTH = 1024
SSM_GROUP = 16
SSM_GROUPS = SSM_WIDTH // SSM_GROUP
SSM_STATE = 64
REL_BUCKETS = 32
REL_MAX_DIST = 128
DEEPNORM_ALPHA = 2.0 ** 0.25
LN_EPS = 1e-5
MASK_VALUE = -1e30
FORCE_VALUE = 1e4
NEVER_VALUE = -3e38
SSM_CHUNK = 16

COL_Q, COL_ZA, COL_US, COL_ZB, COL_GA, COL_GB, COL_KV, COL_NG = 0, 1024, 2048, 3072, 4096, 6144, 8192, 9728
PROJ_COLS = 9856
VMEM_LIMIT = 56 * 1024 * 1024


def _cparams(sem):
    return pltpu.CompilerParams(dimension_semantics=sem, vmem_limit_bytes=VMEM_LIMIT)


def _ada_kernel(c_ref, w_ref, b_ref, o_ref):
    o_ref[...] = jnp.dot(c_ref[...], w_ref[...], preferred_element_type=F32, precision=HIGHEST) + b_ref[...]


def _ada_mod(c, w_ada, b_ada):
    B, D = c.shape
    n = w_ada.shape[1]
    tn = 1536
    return pl.pallas_call(
        _ada_kernel,
        grid=(n // tn,),
        in_specs=[pl.BlockSpec((B, D), lambda j: (0, 0)),
                  pl.BlockSpec((D, tn), lambda j: (0, j)),
                  pl.BlockSpec((1, tn), lambda j: (0, j))],
        out_specs=pl.BlockSpec((B, tn), lambda j: (0, j)),
        out_shape=jax.ShapeDtypeStruct((B, n), F32),
        compiler_params=_cparams(("parallel",)),
        name="ada_mod",
    )(c, w_ada, b_ada.reshape(1, n))


def _inproj_kernel(x_ref, scale_ref, shift_ref, w_ref, o_ref, h_ref):
    @pl.when(pl.program_id(1) == 0)
    def _():
        def rows(k, carry):
            r0 = pl.multiple_of(k * LN_ROWS, LN_ROWS)
            x = x_ref[pl.ds(r0, LN_ROWS), :]
            mu = jnp.mean(x, axis=-1, keepdims=True)
            xc = x - mu
            var = jnp.mean(xc * xc, axis=-1, keepdims=True)
            hn = xc * lax.rsqrt(var + LN_EPS)
            h_ref[pl.ds(r0, LN_ROWS), :] = (hn * (1.0 + scale_ref[...]) + shift_ref[...]).astype(BF16)
            return carry

        lax.fori_loop(0, x_ref.shape[0] // LN_ROWS, rows, 0)

    o_ref[...] = jnp.dot(h_ref[...], w_ref[...], preferred_element_type=F32).astype(o_ref.dtype)


LN_ROWS = 256


def _in_proj(x2, scale, shift, w, L):
    N, D = x2.shape
    ncol = w.shape[1]
    tm = min(1024, L)
    tn = 1408
    assert N % tm == 0 and L % tm == 0 and ncol % tn == 0
    B = scale.shape[0]
    return pl.pallas_call(
        _inproj_kernel,
        grid=(N // tm, ncol // tn),
        in_specs=[pl.BlockSpec((tm, D), lambda i, j: (i, 0)),
                  pl.BlockSpec((None, 1, D), lambda i, j: ((i * tm) // L, 0, 0)),
                  pl.BlockSpec((None, 1, D), lambda i, j: ((i * tm) // L, 0, 0)),
                  pl.BlockSpec((D, tn), lambda i, j: (0, j))],
        out_specs=pl.BlockSpec((tm, tn), lambda i, j: (i, j)),
        out_shape=jax.ShapeDtypeStruct((N, ncol), BF16),
        scratch_shapes=[pltpu.VMEM((tm, D), BF16)],
        compiler_params=_cparams(("parallel", "arbitrary")),
        name="in_proj",
    )(x2, scale.reshape(B, 1, D), shift.reshape(B, 1, D), w)


def _cmp_kernel(x_ref, w1_ref, w2_ref, pos_ref, o_ref):
    half = CMP_STRIDE * NSA_HEAD_DIM
    x = x_ref[...]
    w1 = w1_ref[...]
    a = jnp.dot(x, w1[:half].astype(BF16), preferred_element_type=F32)
    b = jnp.dot(x, w1[half:].astype(BF16), preferred_element_type=F32)
    pw = jnp.dot(pos_ref[...], w1, preferred_element_type=F32, precision=HIGHEST)[0:1]
    ncp = x.shape[0]
    h = a + pltpu.roll(b, ncp - 1, 0) + pw
    y = jnp.dot(jax.nn.gelu(h).astype(BF16), w2_ref[...].astype(BF16), preferred_element_type=F32)
    o_ref[...] = y


def _nsa_compress(xc, w1s, w2s, poss):
    _, B, G, ncp, width = xc.shape
    return pl.pallas_call(
        _cmp_kernel,
        grid=(2, B, G),
        in_specs=[pl.BlockSpec((None, None, None, ncp, width), lambda s, b, g: (s, b, g, 0, 0)),
                  pl.BlockSpec((None,) + w1s.shape[1:], lambda s, b, g: (s, 0, 0)),
                  pl.BlockSpec((None,) + w2s.shape[1:], lambda s, b, g: (s, 0, 0)),
                  pl.BlockSpec((None,) + poss.shape[1:], lambda s, b, g: (s, 0, 0))],
        out_specs=pl.BlockSpec((None, None, None, ncp, NSA_HEAD_DIM), lambda s, b, g: (s, b, g, 0, 0)),
        out_shape=jax.ShapeDtypeStruct((2, B, G, ncp, NSA_HEAD_DIM), F32),
        compiler_params=_cparams(("parallel", "parallel", "parallel")),
        name="nsa_compress",
    )(xc, w1s, w2s, poss)


SEL_TILE = 512
SEL_TABLE_FAR = 640


def _tree(op, parts):
    while len(parts) > 1:
        parts = [op(parts[i], parts[i + 1]) if i + 1 < len(parts) else parts[i] for i in range(0, len(parts), 2)]
    return parts[0]


def _fold8(op, x):
    return _tree(op, [x[k:k + 8] for k in range(0, x.shape[0], 8)])


def _safe_inv(l):
    return jnp.where(l > 0.0, 1.0 / jnp.where(l > 0.0, l, 1.0), 0.0)


def _nsa_kernel(q_ref, za_ref, ng_ref, kc_ref, vct_ref, ks_ref, vst_ref, kw_ref, vwt_ref,
                tc_ref, tw_ref, ts_ref, ovt_ref, o_ref, s_ref, acc_ref, gt_ref, *, L):
    R, dh, QB = NSA_REP, NSA_HEAD_DIM, Q_BLOCK
    ncp = L // CMP_STRIDE
    nsel = L // SEL_BLOCK
    W = R * QB
    g = pl.program_id(1)
    qb = pl.program_id(2)

    qt = (q_ref[...].astype(F32) * (dh ** -0.5)).T
    qT = jnp.concatenate([qt[r * dh:(r + 1) * dh] for r in range(R)], axis=1).astype(BF16)

    c_off = pl.multiple_of((ncp - 8) - 8 * qb, 8)
    sc = jnp.dot(kc_ref[...], qT, preferred_element_type=F32) + tc_ref[pl.ds(c_off, ncp), :]
    m = jnp.maximum(jnp.max(_fold8(jnp.maximum, sc), axis=0, keepdims=True), 0.1 * MASK_VALUE)
    e = jnp.exp(sc - m)
    l = jnp.sum(_fold8(jnp.add, e), axis=0, keepdims=True)
    p_c = e * _safe_inv(l)
    oc = jnp.dot(vct_ref[...], p_c.astype(BF16), preferred_element_type=F32)

    psum = _tree(jnp.add, [p_c[:, r * QB:(r + 1) * QB] for r in range(R)])
    p_hi = psum.astype(BF16)
    p_lo = (psum - p_hi.astype(F32)).astype(BF16)
    ovt = ovt_ref[...]
    imp = jnp.dot(ovt, p_hi, preferred_element_type=F32) + jnp.dot(ovt, p_lo, preferred_element_type=F32)
    jj = lax.broadcasted_iota(jnp.int32, (nsel, QB), 0)
    ii = lax.broadcasted_iota(jnp.int32, (nsel, QB), 1)
    cur = 2 * qb + (ii >= SEL_BLOCK).astype(jnp.int32)
    forced = (jj == 0) | (jj == cur) | (jj == cur - 1)
    imp = jnp.where(forced, FORCE_VALUE, imp)
    imp = jnp.where(jj <= cur, imp, MASK_VALUE)
    jf = jj.astype(F32)
    sel = jnp.zeros((nsel, QB), F32)
    for _ in range(min(SEL_TOPK, nsel)):
        mx = jnp.max(_fold8(jnp.maximum, imp), axis=0, keepdims=True)
        idx = jnp.min(_fold8(jnp.minimum, jnp.where(imp == mx, jf, 1e9)), axis=0, keepdims=True)
        hit = jf == idx
        sel = jnp.where(hit & (mx > 0.1 * MASK_VALUE), 1.0, sel)
        imp = jnp.where(hit, -jnp.inf, imp)
    selbias = jnp.where(sel > 0.5, 0.0, MASK_VALUE).astype(BF16)
    q_sel = jnp.concatenate([qT, jnp.concatenate([selbias] * R, axis=1)], axis=0)

    w0 = pl.multiple_of(qb * QB, QB)
    flag = jnp.where(lax.broadcasted_iota(jnp.int32, (dh, W), 0) == 0, MASK_VALUE, 0.0).astype(BF16)
    q_win = jnp.concatenate([qT, flag], axis=0)
    sw = jnp.dot(kw_ref[pl.ds(w0, WINDOW + QB), :], q_win, preferred_element_type=F32) + tw_ref[...]
    m_w = jnp.max(_fold8(jnp.maximum, sw), axis=0, keepdims=True)
    pw = jnp.exp(sw - m_w).astype(BF16)
    accw = jnp.dot(vwt_ref[:, pl.ds(w0, WINDOW + QB)], pw, preferred_element_type=F32)
    owin = accw[:dh] * _safe_inv(accw[dh:dh + 1])

    gt_ref[...] = jax.nn.sigmoid(ng_ref[...].astype(F32)).T

    def gate_row(branch):
        return jnp.concatenate([gt_ref[pl.ds(3 * (g * R + r) + branch, 1), :] for r in range(R)], axis=1)

    o_cw = oc * gate_row(0) + owin * gate_row(2)
    gate_s = gate_row(1)

    npair = (qb // (SEL_TILE // QB) + 2) // 2

    def pass1(u, macc):
        for h in range(2):
            t = 2 * u + h
            k0 = pl.multiple_of(t * SEL_TILE, SEL_TILE)
            x0 = pl.multiple_of(jnp.maximum(t * SEL_TILE - qb * QB + SEL_TABLE_FAR, 0), QB)
            s = (jnp.dot(ks_ref[pl.ds(k0, SEL_TILE), :], q_sel, preferred_element_type=F32)
                 + ts_ref[pl.ds(x0, SEL_TILE), :])
            s_ref[pl.ds(k0, SEL_TILE), :] = s
            macc = jnp.maximum(macc, _fold8(jnp.maximum, s))
        return macc

    macc = lax.fori_loop(0, npair, pass1, jnp.full((8, W), NEVER_VALUE, F32))
    m_s = jnp.max(macc, axis=0, keepdims=True)

    acc_ref[...] = jnp.zeros_like(acc_ref)

    def pass2(u, carry):
        part = []
        for h in range(2):
            k0 = pl.multiple_of((2 * u + h) * SEL_TILE, SEL_TILE)
            p = jnp.exp(s_ref[pl.ds(k0, SEL_TILE), :] - m_s).astype(BF16)
            part.append(jnp.dot(vst_ref[:, pl.ds(k0, SEL_TILE)], p, preferred_element_type=F32))
        acc_ref[...] += part[0] + part[1]
        return carry

    lax.fori_loop(0, npair, pass2, 0)
    acc = acc_ref[...]
    osel = acc[:dh] * _safe_inv(acc[dh:dh + 1])

    ot = o_cw + osel * gate_s
    o = jnp.concatenate([ot[:, r * QB:(r + 1) * QB] for r in range(R)], axis=0).T
    za = za_ref[...].astype(F32)
    o_ref[...] = (o * (za * jax.nn.sigmoid(za))).astype(o_ref.dtype)


def _nsa_attend(proj, kc, vct, ks, vst, kw, vwt, tc, tw, ts, ovt, B, L):
    R, dh, G, QB = NSA_REP, NSA_HEAD_DIM, NSA_GROUPS, Q_BLOCK
    assert L % (2 * SEL_TILE) == 0
    nqb = L // QB
    gw = R * dh
    W = R * QB
    row = lambda b, g, i: b * nqb + i
    bg = lambda b, g, i: (b, g, 0, 0)
    grp = lambda b, g, i: (g, 0, 0)
    return pl.pallas_call(
        functools.partial(_nsa_kernel, L=L),
        grid=(B, G, nqb),
        in_specs=[
            pl.BlockSpec((QB, gw), lambda b, g, i: (row(b, g, i), COL_Q // gw + g)),
            pl.BlockSpec((QB, gw), lambda b, g, i: (row(b, g, i), COL_ZA // gw + g)),
            pl.BlockSpec((QB, 128), lambda b, g, i: (row(b, g, i), COL_NG // 128)),
            pl.BlockSpec((None, None) + kc.shape[2:], bg),
            pl.BlockSpec((None, None) + vct.shape[2:], bg),
            pl.BlockSpec((None, None) + ks.shape[2:], bg),
            pl.BlockSpec((None, None) + vst.shape[2:], bg),
            pl.BlockSpec((None, None) + kw.shape[2:], bg),
            pl.BlockSpec((None, None) + vwt.shape[2:], bg),
            pl.BlockSpec((None,) + tc.shape[1:], grp),
            pl.BlockSpec((None,) + tw.shape[1:], grp),
            pl.BlockSpec((None,) + ts.shape[1:], grp),
            pl.BlockSpec(ovt.shape, lambda b, g, i: (0, 0)),
        ],
        out_specs=pl.BlockSpec((QB, gw), lambda b, g, i: (row(b, g, i), g)),
        out_shape=jax.ShapeDtypeStruct((B * L, NSA_WIDTH), BF16),
        scratch_shapes=[pltpu.VMEM((L, W), F32), pltpu.VMEM((2 * dh, W), F32), pltpu.VMEM((128, QB), F32)],
        compiler_params=_cparams(("parallel", "parallel", "arbitrary")),
        name="nsa_attend",
    )(proj, proj, proj, kc, vct, ks, vst, kw, vwt, tc, tw, ts, ovt)


def _t5_bucket(dist):
    n = jnp.maximum(dist, 0)
    max_exact = REL_BUCKETS // 2
    nf = jnp.maximum(n, max_exact).astype(F32)
    large = max_exact + (jnp.log(nf / max_exact) / math.log(REL_MAX_DIST / max_exact)
                         * (REL_BUCKETS - max_exact)).astype(jnp.int32)
    large = jnp.minimum(large, REL_BUCKETS - 1)
    return jnp.where(n < max_exact, n, large)


def _toeplitz(f, d0, base, step, n_rows, width):
    rpb = width // step
    nblk = n_rows // rpb
    assert rpb * step == width and nblk * rpb == n_rows
    lo = base - width * nblk - d0
    seg = f[..., lo:lo + width * (nblk + 1)].reshape(f.shape[:-1] + (nblk + 1, width))[..., ::-1, :]
    win = jnp.concatenate([seg[..., :-1, :], seg[..., 1:, :]], axis=-1)
    t = jnp.tile(win, rpb)[..., :rpb * (2 * width - step)].reshape(win.shape[:-1] + (rpb, 2 * width - step))
    return t[..., :width].reshape(f.shape[:-1] + (n_rows, width))


def _nsa_tables(rel_bias, L):
    QB, G, R = Q_BLOCK, NSA_GROUPS, NSA_REP
    ncp = L // CMP_STRIDE
    nsel = L // SEL_BLOCK
    npos = L + QB
    tbl = rel_bias.astype(F32)
    bpos = (tbl[_t5_bucket(jnp.arange(npos, dtype=jnp.int32))] - tbl[REL_BUCKETS - 1][None, :]).T
    f = jnp.concatenate([jnp.full((NSA_HEADS, npos), MASK_VALUE, F32), bpos], axis=-1)
    d = np.arange(-npos, npos)
    fw = jnp.where(jnp.asarray(d < WINDOW), f, MASK_VALUE)
    tc = _toeplitz(f, -npos, L - QB - CMP_BLOCK + 1, CMP_STRIDE, 2 * ncp - 8, QB)
    tw = _toeplitz(fw, -npos, WINDOW, 1, WINDOW + QB, QB)
    ts = _toeplitz(f, -npos, SEL_TABLE_FAR, 1, SEL_TABLE_FAR + 2 * SEL_TILE, QB)
    lanes = lambda t: t.reshape(G, R, t.shape[1], QB).transpose(0, 2, 1, 3).reshape(G, t.shape[1], R * QB)
    c = np.arange(ncp)[None, :]
    j = np.arange(nsel)[:, None]
    ovt = ((c - 4 * j >= -1) & (c - 4 * j <= 3)).astype(np.float32)
    return lanes(tc), lanes(tw), lanes(ts), jnp.asarray(ovt, BF16)


S5_LANES = 128
S5_GT = S5_LANES // SSM_GROUP
S5_SW = S5_GT * SSM_STATE
S5_COLS = 4 * S5_LANES


def _s5_kernel(us_ref, kt_ref, zt_ref, cl_ref, lam_ref, ex_ref, y_ref,
               xs_ref, u8_ref, m8_ref, ws8_ref, wo8_ref, *, nk):
    T, C, P, GT, LT, SW = SSM_CHUNK, SSM_GROUP, SSM_STATE, S5_GT, S5_LANES, S5_SW

    @pl.when(pl.program_id(1) == 0)
    def _build():
        tile = lambda x: jnp.concatenate([x] * GT, axis=0)
        r = lax.broadcasted_iota(jnp.int32, (LT, LT), 0)
        c = lax.broadcasted_iota(jnp.int32, (LT, LT), 1)
        same = (r // C) == (c // C)
        m8_ref[...] = jnp.zeros_like(m8_ref)
        for j in range(T):
            bd = jnp.where(same, tile(kt_ref[j]), 0.0).astype(BF16)
            for b in range(T - j):
                m8_ref[b * LT:(b + 1) * LT, (b + j) * LT:(b + j + 1) * LT] = bd
        r = lax.broadcasted_iota(jnp.int32, (LT, 2 * SW), 0)
        c = lax.broadcasted_iota(jnp.int32, (LT, 2 * SW), 1)
        same = (r // C) == ((c % SW) // P)
        for b in range(T):
            ws8_ref[b * LT:(b + 1) * LT, :] = jnp.where(same, tile(zt_ref[b]), 0.0).astype(BF16)
        r = lax.broadcasted_iota(jnp.int32, (2 * SW, LT), 0)
        c = lax.broadcasted_iota(jnp.int32, (2 * SW, LT), 1)
        same = ((r % SW) // P) == (c // C)
        cl = cl_ref[...]
        for a in range(T):
            blk = jnp.dot(cl, ex_ref[a], preferred_element_type=F32)
            wo8_ref[:, a * LT:(a + 1) * LT] = jnp.where(same, blk, 0.0).astype(BF16)

    xs_ref[...] = us_ref[...].astype(F32)
    for b in range(T):
        u8_ref[:, b * LT:(b + 1) * LT] = xs_ref[pl.ds(b, nk, stride=T), :].astype(BF16)
    u8 = u8_ref[...]
    s = jnp.dot(u8, ws8_ref[...], preferred_element_type=F32)
    hr, hi = s[:, :SW], s[:, SW:]
    kidx = lax.broadcasted_iota(jnp.int32, (nk, SW), 0)
    d, step = 1, 0
    while d < nk:
        lr = lam_ref[step:step + 1, :SW]
        li = lam_ref[step:step + 1, SW:]
        keep = kidx >= d
        sr = jnp.where(keep, pltpu.roll(hr, d, 0), 0.0)
        si = jnp.where(keep, pltpu.roll(hi, d, 0), 0.0)
        hr, hi = hr + lr * sr - li * si, hi + lr * si + li * sr
        d, step = 2 * d, step + 1
    keep = kidx >= 1
    pr = jnp.where(keep, pltpu.roll(hr, 1, 0), 0.0)
    pi = jnp.where(keep, pltpu.roll(hi, 1, 0), 0.0)
    hcat = jnp.concatenate([pr, pi], axis=-1).astype(BF16)
    for q in range(T * LT // S5_COLS):
        kq = (q + 1) * S5_COLS
        yq = (jnp.dot(hcat, wo8_ref[:, q * S5_COLS:kq], preferred_element_type=F32)
              + jnp.dot(u8[:, :kq], m8_ref[0:kq, q * S5_COLS:kq], preferred_element_type=F32))
        for a4 in range(S5_COLS // LT):
            a = q * (S5_COLS // LT) + a4
            y_ref[pl.ds(a, nk, stride=T), :] = yq[:, a4 * LT:(a4 + 1) * LT]


def _s5_scan(proj, kt, zt, clc, lamp, ex, B, L):
    T, LT, SW = SSM_CHUNK, S5_LANES, S5_SW
    nk = L // T
    nt = SSM_WIDTH // LT
    per_tile = lambda a: pl.BlockSpec((None,) + a.shape[1:], lambda t, b: (t,) + (0,) * (a.ndim - 1))
    return pl.pallas_call(
        functools.partial(_s5_kernel, nk=nk),
        grid=(nt, B),
        in_specs=[pl.BlockSpec((L, LT), lambda t, b: (b, COL_US // LT + t)),
                  per_tile(kt), per_tile(zt), per_tile(clc), per_tile(lamp),
                  pl.BlockSpec(ex.shape, lambda t, b: (0, 0, 0))],
        out_specs=pl.BlockSpec((L, LT), lambda t, b: (b, t)),
        out_shape=jax.ShapeDtypeStruct((B * L, SSM_WIDTH), F32),
        scratch_shapes=[pltpu.VMEM((L, LT), F32), pltpu.VMEM((nk, T * LT), BF16),
                        pltpu.VMEM((T * LT, T * LT), BF16), pltpu.VMEM((T * LT, 2 * SW), BF16),
                        pltpu.VMEM((2 * SW, T * LT), BF16)],
        compiler_params=_cparams(("parallel", "arbitrary")),
        name="s5_scan",
    )(proj, kt, zt, clc, lamp, ex)


def _cmul(ar, ai, br, bi):
    return ar * br - ai * bi, ar * bi + ai * br


def _s5_tables(a_re, a_im, log_dt, b_re, b_im, c_re, c_im, nk):
    T, P, C, Gs = SSM_CHUNK, SSM_STATE, SSM_GROUP, SSM_GROUPS
    dt = jnp.exp(log_dt.astype(F32))[:, None]
    ar, ai = a_re.astype(F32), a_im.astype(F32)
    mag = jnp.exp(ar * dt)
    lr, li = mag * jnp.cos(ai * dt), mag * jnp.sin(ai * dt)
    den = ar * ar + ai * ai
    nr, ni = lr - 1.0, li
    fr, fi = (nr * ar + ni * ai) / den, (ni * ar - nr * ai) / den
    br, bim = b_re.astype(F32), b_im.astype(F32)
    bbr = fr[..., None] * br - fi[..., None] * bim
    bbi = fr[..., None] * bim + fi[..., None] * br
    jv = jnp.arange(T + 1, dtype=F32)[:, None, None]
    pmag = jnp.exp(jv * (ar * dt)[None])
    pwr, pwi = pmag * jnp.cos(jv * (ai * dt)[None]), pmag * jnp.sin(jv * (ai * dt)[None])
    zr = pwr[..., None] * bbr[None] - pwi[..., None] * bbi[None]
    zi = pwr[..., None] * bbi[None] + pwi[..., None] * bbr[None]
    cr, ci = c_re.astype(F32), c_im.astype(F32)
    kj = (jnp.einsum('gcp,jgpd->gjcd', cr, zr[:T], precision=HIGHEST)
          - jnp.einsum('gcp,jgpd->gjcd', ci, zi[:T], precision=HIGHEST))
    GT, NT = S5_GT, Gs // S5_GT
    kt = kj.reshape(NT, GT, T, C, C).transpose(0, 2, 4, 1, 3).reshape(NT, T, C, GT * C)
    lay_z = lambda z: z[:T][::-1].reshape(T, NT, GT, P, C).transpose(1, 0, 4, 2, 3).reshape(NT, T, C, GT * P)
    zt = jnp.concatenate([lay_z(zr), lay_z(zi)], axis=-1)
    pa_r, pa_i = pwr[1:].transpose(1, 0, 2)[:, :, None, :], pwi[1:].transpose(1, 0, 2)[:, :, None, :]
    clr = cr[:, None] * pa_r - ci[:, None] * pa_i
    cli = cr[:, None] * pa_i + ci[:, None] * pa_r
    lay_c = lambda x: x.reshape(NT, GT, T, C, P).transpose(0, 1, 4, 2, 3).reshape(NT, GT * P, T * C)
    clc = jnp.concatenate([lay_c(clr), lay_c(-cli)], axis=1).astype(BF16)
    qr, qi = pwr[T], pwi[T]
    steps = []
    d = 1
    while d < nk:
        steps.append(jnp.concatenate([qr.reshape(NT, GT * P), qi.reshape(NT, GT * P)], -1))
        qr, qi = _cmul(qr, qi, qr, qi)
        d *= 2
    lamp = jnp.stack(steps, 1)
    col = np.arange(T * C)[:, None]
    lane = np.arange(GT * C)[None, :]
    ex = np.stack([(col // C == a) & (col % C == lane % C) for a in range(T)]).astype(np.float32)
    return kt, zt, clc, lamp, jnp.asarray(ex, BF16)


def _glu_kernel(y_ref, u_ref, zb_ref, d_ref, w_ref, b_ref, o_ref):
    y = y_ref[...] + d_ref[...] * u_ref[...].astype(F32)
    yg = jax.nn.gelu(y).astype(BF16)
    z = jnp.dot(yg, w_ref[...], preferred_element_type=F32) + b_ref[...]
    zb = zb_ref[...].astype(F32)
    o_ref[...] = (yg.astype(F32) * jax.nn.sigmoid(z) * (zb * jax.nn.sigmoid(zb))).astype(o_ref.dtype)


def _s5_glu(y, proj, d_skip, w_glu, b_glu):
    N, W = y.shape
    tm = min(1024, N)
    return pl.pallas_call(
        _glu_kernel,
        grid=(N // tm,),
        in_specs=[pl.BlockSpec((tm, W), lambda i: (i, 0)),
                  pl.BlockSpec((tm, W), lambda i: (i, COL_US // W)),
                  pl.BlockSpec((tm, W), lambda i: (i, COL_ZB // W)),
                  pl.BlockSpec((1, W), lambda i: (0, 0)),
                  pl.BlockSpec((W, W), lambda i: (0, 0)),
                  pl.BlockSpec((1, W), lambda i: (0, 0))],
        out_specs=pl.BlockSpec((tm, W), lambda i: (i, 0)),
        out_shape=jax.ShapeDtypeStruct((N, W), BF16),
        compiler_params=_cparams(("parallel",)),
        name="s5_glu",
    )(y, proj, proj, d_skip.reshape(1, W), w_glu, b_glu.reshape(1, W))


def _merge_kernel(oa_ref, ob_ref, ga_ref, gb_ref, x_ref, gate_ref, wa_ref, wb_ref, wo_ref, lg_ref, lb_ref, o_ref):
    pa = jnp.dot(oa_ref[...], wa_ref[...], preferred_element_type=F32)
    pb = jnp.dot(ob_ref[...], wb_ref[...], preferred_element_type=F32)
    m = jax.nn.sigmoid(ga_ref[...].astype(F32)) * pa + jax.nn.sigmoid(gb_ref[...].astype(F32)) * pb
    y = jnp.dot(m.astype(BF16), wo_ref[...], preferred_element_type=F32)
    r = DEEPNORM_ALPHA * x_ref[...] + gate_ref[...] * y
    mu = jnp.mean(r, axis=-1, keepdims=True)
    rc = r - mu
    var = jnp.mean(rc * rc, axis=-1, keepdims=True)
    o_ref[...] = rc * lax.rsqrt(var + LN_EPS) * lg_ref[...] + lb_ref[...]


def _merge_out(oa, ob, proj, x2, gate, wa, wb, wo, ln_g, ln_b, L):
    N, D = x2.shape
    W = oa.shape[1]
    B = gate.shape[0]
    tm = min(256, L)
    const = lambda i: (0, 0)
    return pl.pallas_call(
        _merge_kernel,
        grid=(N // tm,),
        in_specs=[pl.BlockSpec((tm, W), lambda i: (i, 0)),
                  pl.BlockSpec((tm, W), lambda i: (i, 0)),
                  pl.BlockSpec((tm, D), lambda i: (i, COL_GA // D)),
                  pl.BlockSpec((tm, D), lambda i: (i, COL_GB // D)),
                  pl.BlockSpec((tm, D), lambda i: (i, 0)),
                  pl.BlockSpec((None, 1, D), lambda i: ((i * tm) // L, 0, 0)),
                  pl.BlockSpec((W, D), const),
                  pl.BlockSpec((W, D), const),
                  pl.BlockSpec((D, D), const),
                  pl.BlockSpec((1, D), const),
                  pl.BlockSpec((1, D), const)],
        out_specs=pl.BlockSpec((tm, D), lambda i: (i, 0)),
        out_shape=jax.ShapeDtypeStruct((N, D), F32),
        compiler_params=_cparams(("parallel",)),
        name="merge_out",
    )(oa, ob, proj, proj, x2, gate.reshape(B, 1, D), wa, wb, wo, ln_g.reshape(1, D), ln_b.reshape(1, D))


def _layer(x, c, w_ada, b_ada, w_in, rel_bias, cmp_pos_k, cmp_pos_v, w_cmp_k1, w_cmp_k2, w_cmp_v1, w_cmp_v2,
           ssm_a_re, ssm_a_im, ssm_log_dt, ssm_b_re, ssm_b_im, ssm_c_re, ssm_c_im, ssm_d, w_glu, b_glu,
           w_branch_nsa, w_branch_ssm, w_out, ln_g, ln_b):
    B, L, D = x.shape
    N = B * L
    G, dh = NSA_GROUPS, NSA_HEAD_DIM
    x2 = x.reshape(N, D)

    mod = _ada_mod(c, w_ada, b_ada)
    shift, scale, gate = mod[:, :D], mod[:, D:2 * D], mod[:, 2 * D:]

    o_q, o_kv, o_ng, o_za, o_us, o_zb, o_ga, o_gb = np.cumsum(
        [0, NSA_WIDTH, 6 * KV_WIDTH, 3 * NSA_HEADS, NSA_WIDTH, SSM_WIDTH, SSM_WIDTH, D_MODEL]).tolist()
    w_re = jnp.concatenate([
        w_in[:, o_q:o_kv], w_in[:, o_za:o_us], w_in[:, o_us:o_zb], w_in[:, o_zb:o_ga], w_in[:, o_ga:o_gb],
        w_in[:, o_gb:], w_in[:, o_kv:o_ng], w_in[:, o_ng:o_za],
        jnp.zeros((D, PROJ_COLS - COL_NG - 3 * NSA_HEADS), w_in.dtype)], axis=1).astype(BF16)
    proj = _in_proj(x2, scale, shift, w_re, L)

    kv = proj[:, COL_KV:COL_KV + 6 * KV_WIDTH].reshape(B, L, 6, G, dh)
    ncp = L // CMP_STRIDE
    xc = kv[:, :, 0:2].reshape(B, ncp, CMP_STRIDE, 2, G, dh).transpose(3, 0, 4, 1, 2, 5)
    xc = xc.reshape(2, B, G, ncp, CMP_STRIDE * dh)
    w1s = jnp.stack([w_cmp_k1, w_cmp_v1])
    w2s = jnp.stack([w_cmp_k2, w_cmp_v2])
    poss = jnp.broadcast_to(jnp.stack([cmp_pos_k, cmp_pos_v]).reshape(2, 1, CMP_BLOCK * dh), (2, 8, CMP_BLOCK * dh))
    kcv = _nsa_compress(xc, w1s, w2s, poss)
    kc = kcv[0].astype(BF16)
    vct = kcv[1].swapaxes(-1, -2).astype(BF16)
    nsel = L // SEL_BLOCK
    e_abs = jnp.asarray(np.arange(L)[:, None] // SEL_BLOCK == np.arange(nsel)[None, :], BF16)
    ks = jnp.concatenate([kv[:, :, 2].transpose(0, 2, 1, 3), jnp.broadcast_to(e_abs, (B, G, L, nsel))], axis=-1)
    vrows = lambda n: [jnp.ones((B, G, 1, n), BF16), jnp.zeros((B, G, dh - 1, n), BF16)]
    vst = jnp.concatenate([kv[:, :, 3].transpose(0, 2, 3, 1)] + vrows(L), axis=2)
    padflag = jnp.asarray((np.arange(L + WINDOW) < WINDOW)[:, None] & (np.arange(dh) == 0)[None, :], BF16)
    kw = jnp.pad(kv[:, :, 4].transpose(0, 2, 1, 3), ((0, 0), (0, 0), (WINDOW, 0), (0, 0)))
    kw = jnp.concatenate([kw, jnp.broadcast_to(padflag, (B, G, L + WINDOW, dh))], axis=-1)
    vwt = jnp.pad(kv[:, :, 5].transpose(0, 2, 3, 1), ((0, 0), (0, 0), (0, 0), (WINDOW, 0)))
    vwt = jnp.concatenate([vwt] + vrows(L + WINDOW), axis=2)
    tc, tw, ts, ovt = _nsa_tables(rel_bias, L)
    o_a = _nsa_attend(proj, kc, vct, ks, vst, kw, vwt, tc, tw, ts, ovt, B, L)

    s5_tabs = _s5_tables(ssm_a_re, ssm_a_im, ssm_log_dt, ssm_b_re, ssm_b_im, ssm_c_re, ssm_c_im, L // SSM_CHUNK)
    y = _s5_scan(proj, *s5_tabs, B, L)
    o_b = _s5_glu(y, proj, ssm_d, w_glu.astype(BF16), b_glu)

    out = _merge_out(o_a, o_b, proj, x2, gate, w_branch_nsa.astype(BF16), w_branch_ssm.astype(BF16),
                     w_out.astype(BF16), ln_g, ln_b, L)
    return out.reshape(B, L, D)


def kernel(x, c, w_ada, b_ada, w_in, rel_bias, cmp_pos_k, cmp_pos_v, w_cmp_k1, w_cmp_k2, w_cmp_v1, w_cmp_v2,
           ssm_a_re, ssm_a_im, ssm_log_dt, ssm_b_re, ssm_b_im, ssm_c_re, ssm_c_im, ssm_d, w_glu, b_glu,
           w_branch_nsa, w_branch_ssm, w_out, ln_g, ln_b):
    for i in range(w_ada.shape[0]):
        x = _layer(x, c, w_ada[i], b_ada[i], w_in[i], rel_bias, cmp_pos_k[i], cmp_pos_v[i], w_cmp_k1[i],
                   w_cmp_k2[i], w_cmp_v1[i], w_cmp_v2[i], ssm_a_re[i], ssm_a_im[i], ssm_log_dt[i], ssm_b_re[i],
                   ssm_b_im[i], ssm_c_re[i], ssm_c_im[i], ssm_d[i], w_glu[i], b_glu[i], w_branch_nsa[i],
                   w_branch_ssm[i], w_out[i], ln_g[i], ln_b[i])
    return x
```

```python
import functools
import math

import numpy as np
import jax
import jax.numpy as jnp
from jax import lax
from jax.experimental import pallas as pl
from jax.experimental.pallas import tpu as pltpu

F32 = jnp.float32
BF16 = jnp.bfloat16
HIGHEST = lax.Precision.HIGHEST

D_MODEL = 2048
NSA_HEADS = 16
NSA_GROUPS = 4
NSA_HEAD_DIM = 64
NSA_REP = NSA_HEADS // NSA_GROUPS
NSA_WIDTH = NSA_HEADS * NSA_HEAD_DIM
KV_WIDTH = NSA_GROUPS * NSA_HEAD_DIM
CMP_BLOCK = 32
CMP_STRIDE = 16
CMP_HIDDEN = 128
SEL_BLOCK = 64
SEL_TOPK = 8
WINDOW = 512
Q_BLOCK = 128
SSM_WIDTH = 1024
SSM_GROUP = 16
SSM_GROUPS = SSM_WIDTH // SSM_GROUP
SSM_STATE = 64
REL_BUCKETS = 32
REL_MAX_DIST = 128
DEEPNORM_ALPHA = 2.0 ** 0.25
LN_EPS = 1e-5
MASK_VALUE = -1e30
FORCE_VALUE = 1e4
NEVER_VALUE = -3e38
SSM_CHUNK = 16

COL_Q, COL_ZA, COL_US, COL_ZB, COL_GA, COL_GB, COL_KV, COL_NG = 0, 1024, 2048, 3072, 4096, 6144, 8192, 9728
PROJ_COLS = 9856
VMEM_LIMIT = 56 * 1024 * 1024


def _cparams(sem):
    return pltpu.CompilerParams(dimension_semantics=sem, vmem_limit_bytes=VMEM_LIMIT)


def _ada_kernel(c_ref, w_ref, b_ref, o_ref):
    o_ref[...] = jnp.dot(c_ref[...], w_ref[...], preferred_element_type=F32, precision=HIGHEST) + b_ref[...]


def _ada_mod(c, w_ada, b_ada):
    B, D = c.shape
    n = w_ada.shape[1]
    tn = 1536
    return pl.pallas_call(
        _ada_kernel,
        grid=(n // tn,),
        in_specs=[pl.BlockSpec((B, D), lambda j: (0, 0)),
                  pl.BlockSpec((D, tn), lambda j: (0, j)),
                  pl.BlockSpec((1, tn), lambda j: (0, j))],
        out_specs=pl.BlockSpec((B, tn), lambda j: (0, j)),
        out_shape=jax.ShapeDtypeStruct((B, n), F32),
        compiler_params=_cparams(("parallel",)),
        name="ada_mod",
    )(c, w_ada, b_ada.reshape(1, n))


def _inproj_kernel(x_ref, scale_ref, shift_ref, w_ref, o_ref, h_ref):
    @pl.when(pl.program_id(1) == 0)
    def _():
        def rows(k, carry):
            r0 = pl.multiple_of(k * LN_ROWS, LN_ROWS)
            x = x_ref[pl.ds(r0, LN_ROWS), :]
            mu = jnp.mean(x, axis=-1, keepdims=True)
            xc = x - mu
            var = jnp.mean(xc * xc, axis=-1, keepdims=True)
            hn = xc * lax.rsqrt(var + LN_EPS)
            h_ref[pl.ds(r0, LN_ROWS), :] = (hn * (1.0 + scale_ref[...]) + shift_ref[...]).astype(BF16)
            return carry

        lax.fori_loop(0, x_ref.shape[0] // LN_ROWS, rows, 0)

    o_ref[...] = jnp.dot(h_ref[...], w_ref[...], preferred_element_type=F32).astype(o_ref.dtype)


LN_ROWS = 256


def _in_proj(x2, scale, shift, w, L):
    N, D = x2.shape
    ncol = w.shape[1]
    tm = min(1024, L)
    tn = 1408
    assert N % tm == 0 and L % tm == 0 and ncol % tn == 0
    B = scale.shape[0]
    return pl.pallas_call(
        _inproj_kernel,
        grid=(N // tm, ncol // tn),
        in_specs=[pl.BlockSpec((tm, D), lambda i, j: (i, 0)),
                  pl.BlockSpec((None, 1, D), lambda i, j: ((i * tm) // L, 0, 0)),
                  pl.BlockSpec((None, 1, D), lambda i, j: ((i * tm) // L, 0, 0)),
                  pl.BlockSpec((D, tn), lambda i, j: (0, j))],
        out_specs=pl.BlockSpec((tm, tn), lambda i, j: (i, j)),
        out_shape=jax.ShapeDtypeStruct((N, ncol), BF16),
        scratch_shapes=[pltpu.VMEM((tm, D), BF16)],
        compiler_params=_cparams(("parallel", "arbitrary")),
        name="in_proj",
    )(x2, scale.reshape(B, 1, D), shift.reshape(B, 1, D), w)


def _cmp_kernel(x_ref, w1_ref, w2_ref, pos_ref, o_ref):
    half = CMP_STRIDE * NSA_HEAD_DIM
    x = x_ref[...]
    w1 = w1_ref[...]
    a = jnp.dot(x, w1[:half].astype(BF16), preferred_element_type=F32)
    b = jnp.dot(x, w1[half:].astype(BF16), preferred_element_type=F32)
    pw = jnp.dot(pos_ref[...], w1, preferred_element_type=F32, precision=HIGHEST)[0:1]
    ncp = x.shape[0]
    h = a + pltpu.roll(b, ncp - 1, 0) + pw
    y = jnp.dot(jax.nn.gelu(h).astype(BF16), w2_ref[...].astype(BF16), preferred_element_type=F32)
    o_ref[...] = y


def _nsa_compress(xc, w1s, w2s, poss):
    _, B, G, ncp, width = xc.shape
    return pl.pallas_call(
        _cmp_kernel,
        grid=(2, B, G),
        in_specs=[pl.BlockSpec((None, None, None, ncp, width), lambda s, b, g: (s, b, g, 0, 0)),
                  pl.BlockSpec((None,) + w1s.shape[1:], lambda s, b, g: (s, 0, 0)),
                  pl.BlockSpec((None,) + w2s.shape[1:], lambda s, b, g: (s, 0, 0)),
                  pl.BlockSpec((None,) + poss.shape[1:], lambda s, b, g: (s, 0, 0))],
        out_specs=pl.BlockSpec((None, None, None, ncp, NSA_HEAD_DIM), lambda s, b, g: (s, b, g, 0, 0)),
        out_shape=jax.ShapeDtypeStruct((2, B, G, ncp, NSA_HEAD_DIM), F32),
        compiler_params=_cparams(("parallel", "parallel", "parallel")),
        name="nsa_compress",
    )(xc, w1s, w2s, poss)


SEL_TILE = 512
SEL_TABLE_FAR = 640
NSA_PROBLEMS = 2


def _tree(op, parts):
    while len(parts) > 1:
        parts = [op(parts[i], parts[i + 1]) if i + 1 < len(parts) else parts[i] for i in range(0, len(parts), 2)]
    return parts[0]


def _fold8(op, x):
    return _tree(op, [x[k:k + 8] for k in range(0, x.shape[0], 8)])


def _safe_inv(l):
    return jnp.where(l > 0.0, 1.0 / jnp.where(l > 0.0, l, 1.0), 0.0)


def _nsa_kernel(q_ref, za_ref, ng_ref, kc_ref, vct_ref, ks_ref, vst_ref, kw_ref, vwt_ref,
                tc_ref, tw_ref, ts_ref, ovt_ref, o_ref, s_ref, acc_ref, gt_ref, *, L):
    R, dh, QB = NSA_REP, NSA_HEAD_DIM, Q_BLOCK
    ncp = L // CMP_STRIDE
    nsel = L // SEL_BLOCK
    W = R * QB
    g = pl.program_id(1)
    qb = pl.program_id(2)

    NP = q_ref.shape[0]

    def front(p):
        qt = (q_ref[p].astype(F32) * (dh ** -0.5)).T
        qT = jnp.concatenate([qt[r * dh:(r + 1) * dh] for r in range(R)], axis=1).astype(BF16)

        c_off = pl.multiple_of((ncp - 8) - 8 * qb, 8)
        sc = jnp.dot(kc_ref[p], qT, preferred_element_type=F32) + tc_ref[pl.ds(c_off, ncp), :]
        m = jnp.maximum(jnp.max(_fold8(jnp.maximum, sc), axis=0, keepdims=True), 0.1 * MASK_VALUE)
        e = jnp.exp(sc - m)
        l = jnp.sum(_fold8(jnp.add, e), axis=0, keepdims=True)
        p_c = e * _safe_inv(l)
        oc = jnp.dot(vct_ref[p], p_c.astype(BF16), preferred_element_type=F32)

        psum = _tree(jnp.add, [p_c[:, r * QB:(r + 1) * QB] for r in range(R)])
        p_hi = psum.astype(BF16)
        p_lo = (psum - p_hi.astype(F32)).astype(BF16)
        ovt = ovt_ref[...]
        imp = jnp.dot(ovt, p_hi, preferred_element_type=F32) + jnp.dot(ovt, p_lo, preferred_element_type=F32)
        jj = lax.broadcasted_iota(jnp.int32, (nsel, QB), 0)
        ii = lax.broadcasted_iota(jnp.int32, (nsel, QB), 1)
        cur = 2 * qb + (ii >= SEL_BLOCK).astype(jnp.int32)
        forced = (jj == 0) | (jj == cur) | (jj == cur - 1)
        imp = jnp.where(forced, FORCE_VALUE, imp)
        imp = jnp.where(jj <= cur, imp, MASK_VALUE)
        jf = jj.astype(F32)
        sel = jnp.zeros((nsel, QB), F32)
        for _ in range(min(SEL_TOPK, nsel)):
            mx = jnp.max(_fold8(jnp.maximum, imp), axis=0, keepdims=True)
            idx = jnp.min(_fold8(jnp.minimum, jnp.where(imp == mx, jf, 1e9)), axis=0, keepdims=True)
            hit = jf == idx
            sel = jnp.where(hit & (mx > 0.1 * MASK_VALUE), 1.0, sel)
            imp = jnp.where(hit, -jnp.inf, imp)
        selbias = jnp.where(sel > 0.5, 0.0, MASK_VALUE).astype(BF16)
        q_sel = jnp.concatenate([qT, jnp.concatenate([selbias] * R, axis=1)], axis=0)

        w0 = pl.multiple_of(qb * QB, QB)
        flag = jnp.where(lax.broadcasted_iota(jnp.int32, (dh, W), 0) == 0, MASK_VALUE, 0.0).astype(BF16)
        q_win = jnp.concatenate([qT, flag], axis=0)
        sw = jnp.dot(kw_ref[p, pl.ds(w0, WINDOW + QB), :], q_win, preferred_element_type=F32) + tw_ref[...]
        m_w = jnp.max(_fold8(jnp.maximum, sw), axis=0, keepdims=True)
        pw = jnp.exp((sw - m_w).astype(BF16))
        accw = jnp.dot(vwt_ref[p, :, pl.ds(w0, WINDOW + QB)], pw, preferred_element_type=F32)
        owin = accw[:dh] * _safe_inv(accw[dh:dh + 1])

        gt_ref[p] = jax.nn.sigmoid(ng_ref[p].astype(F32)).T

        def gate_row(branch):
            return jnp.concatenate([gt_ref[p, pl.ds(3 * (g * R + r) + branch, 1), :] for r in range(R)], axis=1)

        return q_sel, oc * gate_row(0) + owin * gate_row(2), gate_row(1)

    fronts = [front(p) for p in range(NP)]

    npair = (qb // (SEL_TILE // QB) + 2) // 2

    def pass1(u, maccs):
        maccs = list(maccs)
        for h in range(2):
            t = 2 * u + h
            k0 = pl.multiple_of(t * SEL_TILE, SEL_TILE)
            x0 = pl.multiple_of(jnp.maximum(t * SEL_TILE - qb * QB + SEL_TABLE_FAR, 0), QB)
            for p in range(NP):
                s = (jnp.dot(ks_ref[p, pl.ds(k0, SEL_TILE), :], fronts[p][0], preferred_element_type=F32)
                     + ts_ref[pl.ds(x0, SEL_TILE), :])
                s_ref[p, pl.ds(k0, SEL_TILE), :] = s
                maccs[p] = jnp.maximum(maccs[p], _fold8(jnp.maximum, s))
        return tuple(maccs)

    maccs = lax.fori_loop(0, npair, pass1, tuple(jnp.full((8, W), NEVER_VALUE, F32) for _ in range(NP)))
    m_s = [jnp.max(mc, axis=0, keepdims=True) for mc in maccs]

    acc_ref[...] = jnp.zeros_like(acc_ref)

    def pass2(u, carry):
        for p in range(NP):
            part = []
            for h in range(2):
                k0 = pl.multiple_of((2 * u + h) * SEL_TILE, SEL_TILE)
                pr = jnp.exp((s_ref[p, pl.ds(k0, SEL_TILE), :] - m_s[p]).astype(BF16))
                part.append(jnp.dot(vst_ref[p, :, pl.ds(k0, SEL_TILE)], pr, preferred_element_type=F32))
            acc_ref[p] += part[0] + part[1]
        return carry

    lax.fori_loop(0, npair, pass2, 0)

    for p in range(NP):
        acc = acc_ref[p]
        osel = acc[:dh] * _safe_inv(acc[dh:dh + 1])
        ot = fronts[p][1] + osel * fronts[p][2]
        o = jnp.concatenate([ot[:, r * QB:(r + 1) * QB] for r in range(R)], axis=0).T
        za = za_ref[p].astype(F32)
        o_ref[p] = (o * (za * jax.nn.sigmoid(za))).astype(o_ref.dtype)


def _nsa_attend(proj, kc, vct, ks, vst, kw, vwt, tc, tw, ts, ovt, B, L):
    R, dh, G, QB = NSA_REP, NSA_HEAD_DIM, NSA_GROUPS, Q_BLOCK
    assert L % (2 * SEL_TILE) == 0
    nqb = L // QB
    gw = R * dh
    W = R * QB
    NP = NSA_PROBLEMS if B % NSA_PROBLEMS == 0 else 1
    proj4 = proj.reshape(B // NP, NP, L, proj.shape[-1])
    tok = lambda col: pl.BlockSpec((None, NP, QB, col[1]), lambda b, g, i: (b, 0, i, col[0] // col[1] + col[2] * g))
    bg = lambda a: pl.BlockSpec((NP, None) + a.shape[2:], lambda b, g, i: (b, g, 0, 0))
    grp = lambda a: pl.BlockSpec((None,) + a.shape[1:], lambda b, g, i: (g, 0, 0), pipeline_mode=pl.Buffered(1))
    out = pl.pallas_call(
        functools.partial(_nsa_kernel, L=L),
        grid=(B // NP, G, nqb),
        in_specs=[tok((COL_Q, gw, 1)), tok((COL_ZA, gw, 1)), tok((COL_NG, 128, 0)),
                  bg(kc), bg(vct), bg(ks), bg(vst), bg(kw), bg(vwt), grp(tc), grp(tw), grp(ts),
                  pl.BlockSpec(ovt.shape, lambda b, g, i: (0, 0))],
        out_specs=pl.BlockSpec((None, NP, QB, gw), lambda b, g, i: (b, 0, i, g)),
        out_shape=jax.ShapeDtypeStruct((B // NP, NP, L, NSA_WIDTH), BF16),
        scratch_shapes=[pltpu.VMEM((NP, L, W), F32), pltpu.VMEM((NP, 2 * dh, W), F32),
                        pltpu.VMEM((NP, 128, QB), F32)],
        compiler_params=_cparams(("parallel", "parallel", "arbitrary")),
        name="nsa_attend",
    )(proj4, proj4, proj4, kc, vct, ks, vst, kw, vwt, tc, tw, ts, ovt)
    return out.reshape(B * L, NSA_WIDTH)


def _t5_bucket(dist):
    n = jnp.maximum(dist, 0)
    max_exact = REL_BUCKETS // 2
    nf = jnp.maximum(n, max_exact).astype(F32)
    large = max_exact + (jnp.log(nf / max_exact) / math.log(REL_MAX_DIST / max_exact)
                         * (REL_BUCKETS - max_exact)).astype(jnp.int32)
    large = jnp.minimum(large, REL_BUCKETS - 1)
    return jnp.where(n < max_exact, n, large)


def _toeplitz(f, d0, base, step, n_rows, width):
    rpb = width // step
    nblk = n_rows // rpb
    assert rpb * step == width and nblk * rpb == n_rows
    lo = base - width * nblk - d0
    seg = f[..., lo:lo + width * (nblk + 1)].reshape(f.shape[:-1] + (nblk + 1, width))[..., ::-1, :]
    win = jnp.concatenate([seg[..., :-1, :], seg[..., 1:, :]], axis=-1)
    t = jnp.tile(win, rpb)[..., :rpb * (2 * width - step)].reshape(win.shape[:-1] + (rpb, 2 * width - step))
    return t[..., :width].reshape(f.shape[:-1] + (n_rows, width))


def _nsa_tables(rel_bias, L):
    QB, G, R = Q_BLOCK, NSA_GROUPS, NSA_REP
    ncp = L // CMP_STRIDE
    nsel = L // SEL_BLOCK
    npos = L + QB
    tbl = rel_bias.astype(F32)
    bpos = (tbl[_t5_bucket(jnp.arange(npos, dtype=jnp.int32))] - tbl[REL_BUCKETS - 1][None, :]).T
    f = jnp.concatenate([jnp.full((NSA_HEADS, npos), MASK_VALUE, F32), bpos], axis=-1)
    d = np.arange(-npos, npos)
    fw = jnp.where(jnp.asarray(d < WINDOW), f, MASK_VALUE)
    tc = _toeplitz(f, -npos, L - QB - CMP_BLOCK + 1, CMP_STRIDE, 2 * ncp - 8, QB)
    tw = _toeplitz(fw, -npos, WINDOW, 1, WINDOW + QB, QB)
    ts = _toeplitz(f, -npos, SEL_TABLE_FAR, 1, SEL_TABLE_FAR + 2 * SEL_TILE, QB)
    lanes = lambda t: t.reshape(G, R, t.shape[1], QB).transpose(0, 2, 1, 3).reshape(G, t.shape[1], R * QB)
    c = np.arange(ncp)[None, :]
    j = np.arange(nsel)[:, None]
    ovt = ((c - 4 * j >= -1) & (c - 4 * j <= 3)).astype(np.float32)
    return lanes(tc), lanes(tw), lanes(ts), jnp.asarray(ovt, BF16)


S5_LANES = 128
S5_GT = S5_LANES // SSM_GROUP
S5_SW = S5_GT * SSM_STATE
S5_COLS = 4 * S5_LANES


def _s5_kernel(us_ref, kt_ref, zt_ref, cl_ref, lam_ref, ex_ref, y_ref,
               xs_ref, u8_ref, m8_ref, ws8_ref, wo8_ref, *, nk):
    T, C, P, GT, LT, SW = SSM_CHUNK, SSM_GROUP, SSM_STATE, S5_GT, S5_LANES, S5_SW

    @pl.when(pl.program_id(1) == 0)
    def _build():
        tile = lambda x: jnp.concatenate([x] * GT, axis=0)
        r = lax.broadcasted_iota(jnp.int32, (LT, LT), 0)
        c = lax.broadcasted_iota(jnp.int32, (LT, LT), 1)
        same = (r // C) == (c // C)
        m8_ref[...] = jnp.zeros_like(m8_ref)
        for j in range(T):
            bd = jnp.where(same, tile(kt_ref[j]), 0.0).astype(BF16)
            for b in range(T - j):
                m8_ref[b * LT:(b + 1) * LT, (b + j) * LT:(b + j + 1) * LT] = bd
        r = lax.broadcasted_iota(jnp.int32, (LT, 2 * SW), 0)
        c = lax.broadcasted_iota(jnp.int32, (LT, 2 * SW), 1)
        same = (r // C) == ((c % SW) // P)
        for b in range(T):
            ws8_ref[b * LT:(b + 1) * LT, :] = jnp.where(same, tile(zt_ref[b]), 0.0).astype(BF16)
        r = lax.broadcasted_iota(jnp.int32, (2 * SW, LT), 0)
        c = lax.broadcasted_iota(jnp.int32, (2 * SW, LT), 1)
        same = ((r % SW) // P) == (c // C)
        cl = cl_ref[...]
        for a in range(T):
            blk = jnp.dot(cl, ex_ref[a], preferred_element_type=F32)
            wo8_ref[:, a * LT:(a + 1) * LT] = jnp.where(same, blk, 0.0).astype(BF16)

    xs_ref[...] = us_ref[...].astype(F32)
    for b in range(T):
        u8_ref[:, b * LT:(b + 1) * LT] = xs_ref[pl.ds(b, nk, stride=T), :].astype(BF16)
    u8 = u8_ref[...]
    s = jnp.dot(u8, ws8_ref[...], preferred_element_type=F32)
    hr, hi = s[:, :SW], s[:, SW:]
    kidx = lax.broadcasted_iota(jnp.int32, (nk, SW), 0)
    d, step = 1, 0
    while d < nk:
        lr = lam_ref[step:step + 1, :SW]
        li = lam_ref[step:step + 1, SW:]
        keep = kidx >= d
        sr = jnp.where(keep, pltpu.roll(hr, d, 0), 0.0)
        si = jnp.where(keep, pltpu.roll(hi, d, 0), 0.0)
        hr, hi = hr + lr * sr - li * si, hi + lr * si + li * sr
        d, step = 2 * d, step + 1
    keep = kidx >= 1
    pr = jnp.where(keep, pltpu.roll(hr, 1, 0), 0.0)
    pi = jnp.where(keep, pltpu.roll(hi, 1, 0), 0.0)
    hcat = jnp.concatenate([pr, pi], axis=-1).astype(BF16)
    for q in range(T * LT // S5_COLS):
        kq = (q + 1) * S5_COLS
        yq = (jnp.dot(hcat, wo8_ref[:, q * S5_COLS:kq], preferred_element_type=F32)
              + jnp.dot(u8[:, :kq], m8_ref[0:kq, q * S5_COLS:kq], preferred_element_type=F32))
        for a4 in range(S5_COLS // LT):
            a = q * (S5_COLS // LT) + a4
            y_ref[pl.ds(a, nk, stride=T), :] = yq[:, a4 * LT:(a4 + 1) * LT]


def _s5_scan(proj, kt, zt, clc, lamp, ex, B, L):
    T, LT, SW = SSM_CHUNK, S5_LANES, S5_SW
    nk = L // T
    nt = SSM_WIDTH // LT
    per_tile = lambda a: pl.BlockSpec((None,) + a.shape[1:], lambda t, b: (t,) + (0,) * (a.ndim - 1))
    return pl.pallas_call(
        functools.partial(_s5_kernel, nk=nk),
        grid=(nt, B),
        in_specs=[pl.BlockSpec((L, LT), lambda t, b: (b, COL_US // LT + t)),
                  per_tile(kt), per_tile(zt), per_tile(clc), per_tile(lamp),
                  pl.BlockSpec(ex.shape, lambda t, b: (0, 0, 0))],
        out_specs=pl.BlockSpec((L, LT), lambda t, b: (b, t)),
        out_shape=jax.ShapeDtypeStruct((B * L, SSM_WIDTH), F32),
        scratch_shapes=[pltpu.VMEM((L, LT), F32), pltpu.VMEM((nk, T * LT), BF16),
                        pltpu.VMEM((T * LT, T * LT), BF16), pltpu.VMEM((T * LT, 2 * SW), BF16),
                        pltpu.VMEM((2 * SW, T * LT), BF16)],
        compiler_params=_cparams(("parallel", "arbitrary")),
        name="s5_scan",
    )(proj, kt, zt, clc, lamp, ex)


def _cmul(ar, ai, br, bi):
    return ar * br - ai * bi, ar * bi + ai * br


def _s5_tables(a_re, a_im, log_dt, b_re, b_im, c_re, c_im, nk):
    T, P, C, Gs = SSM_CHUNK, SSM_STATE, SSM_GROUP, SSM_GROUPS
    dt = jnp.exp(log_dt.astype(F32))[:, None]
    ar, ai = a_re.astype(F32), a_im.astype(F32)
    mag = jnp.exp(ar * dt)
    lr, li = mag * jnp.cos(ai * dt), mag * jnp.sin(ai * dt)
    den = ar * ar + ai * ai
    nr, ni = lr - 1.0, li
    fr, fi = (nr * ar + ni * ai) / den, (ni * ar - nr * ai) / den
    br, bim = b_re.astype(F32), b_im.astype(F32)
    bbr = fr[..., None] * br - fi[..., None] * bim
    bbi = fr[..., None] * bim + fi[..., None] * br
    jv = jnp.arange(T + 1, dtype=F32)[:, None, None]
    pmag = jnp.exp(jv * (ar * dt)[None])
    pwr, pwi = pmag * jnp.cos(jv * (ai * dt)[None]), pmag * jnp.sin(jv * (ai * dt)[None])
    zr = pwr[..., None] * bbr[None] - pwi[..., None] * bbi[None]
    zi = pwr[..., None] * bbi[None] + pwi[..., None] * bbr[None]
    cr, ci = c_re.astype(F32), c_im.astype(F32)
    kj = (jnp.einsum('gcp,jgpd->gjcd', cr, zr[:T], precision=HIGHEST)
          - jnp.einsum('gcp,jgpd->gjcd', ci, zi[:T], precision=HIGHEST))
    GT, NT = S5_GT, Gs // S5_GT
    kt = kj.reshape(NT, GT, T, C, C).transpose(0, 2, 4, 1, 3).reshape(NT, T, C, GT * C)
    lay_z = lambda z: z[:T][::-1].reshape(T, NT, GT, P, C).transpose(1, 0, 4, 2, 3).reshape(NT, T, C, GT * P)
    zt = jnp.concatenate([lay_z(zr), lay_z(zi)], axis=-1)
    pa_r, pa_i = pwr[1:].transpose(1, 0, 2)[:, :, None, :], pwi[1:].transpose(1, 0, 2)[:, :, None, :]
    clr = cr[:, None] * pa_r - ci[:, None] * pa_i
    cli = cr[:, None] * pa_i + ci[:, None] * pa_r
    lay_c = lambda x: x.reshape(NT, GT, T, C, P).transpose(0, 1, 4, 2, 3).reshape(NT, GT * P, T * C)
    clc = jnp.concatenate([lay_c(clr), lay_c(-cli)], axis=1).astype(BF16)
    qr, qi = pwr[T], pwi[T]
    steps = []
    d = 1
    while d < nk:
        steps.append(jnp.concatenate([qr.reshape(NT, GT * P), qi.reshape(NT, GT * P)], -1))
        qr, qi = _cmul(qr, qi, qr, qi)
        d *= 2
    lamp = jnp.stack(steps, 1)
    col = np.arange(T * C)[:, None]
    lane = np.arange(GT * C)[None, :]
    ex = np.stack([(col // C == a) & (col % C == lane % C) for a in range(T)]).astype(np.float32)
    return kt, zt, clc, lamp, jnp.asarray(ex, BF16)


def _glu_kernel(y_ref, u_ref, zb_ref, d_ref, w_ref, b_ref, o_ref):
    y = y_ref[...] + d_ref[...] * u_ref[...].astype(F32)
    yg = jax.nn.gelu(y).astype(BF16)
    z = jnp.dot(yg, w_ref[...], preferred_element_type=F32) + b_ref[...]
    zb = zb_ref[...].astype(F32)
    o_ref[...] = (yg.astype(F32) * jax.nn.sigmoid(z) * (zb * jax.nn.sigmoid(zb))).astype(o_ref.dtype)


def _s5_glu(y, proj, d_skip, w_glu, b_glu):
    N, W = y.shape
    tm = min(1024, N)
    return pl.pallas_call(
        _glu_kernel,
        grid=(N // tm,),
        in_specs=[pl.BlockSpec((tm, W), lambda i: (i, 0)),
                  pl.BlockSpec((tm, W), lambda i: (i, COL_US // W)),
                  pl.BlockSpec((tm, W), lambda i: (i, COL_ZB // W)),
                  pl.BlockSpec((1, W), lambda i: (0, 0)),
                  pl.BlockSpec((W, W), lambda i: (0, 0)),
                  pl.BlockSpec((1, W), lambda i: (0, 0))],
        out_specs=pl.BlockSpec((tm, W), lambda i: (i, 0)),
        out_shape=jax.ShapeDtypeStruct((N, W), BF16),
        compiler_params=_cparams(("parallel",)),
        name="s5_glu",
    )(y, proj, proj, d_skip.reshape(1, W), w_glu, b_glu.reshape(1, W))


def _merge_kernel(oa_ref, ob_ref, ga_ref, gb_ref, x_ref, gate_ref, wa_ref, wb_ref, wo_ref, lg_ref, lb_ref, o_ref):
    pa = jnp.dot(oa_ref[...], wa_ref[...], preferred_element_type=F32)
    pb = jnp.dot(ob_ref[...], wb_ref[...], preferred_element_type=F32)
    m = jax.nn.sigmoid(ga_ref[...].astype(F32)) * pa + jax.nn.sigmoid(gb_ref[...].astype(F32)) * pb
    y = jnp.dot(m.astype(BF16), wo_ref[...], preferred_element_type=F32)
    r = DEEPNORM_ALPHA * x_ref[...] + gate_ref[...] * y
    mu = jnp.mean(r, axis=-1, keepdims=True)
    rc = r - mu
    var = jnp.mean(rc * rc, axis=-1, keepdims=True)
    o_ref[...] = rc * lax.rsqrt(var + LN_EPS) * lg_ref[...] + lb_ref[...]


def _merge_out(oa, ob, proj, x2, gate, wa, wb, wo, ln_g, ln_b, L):
    N, D = x2.shape
    W = oa.shape[1]
    B = gate.shape[0]
    tm = min(256, L)
    const = lambda i: (0, 0)
    return pl.pallas_call(
        _merge_kernel,
        grid=(N // tm,),
        in_specs=[pl.BlockSpec((tm, W), lambda i: (i, 0)),
                  pl.BlockSpec((tm, W), lambda i: (i, 0)),
                  pl.BlockSpec((tm, D), lambda i: (i, COL_GA // D)),
                  pl.BlockSpec((tm, D), lambda i: (i, COL_GB // D)),
                  pl.BlockSpec((tm, D), lambda i: (i, 0)),
                  pl.BlockSpec((None, 1, D), lambda i: ((i * tm) // L, 0, 0)),
                  pl.BlockSpec((W, D), const),
                  pl.BlockSpec((W, D), const),
                  pl.BlockSpec((D, D), const),
                  pl.BlockSpec((1, D), const),
                  pl.BlockSpec((1, D), const)],
        out_specs=pl.BlockSpec((tm, D), lambda i: (i, 0)),
        out_shape=jax.ShapeDtypeStruct((N, D), F32),
        compiler_params=_cparams(("parallel",)),
        name="merge_out",
    )(oa, ob, proj, proj, x2, gate.reshape(B, 1, D), wa, wb, wo, ln_g.reshape(1, D), ln_b.reshape(1, D))


def _layer(x, c, w_ada, b_ada, w_in, rel_bias, cmp_pos_k, cmp_pos_v, w_cmp_k1, w_cmp_k2, w_cmp_v1, w_cmp_v2,
           ssm_a_re, ssm_a_im, ssm_log_dt, ssm_b_re, ssm_b_im, ssm_c_re, ssm_c_im, ssm_d, w_glu, b_glu,
           w_branch_nsa, w_branch_ssm, w_out, ln_g, ln_b):
    B, L, D = x.shape
    N = B * L
    G, dh = NSA_GROUPS, NSA_HEAD_DIM
    x2 = x.reshape(N, D)

    mod = _ada_mod(c, w_ada, b_ada)
    shift, scale, gate = mod[:, :D], mod[:, D:2 * D], mod[:, 2 * D:]

    o_q, o_kv, o_ng, o_za, o_us, o_zb, o_ga, o_gb = np.cumsum(
        [0, NSA_WIDTH, 6 * KV_WIDTH, 3 * NSA_HEADS, NSA_WIDTH, SSM_WIDTH, SSM_WIDTH, D_MODEL]).tolist()
    w_re = jnp.concatenate([
        w_in[:, o_q:o_kv], w_in[:, o_za:o_us], w_in[:, o_us:o_zb], w_in[:, o_zb:o_ga], w_in[:, o_ga:o_gb],
        w_in[:, o_gb:], w_in[:, o_kv:o_ng], w_in[:, o_ng:o_za],
        jnp.zeros((D, PROJ_COLS - COL_NG - 3 * NSA_HEADS), w_in.dtype)], axis=1).astype(BF16)
    proj = _in_proj(x2, scale, shift, w_re, L)

    kv = proj[:, COL_KV:COL_KV + 6 * KV_WIDTH].reshape(B, L, 6, G, dh)
    ncp = L // CMP_STRIDE
    xc = kv[:, :, 0:2].reshape(B, ncp, CMP_STRIDE, 2, G, dh).transpose(3, 0, 4, 1, 2, 5)
    xc = xc.reshape(2, B, G, ncp, CMP_STRIDE * dh)
    w1s = jnp.stack([w_cmp_k1, w_cmp_v1])
    w2s = jnp.stack([w_cmp_k2, w_cmp_v2])
    poss = jnp.broadcast_to(jnp.stack([cmp_pos_k, cmp_pos_v]).reshape(2, 1, CMP_BLOCK * dh), (2, 8, CMP_BLOCK * dh))
    kcv = _nsa_compress(xc, w1s, w2s, poss)
    kc = kcv[0].astype(BF16)
    vct = kcv[1].swapaxes(-1, -2).astype(BF16)
    nsel = L // SEL_BLOCK
    e_abs = jnp.asarray(np.arange(L)[:, None] // SEL_BLOCK == np.arange(nsel)[None, :], BF16)
    ks = jnp.concatenate([kv[:, :, 2].transpose(0, 2, 1, 3), jnp.broadcast_to(e_abs, (B, G, L, nsel))], axis=-1)
    vrows = lambda n: [jnp.ones((B, G, 1, n), BF16), jnp.zeros((B, G, dh - 1, n), BF16)]
    vst = jnp.concatenate([kv[:, :, 3].transpose(0, 2, 3, 1)] + vrows(L), axis=2)
    padflag = jnp.asarray((np.arange(L + WINDOW) < WINDOW)[:, None] & (np.arange(dh) == 0)[None, :], BF16)
    kw = jnp.pad(kv[:, :, 4].transpose(0, 2, 1, 3), ((0, 0), (0, 0), (WINDOW, 0), (0, 0)))
    kw = jnp.concatenate([kw, jnp.broadcast_to(padflag, (B, G, L + WINDOW, dh))], axis=-1)
    vwt = jnp.pad(kv[:, :, 5].transpose(0, 2, 3, 1), ((0, 0), (0, 0), (0, 0), (WINDOW, 0)))
    vwt = jnp.concatenate([vwt] + vrows(L + WINDOW), axis=2)
    tc, tw, ts, ovt = _nsa_tables(rel_bias, L)
    o_a = _nsa_attend(proj, kc, vct, ks, vst, kw, vwt, tc, tw, ts, ovt, B, L)

    s5_tabs = _s5_tables(ssm_a_re, ssm_a_im, ssm_log_dt, ssm_b_re, ssm_b_im, ssm_c_re, ssm_c_im, L // SSM_CHUNK)
    y = _s5_scan(proj, *s5_tabs, B, L)
    o_b = _s5_glu(y, proj, ssm_d, w_glu.astype(BF16), b_glu)

    out = _merge_out(o_a, o_b, proj, x2, gate, w_branch_nsa.astype(BF16), w_branch_ssm.astype(BF16),
                     w_out.astype(BF16), ln_g, ln_b, L)
    return out.reshape(B, L, D)


def kernel(x, c, w_ada, b_ada, w_in, rel_bias, cmp_pos_k, cmp_pos_v, w_cmp_k1, w_cmp_k2, w_cmp_v1, w_cmp_v2,
           ssm_a_re, ssm_a_im, ssm_log_dt, ssm_b_re, ssm_b_im, ssm_c_re, ssm_c_im, ssm_d, w_glu, b_glu,
           w_branch_nsa, w_branch_ssm, w_out, ln_g, ln_b):
    for i in range(w_ada.shape[0]):
        x = _layer(x, c, w_ada[i], b_ada[i], w_in[i], rel_bias, cmp_pos_k[i], cmp_pos_v[i], w_cmp_k1[i],
                   w_cmp_k2[i], w_cmp_v1[i], w_cmp_v2[i], ssm_a_re[i], ssm_a_im[i], ssm_log_dt[i], ssm_b_re[i],
                   ssm_b_im[i], ssm_c_re[i], ssm_c_im[i], ssm_d[i], w_glu[i], b_glu[i], w_branch_nsa[i],
                   w_branch_ssm[i], w_out[i], ln_g[i], ln_b[i])
    return x
```

```python
import functools
import math

import numpy as np
import jax
import jax.numpy as jnp
from jax import lax
from jax.experimental import pallas as pl
from jax.experimental.pallas import tpu as pltpu

F32 = jnp.float32
BF16 = jnp.bfloat16
HIGHEST = lax.Precision.HIGHEST

D_MODEL = 2048
NSA_HEADS = 16
NSA_GROUPS = 4
NSA_HEAD_DIM = 64
NSA_REP = NSA_HEADS // NSA_GROUPS
NSA_WIDTH = NSA_HEADS * NSA_HEAD_DIM
KV_WIDTH = NSA_GROUPS * NSA_HEAD_DIM
CMP_BLOCK = 32
CMP_STRIDE = 16
CMP_HIDDEN = 128
SEL_BLOCK = 64
SEL_TOPK = 8
WINDOW = 512
Q_BLOCK = 128
SSM_WIDTH = 1024
SSM_GROUP = 16
SSM_GROUPS = SSM_WIDTH // SSM_GROUP
SSM_STATE = 64
REL_BUCKETS = 32
REL_MAX_DIST = 128
DEEPNORM_ALPHA = 2.0 ** 0.25
LN_EPS = 1e-5
MASK_VALUE = -1e30
FORCE_VALUE = 1e4
NEVER_VALUE = -3e38
SSM_CHUNK = 16

COL_Q, COL_ZA, COL_US, COL_ZB, COL_GA, COL_GB, COL_KV, COL_KC, COL_NG = 0, 1024, 2048, 3072, 4096, 6144, 8192, 9216, 9728
PROJ_COLS = 9856
VMEM_LIMIT = 56 * 1024 * 1024


def _cparams(sem):
    return pltpu.CompilerParams(dimension_semantics=sem, vmem_limit_bytes=VMEM_LIMIT)


def _ada_kernel(c_ref, w_ref, b_ref, o_ref):
    o_ref[...] = jnp.dot(c_ref[...], w_ref[...], preferred_element_type=F32, precision=HIGHEST) + b_ref[...]


def _ada_mod(c, w_ada, b_ada):
    B, D = c.shape
    n = w_ada.shape[1]
    tn = 1536
    return pl.pallas_call(
        _ada_kernel,
        grid=(n // tn,),
        in_specs=[pl.BlockSpec((B, D), lambda j: (0, 0)),
                  pl.BlockSpec((D, tn), lambda j: (0, j)),
                  pl.BlockSpec((1, tn), lambda j: (0, j))],
        out_specs=pl.BlockSpec((B, tn), lambda j: (0, j)),
        out_shape=jax.ShapeDtypeStruct((B, n), F32),
        compiler_params=_cparams(("parallel",)),
        name="ada_mod",
    )(c, w_ada, b_ada.reshape(1, n))


def _inproj_kernel(x_ref, scale_ref, shift_ref, w_ref, o_ref, h_ref):
    @pl.when(pl.program_id(1) == 0)
    def _():
        def rows(k, carry):
            r0 = pl.multiple_of(k * LN_ROWS, LN_ROWS)
            x = x_ref[pl.ds(r0, LN_ROWS), :]
            mu = jnp.mean(x, axis=-1, keepdims=True)
            xc = x - mu
            var = jnp.mean(xc * xc, axis=-1, keepdims=True)
            hn = xc * lax.rsqrt(var + LN_EPS)
            h_ref[pl.ds(r0, LN_ROWS), :] = (hn * (1.0 + scale_ref[...]) + shift_ref[...]).astype(BF16)
            return carry

        lax.fori_loop(0, x_ref.shape[0] // LN_ROWS, rows, 0)

    o_ref[...] = jnp.dot(h_ref[...], w_ref[...], preferred_element_type=F32).astype(o_ref.dtype)


LN_ROWS = 256


def _in_proj(x2, scale, shift, w, L):
    N, D = x2.shape
    ncol = w.shape[1]
    tm = min(1024, L)
    tn = 1408
    assert N % tm == 0 and L % tm == 0 and ncol % tn == 0
    B = scale.shape[0]
    return pl.pallas_call(
        _inproj_kernel,
        grid=(N // tm, ncol // tn),
        in_specs=[pl.BlockSpec((tm, D), lambda i, j: (i, 0)),
                  pl.BlockSpec((None, 1, D), lambda i, j: ((i * tm) // L, 0, 0)),
                  pl.BlockSpec((None, 1, D), lambda i, j: ((i * tm) // L, 0, 0)),
                  pl.BlockSpec((D, tn), lambda i, j: (0, j))],
        out_specs=pl.BlockSpec((tm, tn), lambda i, j: (i, j)),
        out_shape=jax.ShapeDtypeStruct((N, ncol), BF16),
        scratch_shapes=[pltpu.VMEM((tm, D), BF16)],
        compiler_params=_cparams(("parallel", "arbitrary")),
        name="in_proj",
    )(x2, scale.reshape(B, 1, D), shift.reshape(B, 1, D), w)


def _cmp_kernel(x_ref, w1_ref, w1bd_ref, w2_ref, pos_ref, o_ref, xs_ref, *, ncp):
    S, dh = CMP_STRIDE, NSA_HEAD_DIM
    xs_ref[...] = x_ref[...].astype(F32)
    a = jnp.zeros((ncp, 2 * CMP_HIDDEN), F32)
    b = jnp.zeros((ncp, 2 * CMP_HIDDEN), F32)
    for j in range(S):
        xj = xs_ref[pl.ds(j, ncp, stride=S), :].astype(BF16)
        a = a + jnp.dot(xj, w1bd_ref[0, j], preferred_element_type=F32)
        b = b + jnp.dot(xj, w1bd_ref[1, j], preferred_element_type=F32)
    pw = jnp.dot(pos_ref[...], w1_ref[...], preferred_element_type=F32, precision=HIGHEST)[0:1]
    h = a + pltpu.roll(b, ncp - 1, 0) + jnp.concatenate([pw, pw], axis=1)
    w2 = w2_ref[...].astype(BF16)
    for g in range(2):
        hg = jax.nn.gelu(h[:, g * CMP_HIDDEN:(g + 1) * CMP_HIDDEN]).astype(BF16)
        o_ref[g] = jnp.dot(hg, w2, preferred_element_type=F32)


def _nsa_compress(proj, w1s, w2s, poss, B, L):
    G, dh, S = NSA_GROUPS, NSA_HEAD_DIM, CMP_STRIDE
    ncp = L // S
    w = w1s.reshape(2, 2, S, dh, CMP_HIDDEN).astype(BF16)
    z = jnp.zeros_like(w)
    w1bd = jnp.concatenate([jnp.concatenate([w, z], -1), jnp.concatenate([z, w], -1)], axis=-2)
    return pl.pallas_call(
        functools.partial(_cmp_kernel, ncp=ncp),
        grid=(2, B, G // 2),
        in_specs=[pl.BlockSpec((L, 2 * dh), lambda s, b, t: (b, COL_KC // (2 * dh) + 2 * s + t)),
                  pl.BlockSpec((None,) + w1s.shape[1:], lambda s, b, t: (s, 0, 0)),
                  pl.BlockSpec((None,) + w1bd.shape[1:], lambda s, b, t: (s, 0, 0, 0, 0)),
                  pl.BlockSpec((None,) + w2s.shape[1:], lambda s, b, t: (s, 0, 0)),
                  pl.BlockSpec((None,) + poss.shape[1:], lambda s, b, t: (s, 0, 0))],
        out_specs=pl.BlockSpec((None, None, 2, ncp, dh), lambda s, b, t: (s, b, t, 0, 0)),
        out_shape=jax.ShapeDtypeStruct((2, B, G, ncp, dh), F32),
        scratch_shapes=[pltpu.VMEM((L, 2 * dh), F32)],
        compiler_params=_cparams(("parallel", "parallel", "parallel")),
        name="nsa_compress",
    )(proj, w1s, w1bd, w2s, poss)


V_ROWS = NSA_HEAD_DIM + 16


def _kvprep_kernel(x_ref, ks_ref, vst_ref, kw_ref, vwt_ref, *, TL):
    G, dh = NSA_GROUPS, NSA_HEAD_DIM
    i = pl.program_id(1)
    tile = jnp.maximum(i - 1, 0)
    x = x_ref[...].astype(F32)
    lane = lax.broadcasted_iota(jnp.int32, (TL, 2 * dh), 1)
    row = lax.broadcasted_iota(jnp.int32, (TL, 2 * dh), 0) + tile * TL
    onehot = jnp.where(lane - dh == row // SEL_BLOCK, 1.0, 0.0)
    padrow = jnp.where(lane == dh, 1.0, 0.0)
    tail = jnp.where(lax.broadcasted_iota(jnp.int32, (V_ROWS - dh, TL), 0) == 0, 1.0, 0.0)
    is_pad = i == 0
    for t in range(G // 2):
        xk, xv, xwk, xwv = [x[:, (s * G + 2 * t) * dh:(s * G + 2 * t + 2) * dh] for s in range(4)]
        xvt, xwvt = xv.T, xwv.T
        for h in range(2):
            g = 2 * t + h
            kk = xk if h == 0 else pltpu.roll(xk, dh, 1)
            kwk = xwk if h == 0 else pltpu.roll(xwk, dh, 1)
            ks_ref[g] = jnp.where(lane < dh, kk, onehot).astype(BF16)
            kw_ref[g] = jnp.where(is_pad, padrow, jnp.where(lane < dh, kwk, 0.0)).astype(BF16)
            vst_ref[g] = jnp.concatenate([xvt[h * dh:(h + 1) * dh], tail], axis=0).astype(BF16)
            vw = jnp.concatenate([xwvt[h * dh:(h + 1) * dh], tail], axis=0)
            vwt_ref[g] = jnp.where(is_pad, 0.0, vw).astype(BF16)


def _kv_prep(proj, B, L):
    G, dh, TL = NSA_GROUPS, NSA_HEAD_DIM, WINDOW
    nt = L // TL
    wcols = 4 * G * dh
    assert L // SEL_BLOCK <= dh and COL_KV % wcols == 0 and L % TL == 0
    data = lambda b, i: jnp.maximum(i - 1, 0)
    k_shape = jax.ShapeDtypeStruct((B, G, L, 2 * dh), BF16)
    v_shape = jax.ShapeDtypeStruct((B, G, V_ROWS, L), BF16)
    kw_shape = jax.ShapeDtypeStruct((B, G, L + WINDOW, 2 * dh), BF16)
    vw_shape = jax.ShapeDtypeStruct((B, G, V_ROWS, L + WINDOW), BF16)
    return pl.pallas_call(
        functools.partial(_kvprep_kernel, TL=TL),
        grid=(B, nt + 1),
        in_specs=[pl.BlockSpec((TL, wcols), lambda b, i: (b * nt + data(b, i), COL_KV // wcols))],
        out_specs=[pl.BlockSpec((None, G, TL, 2 * dh), lambda b, i: (b, 0, data(b, i), 0)),
                   pl.BlockSpec((None, G, V_ROWS, TL), lambda b, i: (b, 0, 0, data(b, i))),
                   pl.BlockSpec((None, G, TL, 2 * dh), lambda b, i: (b, 0, i, 0)),
                   pl.BlockSpec((None, G, V_ROWS, TL), lambda b, i: (b, 0, 0, i))],
        out_shape=[k_shape, v_shape, kw_shape, vw_shape],
        compiler_params=_cparams(("parallel", "arbitrary")),
        name="kv_prep",
    )(proj)


SEL_TILE = 512
SEL_TABLE_FAR = 640
NSA_PROBLEMS = 2


def _tree(op, parts):
    while len(parts) > 1:
        parts = [op(parts[i], parts[i + 1]) if i + 1 < len(parts) else parts[i] for i in range(0, len(parts), 2)]
    return parts[0]


def _fold8(op, x):
    return _tree(op, [x[k:k + 8] for k in range(0, x.shape[0], 8)])


def _safe_inv(l):
    return jnp.where(l > 0.0, 1.0 / jnp.where(l > 0.0, l, 1.0), 0.0)


def _nsa_kernel(q_ref, za_ref, ng_ref, kc_ref, vct_ref, ks_ref, vst_ref, kw_ref, vwt_ref,
                tc_ref, tw_ref, ts_ref, ovt_ref, o_ref, s_ref, acc_ref, gt_ref, *, L):
    R, dh, QB = NSA_REP, NSA_HEAD_DIM, Q_BLOCK
    ncp = L // CMP_STRIDE
    nsel = L // SEL_BLOCK
    W = R * QB
    g = pl.program_id(1)
    qb = pl.program_id(2)

    NP = q_ref.shape[0]

    def front(p):
        qt = (q_ref[p].astype(F32) * (dh ** -0.5)).T
        qT = jnp.concatenate([qt[r * dh:(r + 1) * dh] for r in range(R)], axis=1).astype(BF16)

        c_off = pl.multiple_of((ncp - 8) - 8 * qb, 8)
        sc = jnp.dot(kc_ref[p], qT, preferred_element_type=F32) + tc_ref[pl.ds(c_off, ncp), :]
        m = jnp.maximum(jnp.max(_fold8(jnp.maximum, sc), axis=0, keepdims=True), 0.1 * MASK_VALUE)
        e = jnp.exp(sc - m)
        l = jnp.sum(_fold8(jnp.add, e), axis=0, keepdims=True)
        p_c = e * _safe_inv(l)
        oc = jnp.dot(vct_ref[p], p_c.astype(BF16), preferred_element_type=F32)

        psum = _tree(jnp.add, [p_c[:, r * QB:(r + 1) * QB] for r in range(R)])
        p_hi = psum.astype(BF16)
        p_lo = (psum - p_hi.astype(F32)).astype(BF16)
        ovt = ovt_ref[...]
        imp = jnp.dot(ovt, p_hi, preferred_element_type=F32) + jnp.dot(ovt, p_lo, preferred_element_type=F32)
        jj = lax.broadcasted_iota(jnp.int32, (nsel, QB), 0)
        ii = lax.broadcasted_iota(jnp.int32, (nsel, QB), 1)
        cur = 2 * qb + (ii >= SEL_BLOCK).astype(jnp.int32)
        forced = (jj == 0) | (jj == cur) | (jj == cur - 1)
        imp = jnp.where(forced, FORCE_VALUE, imp)
        imp = jnp.where(jj <= cur, imp, MASK_VALUE)
        jf = jj.astype(F32)
        sel = jnp.zeros((nsel, QB), F32)
        for _ in range(min(SEL_TOPK, nsel)):
            mx = jnp.max(_fold8(jnp.maximum, imp), axis=0, keepdims=True)
            idx = jnp.min(_fold8(jnp.minimum, jnp.where(imp == mx, jf, 1e9)), axis=0, keepdims=True)
            hit = jf == idx
            sel = jnp.where(hit & (mx > 0.1 * MASK_VALUE), 1.0, sel)
            imp = jnp.where(hit, -jnp.inf, imp)
        selbias = jnp.where(sel > 0.5, 0.0, MASK_VALUE).astype(BF16)
        sel_rows = [jnp.concatenate([selbias] * R, axis=1)]
        if nsel < dh:
            sel_rows.append(jnp.zeros((dh - nsel, W), BF16))
        q_sel = jnp.concatenate([qT] + sel_rows, axis=0)

        w0 = pl.multiple_of(qb * QB, QB)
        flag = jnp.where(lax.broadcasted_iota(jnp.int32, (dh, W), 0) == 0, MASK_VALUE, 0.0).astype(BF16)
        q_win = jnp.concatenate([qT, flag], axis=0)
        sw = jnp.dot(kw_ref[p, pl.ds(w0, WINDOW + QB), :], q_win, preferred_element_type=F32) + tw_ref[...]
        m_w = jnp.max(_fold8(jnp.maximum, sw), axis=0, keepdims=True)
        pw = jnp.exp((sw - m_w).astype(BF16))
        accw = jnp.dot(vwt_ref[p, :, pl.ds(w0, WINDOW + QB)], pw, preferred_element_type=F32)
        owin = accw[:dh] * _safe_inv(accw[dh:dh + 1])

        gt_ref[p] = jax.nn.sigmoid(ng_ref[p].astype(F32)).T

        def gate_row(branch):
            return jnp.concatenate([gt_ref[p, pl.ds(3 * (g * R + r) + branch, 1), :] for r in range(R)], axis=1)

        return q_sel, oc * gate_row(0) + owin * gate_row(2), gate_row(1)

    fronts = [front(p) for p in range(NP)]

    npair = (qb // (SEL_TILE // QB) + 2) // 2

    def pass1(u, maccs):
        maccs = list(maccs)
        for h in range(2):
            t = 2 * u + h
            k0 = pl.multiple_of(t * SEL_TILE, SEL_TILE)
            x0 = pl.multiple_of(jnp.maximum(t * SEL_TILE - qb * QB + SEL_TABLE_FAR, 0), QB)
            for p in range(NP):
                s = (jnp.dot(ks_ref[p, pl.ds(k0, SEL_TILE), :], fronts[p][0], preferred_element_type=F32)
                     + ts_ref[pl.ds(x0, SEL_TILE), :])
                s_ref[p, pl.ds(k0, SEL_TILE), :] = s
                maccs[p] = jnp.maximum(maccs[p], _fold8(jnp.maximum, s))
        return tuple(maccs)

    maccs = lax.fori_loop(0, npair, pass1, tuple(jnp.full((8, W), NEVER_VALUE, F32) for _ in range(NP)))
    m_s = [jnp.max(mc, axis=0, keepdims=True) for mc in maccs]

    acc_ref[...] = jnp.zeros_like(acc_ref)

    def pass2(u, carry):
        for p in range(NP):
            part = []
            for h in range(2):
                k0 = pl.multiple_of((2 * u + h) * SEL_TILE, SEL_TILE)
                pr = jnp.exp((s_ref[p, pl.ds(k0, SEL_TILE), :] - m_s[p]).astype(BF16))
                part.append(jnp.dot(vst_ref[p, :, pl.ds(k0, SEL_TILE)], pr, preferred_element_type=F32))
            acc_ref[p] += part[0] + part[1]
        return carry

    lax.fori_loop(0, npair, pass2, 0)

    for p in range(NP):
        acc = acc_ref[p]
        osel = acc[:dh] * _safe_inv(acc[dh:dh + 1])
        ot = fronts[p][1] + osel * fronts[p][2]
        o = jnp.concatenate([ot[:, r * QB:(r + 1) * QB] for r in range(R)], axis=0).T
        za = za_ref[p].astype(F32)
        o_ref[p] = (o * (za * jax.nn.sigmoid(za))).astype(o_ref.dtype)


def _nsa_attend(proj, kc, vct, ks, vst, kw, vwt, tc, tw, ts, ovt, B, L):
    R, dh, G, QB = NSA_REP, NSA_HEAD_DIM, NSA_GROUPS, Q_BLOCK
    assert L % (2 * SEL_TILE) == 0
    nqb = L // QB
    gw = R * dh
    W = R * QB
    NP = NSA_PROBLEMS if B % NSA_PROBLEMS == 0 else 1
    proj4 = proj.reshape(B // NP, NP, L, proj.shape[-1])
    tok = lambda col: pl.BlockSpec((None, NP, QB, col[1]), lambda b, g, i: (b, 0, i, col[0] // col[1] + col[2] * g))
    bg = lambda a: pl.BlockSpec((NP, None) + a.shape[2:], lambda b, g, i: (b, g, 0, 0))
    grp = lambda a: pl.BlockSpec((None,) + a.shape[1:], lambda b, g, i: (g, 0, 0), pipeline_mode=pl.Buffered(1))
    out = pl.pallas_call(
        functools.partial(_nsa_kernel, L=L),
        grid=(B // NP, G, nqb),
        in_specs=[tok((COL_Q, gw, 1)), tok((COL_ZA, gw, 1)), tok((COL_NG, 128, 0)),
                  bg(kc), bg(vct), bg(ks), bg(vst), bg(kw), bg(vwt), grp(tc), grp(tw), grp(ts),
                  pl.BlockSpec(ovt.shape, lambda b, g, i: (0, 0))],
        out_specs=pl.BlockSpec((None, NP, QB, gw), lambda b, g, i: (b, 0, i, g)),
        out_shape=jax.ShapeDtypeStruct((B // NP, NP, L, NSA_WIDTH), BF16),
        scratch_shapes=[pltpu.VMEM((NP, L, W), F32), pltpu.VMEM((NP, V_ROWS, W), F32),
                        pltpu.VMEM((NP, 128, QB), F32)],
        compiler_params=_cparams(("parallel", "parallel", "arbitrary")),
        name="nsa_attend",
    )(proj4, proj4, proj4, kc, vct, ks, vst, kw, vwt, tc, tw, ts, ovt)
    return out.reshape(B * L, NSA_WIDTH)


def _t5_bucket(dist):
    n = jnp.maximum(dist, 0)
    max_exact = REL_BUCKETS // 2
    nf = jnp.maximum(n, max_exact).astype(F32)
    large = max_exact + (jnp.log(nf / max_exact) / math.log(REL_MAX_DIST / max_exact)
                         * (REL_BUCKETS - max_exact)).astype(jnp.int32)
    large = jnp.minimum(large, REL_BUCKETS - 1)
    return jnp.where(n < max_exact, n, large)


def _toeplitz(f, d0, base, step, n_rows, width):
    rpb = width // step
    nblk = n_rows // rpb
    assert rpb * step == width and nblk * rpb == n_rows
    lo = base - width * nblk - d0
    seg = f[..., lo:lo + width * (nblk + 1)].reshape(f.shape[:-1] + (nblk + 1, width))[..., ::-1, :]
    win = jnp.concatenate([seg[..., :-1, :], seg[..., 1:, :]], axis=-1)
    t = jnp.tile(win, rpb)[..., :rpb * (2 * width - step)].reshape(win.shape[:-1] + (rpb, 2 * width - step))
    return t[..., :width].reshape(f.shape[:-1] + (n_rows, width))


def _nsa_tables(rel_bias, L):
    QB, G, R = Q_BLOCK, NSA_GROUPS, NSA_REP
    ncp = L // CMP_STRIDE
    nsel = L // SEL_BLOCK
    npos = L + QB
    tbl = rel_bias.astype(F32)
    bpos = (tbl[_t5_bucket(jnp.arange(npos, dtype=jnp.int32))] - tbl[REL_BUCKETS - 1][None, :]).T
    f = jnp.concatenate([jnp.full((NSA_HEADS, npos), MASK_VALUE, F32), bpos], axis=-1)
    d = np.arange(-npos, npos)
    fw = jnp.where(jnp.asarray(d < WINDOW), f, MASK_VALUE)
    tc = _toeplitz(f, -npos, L - QB - CMP_BLOCK + 1, CMP_STRIDE, 2 * ncp - 8, QB)
    tw = _toeplitz(fw, -npos, WINDOW, 1, WINDOW + QB, QB)
    ts = _toeplitz(f, -npos, SEL_TABLE_FAR, 1, SEL_TABLE_FAR + 2 * SEL_TILE, QB)
    lanes = lambda t: t.reshape(G, R, t.shape[1], QB).transpose(0, 2, 1, 3).reshape(G, t.shape[1], R * QB)
    c = np.arange(ncp)[None, :]
    j = np.arange(nsel)[:, None]
    ovt = ((c - 4 * j >= -1) & (c - 4 * j <= 3)).astype(np.float32)
    return lanes(tc), lanes(tw), lanes(ts), jnp.asarray(ovt, BF16)


S5_LANES = 128
S5_GT = S5_LANES // SSM_GROUP
S5_SW = S5_GT * SSM_STATE
S5_COLS = 4 * S5_LANES


def _s5_kernel(us_ref, kt_ref, zt_ref, cl_ref, lam_ref, ex_ref, y_ref,
               xs_ref, u8_ref, m8_ref, ws8_ref, wo8_ref, *, nk):
    T, C, P, GT, LT, SW = SSM_CHUNK, SSM_GROUP, SSM_STATE, S5_GT, S5_LANES, S5_SW

    @pl.when(pl.program_id(1) == 0)
    def _build():
        tile = lambda x: jnp.concatenate([x] * GT, axis=0)
        r = lax.broadcasted_iota(jnp.int32, (LT, LT), 0)
        c = lax.broadcasted_iota(jnp.int32, (LT, LT), 1)
        same = (r // C) == (c // C)
        m8_ref[...] = jnp.zeros_like(m8_ref)
        for j in range(T):
            bd = jnp.where(same, tile(kt_ref[j]), 0.0).astype(BF16)
            for b in range(T - j):
                m8_ref[b * LT:(b + 1) * LT, (b + j) * LT:(b + j + 1) * LT] = bd
        r = lax.broadcasted_iota(jnp.int32, (LT, 2 * SW), 0)
        c = lax.broadcasted_iota(jnp.int32, (LT, 2 * SW), 1)
        same = (r // C) == ((c % SW) // P)
        for b in range(T):
            ws8_ref[b * LT:(b + 1) * LT, :] = jnp.where(same, tile(zt_ref[b]), 0.0).astype(BF16)
        r = lax.broadcasted_iota(jnp.int32, (2 * SW, LT), 0)
        c = lax.broadcasted_iota(jnp.int32, (2 * SW, LT), 1)
        same = ((r % SW) // P) == (c // C)
        cl = cl_ref[...]
        for a in range(T):
            blk = jnp.dot(cl, ex_ref[a], preferred_element_type=F32)
            wo8_ref[:, a * LT:(a + 1) * LT] = jnp.where(same, blk, 0.0).astype(BF16)

    xs_ref[...] = us_ref[...].astype(F32)
    for b in range(T):
        u8_ref[:, b * LT:(b + 1) * LT] = xs_ref[pl.ds(b, nk, stride=T), :].astype(BF16)
    u8 = u8_ref[...]
    s = jnp.dot(u8, ws8_ref[...], preferred_element_type=F32)
    hr, hi = s[:, :SW], s[:, SW:]
    kidx = lax.broadcasted_iota(jnp.int32, (nk, SW), 0)
    d, step = 1, 0
    while d < nk:
        lr = lam_ref[step:step + 1, :SW]
        li = lam_ref[step:step + 1, SW:]
        keep = kidx >= d
        sr = jnp.where(keep, pltpu.roll(hr, d, 0), 0.0)
        si = jnp.where(keep, pltpu.roll(hi, d, 0), 0.0)
        hr, hi = hr + lr * sr - li * si, hi + lr * si + li * sr
        d, step = 2 * d, step + 1
    keep = kidx >= 1
    pr = jnp.where(keep, pltpu.roll(hr, 1, 0), 0.0)
    pi = jnp.where(keep, pltpu.roll(hi, 1, 0), 0.0)
    hcat = jnp.concatenate([pr, pi], axis=-1).astype(BF16)
    for q in range(T * LT // S5_COLS):
        kq = (q + 1) * S5_COLS
        yq = (jnp.dot(hcat, wo8_ref[:, q * S5_COLS:kq], preferred_element_type=F32)
              + jnp.dot(u8[:, :kq], m8_ref[0:kq, q * S5_COLS:kq], preferred_element_type=F32))
        for a4 in range(S5_COLS // LT):
            a = q * (S5_COLS // LT) + a4
            y_ref[pl.ds(a, nk, stride=T), :] = yq[:, a4 * LT:(a4 + 1) * LT]


def _s5_scan(proj, kt, zt, clc, lamp, ex, B, L):
    T, LT, SW = SSM_CHUNK, S5_LANES, S5_SW
    nk = L // T
    nt = SSM_WIDTH // LT
    per_tile = lambda a: pl.BlockSpec((None,) + a.shape[1:], lambda t, b: (t,) + (0,) * (a.ndim - 1))
    return pl.pallas_call(
        functools.partial(_s5_kernel, nk=nk),
        grid=(nt, B),
        in_specs=[pl.BlockSpec((L, LT), lambda t, b: (b, COL_US // LT + t)),
                  per_tile(kt), per_tile(zt), per_tile(clc), per_tile(lamp),
                  pl.BlockSpec(ex.shape, lambda t, b: (0, 0, 0))],
        out_specs=pl.BlockSpec((L, LT), lambda t, b: (b, t)),
        out_shape=jax.ShapeDtypeStruct((B * L, SSM_WIDTH), F32),
        scratch_shapes=[pltpu.VMEM((L, LT), F32), pltpu.VMEM((nk, T * LT), BF16),
                        pltpu.VMEM((T * LT, T * LT), BF16), pltpu.VMEM((T * LT, 2 * SW), BF16),
                        pltpu.VMEM((2 * SW, T * LT), BF16)],
        compiler_params=_cparams(("parallel", "arbitrary")),
        name="s5_scan",
    )(proj, kt, zt, clc, lamp, ex)


def _cmul(ar, ai, br, bi):
    return ar * br - ai * bi, ar * bi + ai * br


def _s5_tables(a_re, a_im, log_dt, b_re, b_im, c_re, c_im, nk):
    T, P, C, Gs = SSM_CHUNK, SSM_STATE, SSM_GROUP, SSM_GROUPS
    dt = jnp.exp(log_dt.astype(F32))[:, None]
    ar, ai = a_re.astype(F32), a_im.astype(F32)
    mag = jnp.exp(ar * dt)
    lr, li = mag * jnp.cos(ai * dt), mag * jnp.sin(ai * dt)
    den = ar * ar + ai * ai
    nr, ni = lr - 1.0, li
    fr, fi = (nr * ar + ni * ai) / den, (ni * ar - nr * ai) / den
    br, bim = b_re.astype(F32), b_im.astype(F32)
    bbr = fr[..., None] * br - fi[..., None] * bim
    bbi = fr[..., None] * bim + fi[..., None] * br
    jv = jnp.arange(T + 1, dtype=F32)[:, None, None]
    pmag = jnp.exp(jv * (ar * dt)[None])
    pwr, pwi = pmag * jnp.cos(jv * (ai * dt)[None]), pmag * jnp.sin(jv * (ai * dt)[None])
    zr = pwr[..., None] * bbr[None] - pwi[..., None] * bbi[None]
    zi = pwr[..., None] * bbi[None] + pwi[..., None] * bbr[None]
    cr, ci = c_re.astype(F32), c_im.astype(F32)
    kj = (jnp.einsum('gcp,jgpd->gjcd', cr, zr[:T], precision=HIGHEST)
          - jnp.einsum('gcp,jgpd->gjcd', ci, zi[:T], precision=HIGHEST))
    GT, NT = S5_GT, Gs // S5_GT
    kt = kj.reshape(NT, GT, T, C, C).transpose(0, 2, 4, 1, 3).reshape(NT, T, C, GT * C)
    lay_z = lambda z: z[:T][::-1].reshape(T, NT, GT, P, C).transpose(1, 0, 4, 2, 3).reshape(NT, T, C, GT * P)
    zt = jnp.concatenate([lay_z(zr), lay_z(zi)], axis=-1)
    pa_r, pa_i = pwr[1:].transpose(1, 0, 2)[:, :, None, :], pwi[1:].transpose(1, 0, 2)[:, :, None, :]
    clr = cr[:, None] * pa_r - ci[:, None] * pa_i
    cli = cr[:, None] * pa_i + ci[:, None] * pa_r
    lay_c = lambda x: x.reshape(NT, GT, T, C, P).transpose(0, 1, 4, 2, 3).reshape(NT, GT * P, T * C)
    clc = jnp.concatenate([lay_c(clr), lay_c(-cli)], axis=1).astype(BF16)
    qr, qi = pwr[T], pwi[T]
    steps = []
    d = 1
    while d < nk:
        steps.append(jnp.concatenate([qr.reshape(NT, GT * P), qi.reshape(NT, GT * P)], -1))
        qr, qi = _cmul(qr, qi, qr, qi)
        d *= 2
    lamp = jnp.stack(steps, 1)
    col = np.arange(T * C)[:, None]
    lane = np.arange(GT * C)[None, :]
    ex = np.stack([(col // C == a) & (col % C == lane % C) for a in range(T)]).astype(np.float32)
    return kt, zt, clc, lamp, jnp.asarray(ex, BF16)


def _glu_kernel(y_ref, u_ref, zb_ref, d_ref, w_ref, b_ref, o_ref):
    y = y_ref[...] + d_ref[...] * u_ref[...].astype(F32)
    yg = jax.nn.gelu(y).astype(BF16)
    z = jnp.dot(yg, w_ref[...], preferred_element_type=F32) + b_ref[...]
    zb = zb_ref[...].astype(F32)
    o_ref[...] = (yg.astype(F32) * jax.nn.sigmoid(z) * (zb * jax.nn.sigmoid(zb))).astype(o_ref.dtype)


def _s5_glu(y, proj, d_skip, w_glu, b_glu):
    N, W = y.shape
    tm = min(1024, N)
    return pl.pallas_call(
        _glu_kernel,
        grid=(N // tm,),
        in_specs=[pl.BlockSpec((tm, W), lambda i: (i, 0)),
                  pl.BlockSpec((tm, W), lambda i: (i, COL_US // W)),
                  pl.BlockSpec((tm, W), lambda i: (i, COL_ZB // W)),
                  pl.BlockSpec((1, W), lambda i: (0, 0)),
                  pl.BlockSpec((W, W), lambda i: (0, 0)),
                  pl.BlockSpec((1, W), lambda i: (0, 0))],
        out_specs=pl.BlockSpec((tm, W), lambda i: (i, 0)),
        out_shape=jax.ShapeDtypeStruct((N, W), BF16),
        compiler_params=_cparams(("parallel",)),
        name="s5_glu",
    )(y, proj, proj, d_skip.reshape(1, W), w_glu, b_glu.reshape(1, W))


def _merge_kernel(oa_ref, ob_ref, ga_ref, gb_ref, x_ref, gate_ref, wa_ref, wb_ref, wo_ref, lg_ref, lb_ref, o_ref):
    pa = jnp.dot(oa_ref[...], wa_ref[...], preferred_element_type=F32)
    pb = jnp.dot(ob_ref[...], wb_ref[...], preferred_element_type=F32)
    m = jax.nn.sigmoid(ga_ref[...].astype(F32)) * pa + jax.nn.sigmoid(gb_ref[...].astype(F32)) * pb
    y = jnp.dot(m.astype(BF16), wo_ref[...], preferred_element_type=F32)
    r = DEEPNORM_ALPHA * x_ref[...] + gate_ref[...] * y
    mu = jnp.mean(r, axis=-1, keepdims=True)
    rc = r - mu
    var = jnp.mean(rc * rc, axis=-1, keepdims=True)
    o_ref[...] = rc * lax.rsqrt(var + LN_EPS) * lg_ref[...] + lb_ref[...]


def _merge_out(oa, ob, proj, x2, gate, wa, wb, wo, ln_g, ln_b, L):
    N, D = x2.shape
    W = oa.shape[1]
    B = gate.shape[0]
    tm = min(256, L)
    const = lambda i: (0, 0)
    return pl.pallas_call(
        _merge_kernel,
        grid=(N // tm,),
        in_specs=[pl.BlockSpec((tm, W), lambda i: (i, 0)),
                  pl.BlockSpec((tm, W), lambda i: (i, 0)),
                  pl.BlockSpec((tm, D), lambda i: (i, COL_GA // D)),
                  pl.BlockSpec((tm, D), lambda i: (i, COL_GB // D)),
                  pl.BlockSpec((tm, D), lambda i: (i, 0)),
                  pl.BlockSpec((None, 1, D), lambda i: ((i * tm) // L, 0, 0)),
                  pl.BlockSpec((W, D), const),
                  pl.BlockSpec((W, D), const),
                  pl.BlockSpec((D, D), const),
                  pl.BlockSpec((1, D), const),
                  pl.BlockSpec((1, D), const)],
        out_specs=pl.BlockSpec((tm, D), lambda i: (i, 0)),
        out_shape=jax.ShapeDtypeStruct((N, D), F32),
        compiler_params=_cparams(("parallel",)),
        name="merge_out",
    )(oa, ob, proj, proj, x2, gate.reshape(B, 1, D), wa, wb, wo, ln_g.reshape(1, D), ln_b.reshape(1, D))


def _layer(x, c, w_ada, b_ada, w_in, rel_bias, cmp_pos_k, cmp_pos_v, w_cmp_k1, w_cmp_k2, w_cmp_v1, w_cmp_v2,
           ssm_a_re, ssm_a_im, ssm_log_dt, ssm_b_re, ssm_b_im, ssm_c_re, ssm_c_im, ssm_d, w_glu, b_glu,
           w_branch_nsa, w_branch_ssm, w_out, ln_g, ln_b):
    B, L, D = x.shape
    N = B * L
    G, dh = NSA_GROUPS, NSA_HEAD_DIM
    x2 = x.reshape(N, D)

    mod = _ada_mod(c, w_ada, b_ada)
    shift, scale, gate = mod[:, :D], mod[:, D:2 * D], mod[:, 2 * D:]

    o_q, o_kv, o_ng, o_za, o_us, o_zb, o_ga, o_gb = np.cumsum(
        [0, NSA_WIDTH, 6 * KV_WIDTH, 3 * NSA_HEADS, NSA_WIDTH, SSM_WIDTH, SSM_WIDTH, D_MODEL]).tolist()
    w_re = jnp.concatenate([
        w_in[:, o_q:o_kv], w_in[:, o_za:o_us], w_in[:, o_us:o_zb], w_in[:, o_zb:o_ga], w_in[:, o_ga:o_gb],
        w_in[:, o_gb:], w_in[:, o_kv + 2 * KV_WIDTH:o_ng], w_in[:, o_kv:o_kv + 2 * KV_WIDTH], w_in[:, o_ng:o_za],
        jnp.zeros((D, PROJ_COLS - COL_NG - 3 * NSA_HEADS), w_in.dtype)], axis=1).astype(BF16)
    proj = _in_proj(x2, scale, shift, w_re, L)

    w1s = jnp.stack([w_cmp_k1, w_cmp_v1])
    w2s = jnp.stack([w_cmp_k2, w_cmp_v2])
    poss = jnp.broadcast_to(jnp.stack([cmp_pos_k, cmp_pos_v]).reshape(2, 1, CMP_BLOCK * dh), (2, 8, CMP_BLOCK * dh))
    kcv = _nsa_compress(proj, w1s, w2s, poss, B, L)
    kc = kcv[0].astype(BF16)
    vct = kcv[1].swapaxes(-1, -2).astype(BF16)
    ks, vst, kw, vwt = _kv_prep(proj, B, L)
    tc, tw, ts, ovt = _nsa_tables(rel_bias, L)
    o_a = _nsa_attend(proj, kc, vct, ks, vst, kw, vwt, tc, tw, ts, ovt, B, L)

    s5_tabs = _s5_tables(ssm_a_re, ssm_a_im, ssm_log_dt, ssm_b_re, ssm_b_im, ssm_c_re, ssm_c_im, L // SSM_CHUNK)
    y = _s5_scan(proj, *s5_tabs, B, L)
    o_b = _s5_glu(y, proj, ssm_d, w_glu.astype(BF16), b_glu)

    out = _merge_out(o_a, o_b, proj, x2, gate, w_branch_nsa.astype(BF16), w_branch_ssm.astype(BF16),
                     w_out.astype(BF16), ln_g, ln_b, L)
    return out.reshape(B, L, D)


def kernel(x, c, w_ada, b_ada, w_in, rel_bias, cmp_pos_k, cmp_pos_v, w_cmp_k1, w_cmp_k2, w_cmp_v1, w_cmp_v2,
           ssm_a_re, ssm_a_im, ssm_log_dt, ssm_b_re, ssm_b_im, ssm_c_re, ssm_c_im, ssm_d, w_glu, b_glu,
           w_branch_nsa, w_branch_ssm, w_out, ln_g, ln_b):
    for i in range(w_ada.shape[0]):
        x = _layer(x, c, w_ada[i], b_ada[i], w_in[i], rel_bias, cmp_pos_k[i], cmp_pos_v[i], w_cmp_k1[i],
                   w_cmp_k2[i], w_cmp_v1[i], w_cmp_v2[i], ssm_a_re[i], ssm_a_im[i], ssm_log_dt[i], ssm_b_re[i],
                   ssm_b_im[i], ssm_c_re[i], ssm_c_im[i], ssm_d[i], w_glu[i], b_glu[i], w_branch_nsa[i],
                   w_branch_ssm[i], w_out[i], ln_g[i], ln_b[i])
    return x
```

```python
import functools
import math

import numpy as np
import jax
import jax.numpy as jnp
from jax import lax
from jax.experimental import pallas as pl
from jax.experimental.pallas import tpu as pltpu

F32 = jnp.float32
BF16 = jnp.bfloat16
HIGHEST = lax.Precision.HIGHEST

D_MODEL = 2048
NSA_HEADS = 16
NSA_GROUPS = 4
NSA_HEAD_DIM = 64
NSA_REP = NSA_HEADS // NSA_GROUPS
NSA_WIDTH = NSA_HEADS * NSA_HEAD_DIM
KV_WIDTH = NSA_GROUPS * NSA_HEAD_DIM
CMP_BLOCK = 32
CMP_STRIDE = 16
CMP_HIDDEN = 128
SEL_BLOCK = 64
SEL_TOPK = 8
WINDOW = 512
Q_BLOCK = 128
SSM_WIDTH = 1024
SSM_GROUP = 16
SSM_GROUPS = SSM_WIDTH // SSM_GROUP
SSM_STATE = 64
REL_BUCKETS = 32
REL_MAX_DIST = 128
DEEPNORM_ALPHA = 2.0 ** 0.25
LN_EPS = 1e-5
MASK_VALUE = -1e30
FORCE_VALUE = 1e4
NEVER_VALUE = -3e38
SSM_CHUNK = 16

COL_Q, COL_ZA, COL_US, COL_ZB, COL_GA, COL_GB, COL_KV, COL_KC, COL_NG = 0, 1024, 2048, 3072, 4096, 6144, 8192, 9216, 9728
PROJ_COLS = 9856
VMEM_LIMIT = 56 * 1024 * 1024


def _cparams(sem):
    return pltpu.CompilerParams(dimension_semantics=sem, vmem_limit_bytes=VMEM_LIMIT)


def _ada_kernel(c_ref, w_ref, b_ref, o_ref):
    o_ref[...] = jnp.dot(c_ref[...], w_ref[...], preferred_element_type=F32, precision=HIGHEST) + b_ref[...]


def _ada_mod(c, w_ada, b_ada):
    B, D = c.shape
    n = w_ada.shape[1]
    tn = 1536
    return pl.pallas_call(
        _ada_kernel,
        grid=(n // tn,),
        in_specs=[pl.BlockSpec((B, D), lambda j: (0, 0)),
                  pl.BlockSpec((D, tn), lambda j: (0, j)),
                  pl.BlockSpec((1, tn), lambda j: (0, j))],
        out_specs=pl.BlockSpec((B, tn), lambda j: (0, j)),
        out_shape=jax.ShapeDtypeStruct((B, n), F32),
        compiler_params=_cparams(("parallel",)),
        name="ada_mod",
    )(c, w_ada, b_ada.reshape(1, n))


def _inproj_kernel(x_ref, scale_ref, shift_ref, w_ref, o_ref, h_ref):
    @pl.when(pl.program_id(1) == 0)
    def _():
        def rows(k, carry):
            r0 = pl.multiple_of(k * LN_ROWS, LN_ROWS)
            x = x_ref[pl.ds(r0, LN_ROWS), :]
            mu = jnp.mean(x, axis=-1, keepdims=True)
            xc = x - mu
            var = jnp.mean(xc * xc, axis=-1, keepdims=True)
            hn = xc * lax.rsqrt(var + LN_EPS)
            h_ref[pl.ds(r0, LN_ROWS), :] = (hn * (1.0 + scale_ref[...]) + shift_ref[...]).astype(BF16)
            return carry

        lax.fori_loop(0, x_ref.shape[0] // LN_ROWS, rows, 0)

    o_ref[...] = jnp.dot(h_ref[...], w_ref[...], preferred_element_type=F32).astype(o_ref.dtype)


LN_ROWS = 256


def _in_proj(x2, scale, shift, w, L):
    N, D = x2.shape
    ncol = w.shape[1]
    tm = min(1024, L)
    tn = 1408
    assert N % tm == 0 and L % tm == 0 and ncol % tn == 0
    B = scale.shape[0]
    return pl.pallas_call(
        _inproj_kernel,
        grid=(N // tm, ncol // tn),
        in_specs=[pl.BlockSpec((tm, D), lambda i, j: (i, 0)),
                  pl.BlockSpec((None, 1, D), lambda i, j: ((i * tm) // L, 0, 0)),
                  pl.BlockSpec((None, 1, D), lambda i, j: ((i * tm) // L, 0, 0)),
                  pl.BlockSpec((D, tn), lambda i, j: (0, j))],
        out_specs=pl.BlockSpec((tm, tn), lambda i, j: (i, j)),
        out_shape=jax.ShapeDtypeStruct((N, ncol), BF16),
        scratch_shapes=[pltpu.VMEM((tm, D), BF16)],
        compiler_params=_cparams(("parallel", "arbitrary")),
        name="in_proj",
    )(x2, scale.reshape(B, 1, D), shift.reshape(B, 1, D), w)


def _cmp_kernel(x_ref, w1_ref, w1bd_ref, w2_ref, pos_ref, o_ref, xs_ref, *, ncp):
    S, dh = CMP_STRIDE, NSA_HEAD_DIM
    xs_ref[...] = x_ref[...].astype(F32)
    a = jnp.zeros((ncp, 2 * CMP_HIDDEN), F32)
    b = jnp.zeros((ncp, 2 * CMP_HIDDEN), F32)
    for j in range(S):
        xj = xs_ref[pl.ds(j, ncp, stride=S), :].astype(BF16)
        a = a + jnp.dot(xj, w1bd_ref[0, j], preferred_element_type=F32)
        b = b + jnp.dot(xj, w1bd_ref[1, j], preferred_element_type=F32)
    pw = jnp.dot(pos_ref[...], w1_ref[...], preferred_element_type=F32, precision=HIGHEST)[0:1]
    h = a + pltpu.roll(b, ncp - 1, 0) + jnp.concatenate([pw, pw], axis=1)
    w2 = w2_ref[...].astype(BF16)
    for g in range(2):
        hg = jax.nn.gelu(h[:, g * CMP_HIDDEN:(g + 1) * CMP_HIDDEN]).astype(BF16)
        o_ref[g] = jnp.dot(hg, w2, preferred_element_type=F32)


def _nsa_compress(proj, w1s, w2s, poss, B, L):
    G, dh, S = NSA_GROUPS, NSA_HEAD_DIM, CMP_STRIDE
    ncp = L // S
    w = w1s.reshape(2, 2, S, dh, CMP_HIDDEN).astype(BF16)
    z = jnp.zeros_like(w)
    w1bd = jnp.concatenate([jnp.concatenate([w, z], -1), jnp.concatenate([z, w], -1)], axis=-2)
    return pl.pallas_call(
        functools.partial(_cmp_kernel, ncp=ncp),
        grid=(2, B, G // 2),
        in_specs=[pl.BlockSpec((L, 2 * dh), lambda s, b, t: (b, COL_KC // (2 * dh) + 2 * s + t)),
                  pl.BlockSpec((None,) + w1s.shape[1:], lambda s, b, t: (s, 0, 0)),
                  pl.BlockSpec((None,) + w1bd.shape[1:], lambda s, b, t: (s, 0, 0, 0, 0)),
                  pl.BlockSpec((None,) + w2s.shape[1:], lambda s, b, t: (s, 0, 0)),
                  pl.BlockSpec((None,) + poss.shape[1:], lambda s, b, t: (s, 0, 0))],
        out_specs=pl.BlockSpec((None, None, 2, ncp, dh), lambda s, b, t: (s, b, t, 0, 0)),
        out_shape=jax.ShapeDtypeStruct((2, B, G, ncp, dh), F32),
        scratch_shapes=[pltpu.VMEM((L, 2 * dh), F32)],
        compiler_params=_cparams(("parallel", "parallel", "parallel")),
        name="nsa_compress",
    )(proj, w1s, w1bd, w2s, poss)


V_ROWS = NSA_HEAD_DIM + 16


def _kvprep_kernel(x_ref, ks_ref, vst_ref, kw_ref, vwt_ref, *, TL):
    G, dh = NSA_GROUPS, NSA_HEAD_DIM
    i = pl.program_id(1)
    tile = jnp.maximum(i - 1, 0)
    x = x_ref[...].astype(F32)
    lane = lax.broadcasted_iota(jnp.int32, (TL, 2 * dh), 1)
    row = lax.broadcasted_iota(jnp.int32, (TL, 2 * dh), 0) + tile * TL
    onehot = jnp.where(lane - dh == row // SEL_BLOCK, 1.0, 0.0)
    padrow = jnp.where(lane == dh, 1.0, 0.0)
    tail = jnp.where(lax.broadcasted_iota(jnp.int32, (V_ROWS - dh, TL), 0) == 0, 1.0, 0.0)
    is_pad = i == 0
    for t in range(G // 2):
        xk, xv, xwk, xwv = [x[:, (s * G + 2 * t) * dh:(s * G + 2 * t + 2) * dh] for s in range(4)]
        xvt, xwvt = xv.T, xwv.T
        for h in range(2):
            g = 2 * t + h
            kk = xk if h == 0 else pltpu.roll(xk, dh, 1)
            kwk = xwk if h == 0 else pltpu.roll(xwk, dh, 1)
            ks_ref[g] = jnp.where(lane < dh, kk, onehot).astype(BF16)
            kw_ref[g] = jnp.where(is_pad, padrow, jnp.where(lane < dh, kwk, 0.0)).astype(BF16)
            vst_ref[g] = jnp.concatenate([xvt[h * dh:(h + 1) * dh], tail], axis=0).astype(BF16)
            vw = jnp.concatenate([xwvt[h * dh:(h + 1) * dh], tail], axis=0)
            vwt_ref[g] = jnp.where(is_pad, 0.0, vw).astype(BF16)


def _kv_prep(proj, B, L):
    G, dh, TL = NSA_GROUPS, NSA_HEAD_DIM, WINDOW
    nt = L // TL
    wcols = 4 * G * dh
    assert L // SEL_BLOCK <= dh and COL_KV % wcols == 0 and L % TL == 0
    data = lambda b, i: jnp.maximum(i - 1, 0)
    k_shape = jax.ShapeDtypeStruct((B, G, L, 2 * dh), BF16)
    v_shape = jax.ShapeDtypeStruct((B, G, V_ROWS, L), BF16)
    kw_shape = jax.ShapeDtypeStruct((B, G, L + WINDOW, 2 * dh), BF16)
    vw_shape = jax.ShapeDtypeStruct((B, G, V_ROWS, L + WINDOW), BF16)
    return pl.pallas_call(
        functools.partial(_kvprep_kernel, TL=TL),
        grid=(B, nt + 1),
        in_specs=[pl.BlockSpec((TL, wcols), lambda b, i: (b * nt + data(b, i), COL_KV // wcols))],
        out_specs=[pl.BlockSpec((None, G, TL, 2 * dh), lambda b, i: (b, 0, data(b, i), 0)),
                   pl.BlockSpec((None, G, V_ROWS, TL), lambda b, i: (b, 0, 0, data(b, i))),
                   pl.BlockSpec((None, G, TL, 2 * dh), lambda b, i: (b, 0, i, 0)),
                   pl.BlockSpec((None, G, V_ROWS, TL), lambda b, i: (b, 0, 0, i))],
        out_shape=[k_shape, v_shape, kw_shape, vw_shape],
        compiler_params=_cparams(("parallel", "arbitrary")),
        name="kv_prep",
    )(proj)


SEL_TILE = 512
SEL_TABLE_FAR = 640
NSA_PROBLEMS = 2


def _tree(op, parts):
    while len(parts) > 1:
        parts = [op(parts[i], parts[i + 1]) if i + 1 < len(parts) else parts[i] for i in range(0, len(parts), 2)]
    return parts[0]


def _fold8(op, x):
    return _tree(op, [x[k:k + 8] for k in range(0, x.shape[0], 8)])


def _safe_inv(l):
    return jnp.where(l > 0.0, 1.0 / jnp.where(l > 0.0, l, 1.0), 0.0)


def _nsa_kernel(q_ref, za_ref, ng_ref, kc_ref, vct_ref, ks_ref, vst_ref, kw_ref, vwt_ref,
                tc_ref, tw_ref, ts_ref, ovt_ref, o_ref, s_ref, s2_ref, acc_ref, gt_ref, *, L):
    R, dh, QB = NSA_REP, NSA_HEAD_DIM, Q_BLOCK
    ncp = L // CMP_STRIDE
    nsel = L // SEL_BLOCK
    W = R * QB
    g = pl.program_id(1)
    qb = pl.program_id(2)

    NP = q_ref.shape[0]

    def front(p):
        qt = (q_ref[p].astype(F32) * (dh ** -0.5)).T
        qT = jnp.concatenate([qt[r * dh:(r + 1) * dh] for r in range(R)], axis=1).astype(BF16)

        c_off = pl.multiple_of((ncp - 8) - 8 * qb, 8)
        sc = jnp.dot(kc_ref[p], qT, preferred_element_type=F32) + tc_ref[pl.ds(c_off, ncp), :]
        m = jnp.maximum(jnp.max(_fold8(jnp.maximum, sc), axis=0, keepdims=True), 0.1 * MASK_VALUE)
        e = jnp.exp(sc - m)
        l = jnp.sum(_fold8(jnp.add, e), axis=0, keepdims=True)
        p_c = e * _safe_inv(l)
        oc = jnp.dot(vct_ref[p], p_c.astype(BF16), preferred_element_type=F32)

        psum = _tree(jnp.add, [p_c[:, r * QB:(r + 1) * QB] for r in range(R)])
        p_hi = psum.astype(BF16)
        p_lo = (psum - p_hi.astype(F32)).astype(BF16)
        ovt = ovt_ref[...]
        imp = jnp.dot(ovt, p_hi, preferred_element_type=F32) + jnp.dot(ovt, p_lo, preferred_element_type=F32)
        jj = lax.broadcasted_iota(jnp.int32, (nsel, QB), 0)
        ii = lax.broadcasted_iota(jnp.int32, (nsel, QB), 1)
        cur = 2 * qb + (ii >= SEL_BLOCK).astype(jnp.int32)
        forced = (jj == 0) | (jj == cur) | (jj == cur - 1)
        imp = jnp.where(forced, FORCE_VALUE, imp)
        imp = jnp.where(jj <= cur, imp, MASK_VALUE)
        jf = jj.astype(F32)
        sel = jnp.zeros((nsel, QB), F32)
        for _ in range(min(SEL_TOPK, nsel)):
            mx = jnp.max(_fold8(jnp.maximum, imp), axis=0, keepdims=True)
            idx = jnp.min(_fold8(jnp.minimum, jnp.where(imp == mx, jf, 1e9)), axis=0, keepdims=True)
            hit = jf == idx
            sel = jnp.where(hit & (mx > 0.1 * MASK_VALUE), 1.0, sel)
            imp = jnp.where(hit, -jnp.inf, imp)
        selbias = jnp.where(sel > 0.5, 0.0, MASK_VALUE).astype(BF16)
        sel_rows = [jnp.concatenate([selbias] * R, axis=1)]
        if nsel < dh:
            sel_rows.append(jnp.zeros((dh - nsel, W), BF16))
        q_sel = jnp.concatenate([qT] + sel_rows, axis=0)

        w0 = pl.multiple_of(qb * QB, QB)
        flag = jnp.where(lax.broadcasted_iota(jnp.int32, (dh, W), 0) == 0, MASK_VALUE, 0.0).astype(BF16)
        q_win = jnp.concatenate([qT, flag], axis=0)
        sw = jnp.dot(kw_ref[p, pl.ds(w0, WINDOW + QB), :], q_win, preferred_element_type=F32) + tw_ref[...]
        m_w = jnp.max(_fold8(jnp.maximum, sw), axis=0, keepdims=True)
        pw = jnp.exp((sw - m_w).astype(BF16))
        accw = jnp.dot(vwt_ref[p, :, pl.ds(w0, WINDOW + QB)], pw, preferred_element_type=F32)
        owin = accw[:dh] * _safe_inv(accw[dh:dh + 1])

        gt_ref[p] = jax.nn.sigmoid(ng_ref[p].astype(F32)).T

        def gate_row(branch):
            return jnp.concatenate([gt_ref[p, pl.ds(3 * (g * R + r) + branch, 1), :] for r in range(R)], axis=1)

        return q_sel, oc * gate_row(0) + owin * gate_row(2), gate_row(1)

    fronts = [front(p) for p in range(NP)]

    npair = (qb // (SEL_TILE // QB) + 2) // 2
    PAIR = 2 * SEL_TILE

    slots = (s_ref, s2_ref)

    def scores(u, p, slot):
        mx = None
        for h in range(2):
            t = 2 * u + h
            k0 = pl.multiple_of(t * SEL_TILE, SEL_TILE)
            x0 = pl.multiple_of(jnp.maximum(t * SEL_TILE - qb * QB + SEL_TABLE_FAR, 0), QB)
            s = (jnp.dot(ks_ref[p, pl.ds(k0, SEL_TILE), :], fronts[p][0], preferred_element_type=F32)
                 + ts_ref[pl.ds(x0, SEL_TILE), :])
            slots[slot][p, h * SEL_TILE:(h + 1) * SEL_TILE, :] = s
            f = _fold8(jnp.maximum, s)
            mx = f if mx is None else jnp.maximum(mx, f)
        return jnp.max(mx, axis=0, keepdims=True)

    def accumulate(u, p, slot, m_new, m_old):
        part = []
        for h in range(2):
            k0 = pl.multiple_of((2 * u + h) * SEL_TILE, SEL_TILE)
            pr = jnp.exp((slots[slot][p, h * SEL_TILE:(h + 1) * SEL_TILE, :] - m_new).astype(BF16))
            part.append(jnp.dot(vst_ref[p, :, pl.ds(k0, SEL_TILE)], pr, preferred_element_type=F32))
        acc_ref[p] = acc_ref[p] * jnp.exp(m_old - m_new) + part[0] + part[1]

    def phase(u, slot, m_cur, m_acc):
        m_next = []
        for p in range(NP):
            accumulate(u, p, slot, m_cur[p], m_acc[p])
            m_next.append(jnp.maximum(m_cur[p], scores(u + 1, p, 1 - slot)))
        return tuple(m_next), m_cur

    acc_ref[...] = jnp.zeros_like(acc_ref)
    m_first = tuple(scores(0, p, 0) for p in range(NP))

    def trip(v, carry):
        carry = phase(2 * v, 0, *carry)
        return phase(2 * v + 1, 1, *carry)

    n_rest = npair - 1
    m_cur, m_acc = lax.fori_loop(0, n_rest // 2, trip, (m_first, m_first))
    last = 2 * (n_rest // 2)

    @pl.when(n_rest % 2 == 1)
    def _():
        m2, m1 = phase(last, 0, m_cur, m_acc)
        for p in range(NP):
            accumulate(last + 1, p, 1, m2[p], m1[p])

    @pl.when(n_rest % 2 == 0)
    def _():
        for p in range(NP):
            accumulate(last, p, 0, m_cur[p], m_acc[p])

    for p in range(NP):
        acc = acc_ref[p]
        osel = acc[:dh] * _safe_inv(acc[dh:dh + 1])
        ot = fronts[p][1] + osel * fronts[p][2]
        o = jnp.concatenate([ot[:, r * QB:(r + 1) * QB] for r in range(R)], axis=0).T
        za = za_ref[p].astype(F32)
        o_ref[p] = (o * (za * jax.nn.sigmoid(za))).astype(o_ref.dtype)


def _nsa_attend(proj, kc, vct, ks, vst, kw, vwt, tc, tw, ts, ovt, B, L):
    R, dh, G, QB = NSA_REP, NSA_HEAD_DIM, NSA_GROUPS, Q_BLOCK
    assert L % (2 * SEL_TILE) == 0
    nqb = L // QB
    gw = R * dh
    W = R * QB
    NP = NSA_PROBLEMS if B % NSA_PROBLEMS == 0 else 1
    proj4 = proj.reshape(B // NP, NP, L, proj.shape[-1])
    tok = lambda col: pl.BlockSpec((None, NP, QB, col[1]), lambda b, g, i: (b, 0, i, col[0] // col[1] + col[2] * g))
    bg = lambda a: pl.BlockSpec((NP, None) + a.shape[2:], lambda b, g, i: (b, g, 0, 0))
    grp = lambda a: pl.BlockSpec((None,) + a.shape[1:], lambda b, g, i: (g, 0, 0), pipeline_mode=pl.Buffered(1))
    out = pl.pallas_call(
        functools.partial(_nsa_kernel, L=L),
        grid=(B // NP, G, nqb),
        in_specs=[tok((COL_Q, gw, 1)), tok((COL_ZA, gw, 1)), tok((COL_NG, 128, 0)),
                  bg(kc), bg(vct), bg(ks), bg(vst), bg(kw), bg(vwt), grp(tc), grp(tw), grp(ts),
                  pl.BlockSpec(ovt.shape, lambda b, g, i: (0, 0))],
        out_specs=pl.BlockSpec((None, NP, QB, gw), lambda b, g, i: (b, 0, i, g)),
        out_shape=jax.ShapeDtypeStruct((B // NP, NP, L, NSA_WIDTH), BF16),
        scratch_shapes=[pltpu.VMEM((NP, 2 * SEL_TILE, W), F32), pltpu.VMEM((NP, 2 * SEL_TILE, W), F32),
                        pltpu.VMEM((NP, V_ROWS, W), F32),
                        pltpu.VMEM((NP, 128, QB), F32)],
        compiler_params=_cparams(("parallel", "parallel", "arbitrary")),
        name="nsa_attend",
    )(proj4, proj4, proj4, kc, vct, ks, vst, kw, vwt, tc, tw, ts, ovt)
    return out.reshape(B * L, NSA_WIDTH)


def _t5_bucket(dist):
    n = jnp.maximum(dist, 0)
    max_exact = REL_BUCKETS // 2
    nf = jnp.maximum(n, max_exact).astype(F32)
    large = max_exact + (jnp.log(nf / max_exact) / math.log(REL_MAX_DIST / max_exact)
                         * (REL_BUCKETS - max_exact)).astype(jnp.int32)
    large = jnp.minimum(large, REL_BUCKETS - 1)
    return jnp.where(n < max_exact, n, large)


def _toeplitz(f, d0, base, step, n_rows, width):
    rpb = width // step
    nblk = n_rows // rpb
    assert rpb * step == width and nblk * rpb == n_rows
    lo = base - width * nblk - d0
    seg = f[..., lo:lo + width * (nblk + 1)].reshape(f.shape[:-1] + (nblk + 1, width))[..., ::-1, :]
    return jnp.concatenate([seg[..., :-1, :], seg[..., 1:, :]], axis=-1)


TABLE_STEPS = (CMP_STRIDE, 1, 1)


def _tables_kernel(*refs):
    R, QB = NSA_REP, Q_BLOCK
    n = len(TABLE_STEPS)
    tmp_ref = refs[2 * n]
    for win_ref, out_ref, step in zip(refs[:n], refs[n:2 * n], TABLE_STEPS):
        rpb = QB // step

        def block(a, carry, win_ref=win_ref, out_ref=out_ref, step=step, rpb=rpb):
            r0 = pl.multiple_of(a * rpb, rpb)
            for r in range(R):
                x = jnp.broadcast_to(win_ref[r, pl.ds(a, 1), :], (QB, 2 * QB))
                y = pltpu.roll(x, 0, 1, stride=1, stride_axis=0)[:, :QB]
                if step > 1:
                    tmp_ref[...] = y
                    y = tmp_ref[pl.ds(0, rpb, stride=step), :]
                out_ref[pl.ds(r0, rpb), r * QB:(r + 1) * QB] = y
            return carry

        lax.fori_loop(0, win_ref.shape[1], block, 0)


def _expand_tables(wins):
    G, R, QB = NSA_GROUPS, NSA_REP, Q_BLOCK
    rows = [w.shape[1] * (QB // s) for w, s in zip(wins, TABLE_STEPS)]
    return pl.pallas_call(
        _tables_kernel,
        grid=(G,),
        in_specs=[pl.BlockSpec((R,) + w.shape[1:], lambda g: (g, 0, 0)) for w in wins],
        out_specs=[pl.BlockSpec((None, n, R * QB), lambda g: (g, 0, 0)) for n in rows],
        out_shape=[jax.ShapeDtypeStruct((G, n, R * QB), F32) for n in rows],
        scratch_shapes=[pltpu.VMEM((QB, QB), F32)],
        compiler_params=_cparams(("parallel",)),
        name="nsa_tables",
    )(*wins)


def _nsa_tables(rel_bias, L):
    QB, G, R = Q_BLOCK, NSA_GROUPS, NSA_REP
    ncp = L // CMP_STRIDE
    nsel = L // SEL_BLOCK
    npos = L + QB
    tbl = rel_bias.astype(F32)
    bpos = (tbl[_t5_bucket(jnp.arange(npos, dtype=jnp.int32))] - tbl[REL_BUCKETS - 1][None, :]).T
    f = jnp.concatenate([jnp.full((NSA_HEADS, npos), MASK_VALUE, F32), bpos], axis=-1)
    d = np.arange(-npos, npos)
    fw = jnp.where(jnp.asarray(d < WINDOW), f, MASK_VALUE)
    tc = _toeplitz(f, -npos, L - QB - CMP_BLOCK + 1, CMP_STRIDE, 2 * ncp - 8, QB)
    tw = _toeplitz(fw, -npos, WINDOW, 1, WINDOW + QB, QB)
    ts = _toeplitz(f, -npos, SEL_TABLE_FAR, 1, SEL_TABLE_FAR + 2 * SEL_TILE, QB)
    tc, tw, ts = _expand_tables((tc, tw, ts))
    c = np.arange(ncp)[None, :]
    j = np.arange(nsel)[:, None]
    ovt = ((c - 4 * j >= -1) & (c - 4 * j <= 3)).astype(np.float32)
    return tc, tw, ts, jnp.asarray(ovt, BF16)


S5_LANES = 128
S5_GT = S5_LANES // SSM_GROUP
S5_SW = S5_GT * SSM_STATE
S5_COLS = 4 * S5_LANES


def _s5_kernel(us_ref, kt_ref, zt_ref, cl_ref, lam_ref, ex_ref, y_ref,
               xs_ref, u8_ref, m8_ref, ws8_ref, wo8_ref, *, nk):
    T, C, P, GT, LT, SW = SSM_CHUNK, SSM_GROUP, SSM_STATE, S5_GT, S5_LANES, S5_SW

    @pl.when(pl.program_id(1) == 0)
    def _build():
        tile = lambda x: jnp.concatenate([x] * GT, axis=0)
        r = lax.broadcasted_iota(jnp.int32, (LT, LT), 0)
        c = lax.broadcasted_iota(jnp.int32, (LT, LT), 1)
        same = (r // C) == (c // C)
        m8_ref[...] = jnp.zeros_like(m8_ref)
        for j in range(T):
            bd = jnp.where(same, tile(kt_ref[j]), 0.0).astype(BF16)
            for b in range(T - j):
                m8_ref[b * LT:(b + 1) * LT, (b + j) * LT:(b + j + 1) * LT] = bd
        r = lax.broadcasted_iota(jnp.int32, (LT, 2 * SW), 0)
        c = lax.broadcasted_iota(jnp.int32, (LT, 2 * SW), 1)
        same = (r // C) == ((c % SW) // P)
        for b in range(T):
            ws8_ref[b * LT:(b + 1) * LT, :] = jnp.where(same, tile(zt_ref[b]), 0.0).astype(BF16)
        r = lax.broadcasted_iota(jnp.int32, (2 * SW, LT), 0)
        c = lax.broadcasted_iota(jnp.int32, (2 * SW, LT), 1)
        same = ((r % SW) // P) == (c // C)
        cl = cl_ref[...]
        for a in range(T):
            blk = jnp.dot(cl, ex_ref[a], preferred_element_type=F32)
            wo8_ref[:, a * LT:(a + 1) * LT] = jnp.where(same, blk, 0.0).astype(BF16)

    xs_ref[...] = us_ref[...].astype(F32)
    for b in range(T):
        u8_ref[:, b * LT:(b + 1) * LT] = xs_ref[pl.ds(b, nk, stride=T), :].astype(BF16)
    u8 = u8_ref[...]
    s = jnp.dot(u8, ws8_ref[...], preferred_element_type=F32)
    hr, hi = s[:, :SW], s[:, SW:]
    kidx = lax.broadcasted_iota(jnp.int32, (nk, SW), 0)
    d, step = 1, 0
    while d < nk:
        lr = lam_ref[step:step + 1, :SW]
        li = lam_ref[step:step + 1, SW:]
        keep = kidx >= d
        sr = jnp.where(keep, pltpu.roll(hr, d, 0), 0.0)
        si = jnp.where(keep, pltpu.roll(hi, d, 0), 0.0)
        hr, hi = hr + lr * sr - li * si, hi + lr * si + li * sr
        d, step = 2 * d, step + 1
    keep = kidx >= 1
    pr = jnp.where(keep, pltpu.roll(hr, 1, 0), 0.0)
    pi = jnp.where(keep, pltpu.roll(hi, 1, 0), 0.0)
    hcat = jnp.concatenate([pr, pi], axis=-1).astype(BF16)
    for q in range(T * LT // S5_COLS):
        kq = (q + 1) * S5_COLS
        yq = (jnp.dot(hcat, wo8_ref[:, q * S5_COLS:kq], preferred_element_type=F32)
              + jnp.dot(u8[:, :kq], m8_ref[0:kq, q * S5_COLS:kq], preferred_element_type=F32))
        for a4 in range(S5_COLS // LT):
            a = q * (S5_COLS // LT) + a4
            y_ref[pl.ds(a, nk, stride=T), :] = yq[:, a4 * LT:(a4 + 1) * LT]


def _s5_scan(proj, kt, zt, clc, lamp, ex, B, L):
    T, LT, SW = SSM_CHUNK, S5_LANES, S5_SW
    nk = L // T
    nt = SSM_WIDTH // LT
    per_tile = lambda a: pl.BlockSpec((None,) + a.shape[1:], lambda t, b: (t,) + (0,) * (a.ndim - 1))
    return pl.pallas_call(
        functools.partial(_s5_kernel, nk=nk),
        grid=(nt, B),
        in_specs=[pl.BlockSpec((L, LT), lambda t, b: (b, COL_US // LT + t)),
                  per_tile(kt), per_tile(zt), per_tile(clc), per_tile(lamp),
                  pl.BlockSpec(ex.shape, lambda t, b: (0, 0, 0))],
        out_specs=pl.BlockSpec((L, LT), lambda t, b: (b, t)),
        out_shape=jax.ShapeDtypeStruct((B * L, SSM_WIDTH), F32),
        scratch_shapes=[pltpu.VMEM((L, LT), F32), pltpu.VMEM((nk, T * LT), BF16),
                        pltpu.VMEM((T * LT, T * LT), BF16), pltpu.VMEM((T * LT, 2 * SW), BF16),
                        pltpu.VMEM((2 * SW, T * LT), BF16)],
        compiler_params=_cparams(("parallel", "arbitrary")),
        name="s5_scan",
    )(proj, kt, zt, clc, lamp, ex)


def _cmul(ar, ai, br, bi):
    return ar * br - ai * bi, ar * bi + ai * br


def _s5_tables(a_re, a_im, log_dt, b_re, b_im, c_re, c_im, nk):
    T, P, C, Gs = SSM_CHUNK, SSM_STATE, SSM_GROUP, SSM_GROUPS
    dt = jnp.exp(log_dt.astype(F32))[:, None]
    ar, ai = a_re.astype(F32), a_im.astype(F32)
    mag = jnp.exp(ar * dt)
    lr, li = mag * jnp.cos(ai * dt), mag * jnp.sin(ai * dt)
    den = ar * ar + ai * ai
    nr, ni = lr - 1.0, li
    fr, fi = (nr * ar + ni * ai) / den, (ni * ar - nr * ai) / den
    br, bim = b_re.astype(F32), b_im.astype(F32)
    bbr = fr[..., None] * br - fi[..., None] * bim
    bbi = fr[..., None] * bim + fi[..., None] * br
    jv = jnp.arange(T + 1, dtype=F32)[:, None, None]
    pmag = jnp.exp(jv * (ar * dt)[None])
    pwr, pwi = pmag * jnp.cos(jv * (ai * dt)[None]), pmag * jnp.sin(jv * (ai * dt)[None])
    zr = pwr[..., None] * bbr[None] - pwi[..., None] * bbi[None]
    zi = pwr[..., None] * bbi[None] + pwi[..., None] * bbr[None]
    cr, ci = c_re.astype(F32), c_im.astype(F32)
    kj = (jnp.einsum('gcp,jgpd->gjcd', cr, zr[:T], precision=HIGHEST)
          - jnp.einsum('gcp,jgpd->gjcd', ci, zi[:T], precision=HIGHEST))
    GT, NT = S5_GT, Gs // S5_GT
    kt = kj.reshape(NT, GT, T, C, C).transpose(0, 2, 4, 1, 3).reshape(NT, T, C, GT * C)
    lay_z = lambda z: z[:T][::-1].reshape(T, NT, GT, P, C).transpose(1, 0, 4, 2, 3).reshape(NT, T, C, GT * P)
    zt = jnp.concatenate([lay_z(zr), lay_z(zi)], axis=-1)
    pa_r, pa_i = pwr[1:].transpose(1, 0, 2)[:, :, None, :], pwi[1:].transpose(1, 0, 2)[:, :, None, :]
    clr = cr[:, None] * pa_r - ci[:, None] * pa_i
    cli = cr[:, None] * pa_i + ci[:, None] * pa_r
    lay_c = lambda x: x.reshape(NT, GT, T, C, P).transpose(0, 1, 4, 2, 3).reshape(NT, GT * P, T * C)
    clc = jnp.concatenate([lay_c(clr), lay_c(-cli)], axis=1).astype(BF16)
    qr, qi = pwr[T], pwi[T]
    steps = []
    d = 1
    while d < nk:
        steps.append(jnp.concatenate([qr.reshape(NT, GT * P), qi.reshape(NT, GT * P)], -1))
        qr, qi = _cmul(qr, qi, qr, qi)
        d *= 2
    lamp = jnp.stack(steps, 1)
    col = np.arange(T * C)[:, None]
    lane = np.arange(GT * C)[None, :]
    ex = np.stack([(col // C == a) & (col % C == lane % C) for a in range(T)]).astype(np.float32)
    return kt, zt, clc, lamp, jnp.asarray(ex, BF16)


def _glu_kernel(y_ref, u_ref, zb_ref, d_ref, w_ref, b_ref, o_ref):
    y = y_ref[...] + d_ref[...] * u_ref[...].astype(F32)
    yg = jax.nn.gelu(y).astype(BF16)
    z = jnp.dot(yg, w_ref[...], preferred_element_type=F32) + b_ref[...]
    zb = zb_ref[...].astype(F32)
    o_ref[...] = (yg.astype(F32) * jax.nn.sigmoid(z) * (zb * jax.nn.sigmoid(zb))).astype(o_ref.dtype)


def _s5_glu(y, proj, d_skip, w_glu, b_glu):
    N, W = y.shape
    tm = min(1024, N)
    return pl.pallas_call(
        _glu_kernel,
        grid=(N // tm,),
        in_specs=[pl.BlockSpec((tm, W), lambda i: (i, 0)),
                  pl.BlockSpec((tm, W), lambda i: (i, COL_US // W)),
                  pl.BlockSpec((tm, W), lambda i: (i, COL_ZB // W)),
                  pl.BlockSpec((1, W), lambda i: (0, 0)),
                  pl.BlockSpec((W, W), lambda i: (0, 0)),
                  pl.BlockSpec((1, W), lambda i: (0, 0))],
        out_specs=pl.BlockSpec((tm, W), lambda i: (i, 0)),
        out_shape=jax.ShapeDtypeStruct((N, W), BF16),
        compiler_params=_cparams(("parallel",)),
        name="s5_glu",
    )(y, proj, proj, d_skip.reshape(1, W), w_glu, b_glu.reshape(1, W))


def _merge_kernel(oa_ref, ob_ref, ga_ref, gb_ref, x_ref, gate_ref, wa_ref, wb_ref, wo_ref, lg_ref, lb_ref, o_ref):
    pa = jnp.dot(oa_ref[...], wa_ref[...], preferred_element_type=F32)
    pb = jnp.dot(ob_ref[...], wb_ref[...], preferred_element_type=F32)
    m = jax.nn.sigmoid(ga_ref[...].astype(F32)) * pa + jax.nn.sigmoid(gb_ref[...].astype(F32)) * pb
    y = jnp.dot(m.astype(BF16), wo_ref[...], preferred_element_type=F32)
    r = DEEPNORM_ALPHA * x_ref[...] + gate_ref[...] * y
    mu = jnp.mean(r, axis=-1, keepdims=True)
    rc = r - mu
    var = jnp.mean(rc * rc, axis=-1, keepdims=True)
    o_ref[...] = rc * lax.rsqrt(var + LN_EPS) * lg_ref[...] + lb_ref[...]


def _merge_out(oa, ob, proj, x2, gate, wa, wb, wo, ln_g, ln_b, L):
    N, D = x2.shape
    W = oa.shape[1]
    B = gate.shape[0]
    tm = min(256, L)
    const = lambda i: (0, 0)
    return pl.pallas_call(
        _merge_kernel,
        grid=(N // tm,),
        in_specs=[pl.BlockSpec((tm, W), lambda i: (i, 0)),
                  pl.BlockSpec((tm, W), lambda i: (i, 0)),
                  pl.BlockSpec((tm, D), lambda i: (i, COL_GA // D)),
                  pl.BlockSpec((tm, D), lambda i: (i, COL_GB // D)),
                  pl.BlockSpec((tm, D), lambda i: (i, 0)),
                  pl.BlockSpec((None, 1, D), lambda i: ((i * tm) // L, 0, 0)),
                  pl.BlockSpec((W, D), const),
                  pl.BlockSpec((W, D), const),
                  pl.BlockSpec((D, D), const),
                  pl.BlockSpec((1, D), const),
                  pl.BlockSpec((1, D), const)],
        out_specs=pl.BlockSpec((tm, D), lambda i: (i, 0)),
        out_shape=jax.ShapeDtypeStruct((N, D), F32),
        compiler_params=_cparams(("parallel",)),
        name="merge_out",
    )(oa, ob, proj, proj, x2, gate.reshape(B, 1, D), wa, wb, wo, ln_g.reshape(1, D), ln_b.reshape(1, D))


def _layer(x, c, w_ada, b_ada, w_in, rel_bias, cmp_pos_k, cmp_pos_v, w_cmp_k1, w_cmp_k2, w_cmp_v1, w_cmp_v2,
           ssm_a_re, ssm_a_im, ssm_log_dt, ssm_b_re, ssm_b_im, ssm_c_re, ssm_c_im, ssm_d, w_glu, b_glu,
           w_branch_nsa, w_branch_ssm, w_out, ln_g, ln_b):
    B, L, D = x.shape
    N = B * L
    G, dh = NSA_GROUPS, NSA_HEAD_DIM
    x2 = x.reshape(N, D)

    mod = _ada_mod(c, w_ada, b_ada)
    shift, scale, gate = mod[:, :D], mod[:, D:2 * D], mod[:, 2 * D:]

    o_q, o_kv, o_ng, o_za, o_us, o_zb, o_ga, o_gb = np.cumsum(
        [0, NSA_WIDTH, 6 * KV_WIDTH, 3 * NSA_HEADS, NSA_WIDTH, SSM_WIDTH, SSM_WIDTH, D_MODEL]).tolist()
    w_re = jnp.concatenate([
        w_in[:, o_q:o_kv], w_in[:, o_za:o_us], w_in[:, o_us:o_zb], w_in[:, o_zb:o_ga], w_in[:, o_ga:o_gb],
        w_in[:, o_gb:], w_in[:, o_kv + 2 * KV_WIDTH:o_ng], w_in[:, o_kv:o_kv + 2 * KV_WIDTH], w_in[:, o_ng:o_za],
        jnp.zeros((D, PROJ_COLS - COL_NG - 3 * NSA_HEADS), w_in.dtype)], axis=1).astype(BF16)
    proj = _in_proj(x2, scale, shift, w_re, L)

    w1s = jnp.stack([w_cmp_k1, w_cmp_v1])
    w2s = jnp.stack([w_cmp_k2, w_cmp_v2])
    poss = jnp.broadcast_to(jnp.stack([cmp_pos_k, cmp_pos_v]).reshape(2, 1, CMP_BLOCK * dh), (2, 8, CMP_BLOCK * dh))
    kcv = _nsa_compress(proj, w1s, w2s, poss, B, L)
    kc = kcv[0].astype(BF16)
    vct = kcv[1].swapaxes(-1, -2).astype(BF16)
    ks, vst, kw, vwt = _kv_prep(proj, B, L)
    tc, tw, ts, ovt = _nsa_tables(rel_bias, L)
    o_a = _nsa_attend(proj, kc, vct, ks, vst, kw, vwt, tc, tw, ts, ovt, B, L)

    s5_tabs = _s5_tables(ssm_a_re, ssm_a_im, ssm_log_dt, ssm_b_re, ssm_b_im, ssm_c_re, ssm_c_im, L // SSM_CHUNK)
    y = _s5_scan(proj, *s5_tabs, B, L)
    o_b = _s5_glu(y, proj, ssm_d, w_glu.astype(BF16), b_glu)

    out = _merge_out(o_a, o_b, proj, x2, gate, w_branch_nsa.astype(BF16), w_branch_ssm.astype(BF16),
                     w_out.astype(BF16), ln_g, ln_b, L)
    return out.reshape(B, L, D)


def kernel(x, c, w_ada, b_ada, w_in, rel_bias, cmp_pos_k, cmp_pos_v, w_cmp_k1, w_cmp_k2, w_cmp_v1, w_cmp_v2,
           ssm_a_re, ssm_a_im, ssm_log_dt, ssm_b_re, ssm_b_im, ssm_c_re, ssm_c_im, ssm_d, w_glu, b_glu,
           w_branch_nsa, w_branch_ssm, w_out, ln_g, ln_b):
    for i in range(w_ada.shape[0]):
        x = _layer(x, c, w_ada[i], b_ada[i], w_in[i], rel_bias, cmp_pos_k[i], cmp_pos_v[i], w_cmp_k1[i],
                   w_cmp_k2[i], w_cmp_v1[i], w_cmp_v2[i], ssm_a_re[i], ssm_a_im[i], ssm_log_dt[i], ssm_b_re[i],
                   ssm_b_im[i], ssm_c_re[i], ssm_c_im[i], ssm_d[i], w_glu[i], b_glu[i], w_branch_nsa[i],
                   w_branch_ssm[i], w_out[i], ln_g[i], ln_b[i])
    return x
```

```python
import functools
import math

import numpy as np
import jax
import jax.numpy as jnp
from jax import lax
from jax.experimental import pallas as pl
from jax.experimental.pallas import tpu as pltpu

F32 = jnp.float32
BF16 = jnp.bfloat16
HIGHEST = lax.Precision.HIGHEST

D_MODEL = 2048
NSA_HEADS = 16
NSA_GROUPS = 4
NSA_HEAD_DIM = 64
NSA_REP = NSA_HEADS // NSA_GROUPS
NSA_WIDTH = NSA_HEADS * NSA_HEAD_DIM
KV_WIDTH = NSA_GROUPS * NSA_HEAD_DIM
CMP_BLOCK = 32
CMP_STRIDE = 16
CMP_HIDDEN = 128
SEL_BLOCK = 64
SEL_TOPK = 8
WINDOW = 512
Q_BLOCK = 128
SSM_WIDTH = 1024
SSM_GROUP = 16
SSM_GROUPS = SSM_WIDTH // SSM_GROUP
SSM_STATE = 64
REL_BUCKETS = 32
REL_MAX_DIST = 128
DEEPNORM_ALPHA = 2.0 ** 0.25
LN_EPS = 1e-5
MASK_VALUE = -1e30
FORCE_VALUE = 1e4
NEVER_VALUE = -3e38
SSM_CHUNK = 16

COL_Q, COL_ZA, COL_US, COL_ZB, COL_GA, COL_GB, COL_KV, COL_KC, COL_NG = 0, 1024, 2048, 3072, 4096, 6144, 8192, 9216, 9728
PROJ_COLS = 9856
VMEM_LIMIT = 56 * 1024 * 1024


def _cparams(sem):
    return pltpu.CompilerParams(dimension_semantics=sem, vmem_limit_bytes=VMEM_LIMIT)


def _ada_kernel(c_ref, w_ref, b_ref, o_ref):
    o_ref[...] = jnp.dot(c_ref[...], w_ref[...], preferred_element_type=F32, precision=HIGHEST) + b_ref[...]


def _ada_mod(c, w_ada, b_ada):
    B, D = c.shape
    n = w_ada.shape[1]
    tn = 1536
    return pl.pallas_call(
        _ada_kernel,
        grid=(n // tn,),
        in_specs=[pl.BlockSpec((B, D), lambda j: (0, 0)),
                  pl.BlockSpec((D, tn), lambda j: (0, j)),
                  pl.BlockSpec((1, tn), lambda j: (0, j))],
        out_specs=pl.BlockSpec((B, tn), lambda j: (0, j)),
        out_shape=jax.ShapeDtypeStruct((B, n), F32),
        compiler_params=_cparams(("parallel",)),
        name="ada_mod",
    )(c, w_ada, b_ada.reshape(1, n))


LANES = 128
W_TILE_COPY, W_TILE_SHIFT, W_TILE_NG = 0, 1, 2


def _w_in_tiles():
    src_kv, src_ng = NSA_WIDTH, NSA_WIDTH + 6 * KV_WIDTH
    src_za = src_ng + 3 * NSA_HEADS
    kind, src = [], []
    for j in range(PROJ_COLS // LANES):
        c = j * LANES
        if c < COL_ZA:
            k, s = W_TILE_COPY, c
        elif c < COL_KV:
            k, s = W_TILE_SHIFT, src_za + (c - COL_ZA)
        elif c < COL_KC:
            k, s = W_TILE_COPY, src_kv + 2 * KV_WIDTH + (c - COL_KV)
        elif c < COL_NG:
            k, s = W_TILE_COPY, src_kv + (c - COL_KC)
        else:
            k, s = W_TILE_NG, src_ng
        assert (s % LANES == 0) == (k != W_TILE_SHIFT)
        kind.append(k)
        src.append(s // LANES)
    off = src_za % LANES
    return np.asarray(kind, np.int32), np.asarray(src, np.int32), off


def _wperm_kernel(kind_ref, src_ref, a_ref, b_ref, o_ref, *, off):
    k = kind_ref[pl.program_id(0)]

    @pl.when(k == W_TILE_COPY)
    def _():
        o_ref[...] = a_ref[...].astype(BF16)

    @pl.when(k == W_TILE_SHIFT)
    def _():
        both = jnp.concatenate([a_ref[...], b_ref[...]], axis=1)
        o_ref[...] = both[:, off:off + LANES].astype(BF16)

    @pl.when(k == W_TILE_NG)
    def _():
        lane = lax.broadcasted_iota(jnp.int32, a_ref.shape, 1)
        o_ref[...] = jnp.where(lane < 3 * NSA_HEADS, a_ref[...], 0.0).astype(BF16)


def _permute_w_in(w_in):
    D = w_in.shape[0]
    kind, src, off = _w_in_tiles()
    last = pl.cdiv(w_in.shape[1], LANES) - 1
    return pl.pallas_call(
        functools.partial(_wperm_kernel, off=off),
        grid_spec=pltpu.PrefetchScalarGridSpec(
            num_scalar_prefetch=2,
            grid=(PROJ_COLS // LANES,),
            in_specs=[pl.BlockSpec((D, LANES), lambda j, kind, src: (0, src[j])),
                      pl.BlockSpec((D, LANES), lambda j, kind, src: (0, jnp.minimum(src[j] + 1, last)))],
            out_specs=pl.BlockSpec((D, LANES), lambda j, kind, src: (0, j))),
        out_shape=jax.ShapeDtypeStruct((D, PROJ_COLS), BF16),
        compiler_params=_cparams(("arbitrary",)),
        name="w_in_layout",
    )(jnp.asarray(kind), jnp.asarray(src), w_in, w_in)


def _inproj_kernel(x_ref, scale_ref, shift_ref, w_ref, o_ref, h_ref):
    @pl.when(pl.program_id(1) == 0)
    def _():
        def rows(k, carry):
            r0 = pl.multiple_of(k * LN_ROWS, LN_ROWS)
            x = x_ref[pl.ds(r0, LN_ROWS), :]
            mu = jnp.mean(x, axis=-1, keepdims=True)
            xc = x - mu
            var = jnp.mean(xc * xc, axis=-1, keepdims=True)
            hn = xc * lax.rsqrt(var + LN_EPS)
            h_ref[pl.ds(r0, LN_ROWS), :] = (hn * (1.0 + scale_ref[...]) + shift_ref[...]).astype(BF16)
            return carry

        lax.fori_loop(0, x_ref.shape[0] // LN_ROWS, rows, 0)

    o_ref[...] = jnp.dot(h_ref[...], w_ref[...], preferred_element_type=F32).astype(o_ref.dtype)


LN_ROWS = 256


def _in_proj(x2, scale, shift, w, L):
    N, D = x2.shape
    ncol = w.shape[1]
    tm = min(1024, L)
    tn = 1408
    assert N % tm == 0 and L % tm == 0 and ncol % tn == 0
    B = scale.shape[0]
    return pl.pallas_call(
        _inproj_kernel,
        grid=(N // tm, ncol // tn),
        in_specs=[pl.BlockSpec((tm, D), lambda i, j: (i, 0)),
                  pl.BlockSpec((None, 1, D), lambda i, j: ((i * tm) // L, 0, 0)),
                  pl.BlockSpec((None, 1, D), lambda i, j: ((i * tm) // L, 0, 0)),
                  pl.BlockSpec((D, tn), lambda i, j: (0, j))],
        out_specs=pl.BlockSpec((tm, tn), lambda i, j: (i, j)),
        out_shape=jax.ShapeDtypeStruct((N, ncol), BF16),
        scratch_shapes=[pltpu.VMEM((tm, D), BF16)],
        compiler_params=_cparams(("parallel", "arbitrary")),
        name="in_proj",
    )(x2, scale.reshape(B, 1, D), shift.reshape(B, 1, D), w)


def _cmp_kernel(x_ref, w1_ref, w1bd_ref, w2_ref, pos_ref, o_ref, xs_ref, *, ncp):
    S, dh = CMP_STRIDE, NSA_HEAD_DIM
    xs_ref[...] = x_ref[...].astype(F32)
    a = jnp.zeros((ncp, 2 * CMP_HIDDEN), F32)
    b = jnp.zeros((ncp, 2 * CMP_HIDDEN), F32)
    for j in range(S):
        xj = xs_ref[pl.ds(j, ncp, stride=S), :].astype(BF16)
        a = a + jnp.dot(xj, w1bd_ref[0, j], preferred_element_type=F32)
        b = b + jnp.dot(xj, w1bd_ref[1, j], preferred_element_type=F32)
    pw = jnp.dot(pos_ref[...], w1_ref[...], preferred_element_type=F32, precision=HIGHEST)[0:1]
    h = a + pltpu.roll(b, ncp - 1, 0) + jnp.concatenate([pw, pw], axis=1)
    w2 = w2_ref[...].astype(BF16)
    for g in range(2):
        hg = jax.nn.gelu(h[:, g * CMP_HIDDEN:(g + 1) * CMP_HIDDEN]).astype(BF16)
        o_ref[g] = jnp.dot(hg, w2, preferred_element_type=F32)


def _nsa_compress(proj, w1s, w2s, poss, B, L):
    G, dh, S = NSA_GROUPS, NSA_HEAD_DIM, CMP_STRIDE
    ncp = L // S
    w = w1s.reshape(2, 2, S, dh, CMP_HIDDEN).astype(BF16)
    z = jnp.zeros_like(w)
    w1bd = jnp.concatenate([jnp.concatenate([w, z], -1), jnp.concatenate([z, w], -1)], axis=-2)
    return pl.pallas_call(
        functools.partial(_cmp_kernel, ncp=ncp),
        grid=(2, B, G // 2),
        in_specs=[pl.BlockSpec((L, 2 * dh), lambda s, b, t: (b, COL_KC // (2 * dh) + 2 * s + t)),
                  pl.BlockSpec((None,) + w1s.shape[1:], lambda s, b, t: (s, 0, 0)),
                  pl.BlockSpec((None,) + w1bd.shape[1:], lambda s, b, t: (s, 0, 0, 0, 0)),
                  pl.BlockSpec((None,) + w2s.shape[1:], lambda s, b, t: (s, 0, 0)),
                  pl.BlockSpec((None,) + poss.shape[1:], lambda s, b, t: (s, 0, 0))],
        out_specs=pl.BlockSpec((None, None, 2, ncp, dh), lambda s, b, t: (s, b, t, 0, 0)),
        out_shape=jax.ShapeDtypeStruct((2, B, G, ncp, dh), F32),
        scratch_shapes=[pltpu.VMEM((L, 2 * dh), F32)],
        compiler_params=_cparams(("parallel", "parallel", "parallel")),
        name="nsa_compress",
    )(proj, w1s, w1bd, w2s, poss)


V_ROWS = NSA_HEAD_DIM + 16


def _kvprep_kernel(x_ref, ks_ref, vst_ref, kw_ref, vwt_ref, *, TL):
    G, dh = NSA_GROUPS, NSA_HEAD_DIM
    i = pl.program_id(1)
    tile = jnp.maximum(i - 1, 0)
    x = x_ref[...].astype(F32)
    lane = lax.broadcasted_iota(jnp.int32, (TL, 2 * dh), 1)
    row = lax.broadcasted_iota(jnp.int32, (TL, 2 * dh), 0) + tile * TL
    onehot = jnp.where(lane - dh == row // SEL_BLOCK, 1.0, 0.0)
    padrow = jnp.where(lane == dh, 1.0, 0.0)
    tail = jnp.where(lax.broadcasted_iota(jnp.int32, (V_ROWS - dh, TL), 0) == 0, 1.0, 0.0)
    is_pad = i == 0
    for t in range(G // 2):
        xk, xv, xwk, xwv = [x[:, (s * G + 2 * t) * dh:(s * G + 2 * t + 2) * dh] for s in range(4)]
        xvt, xwvt = xv.T, xwv.T
        for h in range(2):
            g = 2 * t + h
            kk = xk if h == 0 else pltpu.roll(xk, dh, 1)
            kwk = xwk if h == 0 else pltpu.roll(xwk, dh, 1)
            ks_ref[g] = jnp.where(lane < dh, kk, onehot).astype(BF16)
            kw_ref[g] = jnp.where(is_pad, padrow, jnp.where(lane < dh, kwk, 0.0)).astype(BF16)
            vst_ref[g] = jnp.concatenate([xvt[h * dh:(h + 1) * dh], tail], axis=0).astype(BF16)
            vw = jnp.concatenate([xwvt[h * dh:(h + 1) * dh], tail], axis=0)
            vwt_ref[g] = jnp.where(is_pad, 0.0, vw).astype(BF16)


def _kv_prep(proj, B, L):
    G, dh, TL = NSA_GROUPS, NSA_HEAD_DIM, WINDOW
    nt = L // TL
    wcols = 4 * G * dh
    assert L // SEL_BLOCK <= dh and COL_KV % wcols == 0 and L % TL == 0
    data = lambda b, i: jnp.maximum(i - 1, 0)
    k_shape = jax.ShapeDtypeStruct((B, G, L, 2 * dh), BF16)
    v_shape = jax.ShapeDtypeStruct((B, G, V_ROWS, L), BF16)
    kw_shape = jax.ShapeDtypeStruct((B, G, L + WINDOW, 2 * dh), BF16)
    vw_shape = jax.ShapeDtypeStruct((B, G, V_ROWS, L + WINDOW), BF16)
    return pl.pallas_call(
        functools.partial(_kvprep_kernel, TL=TL),
        grid=(B, nt + 1),
        in_specs=[pl.BlockSpec((TL, wcols), lambda b, i: (b * nt + data(b, i), COL_KV // wcols))],
        out_specs=[pl.BlockSpec((None, G, TL, 2 * dh), lambda b, i: (b, 0, data(b, i), 0)),
                   pl.BlockSpec((None, G, V_ROWS, TL), lambda b, i: (b, 0, 0, data(b, i))),
                   pl.BlockSpec((None, G, TL, 2 * dh), lambda b, i: (b, 0, i, 0)),
                   pl.BlockSpec((None, G, V_ROWS, TL), lambda b, i: (b, 0, 0, i))],
        out_shape=[k_shape, v_shape, kw_shape, vw_shape],
        compiler_params=_cparams(("parallel", "arbitrary")),
        name="kv_prep",
    )(proj)


SEL_TILE = 512
SEL_TABLE_FAR = 640
NSA_PROBLEMS = 4


def _tree(op, parts):
    while len(parts) > 1:
        parts = [op(parts[i], parts[i + 1]) if i + 1 < len(parts) else parts[i] for i in range(0, len(parts), 2)]
    return parts[0]


def _fold8(op, x):
    return _tree(op, [x[k:k + 8] for k in range(0, x.shape[0], 8)])


def _safe_inv(l):
    return jnp.where(l > 0.0, 1.0 / jnp.where(l > 0.0, l, 1.0), 0.0)


def _nsa_kernel(q_ref, za_ref, ng_ref, kc_ref, vct_ref, ks_ref, vst_ref, kw_ref, vwt_ref,
                tc_ref, tw_ref, ts_ref, ovt_ref, o_ref, s_ref, s2_ref, acc_ref, gt_ref, *, L):
    R, dh, QB = NSA_REP, NSA_HEAD_DIM, Q_BLOCK
    ncp = L // CMP_STRIDE
    nsel = L // SEL_BLOCK
    W = R * QB
    g = pl.program_id(1)
    qb = pl.program_id(2)

    NP = q_ref.shape[0]

    def front(p):
        qt = (q_ref[p].astype(F32) * (dh ** -0.5)).T
        qT = jnp.concatenate([qt[r * dh:(r + 1) * dh] for r in range(R)], axis=1).astype(BF16)

        c_off = pl.multiple_of((ncp - 8) - 8 * qb, 8)
        sc = jnp.dot(kc_ref[p], qT, preferred_element_type=F32) + tc_ref[pl.ds(c_off, ncp), :]
        m = jnp.maximum(jnp.max(_fold8(jnp.maximum, sc), axis=0, keepdims=True), 0.1 * MASK_VALUE)
        e = jnp.exp(sc - m)
        l = jnp.sum(_fold8(jnp.add, e), axis=0, keepdims=True)
        p_c = e * _safe_inv(l)
        oc = jnp.dot(vct_ref[p], p_c.astype(BF16), preferred_element_type=F32)

        psum = _tree(jnp.add, [p_c[:, r * QB:(r + 1) * QB] for r in range(R)])
        p_hi = psum.astype(BF16)
        p_lo = (psum - p_hi.astype(F32)).astype(BF16)
        ovt = ovt_ref[...]
        imp = jnp.dot(ovt, p_hi, preferred_element_type=F32) + jnp.dot(ovt, p_lo, preferred_element_type=F32)
        jj = lax.broadcasted_iota(jnp.int32, (nsel, QB), 0)
        ii = lax.broadcasted_iota(jnp.int32, (nsel, QB), 1)
        cur = 2 * qb + (ii >= SEL_BLOCK).astype(jnp.int32)
        forced = (jj == 0) | (jj == cur) | (jj == cur - 1)
        imp = jnp.where(forced, FORCE_VALUE, imp)
        imp = jnp.where(jj <= cur, imp, MASK_VALUE)
        jf = jj.astype(F32)
        sel = jnp.zeros((nsel, QB), F32)
        for _ in range(min(SEL_TOPK, nsel)):
            mx = jnp.max(_fold8(jnp.maximum, imp), axis=0, keepdims=True)
            idx = jnp.min(_fold8(jnp.minimum, jnp.where(imp == mx, jf, 1e9)), axis=0, keepdims=True)
            hit = jf == idx
            sel = jnp.where(hit & (mx > 0.1 * MASK_VALUE), 1.0, sel)
            imp = jnp.where(hit, -jnp.inf, imp)
        selbias = jnp.where(sel > 0.5, 0.0, MASK_VALUE).astype(BF16)
        sel_rows = [jnp.concatenate([selbias] * R, axis=1)]
        if nsel < dh:
            sel_rows.append(jnp.zeros((dh - nsel, W), BF16))
        q_sel = jnp.concatenate([qT] + sel_rows, axis=0)

        w0 = pl.multiple_of(qb * QB, QB)
        flag = jnp.where(lax.broadcasted_iota(jnp.int32, (dh, W), 0) == 0, MASK_VALUE, 0.0).astype(BF16)
        q_win = jnp.concatenate([qT, flag], axis=0)
        sw = jnp.dot(kw_ref[p, pl.ds(w0, WINDOW + QB), :], q_win, preferred_element_type=F32) + tw_ref[...]
        m_w = jnp.max(_fold8(jnp.maximum, sw), axis=0, keepdims=True)
        pw = jnp.exp((sw - m_w).astype(BF16))
        accw = jnp.dot(vwt_ref[p, :, pl.ds(w0, WINDOW + QB)], pw, preferred_element_type=F32)
        owin = accw[:dh] * _safe_inv(accw[dh:dh + 1])

        gt_ref[p] = jax.nn.sigmoid(ng_ref[p].astype(F32)).T

        def gate_row(branch):
            return jnp.concatenate([gt_ref[p, pl.ds(3 * (g * R + r) + branch, 1), :] for r in range(R)], axis=1)

        return q_sel, oc * gate_row(0) + owin * gate_row(2), gate_row(1)

    fronts = [front(p) for p in range(NP)]

    npair = (qb // (SEL_TILE // QB) + 2) // 2
    PAIR = 2 * SEL_TILE

    slots = (s_ref, s2_ref)

    def scores(u, p, slot):
        mx = None
        for h in range(2):
            t = 2 * u + h
            k0 = pl.multiple_of(t * SEL_TILE, SEL_TILE)
            x0 = pl.multiple_of(jnp.maximum(t * SEL_TILE - qb * QB + SEL_TABLE_FAR, 0), QB)
            s = (jnp.dot(ks_ref[p, pl.ds(k0, SEL_TILE), :], fronts[p][0], preferred_element_type=F32)
                 + ts_ref[pl.ds(x0, SEL_TILE), :])
            slots[slot][p, h * SEL_TILE:(h + 1) * SEL_TILE, :] = s
            f = _fold8(jnp.maximum, s)
            mx = f if mx is None else jnp.maximum(mx, f)
        return jnp.max(mx, axis=0, keepdims=True)

    def accumulate(u, p, slot, m_new, m_old):
        part = []
        for h in range(2):
            k0 = pl.multiple_of((2 * u + h) * SEL_TILE, SEL_TILE)
            pr = jnp.exp((slots[slot][p, h * SEL_TILE:(h + 1) * SEL_TILE, :] - m_new).astype(BF16))
            part.append(jnp.dot(vst_ref[p, :, pl.ds(k0, SEL_TILE)], pr, preferred_element_type=F32))
        acc_ref[p] = acc_ref[p] * jnp.exp(m_old - m_new) + part[0] + part[1]

    def phase(u, slot, m_cur, m_acc):
        m_next = []
        for p in range(NP):
            accumulate(u, p, slot, m_cur[p], m_acc[p])
            m_next.append(jnp.maximum(m_cur[p], scores(u + 1, p, 1 - slot)))
        return tuple(m_next), m_cur

    acc_ref[...] = jnp.zeros_like(acc_ref)
    m_first = tuple(scores(0, p, 0) for p in range(NP))

    def trip(v, carry):
        carry = phase(2 * v, 0, *carry)
        return phase(2 * v + 1, 1, *carry)

    n_rest = npair - 1
    m_cur, m_acc = lax.fori_loop(0, n_rest // 2, trip, (m_first, m_first))
    last = 2 * (n_rest // 2)

    @pl.when(n_rest % 2 == 1)
    def _():
        m2, m1 = phase(last, 0, m_cur, m_acc)
        for p in range(NP):
            accumulate(last + 1, p, 1, m2[p], m1[p])

    @pl.when(n_rest % 2 == 0)
    def _():
        for p in range(NP):
            accumulate(last, p, 0, m_cur[p], m_acc[p])

    for p in range(NP):
        acc = acc_ref[p]
        osel = acc[:dh] * _safe_inv(acc[dh:dh + 1])
        ot = fronts[p][1] + osel * fronts[p][2]
        o = jnp.concatenate([ot[:, r * QB:(r + 1) * QB] for r in range(R)], axis=0).T
        za = za_ref[p].astype(F32)
        o_ref[p] = (o * (za * jax.nn.sigmoid(za))).astype(o_ref.dtype)


def _nsa_attend(proj, kc, vct, ks, vst, kw, vwt, tc, tw, ts, ovt, B, L):
    R, dh, G, QB = NSA_REP, NSA_HEAD_DIM, NSA_GROUPS, Q_BLOCK
    assert L % (2 * SEL_TILE) == 0
    nqb = L // QB
    gw = R * dh
    W = R * QB
    NP = NSA_PROBLEMS if B % NSA_PROBLEMS == 0 else 1
    proj4 = proj.reshape(B // NP, NP, L, proj.shape[-1])
    tok = lambda col: pl.BlockSpec((None, NP, QB, col[1]), lambda b, g, i: (b, 0, i, col[0] // col[1] + col[2] * g))
    bg = lambda a: pl.BlockSpec((NP, None) + a.shape[2:], lambda b, g, i: (b, g, 0, 0),
                                pipeline_mode=pl.Buffered(1))
    grp = lambda a: pl.BlockSpec((None,) + a.shape[1:], lambda b, g, i: (g, 0, 0), pipeline_mode=pl.Buffered(1))
    out = pl.pallas_call(
        functools.partial(_nsa_kernel, L=L),
        grid=(B // NP, G, nqb),
        in_specs=[tok((COL_Q, gw, 1)), tok((COL_ZA, gw, 1)), tok((COL_NG, 128, 0)),
                  bg(kc), bg(vct), bg(ks), bg(vst), bg(kw), bg(vwt), grp(tc), grp(tw), grp(ts),
                  pl.BlockSpec(ovt.shape, lambda b, g, i: (0, 0))],
        out_specs=pl.BlockSpec((None, NP, QB, gw), lambda b, g, i: (b, 0, i, g)),
        out_shape=jax.ShapeDtypeStruct((B // NP, NP, L, NSA_WIDTH), BF16),
        scratch_shapes=[pltpu.VMEM((NP, 2 * SEL_TILE, W), F32), pltpu.VMEM((NP, 2 * SEL_TILE, W), F32),
                        pltpu.VMEM((NP, V_ROWS, W), F32),
                        pltpu.VMEM((NP, 128, QB), F32)],
        compiler_params=_cparams(("parallel", "parallel", "arbitrary")),
        name="nsa_attend",
    )(proj4, proj4, proj4, kc, vct, ks, vst, kw, vwt, tc, tw, ts, ovt)
    return out.reshape(B * L, NSA_WIDTH)


def _t5_bucket(dist):
    n = jnp.maximum(dist, 0)
    max_exact = REL_BUCKETS // 2
    nf = jnp.maximum(n, max_exact).astype(F32)
    large = max_exact + (jnp.log(nf / max_exact) / math.log(REL_MAX_DIST / max_exact)
                         * (REL_BUCKETS - max_exact)).astype(jnp.int32)
    large = jnp.minimum(large, REL_BUCKETS - 1)
    return jnp.where(n < max_exact, n, large)


def _toeplitz(f, d0, base, step, n_rows, width):
    rpb = width // step
    nblk = n_rows // rpb
    assert rpb * step == width and nblk * rpb == n_rows
    lo = base - width * nblk - d0
    seg = f[..., lo:lo + width * (nblk + 1)].reshape(f.shape[:-1] + (nblk + 1, width))[..., ::-1, :]
    return jnp.concatenate([seg[..., :-1, :], seg[..., 1:, :]], axis=-1)


TABLE_STEPS = (CMP_STRIDE, 1, 1)


def _tables_kernel(*refs):
    R, QB = NSA_REP, Q_BLOCK
    n = len(TABLE_STEPS)
    tmp_ref = refs[2 * n]
    for win_ref, out_ref, step in zip(refs[:n], refs[n:2 * n], TABLE_STEPS):
        rpb = QB // step

        def block(a, carry, win_ref=win_ref, out_ref=out_ref, step=step, rpb=rpb):
            r0 = pl.multiple_of(a * rpb, rpb)
            for r in range(R):
                x = jnp.broadcast_to(win_ref[r, pl.ds(a, 1), :], (QB, 2 * QB))
                y = pltpu.roll(x, 0, 1, stride=1, stride_axis=0)[:, :QB]
                if step > 1:
                    tmp_ref[...] = y
                    y = tmp_ref[pl.ds(0, rpb, stride=step), :]
                out_ref[pl.ds(r0, rpb), r * QB:(r + 1) * QB] = y
            return carry

        lax.fori_loop(0, win_ref.shape[1], block, 0)


def _expand_tables(wins):
    G, R, QB = NSA_GROUPS, NSA_REP, Q_BLOCK
    rows = [w.shape[1] * (QB // s) for w, s in zip(wins, TABLE_STEPS)]
    return pl.pallas_call(
        _tables_kernel,
        grid=(G,),
        in_specs=[pl.BlockSpec((R,) + w.shape[1:], lambda g: (g, 0, 0)) for w in wins],
        out_specs=[pl.BlockSpec((None, n, R * QB), lambda g: (g, 0, 0)) for n in rows],
        out_shape=[jax.ShapeDtypeStruct((G, n, R * QB), F32) for n in rows],
        scratch_shapes=[pltpu.VMEM((QB, QB), F32)],
        compiler_params=_cparams(("parallel",)),
        name="nsa_tables",
    )(*wins)


def _nsa_tables(rel_bias, L):
    QB, G, R = Q_BLOCK, NSA_GROUPS, NSA_REP
    ncp = L // CMP_STRIDE
    nsel = L // SEL_BLOCK
    npos = L + QB
    tbl = rel_bias.astype(F32)
    bpos = (tbl[_t5_bucket(jnp.arange(npos, dtype=jnp.int32))] - tbl[REL_BUCKETS - 1][None, :]).T
    f = jnp.concatenate([jnp.full((NSA_HEADS, npos), MASK_VALUE, F32), bpos], axis=-1)
    d = np.arange(-npos, npos)
    fw = jnp.where(jnp.asarray(d < WINDOW), f, MASK_VALUE)
    tc = _toeplitz(f, -npos, L - QB - CMP_BLOCK + 1, CMP_STRIDE, 2 * ncp - 8, QB)
    tw = _toeplitz(fw, -npos, WINDOW, 1, WINDOW + QB, QB)
    ts = _toeplitz(f, -npos, SEL_TABLE_FAR, 1, SEL_TABLE_FAR + 2 * SEL_TILE, QB)
    tc, tw, ts = _expand_tables((tc, tw, ts))
    c = np.arange(ncp)[None, :]
    j = np.arange(nsel)[:, None]
    ovt = ((c - 4 * j >= -1) & (c - 4 * j <= 3)).astype(np.float32)
    return tc, tw, ts, jnp.asarray(ovt, BF16)


S5_LANES = 128
S5_GT = S5_LANES // SSM_GROUP
S5_SW = S5_GT * SSM_STATE
S5_COLS = 4 * S5_LANES


def _s5_kernel(us_ref, kt_ref, zt_ref, cl_ref, lam_ref, ex_ref, y_ref,
               xs_ref, u8_ref, m8_ref, ws8_ref, wo8_ref, *, nk):
    T, C, P, GT, LT, SW = SSM_CHUNK, SSM_GROUP, SSM_STATE, S5_GT, S5_LANES, S5_SW

    @pl.when(pl.program_id(1) == 0)
    def _build():
        tile = lambda x: jnp.concatenate([x] * GT, axis=0)
        r = lax.broadcasted_iota(jnp.int32, (LT, LT), 0)
        c = lax.broadcasted_iota(jnp.int32, (LT, LT), 1)
        same = (r // C) == (c // C)
        m8_ref[...] = jnp.zeros_like(m8_ref)
        for j in range(T):
            bd = jnp.where(same, tile(kt_ref[j]), 0.0).astype(BF16)
            for b in range(T - j):
                m8_ref[b * LT:(b + 1) * LT, (b + j) * LT:(b + j + 1) * LT] = bd
        r = lax.broadcasted_iota(jnp.int32, (LT, 2 * SW), 0)
        c = lax.broadcasted_iota(jnp.int32, (LT, 2 * SW), 1)
        same = (r // C) == ((c % SW) // P)
        for b in range(T):
            ws8_ref[b * LT:(b + 1) * LT, :] = jnp.where(same, tile(zt_ref[b]), 0.0).astype(BF16)
        r = lax.broadcasted_iota(jnp.int32, (2 * SW, LT), 0)
        c = lax.broadcasted_iota(jnp.int32, (2 * SW, LT), 1)
        same = ((r % SW) // P) == (c // C)
        cl = cl_ref[...]
        for a in range(T):
            blk = jnp.dot(cl, ex_ref[a], preferred_element_type=F32)
            wo8_ref[:, a * LT:(a + 1) * LT] = jnp.where(same, blk, 0.0).astype(BF16)

    xs_ref[...] = us_ref[...].astype(F32)
    for b in range(T):
        u8_ref[:, b * LT:(b + 1) * LT] = xs_ref[pl.ds(b, nk, stride=T), :].astype(BF16)
    u8 = u8_ref[...]
    s = jnp.dot(u8, ws8_ref[...], preferred_element_type=F32)
    hr, hi = s[:, :SW], s[:, SW:]
    kidx = lax.broadcasted_iota(jnp.int32, (nk, SW), 0)
    d, step = 1, 0
    while d < nk:
        lr = lam_ref[step:step + 1, :SW]
        li = lam_ref[step:step + 1, SW:]
        keep = kidx >= d
        sr = jnp.where(keep, pltpu.roll(hr, d, 0), 0.0)
        si = jnp.where(keep, pltpu.roll(hi, d, 0), 0.0)
        hr, hi = hr + lr * sr - li * si, hi + lr * si + li * sr
        d, step = 2 * d, step + 1
    keep = kidx >= 1
    pr = jnp.where(keep, pltpu.roll(hr, 1, 0), 0.0)
    pi = jnp.where(keep, pltpu.roll(hi, 1, 0), 0.0)
    hcat = jnp.concatenate([pr, pi], axis=-1).astype(BF16)
    for q in range(T * LT // S5_COLS):
        kq = (q + 1) * S5_COLS
        yq = (jnp.dot(hcat, wo8_ref[:, q * S5_COLS:kq], preferred_element_type=F32)
              + jnp.dot(u8[:, :kq], m8_ref[0:kq, q * S5_COLS:kq], preferred_element_type=F32))
        for a4 in range(S5_COLS // LT):
            a = q * (S5_COLS // LT) + a4
            y_ref[pl.ds(a, nk, stride=T), :] = yq[:, a4 * LT:(a4 + 1) * LT]


def _s5_scan(proj, kt, zt, clc, lamp, ex, B, L):
    T, LT, SW = SSM_CHUNK, S5_LANES, S5_SW
    nk = L // T
    nt = SSM_WIDTH // LT
    per_tile = lambda a: pl.BlockSpec((None,) + a.shape[1:], lambda t, b: (t,) + (0,) * (a.ndim - 1))
    return pl.pallas_call(
        functools.partial(_s5_kernel, nk=nk),
        grid=(nt, B),
        in_specs=[pl.BlockSpec((L, LT), lambda t, b: (b, COL_US // LT + t)),
                  per_tile(kt), per_tile(zt), per_tile(clc), per_tile(lamp),
                  pl.BlockSpec(ex.shape, lambda t, b: (0, 0, 0))],
        out_specs=pl.BlockSpec((L, LT), lambda t, b: (b, t)),
        out_shape=jax.ShapeDtypeStruct((B * L, SSM_WIDTH), F32),
        scratch_shapes=[pltpu.VMEM((L, LT), F32), pltpu.VMEM((nk, T * LT), BF16),
                        pltpu.VMEM((T * LT, T * LT), BF16), pltpu.VMEM((T * LT, 2 * SW), BF16),
                        pltpu.VMEM((2 * SW, T * LT), BF16)],
        compiler_params=_cparams(("parallel", "arbitrary")),
        name="s5_scan",
    )(proj, kt, zt, clc, lamp, ex)


def _cmul(ar, ai, br, bi):
    return ar * br - ai * bi, ar * bi + ai * br


def _s5_tables(a_re, a_im, log_dt, b_re, b_im, c_re, c_im, nk):
    T, P, C, Gs = SSM_CHUNK, SSM_STATE, SSM_GROUP, SSM_GROUPS
    dt = jnp.exp(log_dt.astype(F32))[:, None]
    ar, ai = a_re.astype(F32), a_im.astype(F32)
    mag = jnp.exp(ar * dt)
    lr, li = mag * jnp.cos(ai * dt), mag * jnp.sin(ai * dt)
    den = ar * ar + ai * ai
    nr, ni = lr - 1.0, li
    fr, fi = (nr * ar + ni * ai) / den, (ni * ar - nr * ai) / den
    br, bim = b_re.astype(F32), b_im.astype(F32)
    bbr = fr[..., None] * br - fi[..., None] * bim
    bbi = fr[..., None] * bim + fi[..., None] * br
    jv = jnp.arange(T + 1, dtype=F32)[:, None, None]
    pmag = jnp.exp(jv * (ar * dt)[None])
    pwr, pwi = pmag * jnp.cos(jv * (ai * dt)[None]), pmag * jnp.sin(jv * (ai * dt)[None])
    zr = pwr[..., None] * bbr[None] - pwi[..., None] * bbi[None]
    zi = pwr[..., None] * bbi[None] + pwi[..., None] * bbr[None]
    cr, ci = c_re.astype(F32), c_im.astype(F32)
    kj = (jnp.einsum('gcp,jgpd->gjcd', cr, zr[:T], precision=HIGHEST)
          - jnp.einsum('gcp,jgpd->gjcd', ci, zi[:T], precision=HIGHEST))
    GT, NT = S5_GT, Gs // S5_GT
    kt = kj.reshape(NT, GT, T, C, C).transpose(0, 2, 4, 1, 3).reshape(NT, T, C, GT * C)
    lay_z = lambda z: z[:T][::-1].reshape(T, NT, GT, P, C).transpose(1, 0, 4, 2, 3).reshape(NT, T, C, GT * P)
    zt = jnp.concatenate([lay_z(zr), lay_z(zi)], axis=-1)
    pa_r, pa_i = pwr[1:].transpose(1, 0, 2)[:, :, None, :], pwi[1:].transpose(1, 0, 2)[:, :, None, :]
    clr = cr[:, None] * pa_r - ci[:, None] * pa_i
    cli = cr[:, None] * pa_i + ci[:, None] * pa_r
    lay_c = lambda x: x.reshape(NT, GT, T, C, P).transpose(0, 1, 4, 2, 3).reshape(NT, GT * P, T * C)
    clc = jnp.concatenate([lay_c(clr), lay_c(-cli)], axis=1).astype(BF16)
    qr, qi = pwr[T], pwi[T]
    steps = []
    d = 1
    while d < nk:
        steps.append(jnp.concatenate([qr.reshape(NT, GT * P), qi.reshape(NT, GT * P)], -1))
        qr, qi = _cmul(qr, qi, qr, qi)
        d *= 2
    lamp = jnp.stack(steps, 1)
    col = np.arange(T * C)[:, None]
    lane = np.arange(GT * C)[None, :]
    ex = np.stack([(col // C == a) & (col % C == lane % C) for a in range(T)]).astype(np.float32)
    return kt, zt, clc, lamp, jnp.asarray(ex, BF16)


def _glu_kernel(y_ref, u_ref, zb_ref, d_ref, w_ref, b_ref, o_ref):
    y = y_ref[...] + d_ref[...] * u_ref[...].astype(F32)
    yg = jax.nn.gelu(y).astype(BF16)
    z = jnp.dot(yg, w_ref[...], preferred_element_type=F32) + b_ref[...]
    zb = zb_ref[...].astype(F32)
    o_ref[...] = (yg.astype(F32) * jax.nn.sigmoid(z) * (zb * jax.nn.sigmoid(zb))).astype(o_ref.dtype)


def _s5_glu(y, proj, d_skip, w_glu, b_glu):
    N, W = y.shape
    tm = min(1024, N)
    return pl.pallas_call(
        _glu_kernel,
        grid=(N // tm,),
        in_specs=[pl.BlockSpec((tm, W), lambda i: (i, 0)),
                  pl.BlockSpec((tm, W), lambda i: (i, COL_US // W)),
                  pl.BlockSpec((tm, W), lambda i: (i, COL_ZB // W)),
                  pl.BlockSpec((1, W), lambda i: (0, 0)),
                  pl.BlockSpec((W, W), lambda i: (0, 0)),
                  pl.BlockSpec((1, W), lambda i: (0, 0))],
        out_specs=pl.BlockSpec((tm, W), lambda i: (i, 0)),
        out_shape=jax.ShapeDtypeStruct((N, W), BF16),
        compiler_params=_cparams(("parallel",)),
        name="s5_glu",
    )(y, proj, proj, d_skip.reshape(1, W), w_glu, b_glu.reshape(1, W))


def _merge_kernel(oa_ref, ob_ref, ga_ref, gb_ref, x_ref, gate_ref, wa_ref, wb_ref, wo_ref, lg_ref, lb_ref, o_ref):
    pa = jnp.dot(oa_ref[...], wa_ref[...], preferred_element_type=F32)
    pb = jnp.dot(ob_ref[...], wb_ref[...], preferred_element_type=F32)
    m = jax.nn.sigmoid(ga_ref[...].astype(F32)) * pa + jax.nn.sigmoid(gb_ref[...].astype(F32)) * pb
    y = jnp.dot(m.astype(BF16), wo_ref[...], preferred_element_type=F32)
    r = DEEPNORM_ALPHA * x_ref[...] + gate_ref[...] * y
    mu = jnp.mean(r, axis=-1, keepdims=True)
    rc = r - mu
    var = jnp.mean(rc * rc, axis=-1, keepdims=True)
    o_ref[...] = rc * lax.rsqrt(var + LN_EPS) * lg_ref[...] + lb_ref[...]


def _merge_out(oa, ob, proj, x2, gate, wa, wb, wo, ln_g, ln_b, L):
    N, D = x2.shape
    W = oa.shape[1]
    B = gate.shape[0]
    tm = min(256, L)
    const = lambda i: (0, 0)
    return pl.pallas_call(
        _merge_kernel,
        grid=(N // tm,),
        in_specs=[pl.BlockSpec((tm, W), lambda i: (i, 0)),
                  pl.BlockSpec((tm, W), lambda i: (i, 0)),
                  pl.BlockSpec((tm, D), lambda i: (i, COL_GA // D)),
                  pl.BlockSpec((tm, D), lambda i: (i, COL_GB // D)),
                  pl.BlockSpec((tm, D), lambda i: (i, 0)),
                  pl.BlockSpec((None, 1, D), lambda i: ((i * tm) // L, 0, 0)),
                  pl.BlockSpec((W, D), const),
                  pl.BlockSpec((W, D), const),
                  pl.BlockSpec((D, D), const),
                  pl.BlockSpec((1, D), const),
                  pl.BlockSpec((1, D), const)],
        out_specs=pl.BlockSpec((tm, D), lambda i: (i, 0)),
        out_shape=jax.ShapeDtypeStruct((N, D), F32),
        compiler_params=_cparams(("parallel",)),
        name="merge_out",
    )(oa, ob, proj, proj, x2, gate.reshape(B, 1, D), wa, wb, wo, ln_g.reshape(1, D), ln_b.reshape(1, D))


def _layer(x, c, w_ada, b_ada, w_in, rel_bias, cmp_pos_k, cmp_pos_v, w_cmp_k1, w_cmp_k2, w_cmp_v1, w_cmp_v2,
           ssm_a_re, ssm_a_im, ssm_log_dt, ssm_b_re, ssm_b_im, ssm_c_re, ssm_c_im, ssm_d, w_glu, b_glu,
           w_branch_nsa, w_branch_ssm, w_out, ln_g, ln_b):
    B, L, D = x.shape
    N = B * L
    G, dh = NSA_GROUPS, NSA_HEAD_DIM
    x2 = x.reshape(N, D)

    mod = _ada_mod(c, w_ada, b_ada)
    shift, scale, gate = mod[:, :D], mod[:, D:2 * D], mod[:, 2 * D:]

    proj = _in_proj(x2, scale, shift, _permute_w_in(w_in), L)

    w1s = jnp.stack([w_cmp_k1, w_cmp_v1])
    w2s = jnp.stack([w_cmp_k2, w_cmp_v2])
    poss = jnp.broadcast_to(jnp.stack([cmp_pos_k, cmp_pos_v]).reshape(2, 1, CMP_BLOCK * dh), (2, 8, CMP_BLOCK * dh))
    kcv = _nsa_compress(proj, w1s, w2s, poss, B, L)
    kc = kcv[0].astype(BF16)
    vct = kcv[1].swapaxes(-1, -2).astype(BF16)
    ks, vst, kw, vwt = _kv_prep(proj, B, L)
    tc, tw, ts, ovt = _nsa_tables(rel_bias, L)
    o_a = _nsa_attend(proj, kc, vct, ks, vst, kw, vwt, tc, tw, ts, ovt, B, L)

    s5_tabs = _s5_tables(ssm_a_re, ssm_a_im, ssm_log_dt, ssm_b_re, ssm_b_im, ssm_c_re, ssm_c_im, L // SSM_CHUNK)
    y = _s5_scan(proj, *s5_tabs, B, L)
    o_b = _s5_glu(y, proj, ssm_d, w_glu.astype(BF16), b_glu)

    out = _merge_out(o_a, o_b, proj, x2, gate, w_branch_nsa.astype(BF16), w_branch_ssm.astype(BF16),
                     w_out.astype(BF16), ln_g, ln_b, L)
    return out.reshape(B, L, D)


def kernel(x, c, w_ada, b_ada, w_in, rel_bias, cmp_pos_k, cmp_pos_v, w_cmp_k1, w_cmp_k2, w_cmp_v1, w_cmp_v2,
           ssm_a_re, ssm_a_im, ssm_log_dt, ssm_b_re, ssm_b_im, ssm_c_re, ssm_c_im, ssm_d, w_glu, b_glu,
           w_branch_nsa, w_branch_ssm, w_out, ln_g, ln_b):
    for i in range(w_ada.shape[0]):
        x = _layer(x, c, w_ada[i], b_ada[i], w_in[i], rel_bias, cmp_pos_k[i], cmp_pos_v[i], w_cmp_k1[i],
                   w_cmp_k2[i], w_cmp_v1[i], w_cmp_v2[i], ssm_a_re[i], ssm_a_im[i], ssm_log_dt[i], ssm_b_re[i],
                   ssm_b_im[i], ssm_c_re[i], ssm_c_im[i], ssm_d[i], w_glu[i], b_glu[i], w_branch_nsa[i],
                   w_branch_ssm[i], w_out[i], ln_g[i], ln_b[i])
    return x
```

```python
import functools
import math

import numpy as np
import jax
import jax.numpy as jnp
from jax import lax
from jax.experimental import pallas as pl
from jax.experimental.pallas import tpu as pltpu

F32 = jnp.float32
BF16 = jnp.bfloat16
HIGHEST = lax.Precision.HIGHEST

D_MODEL = 2048
NSA_HEADS = 16
NSA_GROUPS = 4
NSA_HEAD_DIM = 64
NSA_REP = NSA_HEADS // NSA_GROUPS
NSA_WIDTH = NSA_HEADS * NSA_HEAD_DIM
KV_WIDTH = NSA_GROUPS * NSA_HEAD_DIM
CMP_BLOCK = 32
CMP_STRIDE = 16
CMP_HIDDEN = 128
SEL_BLOCK = 64
SEL_TOPK = 8
WINDOW = 512
Q_BLOCK = 128
SSM_WIDTH = 1024
SSM_GROUP = 16
SSM_GROUPS = SSM_WIDTH // SSM_GROUP
SSM_STATE = 64
REL_BUCKETS = 32
REL_MAX_DIST = 128
DEEPNORM_ALPHA = 2.0 ** 0.25
LN_EPS = 1e-5
MASK_VALUE = -1e30
FORCE_VALUE = 1e4
NEVER_VALUE = -3e38
SSM_CHUNK = 16

COL_Q, COL_ZA, COL_US, COL_ZB, COL_GA, COL_GB, COL_KV, COL_KC, COL_NG = 0, 1024, 2048, 3072, 4096, 6144, 8192, 9216, 9728
PROJ_COLS = 9856
VMEM_LIMIT = 56 * 1024 * 1024


def _cparams(sem):
    return pltpu.CompilerParams(dimension_semantics=sem, vmem_limit_bytes=VMEM_LIMIT)


def _ada_kernel(c_ref, w_ref, b_ref, o_ref):
    o_ref[...] = jnp.dot(c_ref[...], w_ref[...], preferred_element_type=F32, precision=HIGHEST) + b_ref[...]


def _ada_mod(c, w_ada, b_ada):
    B, D = c.shape
    n = w_ada.shape[1]
    tn = 1536
    return pl.pallas_call(
        _ada_kernel,
        grid=(n // tn,),
        in_specs=[pl.BlockSpec((B, D), lambda j: (0, 0)),
                  pl.BlockSpec((D, tn), lambda j: (0, j)),
                  pl.BlockSpec((1, tn), lambda j: (0, j))],
        out_specs=pl.BlockSpec((B, tn), lambda j: (0, j)),
        out_shape=jax.ShapeDtypeStruct((B, n), F32),
        compiler_params=_cparams(("parallel",)),
        name="ada_mod",
    )(c, w_ada, b_ada.reshape(1, n))


def _inproj_kernel(x_ref, scale_ref, shift_ref, w_ref, o_ref, h_ref):
    @pl.when(pl.program_id(1) == 0)
    def _():
        def rows(k, carry):
            r0 = pl.multiple_of(k * LN_ROWS, LN_ROWS)
            x = x_ref[pl.ds(r0, LN_ROWS), :]
            mu = jnp.mean(x, axis=-1, keepdims=True)
            xc = x - mu
            var = jnp.mean(xc * xc, axis=-1, keepdims=True)
            hn = xc * lax.rsqrt(var + LN_EPS)
            h_ref[pl.ds(r0, LN_ROWS), :] = (hn * (1.0 + scale_ref[...]) + shift_ref[...]).astype(BF16)
            return carry

        lax.fori_loop(0, x_ref.shape[0] // LN_ROWS, rows, 0)

    o_ref[...] = jnp.dot(h_ref[...], w_ref[...], preferred_element_type=F32).astype(o_ref.dtype)


LN_ROWS = 256


def _in_proj(x2, scale, shift, w, L):
    N, D = x2.shape
    ncol = w.shape[1]
    tm = min(1024, L)
    tn = 1408
    assert N % tm == 0 and L % tm == 0 and ncol % tn == 0
    B = scale.shape[0]
    return pl.pallas_call(
        _inproj_kernel,
        grid=(N // tm, ncol // tn),
        in_specs=[pl.BlockSpec((tm, D), lambda i, j: (i, 0)),
                  pl.BlockSpec((None, 1, D), lambda i, j: ((i * tm) // L, 0, 0)),
                  pl.BlockSpec((None, 1, D), lambda i, j: ((i * tm) // L, 0, 0)),
                  pl.BlockSpec((D, tn), lambda i, j: (0, j))],
        out_specs=pl.BlockSpec((tm, tn), lambda i, j: (i, j)),
        out_shape=jax.ShapeDtypeStruct((N, ncol), BF16),
        scratch_shapes=[pltpu.VMEM((tm, D), BF16)],
        compiler_params=_cparams(("parallel", "arbitrary")),
        name="in_proj",
    )(x2, scale.reshape(B, 1, D), shift.reshape(B, 1, D), w)


def _cmp_kernel(x_ref, w1_ref, w1bd_ref, w2_ref, pos_ref, o_ref, xs_ref, *, ncp):
    S, dh = CMP_STRIDE, NSA_HEAD_DIM
    xs_ref[...] = x_ref[...].astype(F32)
    a = jnp.zeros((ncp, 2 * CMP_HIDDEN), F32)
    b = jnp.zeros((ncp, 2 * CMP_HIDDEN), F32)
    for j in range(S):
        xj = xs_ref[pl.ds(j, ncp, stride=S), :].astype(BF16)
        a = a + jnp.dot(xj, w1bd_ref[0, j], preferred_element_type=F32)
        b = b + jnp.dot(xj, w1bd_ref[1, j], preferred_element_type=F32)
    pw = jnp.dot(pos_ref[...], w1_ref[...], preferred_element_type=F32, precision=HIGHEST)[0:1]
    h = a + pltpu.roll(b, ncp - 1, 0) + jnp.concatenate([pw, pw], axis=1)
    w2 = w2_ref[...].astype(BF16)
    for g in range(2):
        hg = jax.nn.gelu(h[:, g * CMP_HIDDEN:(g + 1) * CMP_HIDDEN]).astype(BF16)
        o_ref[g] = jnp.dot(hg, w2, preferred_element_type=F32)


def _nsa_compress(proj, w1s, w2s, poss, B, L):
    G, dh, S = NSA_GROUPS, NSA_HEAD_DIM, CMP_STRIDE
    ncp = L // S
    w = w1s.reshape(2, 2, S, dh, CMP_HIDDEN).astype(BF16)
    z = jnp.zeros_like(w)
    w1bd = jnp.concatenate([jnp.concatenate([w, z], -1), jnp.concatenate([z, w], -1)], axis=-2)
    return pl.pallas_call(
        functools.partial(_cmp_kernel, ncp=ncp),
        grid=(2, B, G // 2),
        in_specs=[pl.BlockSpec((L, 2 * dh), lambda s, b, t: (b, COL_KC // (2 * dh) + 2 * s + t)),
                  pl.BlockSpec((None,) + w1s.shape[1:], lambda s, b, t: (s, 0, 0)),
                  pl.BlockSpec((None,) + w1bd.shape[1:], lambda s, b, t: (s, 0, 0, 0, 0)),
                  pl.BlockSpec((None,) + w2s.shape[1:], lambda s, b, t: (s, 0, 0)),
                  pl.BlockSpec((None,) + poss.shape[1:], lambda s, b, t: (s, 0, 0))],
        out_specs=pl.BlockSpec((None, None, 2, ncp, dh), lambda s, b, t: (s, b, t, 0, 0)),
        out_shape=jax.ShapeDtypeStruct((2, B, G, ncp, dh), F32),
        scratch_shapes=[pltpu.VMEM((L, 2 * dh), F32)],
        compiler_params=_cparams(("parallel", "parallel", "parallel")),
        name="nsa_compress",
    )(proj, w1s, w1bd, w2s, poss)


V_ROWS = NSA_HEAD_DIM + 16


def _kvprep_kernel(x_ref, ks_ref, vst_ref, kw_ref, vwt_ref, *, TL):
    G, dh = NSA_GROUPS, NSA_HEAD_DIM
    i = pl.program_id(1)
    tile = jnp.maximum(i - 1, 0)
    x = x_ref[...].astype(F32)
    lane = lax.broadcasted_iota(jnp.int32, (TL, 2 * dh), 1)
    row = lax.broadcasted_iota(jnp.int32, (TL, 2 * dh), 0) + tile * TL
    onehot = jnp.where(lane - dh == row // SEL_BLOCK, 1.0, 0.0)
    padrow = jnp.where(lane == dh, 1.0, 0.0)
    tail = jnp.where(lax.broadcasted_iota(jnp.int32, (V_ROWS - dh, TL), 0) == 0, 1.0, 0.0)
    is_pad = i == 0
    for t in range(G // 2):
        xk, xv, xwk, xwv = [x[:, (s * G + 2 * t) * dh:(s * G + 2 * t + 2) * dh] for s in range(4)]
        xvt, xwvt = xv.T, xwv.T
        for h in range(2):
            g = 2 * t + h
            kk = xk if h == 0 else pltpu.roll(xk, dh, 1)
            kwk = xwk if h == 0 else pltpu.roll(xwk, dh, 1)
            ks_ref[g] = jnp.where(lane < dh, kk, onehot).astype(BF16)
            kw_ref[g] = jnp.where(is_pad, padrow, jnp.where(lane < dh, kwk, 0.0)).astype(BF16)
            vst_ref[g] = jnp.concatenate([xvt[h * dh:(h + 1) * dh], tail], axis=0).astype(BF16)
            vw = jnp.concatenate([xwvt[h * dh:(h + 1) * dh], tail], axis=0)
            vwt_ref[g] = jnp.where(is_pad, 0.0, vw).astype(BF16)


def _kv_prep(proj, B, L):
    G, dh, TL = NSA_GROUPS, NSA_HEAD_DIM, WINDOW
    nt = L // TL
    wcols = 4 * G * dh
    assert L // SEL_BLOCK <= dh and COL_KV % wcols == 0 and L % TL == 0
    data = lambda b, i: jnp.maximum(i - 1, 0)
    k_shape = jax.ShapeDtypeStruct((B, G, L, 2 * dh), BF16)
    v_shape = jax.ShapeDtypeStruct((B, G, V_ROWS, L), BF16)
    kw_shape = jax.ShapeDtypeStruct((B, G, L + WINDOW, 2 * dh), BF16)
    vw_shape = jax.ShapeDtypeStruct((B, G, V_ROWS, L + WINDOW), BF16)
    return pl.pallas_call(
        functools.partial(_kvprep_kernel, TL=TL),
        grid=(B, nt + 1),
        in_specs=[pl.BlockSpec((TL, wcols), lambda b, i: (b * nt + data(b, i), COL_KV // wcols))],
        out_specs=[pl.BlockSpec((None, G, TL, 2 * dh), lambda b, i: (b, 0, data(b, i), 0)),
                   pl.BlockSpec((None, G, V_ROWS, TL), lambda b, i: (b, 0, 0, data(b, i))),
                   pl.BlockSpec((None, G, TL, 2 * dh), lambda b, i: (b, 0, i, 0)),
                   pl.BlockSpec((None, G, V_ROWS, TL), lambda b, i: (b, 0, 0, i))],
        out_shape=[k_shape, v_shape, kw_shape, vw_shape],
        compiler_params=_cparams(("parallel", "arbitrary")),
        name="kv_prep",
    )(proj)


SEL_TILE = 512
SEL_TABLE_FAR = 640
NSA_PROBLEMS = 4


def _tree(op, parts):
    while len(parts) > 1:
        parts = [op(parts[i], parts[i + 1]) if i + 1 < len(parts) else parts[i] for i in range(0, len(parts), 2)]
    return parts[0]


def _fold8(op, x):
    return _tree(op, [x[k:k + 8] for k in range(0, x.shape[0], 8)])


def _safe_inv(l):
    return jnp.where(l > 0.0, 1.0 / jnp.where(l > 0.0, l, 1.0), 0.0)


def _nsa_kernel(q_ref, za_ref, ng_ref, kc_ref, vct_ref, ks_ref, vst_ref, kw_ref, vwt_ref,
                tc_ref, tw_ref, ts_ref, ovt_ref, o_ref, s_ref, s2_ref, acc_ref, gt_ref, *, L):
    R, dh, QB = NSA_REP, NSA_HEAD_DIM, Q_BLOCK
    ncp = L // CMP_STRIDE
    nsel = L // SEL_BLOCK
    W = R * QB
    g = pl.program_id(1)
    qb = pl.program_id(2)

    NP = q_ref.shape[0]

    def front(p):
        qt = (q_ref[p].astype(F32) * (dh ** -0.5)).T
        qT = jnp.concatenate([qt[r * dh:(r + 1) * dh] for r in range(R)], axis=1).astype(BF16)

        c_off = pl.multiple_of((ncp - 8) - 8 * qb, 8)
        sc = jnp.dot(kc_ref[p], qT, preferred_element_type=F32) + tc_ref[pl.ds(c_off, ncp), :]
        m = jnp.maximum(jnp.max(_fold8(jnp.maximum, sc), axis=0, keepdims=True), 0.1 * MASK_VALUE)
        e = jnp.exp(sc - m)
        l = jnp.sum(_fold8(jnp.add, e), axis=0, keepdims=True)
        p_c = e * _safe_inv(l)
        oc = jnp.dot(vct_ref[p], p_c.astype(BF16), preferred_element_type=F32)

        psum = _tree(jnp.add, [p_c[:, r * QB:(r + 1) * QB] for r in range(R)])
        p_hi = psum.astype(BF16)
        p_lo = (psum - p_hi.astype(F32)).astype(BF16)
        ovt = ovt_ref[...]
        imp = jnp.dot(ovt, p_hi, preferred_element_type=F32) + jnp.dot(ovt, p_lo, preferred_element_type=F32)
        jj = lax.broadcasted_iota(jnp.int32, (nsel, QB), 0)
        ii = lax.broadcasted_iota(jnp.int32, (nsel, QB), 1)
        cur = 2 * qb + (ii >= SEL_BLOCK).astype(jnp.int32)
        forced = (jj == 0) | (jj == cur) | (jj == cur - 1)
        imp = jnp.where(forced, FORCE_VALUE, imp)
        imp = jnp.where(jj <= cur, imp, MASK_VALUE)
        jf = jj.astype(F32)
        sel = jnp.zeros((nsel, QB), F32)
        for _ in range(min(SEL_TOPK, nsel)):
            mx = jnp.max(_fold8(jnp.maximum, imp), axis=0, keepdims=True)
            idx = jnp.min(_fold8(jnp.minimum, jnp.where(imp == mx, jf, 1e9)), axis=0, keepdims=True)
            hit = jf == idx
            sel = jnp.where(hit & (mx > 0.1 * MASK_VALUE), 1.0, sel)
            imp = jnp.where(hit, -jnp.inf, imp)
        selbias = jnp.where(sel > 0.5, 0.0, MASK_VALUE).astype(BF16)
        sel_rows = [jnp.concatenate([selbias] * R, axis=1)]
        if nsel < dh:
            sel_rows.append(jnp.zeros((dh - nsel, W), BF16))
        q_sel = jnp.concatenate([qT] + sel_rows, axis=0)

        w0 = pl.multiple_of(qb * QB, QB)
        flag = jnp.where(lax.broadcasted_iota(jnp.int32, (dh, W), 0) == 0, MASK_VALUE, 0.0).astype(BF16)
        q_win = jnp.concatenate([qT, flag], axis=0)
        sw = jnp.dot(kw_ref[p, pl.ds(w0, WINDOW + QB), :], q_win, preferred_element_type=F32) + tw_ref[...]
        m_w = jnp.max(_fold8(jnp.maximum, sw), axis=0, keepdims=True)
        pw = jnp.exp((sw - m_w).astype(BF16))
        accw = jnp.dot(vwt_ref[p, :, pl.ds(w0, WINDOW + QB)], pw, preferred_element_type=F32)
        owin = accw[:dh] * _safe_inv(accw[dh:dh + 1])

        gt_ref[p] = jax.nn.sigmoid(ng_ref[p].astype(F32)).T

        def gate_row(branch):
            return jnp.concatenate([gt_ref[p, pl.ds(3 * (g * R + r) + branch, 1), :] for r in range(R)], axis=1)

        return q_sel, oc * gate_row(0) + owin * gate_row(2), gate_row(1)

    fronts = [front(p) for p in range(NP)]

    npair = (qb // (SEL_TILE // QB) + 2) // 2
    PAIR = 2 * SEL_TILE

    slots = (s_ref, s2_ref)

    def scores(u, p, slot):
        mx = None
        for h in range(2):
            t = 2 * u + h
            k0 = pl.multiple_of(t * SEL_TILE, SEL_TILE)
            x0 = pl.multiple_of(jnp.maximum(t * SEL_TILE - qb * QB + SEL_TABLE_FAR, 0), QB)
            s = (jnp.dot(ks_ref[p, pl.ds(k0, SEL_TILE), :], fronts[p][0], preferred_element_type=F32)
                 + ts_ref[pl.ds(x0, SEL_TILE), :])
            slots[slot][p, h * SEL_TILE:(h + 1) * SEL_TILE, :] = s
            f = _fold8(jnp.maximum, s)
            mx = f if mx is None else jnp.maximum(mx, f)
        return jnp.max(mx, axis=0, keepdims=True)

    def accumulate(u, p, slot, m_new, m_old):
        part = []
        for h in range(2):
            k0 = pl.multiple_of((2 * u + h) * SEL_TILE, SEL_TILE)
            pr = jnp.exp((slots[slot][p, h * SEL_TILE:(h + 1) * SEL_TILE, :] - m_new).astype(BF16))
            part.append(jnp.dot(vst_ref[p, :, pl.ds(k0, SEL_TILE)], pr, preferred_element_type=F32))
        acc_ref[p] = acc_ref[p] * jnp.exp(m_old - m_new) + part[0] + part[1]

    def phase(u, slot, m_cur, m_acc):
        m_next = []
        for p in range(NP):
            accumulate(u, p, slot, m_cur[p], m_acc[p])
            m_next.append(jnp.maximum(m_cur[p], scores(u + 1, p, 1 - slot)))
        return tuple(m_next), m_cur

    acc_ref[...] = jnp.zeros_like(acc_ref)
    m_first = tuple(scores(0, p, 0) for p in range(NP))

    def trip(v, carry):
        carry = phase(2 * v, 0, *carry)
        return phase(2 * v + 1, 1, *carry)

    n_rest = npair - 1
    m_cur, m_acc = lax.fori_loop(0, n_rest // 2, trip, (m_first, m_first))
    last = 2 * (n_rest // 2)

    @pl.when(n_rest % 2 == 1)
    def _():
        m2, m1 = phase(last, 0, m_cur, m_acc)
        for p in range(NP):
            accumulate(last + 1, p, 1, m2[p], m1[p])

    @pl.when(n_rest % 2 == 0)
    def _():
        for p in range(NP):
            accumulate(last, p, 0, m_cur[p], m_acc[p])

    for p in range(NP):
        acc = acc_ref[p]
        osel = acc[:dh] * _safe_inv(acc[dh:dh + 1])
        ot = fronts[p][1] + osel * fronts[p][2]
        o = jnp.concatenate([ot[:, r * QB:(r + 1) * QB] for r in range(R)], axis=0).T
        za = za_ref[p].astype(F32)
        o_ref[p] = (o * (za * jax.nn.sigmoid(za))).astype(o_ref.dtype)


def _nsa_attend(proj, kc, vct, ks, vst, kw, vwt, tc, tw, ts, ovt, B, L):
    R, dh, G, QB = NSA_REP, NSA_HEAD_DIM, NSA_GROUPS, Q_BLOCK
    assert L % (2 * SEL_TILE) == 0
    nqb = L // QB
    gw = R * dh
    W = R * QB
    NP = NSA_PROBLEMS if B % NSA_PROBLEMS == 0 else 1
    proj4 = proj.reshape(B // NP, NP, L, proj.shape[-1])
    tok = lambda col: pl.BlockSpec((None, NP, QB, col[1]), lambda b, g, i: (b, 0, i, col[0] // col[1] + col[2] * g))
    bg = lambda a: pl.BlockSpec((NP, None) + a.shape[2:], lambda b, g, i: (b, g, 0, 0),
                                pipeline_mode=pl.Buffered(1))
    grp = lambda a: pl.BlockSpec((None,) + a.shape[1:], lambda b, g, i: (g, 0, 0), pipeline_mode=pl.Buffered(1))
    out = pl.pallas_call(
        functools.partial(_nsa_kernel, L=L),
        grid=(B // NP, G, nqb),
        in_specs=[tok((COL_Q, gw, 1)), tok((COL_ZA, gw, 1)), tok((COL_NG, 128, 0)),
                  bg(kc), bg(vct), bg(ks), bg(vst), bg(kw), bg(vwt), grp(tc), grp(tw), grp(ts),
                  pl.BlockSpec(ovt.shape, lambda b, g, i: (0, 0))],
        out_specs=pl.BlockSpec((None, NP, QB, gw), lambda b, g, i: (b, 0, i, g)),
        out_shape=jax.ShapeDtypeStruct((B // NP, NP, L, NSA_WIDTH), BF16),
        scratch_shapes=[pltpu.VMEM((NP, 2 * SEL_TILE, W), F32), pltpu.VMEM((NP, 2 * SEL_TILE, W), F32),
                        pltpu.VMEM((NP, V_ROWS, W), F32),
                        pltpu.VMEM((NP, 128, QB), F32)],
        compiler_params=_cparams(("parallel", "parallel", "arbitrary")),
        name="nsa_attend",
    )(proj4, proj4, proj4, kc, vct, ks, vst, kw, vwt, tc, tw, ts, ovt)
    return out.reshape(B * L, NSA_WIDTH)


def _t5_bucket(dist):
    n = jnp.maximum(dist, 0)
    max_exact = REL_BUCKETS // 2
    nf = jnp.maximum(n, max_exact).astype(F32)
    large = max_exact + (jnp.log(nf / max_exact) / math.log(REL_MAX_DIST / max_exact)
                         * (REL_BUCKETS - max_exact)).astype(jnp.int32)
    large = jnp.minimum(large, REL_BUCKETS - 1)
    return jnp.where(n < max_exact, n, large)


def _toeplitz(f, d0, base, step, n_rows, width):
    rpb = width // step
    nblk = n_rows // rpb
    assert rpb * step == width and nblk * rpb == n_rows
    lo = base - width * nblk - d0
    seg = f[..., lo:lo + width * (nblk + 1)].reshape(f.shape[:-1] + (nblk + 1, width))[..., ::-1, :]
    return jnp.concatenate([seg[..., :-1, :], seg[..., 1:, :]], axis=-1)


TABLE_STEPS = (CMP_STRIDE, 1, 1)


def _tables_kernel(*refs, runs):
    R, QB = NSA_REP, Q_BLOCK
    n = len(TABLE_STEPS)
    tmp_ref = refs[2 * n]
    for win_ref, out_ref, step, table_runs in zip(refs[:n], refs[n:2 * n], TABLE_STEPS, runs):
        rpb = QB // step

        def block(a, carry, varies, win_ref=win_ref, out_ref=out_ref, step=step, rpb=rpb):
            r0 = pl.multiple_of(a * rpb, rpb)
            for r in range(R):
                if varies:
                    x = jnp.broadcast_to(win_ref[r, pl.ds(a, 1), :], (QB, 2 * QB))
                    y = pltpu.roll(x, 0, 1, stride=1, stride_axis=0)[:, :QB]
                    if step > 1:
                        tmp_ref[...] = y
                        y = tmp_ref[pl.ds(0, rpb, stride=step), :]
                else:
                    y = jnp.broadcast_to(win_ref[r, pl.ds(a, 1), :][:, :QB], (rpb, QB))
                out_ref[pl.ds(r0, rpb), r * QB:(r + 1) * QB] = y
            return carry

        for first, end, varies in table_runs:
            lax.fori_loop(first, end, functools.partial(block, varies=varies), 0)


def _constant_runs(base, step, n_rows, upper):
    far = REL_MAX_DIST
    rpb = Q_BLOCK // step
    flags = []
    for a in range(n_rows // rpb):
        lo, hi = base - Q_BLOCK * a - Q_BLOCK + step, base - Q_BLOCK * a + Q_BLOCK - 1
        flags.append(not (hi < 0 or lo >= upper or (lo >= far and hi < upper)))
    runs, start = [], 0
    for a in range(1, len(flags) + 1):
        if a == len(flags) or flags[a] != flags[start]:
            runs.append((start, a, flags[start]))
            start = a
    return tuple(runs)


def _expand_tables(wins, runs):
    G, R, QB = NSA_GROUPS, NSA_REP, Q_BLOCK
    rows = [w.shape[1] * (QB // s) for w, s in zip(wins, TABLE_STEPS)]
    return pl.pallas_call(
        functools.partial(_tables_kernel, runs=runs),
        grid=(G,),
        in_specs=[pl.BlockSpec((R,) + w.shape[1:], lambda g: (g, 0, 0)) for w in wins],
        out_specs=[pl.BlockSpec((None, n, R * QB), lambda g: (g, 0, 0)) for n in rows],
        out_shape=[jax.ShapeDtypeStruct((G, n, R * QB), F32) for n in rows],
        scratch_shapes=[pltpu.VMEM((QB, QB), F32)],
        compiler_params=_cparams(("parallel",)),
        name="nsa_tables",
    )(*wins)


def _nsa_tables(rel_bias, L):
    QB, G, R = Q_BLOCK, NSA_GROUPS, NSA_REP
    ncp = L // CMP_STRIDE
    nsel = L // SEL_BLOCK
    npos = L + QB
    tbl = rel_bias.astype(F32)
    bpos = (tbl[_t5_bucket(jnp.arange(npos, dtype=jnp.int32))] - tbl[REL_BUCKETS - 1][None, :]).T
    f = jnp.concatenate([jnp.full((NSA_HEADS, npos), MASK_VALUE, F32), bpos], axis=-1)
    d = np.arange(-npos, npos)
    fw = jnp.where(jnp.asarray(d < WINDOW), f, MASK_VALUE)
    spec_c = (L - QB - CMP_BLOCK + 1, CMP_STRIDE, 2 * ncp - 8)
    spec_w = (WINDOW, 1, WINDOW + QB)
    spec_s = (SEL_TABLE_FAR, 1, SEL_TABLE_FAR + 2 * SEL_TILE)
    assert TABLE_STEPS == (spec_c[1], spec_w[1], spec_s[1])
    wins = [_toeplitz(src, -npos, *spec, QB) for src, spec in ((f, spec_c), (fw, spec_w), (f, spec_s))]
    runs = tuple(_constant_runs(*spec, upper) for spec, upper in ((spec_c, 2 * npos), (spec_w, WINDOW), (spec_s, 2 * npos)))
    tc, tw, ts = _expand_tables(wins, runs)
    c = np.arange(ncp)[None, :]
    j = np.arange(nsel)[:, None]
    ovt = ((c - 4 * j >= -1) & (c - 4 * j <= 3)).astype(np.float32)
    return tc, tw, ts, jnp.asarray(ovt, BF16)


S5_LANES = 128
S5_GT = S5_LANES // SSM_GROUP
S5_SW = S5_GT * SSM_STATE
S5_COLS = 4 * S5_LANES


def _s5_kernel(us_ref, kt_ref, zt_ref, cl_ref, lam_ref, ex_ref, y_ref,
               xs_ref, u8_ref, m8_ref, ws8_ref, wo8_ref, *, nk):
    T, C, P, GT, LT, SW = SSM_CHUNK, SSM_GROUP, SSM_STATE, S5_GT, S5_LANES, S5_SW

    @pl.when(pl.program_id(1) == 0)
    def _build():
        tile = lambda x: jnp.concatenate([x] * GT, axis=0)
        r = lax.broadcasted_iota(jnp.int32, (LT, LT), 0)
        c = lax.broadcasted_iota(jnp.int32, (LT, LT), 1)
        same = (r // C) == (c // C)
        m8_ref[...] = jnp.zeros_like(m8_ref)
        for j in range(T):
            bd = jnp.where(same, tile(kt_ref[j]), 0.0).astype(BF16)
            for b in range(T - j):
                m8_ref[b * LT:(b + 1) * LT, (b + j) * LT:(b + j + 1) * LT] = bd
        r = lax.broadcasted_iota(jnp.int32, (LT, 2 * SW), 0)
        c = lax.broadcasted_iota(jnp.int32, (LT, 2 * SW), 1)
        same = (r // C) == ((c % SW) // P)
        for b in range(T):
            ws8_ref[b * LT:(b + 1) * LT, :] = jnp.where(same, tile(zt_ref[b]), 0.0).astype(BF16)
        r = lax.broadcasted_iota(jnp.int32, (2 * SW, LT), 0)
        c = lax.broadcasted_iota(jnp.int32, (2 * SW, LT), 1)
        same = ((r % SW) // P) == (c // C)
        cl = cl_ref[...]
        for a in range(T):
            blk = jnp.dot(cl, ex_ref[a], preferred_element_type=F32)
            wo8_ref[:, a * LT:(a + 1) * LT] = jnp.where(same, blk, 0.0).astype(BF16)

    xs_ref[...] = us_ref[...].astype(F32)
    for b in range(T):
        u8_ref[:, b * LT:(b + 1) * LT] = xs_ref[pl.ds(b, nk, stride=T), :].astype(BF16)
    u8 = u8_ref[...]
    s = jnp.dot(u8, ws8_ref[...], preferred_element_type=F32)
    hr, hi = s[:, :SW], s[:, SW:]
    kidx = lax.broadcasted_iota(jnp.int32, (nk, SW), 0)
    d, step = 1, 0
    while d < nk:
        lr = lam_ref[step:step + 1, :SW]
        li = lam_ref[step:step + 1, SW:]
        keep = kidx >= d
        sr = jnp.where(keep, pltpu.roll(hr, d, 0), 0.0)
        si = jnp.where(keep, pltpu.roll(hi, d, 0), 0.0)
        hr, hi = hr + lr * sr - li * si, hi + lr * si + li * sr
        d, step = 2 * d, step + 1
    keep = kidx >= 1
    pr = jnp.where(keep, pltpu.roll(hr, 1, 0), 0.0)
    pi = jnp.where(keep, pltpu.roll(hi, 1, 0), 0.0)
    hcat = jnp.concatenate([pr, pi], axis=-1).astype(BF16)
    for q in range(T * LT // S5_COLS):
        kq = (q + 1) * S5_COLS
        yq = (jnp.dot(hcat, wo8_ref[:, q * S5_COLS:kq], preferred_element_type=F32)
              + jnp.dot(u8[:, :kq], m8_ref[0:kq, q * S5_COLS:kq], preferred_element_type=F32))
        for a4 in range(S5_COLS // LT):
            a = q * (S5_COLS // LT) + a4
            y_ref[pl.ds(a, nk, stride=T), :] = yq[:, a4 * LT:(a4 + 1) * LT]


def _s5_scan(proj, kt, zt, clc, lamp, ex, B, L):
    T, LT, SW = SSM_CHUNK, S5_LANES, S5_SW
    nk = L // T
    nt = SSM_WIDTH // LT
    per_tile = lambda a: pl.BlockSpec((None,) + a.shape[1:], lambda t, b: (t,) + (0,) * (a.ndim - 1))
    return pl.pallas_call(
        functools.partial(_s5_kernel, nk=nk),
        grid=(nt, B),
        in_specs=[pl.BlockSpec((L, LT), lambda t, b: (b, COL_US // LT + t)),
                  per_tile(kt), per_tile(zt), per_tile(clc), per_tile(lamp),
                  pl.BlockSpec(ex.shape, lambda t, b: (0, 0, 0))],
        out_specs=pl.BlockSpec((L, LT), lambda t, b: (b, t)),
        out_shape=jax.ShapeDtypeStruct((B * L, SSM_WIDTH), F32),
        scratch_shapes=[pltpu.VMEM((L, LT), F32), pltpu.VMEM((nk, T * LT), BF16),
                        pltpu.VMEM((T * LT, T * LT), BF16), pltpu.VMEM((T * LT, 2 * SW), BF16),
                        pltpu.VMEM((2 * SW, T * LT), BF16)],
        compiler_params=_cparams(("parallel", "arbitrary")),
        name="s5_scan",
    )(proj, kt, zt, clc, lamp, ex)


def _cmul(ar, ai, br, bi):
    return ar * br - ai * bi, ar * bi + ai * br


def _s5_tables(a_re, a_im, log_dt, b_re, b_im, c_re, c_im, nk):
    T, P, C, Gs = SSM_CHUNK, SSM_STATE, SSM_GROUP, SSM_GROUPS
    dt = jnp.exp(log_dt.astype(F32))[:, None]
    ar, ai = a_re.astype(F32), a_im.astype(F32)
    mag = jnp.exp(ar * dt)
    lr, li = mag * jnp.cos(ai * dt), mag * jnp.sin(ai * dt)
    den = ar * ar + ai * ai
    nr, ni = lr - 1.0, li
    fr, fi = (nr * ar + ni * ai) / den, (ni * ar - nr * ai) / den
    br, bim = b_re.astype(F32), b_im.astype(F32)
    bbr = fr[..., None] * br - fi[..., None] * bim
    bbi = fr[..., None] * bim + fi[..., None] * br
    jv = jnp.arange(T + 1, dtype=F32)[:, None, None]
    pmag = jnp.exp(jv * (ar * dt)[None])
    pwr, pwi = pmag * jnp.cos(jv * (ai * dt)[None]), pmag * jnp.sin(jv * (ai * dt)[None])
    zr = pwr[..., None] * bbr[None] - pwi[..., None] * bbi[None]
    zi = pwr[..., None] * bbi[None] + pwi[..., None] * bbr[None]
    cr, ci = c_re.astype(F32), c_im.astype(F32)
    kj = (jnp.einsum('gcp,jgpd->gjcd', cr, zr[:T], precision=HIGHEST)
          - jnp.einsum('gcp,jgpd->gjcd', ci, zi[:T], precision=HIGHEST))
    GT, NT = S5_GT, Gs // S5_GT
    kt = kj.reshape(NT, GT, T, C, C).transpose(0, 2, 4, 1, 3).reshape(NT, T, C, GT * C)
    lay_z = lambda z: z[:T][::-1].reshape(T, NT, GT, P, C).transpose(1, 0, 4, 2, 3).reshape(NT, T, C, GT * P)
    zt = jnp.concatenate([lay_z(zr), lay_z(zi)], axis=-1)
    pa_r, pa_i = pwr[1:].transpose(1, 0, 2)[:, :, None, :], pwi[1:].transpose(1, 0, 2)[:, :, None, :]
    clr = cr[:, None] * pa_r - ci[:, None] * pa_i
    cli = cr[:, None] * pa_i + ci[:, None] * pa_r
    lay_c = lambda x: x.reshape(NT, GT, T, C, P).transpose(0, 1, 4, 2, 3).reshape(NT, GT * P, T * C)
    clc = jnp.concatenate([lay_c(clr), lay_c(-cli)], axis=1).astype(BF16)
    qr, qi = pwr[T], pwi[T]
    steps = []
    d = 1
    while d < nk:
        steps.append(jnp.concatenate([qr.reshape(NT, GT * P), qi.reshape(NT, GT * P)], -1))
        qr, qi = _cmul(qr, qi, qr, qi)
        d *= 2
    lamp = jnp.stack(steps, 1)
    col = np.arange(T * C)[:, None]
    lane = np.arange(GT * C)[None, :]
    ex = np.stack([(col // C == a) & (col % C == lane % C) for a in range(T)]).astype(np.float32)
    return kt, zt, clc, lamp, jnp.asarray(ex, BF16)


def _glu_kernel(y_ref, u_ref, zb_ref, d_ref, w_ref, b_ref, o_ref):
    y = y_ref[...] + d_ref[...] * u_ref[...].astype(F32)
    yg = jax.nn.gelu(y).astype(BF16)
    z = jnp.dot(yg, w_ref[...], preferred_element_type=F32) + b_ref[...]
    zb = zb_ref[...].astype(F32)
    o_ref[...] = (yg.astype(F32) * jax.nn.sigmoid(z) * (zb * jax.nn.sigmoid(zb))).astype(o_ref.dtype)


def _s5_glu(y, proj, d_skip, w_glu, b_glu):
    N, W = y.shape
    tm = min(1024, N)
    return pl.pallas_call(
        _glu_kernel,
        grid=(N // tm,),
        in_specs=[pl.BlockSpec((tm, W), lambda i: (i, 0)),
                  pl.BlockSpec((tm, W), lambda i: (i, COL_US // W)),
                  pl.BlockSpec((tm, W), lambda i: (i, COL_ZB // W)),
                  pl.BlockSpec((1, W), lambda i: (0, 0)),
                  pl.BlockSpec((W, W), lambda i: (0, 0)),
                  pl.BlockSpec((1, W), lambda i: (0, 0))],
        out_specs=pl.BlockSpec((tm, W), lambda i: (i, 0)),
        out_shape=jax.ShapeDtypeStruct((N, W), BF16),
        compiler_params=_cparams(("parallel",)),
        name="s5_glu",
    )(y, proj, proj, d_skip.reshape(1, W), w_glu, b_glu.reshape(1, W))


def _merge_kernel(oa_ref, ob_ref, ga_ref, gb_ref, x_ref, gate_ref, wa_ref, wb_ref, wo_ref, lg_ref, lb_ref, o_ref):
    pa = jnp.dot(oa_ref[...], wa_ref[...], preferred_element_type=F32)
    pb = jnp.dot(ob_ref[...], wb_ref[...], preferred_element_type=F32)
    m = jax.nn.sigmoid(ga_ref[...].astype(F32)) * pa + jax.nn.sigmoid(gb_ref[...].astype(F32)) * pb
    y = jnp.dot(m.astype(BF16), wo_ref[...], preferred_element_type=F32)
    r = DEEPNORM_ALPHA * x_ref[...] + gate_ref[...] * y
    mu = jnp.mean(r, axis=-1, keepdims=True)
    rc = r - mu
    var = jnp.mean(rc * rc, axis=-1, keepdims=True)
    o_ref[...] = rc * lax.rsqrt(var + LN_EPS) * lg_ref[...] + lb_ref[...]


def _merge_out(oa, ob, proj, x2, gate, wa, wb, wo, ln_g, ln_b, L):
    N, D = x2.shape
    W = oa.shape[1]
    B = gate.shape[0]
    tm = min(256, L)
    const = lambda i: (0, 0)
    return pl.pallas_call(
        _merge_kernel,
        grid=(N // tm,),
        in_specs=[pl.BlockSpec((tm, W), lambda i: (i, 0)),
                  pl.BlockSpec((tm, W), lambda i: (i, 0)),
                  pl.BlockSpec((tm, D), lambda i: (i, COL_GA // D)),
                  pl.BlockSpec((tm, D), lambda i: (i, COL_GB // D)),
                  pl.BlockSpec((tm, D), lambda i: (i, 0)),
                  pl.BlockSpec((None, 1, D), lambda i: ((i * tm) // L, 0, 0)),
                  pl.BlockSpec((W, D), const),
                  pl.BlockSpec((W, D), const),
                  pl.BlockSpec((D, D), const),
                  pl.BlockSpec((1, D), const),
                  pl.BlockSpec((1, D), const)],
        out_specs=pl.BlockSpec((tm, D), lambda i: (i, 0)),
        out_shape=jax.ShapeDtypeStruct((N, D), F32),
        compiler_params=_cparams(("parallel",)),
        name="merge_out",
    )(oa, ob, proj, proj, x2, gate.reshape(B, 1, D), wa, wb, wo, ln_g.reshape(1, D), ln_b.reshape(1, D))


def _layer(x, c, w_ada, b_ada, w_in, rel_bias, cmp_pos_k, cmp_pos_v, w_cmp_k1, w_cmp_k2, w_cmp_v1, w_cmp_v2,
           ssm_a_re, ssm_a_im, ssm_log_dt, ssm_b_re, ssm_b_im, ssm_c_re, ssm_c_im, ssm_d, w_glu, b_glu,
           w_branch_nsa, w_branch_ssm, w_out, ln_g, ln_b):
    B, L, D = x.shape
    N = B * L
    G, dh = NSA_GROUPS, NSA_HEAD_DIM
    x2 = x.reshape(N, D)

    mod = _ada_mod(c, w_ada, b_ada)
    shift, scale, gate = mod[:, :D], mod[:, D:2 * D], mod[:, 2 * D:]

    o_q, o_kv, o_ng, o_za, o_us, o_zb, o_ga, o_gb = np.cumsum(
        [0, NSA_WIDTH, 6 * KV_WIDTH, 3 * NSA_HEADS, NSA_WIDTH, SSM_WIDTH, SSM_WIDTH, D_MODEL]).tolist()
    w_re = jnp.concatenate([
        w_in[:, o_q:o_kv], w_in[:, o_za:o_us], w_in[:, o_us:o_zb], w_in[:, o_zb:o_ga], w_in[:, o_ga:o_gb],
        w_in[:, o_gb:], w_in[:, o_kv + 2 * KV_WIDTH:o_ng], w_in[:, o_kv:o_kv + 2 * KV_WIDTH], w_in[:, o_ng:o_za],
        jnp.zeros((D, PROJ_COLS - COL_NG - 3 * NSA_HEADS), w_in.dtype)], axis=1).astype(BF16)
    proj = _in_proj(x2, scale, shift, w_re, L)

    w1s = jnp.stack([w_cmp_k1, w_cmp_v1])
    w2s = jnp.stack([w_cmp_k2, w_cmp_v2])
    poss = jnp.broadcast_to(jnp.stack([cmp_pos_k, cmp_pos_v]).reshape(2, 1, CMP_BLOCK * dh), (2, 8, CMP_BLOCK * dh))
    kcv = _nsa_compress(proj, w1s, w2s, poss, B, L)
    kc = kcv[0].astype(BF16)
    vct = kcv[1].swapaxes(-1, -2).astype(BF16)
    ks, vst, kw, vwt = _kv_prep(proj, B, L)
    tc, tw, ts, ovt = _nsa_tables(rel_bias, L)
    o_a = _nsa_attend(proj, kc, vct, ks, vst, kw, vwt, tc, tw, ts, ovt, B, L)

    s5_tabs = _s5_tables(ssm_a_re, ssm_a_im, ssm_log_dt, ssm_b_re, ssm_b_im, ssm_c_re, ssm_c_im, L // SSM_CHUNK)
    y = _s5_scan(proj, *s5_tabs, B, L)
    o_b = _s5_glu(y, proj, ssm_d, w_glu.astype(BF16), b_glu)

    out = _merge_out(o_a, o_b, proj, x2, gate, w_branch_nsa.astype(BF16), w_branch_ssm.astype(BF16),
                     w_out.astype(BF16), ln_g, ln_b, L)
    return out.reshape(B, L, D)


def kernel(x, c, w_ada, b_ada, w_in, rel_bias, cmp_pos_k, cmp_pos_v, w_cmp_k1, w_cmp_k2, w_cmp_v1, w_cmp_v2,
           ssm_a_re, ssm_a_im, ssm_log_dt, ssm_b_re, ssm_b_im, ssm_c_re, ssm_c_im, ssm_d, w_glu, b_glu,
           w_branch_nsa, w_branch_ssm, w_out, ln_g, ln_b):
    for i in range(w_ada.shape[0]):
        x = _layer(x, c, w_ada[i], b_ada[i], w_in[i], rel_bias, cmp_pos_k[i], cmp_pos_v[i], w_cmp_k1[i],
                   w_cmp_k2[i], w_cmp_v1[i], w_cmp_v2[i], ssm_a_re[i], ssm_a_im[i], ssm_log_dt[i], ssm_b_re[i],
                   ssm_b_im[i], ssm_c_re[i], ssm_c_im[i], ssm_d[i], w_glu[i], b_glu[i], w_branch_nsa[i],
                   w_branch_ssm[i], w_out[i], ln_g[i], ln_b[i])
    return x
```

```python
import functools
import math

import numpy as np
import jax
import jax.numpy as jnp
from jax import lax
from jax.experimental import pallas as pl
from jax.experimental.pallas import tpu as pltpu

F32 = jnp.float32
BF16 = jnp.bfloat16
HIGHEST = lax.Precision.HIGHEST

D_MODEL = 2048
NSA_HEADS = 16
NSA_GROUPS = 4
NSA_HEAD_DIM = 64
NSA_REP = NSA_HEADS // NSA_GROUPS
NSA_WIDTH = NSA_HEADS * NSA_HEAD_DIM
KV_WIDTH = NSA_GROUPS * NSA_HEAD_DIM
CMP_BLOCK = 32
CMP_STRIDE = 16
CMP_HIDDEN = 128
SEL_BLOCK = 64
SEL_TOPK = 8
WINDOW = 512
Q_BLOCK = 128
SSM_WIDTH = 1024
SSM_GROUP = 16
SSM_GROUPS = SSM_WIDTH // SSM_GROUP
SSM_STATE = 64
REL_BUCKETS = 32
REL_MAX_DIST = 128
DEEPNORM_ALPHA = 2.0 ** 0.25
LN_EPS = 1e-5
MASK_VALUE = -1e30
FORCE_VALUE = 1e4
NEVER_VALUE = -3e38
SSM_CHUNK = 16

COL_Q, COL_ZA, COL_US, COL_ZB, COL_GA, COL_GB, COL_KV, COL_KC, COL_NG = 0, 1024, 2048, 3072, 4096, 6144, 8192, 9216, 9728
PROJ_COLS = 9856
VMEM_LIMIT = 56 * 1024 * 1024


def _cparams(sem):
    return pltpu.CompilerParams(dimension_semantics=sem, vmem_limit_bytes=VMEM_LIMIT)


def _ada_kernel(c_ref, w_ref, b_ref, o_ref):
    o_ref[...] = jnp.dot(c_ref[...], w_ref[...], preferred_element_type=F32, precision=HIGHEST) + b_ref[...]


def _ada_mod(c, w_ada, b_ada):
    B, D = c.shape
    n = w_ada.shape[1]
    tn = 1536
    return pl.pallas_call(
        _ada_kernel,
        grid=(n // tn,),
        in_specs=[pl.BlockSpec((B, D), lambda j: (0, 0)),
                  pl.BlockSpec((D, tn), lambda j: (0, j)),
                  pl.BlockSpec((1, tn), lambda j: (0, j))],
        out_specs=pl.BlockSpec((B, tn), lambda j: (0, j)),
        out_shape=jax.ShapeDtypeStruct((B, n), F32),
        compiler_params=_cparams(("parallel",)),
        name="ada_mod",
    )(c, w_ada, b_ada.reshape(1, n))


def _inproj_kernel(x_ref, scale_ref, shift_ref, w_ref, o_ref, h_ref):
    @pl.when(pl.program_id(1) == 0)
    def _():
        def rows(k, carry):
            r0 = pl.multiple_of(k * LN_ROWS, LN_ROWS)
            x = x_ref[pl.ds(r0, LN_ROWS), :]
            mu = jnp.mean(x, axis=-1, keepdims=True)
            xc = x - mu
            var = jnp.mean(xc * xc, axis=-1, keepdims=True)
            hn = xc * lax.rsqrt(var + LN_EPS)
            h_ref[pl.ds(r0, LN_ROWS), :] = (hn * (1.0 + scale_ref[...]) + shift_ref[...]).astype(BF16)
            return carry

        lax.fori_loop(0, x_ref.shape[0] // LN_ROWS, rows, 0)

    o_ref[...] = jnp.dot(h_ref[...], w_ref[...], preferred_element_type=F32).astype(o_ref.dtype)


LN_ROWS = 256


def _in_proj(x2, scale, shift, w, L):
    N, D = x2.shape
    ncol = w.shape[1]
    tm = min(1024, L)
    tn = 1408
    assert N % tm == 0 and L % tm == 0 and ncol % tn == 0
    B = scale.shape[0]
    return pl.pallas_call(
        _inproj_kernel,
        grid=(N // tm, ncol // tn),
        in_specs=[pl.BlockSpec((tm, D), lambda i, j: (i, 0)),
                  pl.BlockSpec((None, 1, D), lambda i, j: ((i * tm) // L, 0, 0)),
                  pl.BlockSpec((None, 1, D), lambda i, j: ((i * tm) // L, 0, 0)),
                  pl.BlockSpec((D, tn), lambda i, j: (0, j))],
        out_specs=pl.BlockSpec((tm, tn), lambda i, j: (i, j)),
        out_shape=jax.ShapeDtypeStruct((N, ncol), BF16),
        scratch_shapes=[pltpu.VMEM((tm, D), BF16)],
        compiler_params=_cparams(("parallel", "arbitrary")),
        name="in_proj",
    )(x2, scale.reshape(B, 1, D), shift.reshape(B, 1, D), w)


def _cmp_kernel(x_ref, w1_ref, w1bd_ref, w2_ref, pos_ref, o_ref, xs_ref, *, ncp):
    S, dh = CMP_STRIDE, NSA_HEAD_DIM
    xs_ref[...] = x_ref[...].astype(F32)
    a = jnp.zeros((ncp, 2 * CMP_HIDDEN), F32)
    b = jnp.zeros((ncp, 2 * CMP_HIDDEN), F32)
    for j in range(S):
        xj = xs_ref[pl.ds(j, ncp, stride=S), :].astype(BF16)
        a = a + jnp.dot(xj, w1bd_ref[0, j], preferred_element_type=F32)
        b = b + jnp.dot(xj, w1bd_ref[1, j], preferred_element_type=F32)
    pw = jnp.dot(pos_ref[...], w1_ref[...], preferred_element_type=F32, precision=HIGHEST)[0:1]
    h = a + pltpu.roll(b, ncp - 1, 0) + jnp.concatenate([pw, pw], axis=1)
    w2 = w2_ref[...].astype(BF16)
    for g in range(2):
        hg = jax.nn.gelu(h[:, g * CMP_HIDDEN:(g + 1) * CMP_HIDDEN]).astype(BF16)
        o_ref[g] = jnp.dot(hg, w2, preferred_element_type=F32)


def _nsa_compress(proj, w1s, w2s, poss, B, L):
    G, dh, S = NSA_GROUPS, NSA_HEAD_DIM, CMP_STRIDE
    ncp = L // S
    w = w1s.reshape(2, 2, S, dh, CMP_HIDDEN).astype(BF16)
    z = jnp.zeros_like(w)
    w1bd = jnp.concatenate([jnp.concatenate([w, z], -1), jnp.concatenate([z, w], -1)], axis=-2)
    return pl.pallas_call(
        functools.partial(_cmp_kernel, ncp=ncp),
        grid=(2, B, G // 2),
        in_specs=[pl.BlockSpec((L, 2 * dh), lambda s, b, t: (b, COL_KC // (2 * dh) + 2 * s + t)),
                  pl.BlockSpec((None,) + w1s.shape[1:], lambda s, b, t: (s, 0, 0)),
                  pl.BlockSpec((None,) + w1bd.shape[1:], lambda s, b, t: (s, 0, 0, 0, 0)),
                  pl.BlockSpec((None,) + w2s.shape[1:], lambda s, b, t: (s, 0, 0)),
                  pl.BlockSpec((None,) + poss.shape[1:], lambda s, b, t: (s, 0, 0))],
        out_specs=pl.BlockSpec((None, None, 2, ncp, dh), lambda s, b, t: (s, b, t, 0, 0)),
        out_shape=jax.ShapeDtypeStruct((2, B, G, ncp, dh), F32),
        scratch_shapes=[pltpu.VMEM((L, 2 * dh), F32)],
        compiler_params=_cparams(("parallel", "parallel", "parallel")),
        name="nsa_compress",
    )(proj, w1s, w1bd, w2s, poss)


V_ROWS = NSA_HEAD_DIM + 16


def _kvprep_kernel(x_ref, ks_ref, vst_ref, kw_ref, vwt_ref, *, TL):
    G, dh = NSA_GROUPS, NSA_HEAD_DIM
    i = pl.program_id(1)
    tile = jnp.maximum(i - 1, 0)
    x = x_ref[...].astype(F32)
    lane = lax.broadcasted_iota(jnp.int32, (TL, 2 * dh), 1)
    row = lax.broadcasted_iota(jnp.int32, (TL, 2 * dh), 0) + tile * TL
    onehot = jnp.where(lane - dh == row // SEL_BLOCK, 1.0, 0.0)
    padrow = jnp.where(lane == dh, 1.0, 0.0)
    tail = jnp.where(lax.broadcasted_iota(jnp.int32, (V_ROWS - dh, TL), 0) == 0, 1.0, 0.0)
    is_pad = i == 0
    for t in range(G // 2):
        xk, xv, xwk, xwv = [x[:, (s * G + 2 * t) * dh:(s * G + 2 * t + 2) * dh] for s in range(4)]
        xvt, xwvt = xv.T, xwv.T
        for h in range(2):
            g = 2 * t + h
            kk = xk if h == 0 else pltpu.roll(xk, dh, 1)
            kwk = xwk if h == 0 else pltpu.roll(xwk, dh, 1)
            ks_ref[g] = jnp.where(lane < dh, kk, onehot).astype(BF16)
            kw_ref[g] = jnp.where(is_pad, padrow, jnp.where(lane < dh, kwk, 0.0)).astype(BF16)
            vst_ref[g] = jnp.concatenate([xvt[h * dh:(h + 1) * dh], tail], axis=0).astype(BF16)
            vw = jnp.concatenate([xwvt[h * dh:(h + 1) * dh], tail], axis=0)
            vwt_ref[g] = jnp.where(is_pad, 0.0, vw).astype(BF16)


def _kv_prep(proj, B, L):
    G, dh, TL = NSA_GROUPS, NSA_HEAD_DIM, WINDOW
    nt = L // TL
    wcols = 4 * G * dh
    assert L // SEL_BLOCK <= dh and COL_KV % wcols == 0 and L % TL == 0
    data = lambda b, i: jnp.maximum(i - 1, 0)
    k_shape = jax.ShapeDtypeStruct((B, G, L, 2 * dh), BF16)
    v_shape = jax.ShapeDtypeStruct((B, G, V_ROWS, L), BF16)
    kw_shape = jax.ShapeDtypeStruct((B, G, L + WINDOW, 2 * dh), BF16)
    vw_shape = jax.ShapeDtypeStruct((B, G, V_ROWS, L + WINDOW), BF16)
    return pl.pallas_call(
        functools.partial(_kvprep_kernel, TL=TL),
        grid=(B, nt + 1),
        in_specs=[pl.BlockSpec((TL, wcols), lambda b, i: (b * nt + data(b, i), COL_KV // wcols))],
        out_specs=[pl.BlockSpec((None, G, TL, 2 * dh), lambda b, i: (b, 0, data(b, i), 0)),
                   pl.BlockSpec((None, G, V_ROWS, TL), lambda b, i: (b, 0, 0, data(b, i))),
                   pl.BlockSpec((None, G, TL, 2 * dh), lambda b, i: (b, 0, i, 0)),
                   pl.BlockSpec((None, G, V_ROWS, TL), lambda b, i: (b, 0, 0, i))],
        out_shape=[k_shape, v_shape, kw_shape, vw_shape],
        compiler_params=_cparams(("parallel", "arbitrary")),
        name="kv_prep",
    )(proj)


SEL_TILE = 512
SEL_TABLE_FAR = 640
NSA_PROBLEMS = 4


def _tree(op, parts):
    while len(parts) > 1:
        parts = [op(parts[i], parts[i + 1]) if i + 1 < len(parts) else parts[i] for i in range(0, len(parts), 2)]
    return parts[0]


def _fold8(op, x):
    return _tree(op, [x[k:k + 8] for k in range(0, x.shape[0], 8)])


def _safe_inv(l):
    return jnp.where(l > 0.0, 1.0 / jnp.where(l > 0.0, l, 1.0), 0.0)


def _nsa_kernel(q_ref, za_ref, ng_ref, kc_ref, vct_ref, ks_ref, vst_ref, kw_ref, vwt_ref,
                tc_ref, tw_ref, ts_ref, ovt_ref, o_ref, s_ref, s2_ref, acc_ref, gt_ref, *, L):
    R, dh, QB = NSA_REP, NSA_HEAD_DIM, Q_BLOCK
    ncp = L // CMP_STRIDE
    nsel = L // SEL_BLOCK
    W = R * QB
    g = pl.program_id(1)
    qb = pl.program_id(2)

    NP = q_ref.shape[0]

    def front(p):
        qt = (q_ref[p].astype(F32) * (dh ** -0.5)).T
        qT = jnp.concatenate([qt[r * dh:(r + 1) * dh] for r in range(R)], axis=1).astype(BF16)

        c_off = pl.multiple_of((ncp - 8) - 8 * qb, 8)
        sc = jnp.dot(kc_ref[p], qT, preferred_element_type=F32) + tc_ref[pl.ds(c_off, ncp), :]
        m = jnp.maximum(jnp.max(_fold8(jnp.maximum, sc), axis=0, keepdims=True), 0.1 * MASK_VALUE)
        e = jnp.exp(sc - m)
        l = jnp.sum(_fold8(jnp.add, e), axis=0, keepdims=True)
        p_c = e * _safe_inv(l)
        oc = jnp.dot(vct_ref[p], p_c.astype(BF16), preferred_element_type=F32)

        psum = _tree(jnp.add, [p_c[:, r * QB:(r + 1) * QB] for r in range(R)])
        p_hi = psum.astype(BF16)
        p_lo = (psum - p_hi.astype(F32)).astype(BF16)
        ovt = ovt_ref[...]
        imp = jnp.dot(ovt, p_hi, preferred_element_type=F32) + jnp.dot(ovt, p_lo, preferred_element_type=F32)
        jj = lax.broadcasted_iota(jnp.int32, (nsel, QB), 0)
        ii = lax.broadcasted_iota(jnp.int32, (nsel, QB), 1)
        cur = 2 * qb + (ii >= SEL_BLOCK).astype(jnp.int32)
        forced = (jj == 0) | (jj == cur) | (jj == cur - 1)
        imp = jnp.where(forced, FORCE_VALUE, imp)
        imp = jnp.where(jj <= cur, imp, MASK_VALUE)
        jf = jj.astype(F32)
        sel = jnp.zeros((nsel, QB), F32)
        for _ in range(min(SEL_TOPK, nsel)):
            mx = jnp.max(_fold8(jnp.maximum, imp), axis=0, keepdims=True)
            idx = jnp.min(_fold8(jnp.minimum, jnp.where(imp == mx, jf, 1e9)), axis=0, keepdims=True)
            hit = jf == idx
            sel = jnp.where(hit & (mx > 0.1 * MASK_VALUE), 1.0, sel)
            imp = jnp.where(hit, -jnp.inf, imp)
        selbias = jnp.where(sel > 0.5, 0.0, MASK_VALUE).astype(BF16)
        sel_rows = [jnp.concatenate([selbias] * R, axis=1)]
        if nsel < dh:
            sel_rows.append(jnp.zeros((dh - nsel, W), BF16))
        q_sel = jnp.concatenate([qT] + sel_rows, axis=0)

        w0 = pl.multiple_of(qb * QB, QB)
        flag = jnp.where(lax.broadcasted_iota(jnp.int32, (dh, W), 0) == 0, MASK_VALUE, 0.0).astype(BF16)
        q_win = jnp.concatenate([qT, flag], axis=0)
        sw = jnp.dot(kw_ref[p, pl.ds(w0, WINDOW + QB), :], q_win, preferred_element_type=F32) + tw_ref[...]
        m_w = jnp.max(_fold8(jnp.maximum, sw), axis=0, keepdims=True)
        pw = jnp.exp((sw - m_w).astype(BF16))
        accw = jnp.dot(vwt_ref[p, :, pl.ds(w0, WINDOW + QB)], pw, preferred_element_type=F32)
        owin = accw[:dh] * _safe_inv(accw[dh:dh + 1])

        gt_ref[p] = jax.nn.sigmoid(ng_ref[p].astype(F32)).T

        def gate_row(branch):
            return jnp.concatenate([gt_ref[p, pl.ds(3 * (g * R + r) + branch, 1), :] for r in range(R)], axis=1)

        return q_sel, oc * gate_row(0) + owin * gate_row(2), gate_row(1)

    fronts = [front(p) for p in range(NP)]

    ntile = qb // (SEL_TILE // QB) + 1
    npair = ntile // 2
    odd = ntile % 2

    slots = (s_ref, s2_ref)

    def scores(t0, p, slot, nt=2):
        mx = None
        for h in range(nt):
            t = t0 + h
            k0 = pl.multiple_of(t * SEL_TILE, SEL_TILE)
            x0 = pl.multiple_of(jnp.maximum(t * SEL_TILE - qb * QB + SEL_TABLE_FAR, 0), QB)
            s = (jnp.dot(ks_ref[p, pl.ds(k0, SEL_TILE), :], fronts[p][0], preferred_element_type=F32)
                 + ts_ref[pl.ds(x0, SEL_TILE), :])
            slots[slot][p, h * SEL_TILE:(h + 1) * SEL_TILE, :] = s
            f = _fold8(jnp.maximum, s)
            mx = f if mx is None else jnp.maximum(mx, f)
        return jnp.max(mx, axis=0, keepdims=True)

    def accumulate(t0, p, slot, m_new, m_old, nt=2):
        part = None
        for h in range(nt):
            k0 = pl.multiple_of((t0 + h) * SEL_TILE, SEL_TILE)
            pr = jnp.exp((slots[slot][p, h * SEL_TILE:(h + 1) * SEL_TILE, :] - m_new).astype(BF16))
            pv = jnp.dot(vst_ref[p, :, pl.ds(k0, SEL_TILE)], pr, preferred_element_type=F32)
            part = pv if part is None else part + pv
        acc_ref[p] = acc_ref[p] * jnp.exp(m_old - m_new) + part

    def phase(t0, slot, m_cur, m_acc, nt_next=2):
        m_next = []
        for p in range(NP):
            accumulate(t0, p, slot, m_cur[p], m_acc[p])
            m_next.append(jnp.maximum(m_cur[p], scores(t0 + 2, p, 1 - slot, nt_next)))
        return tuple(m_next), m_cur

    def finish(t0, slot, m_cur, m_acc):
        @pl.when(odd == 1)
        def _():
            m2, m1 = phase(t0, slot, m_cur, m_acc, nt_next=1)
            for p in range(NP):
                accumulate(t0 + 2, p, 1 - slot, m2[p], m1[p], nt=1)

        @pl.when(odd == 0)
        def _():
            for p in range(NP):
                accumulate(t0, p, slot, m_cur[p], m_acc[p])

    acc_ref[...] = jnp.zeros_like(acc_ref)

    @pl.when(npair == 0)
    def _():
        for p in range(NP):
            m = scores(0, p, 0, nt=1)
            accumulate(0, p, 0, m, m, nt=1)

    @pl.when(npair > 0)
    def _():
        m_first = tuple(scores(0, p, 0) for p in range(NP))

        def trip(v, carry):
            carry = phase(4 * v, 0, *carry)
            return phase(4 * v + 2, 1, *carry)

        n_rest = npair - 1
        m_cur, m_acc = lax.fori_loop(0, n_rest // 2, trip, (m_first, m_first))
        t_last = 4 * (n_rest // 2)

        @pl.when(n_rest % 2 == 1)
        def _():
            finish(t_last + 2, 1, *phase(t_last, 0, m_cur, m_acc))

        @pl.when(n_rest % 2 == 0)
        def _():
            finish(t_last, 0, m_cur, m_acc)

    for p in range(NP):
        acc = acc_ref[p]
        osel = acc[:dh] * _safe_inv(acc[dh:dh + 1])
        ot = fronts[p][1] + osel * fronts[p][2]
        o = jnp.concatenate([ot[:, r * QB:(r + 1) * QB] for r in range(R)], axis=0).T
        za = za_ref[p].astype(F32)
        o_ref[p] = (o * (za * jax.nn.sigmoid(za))).astype(o_ref.dtype)


def _nsa_attend(proj, kc, vct, ks, vst, kw, vwt, tc, tw, ts, ovt, B, L):
    R, dh, G, QB = NSA_REP, NSA_HEAD_DIM, NSA_GROUPS, Q_BLOCK
    assert L % (2 * SEL_TILE) == 0
    nqb = L // QB
    gw = R * dh
    W = R * QB
    NP = NSA_PROBLEMS if B % NSA_PROBLEMS == 0 else 1
    proj4 = proj.reshape(B // NP, NP, L, proj.shape[-1])
    tok = lambda col: pl.BlockSpec((None, NP, QB, col[1]), lambda b, g, i: (b, 0, i, col[0] // col[1] + col[2] * g))
    bg = lambda a: pl.BlockSpec((NP, None) + a.shape[2:], lambda b, g, i: (b, g, 0, 0),
                                pipeline_mode=pl.Buffered(1))
    grp = lambda a: pl.BlockSpec((None,) + a.shape[1:], lambda b, g, i: (g, 0, 0), pipeline_mode=pl.Buffered(1))
    out = pl.pallas_call(
        functools.partial(_nsa_kernel, L=L),
        grid=(B // NP, G, nqb),
        in_specs=[tok((COL_Q, gw, 1)), tok((COL_ZA, gw, 1)), tok((COL_NG, 128, 0)),
                  bg(kc), bg(vct), bg(ks), bg(vst), bg(kw), bg(vwt), grp(tc), grp(tw), grp(ts),
                  pl.BlockSpec(ovt.shape, lambda b, g, i: (0, 0))],
        out_specs=pl.BlockSpec((None, NP, QB, gw), lambda b, g, i: (b, 0, i, g)),
        out_shape=jax.ShapeDtypeStruct((B // NP, NP, L, NSA_WIDTH), BF16),
        scratch_shapes=[pltpu.VMEM((NP, 2 * SEL_TILE, W), F32), pltpu.VMEM((NP, 2 * SEL_TILE, W), F32),
                        pltpu.VMEM((NP, V_ROWS, W), F32),
                        pltpu.VMEM((NP, 128, QB), F32)],
        compiler_params=_cparams(("parallel", "parallel", "arbitrary")),
        name="nsa_attend",
    )(proj4, proj4, proj4, kc, vct, ks, vst, kw, vwt, tc, tw, ts, ovt)
    return out.reshape(B * L, NSA_WIDTH)


def _t5_bucket(dist):
    n = jnp.maximum(dist, 0)
    max_exact = REL_BUCKETS // 2
    nf = jnp.maximum(n, max_exact).astype(F32)
    large = max_exact + (jnp.log(nf / max_exact) / math.log(REL_MAX_DIST / max_exact)
                         * (REL_BUCKETS - max_exact)).astype(jnp.int32)
    large = jnp.minimum(large, REL_BUCKETS - 1)
    return jnp.where(n < max_exact, n, large)


def _toeplitz(f, d0, base, step, n_rows, width):
    rpb = width // step
    nblk = n_rows // rpb
    assert rpb * step == width and nblk * rpb == n_rows
    lo = base - width * nblk - d0
    seg = f[..., lo:lo + width * (nblk + 1)].reshape(f.shape[:-1] + (nblk + 1, width))[..., ::-1, :]
    return jnp.concatenate([seg[..., :-1, :], seg[..., 1:, :]], axis=-1)


TABLE_STEPS = (CMP_STRIDE, 1, 1)


def _tables_kernel(*refs, runs):
    R, QB = NSA_REP, Q_BLOCK
    n = len(TABLE_STEPS)
    tmp_ref = refs[2 * n]
    for win_ref, out_ref, step, table_runs in zip(refs[:n], refs[n:2 * n], TABLE_STEPS, runs):
        rpb = QB // step

        def block(a, carry, varies, win_ref=win_ref, out_ref=out_ref, step=step, rpb=rpb):
            r0 = pl.multiple_of(a * rpb, rpb)
            for r in range(R):
                if varies:
                    x = jnp.broadcast_to(win_ref[r, pl.ds(a, 1), :], (QB, 2 * QB))
                    y = pltpu.roll(x, 0, 1, stride=1, stride_axis=0)[:, :QB]
                    if step > 1:
                        tmp_ref[...] = y
                        y = tmp_ref[pl.ds(0, rpb, stride=step), :]
                else:
                    y = jnp.broadcast_to(win_ref[r, pl.ds(a, 1), :][:, :QB], (rpb, QB))
                out_ref[pl.ds(r0, rpb), r * QB:(r + 1) * QB] = y
            return carry

        for first, end, varies in table_runs:
            lax.fori_loop(first, end, functools.partial(block, varies=varies), 0)


def _constant_runs(base, step, n_rows, upper):
    far = REL_MAX_DIST
    rpb = Q_BLOCK // step
    flags = []
    for a in range(n_rows // rpb):
        lo, hi = base - Q_BLOCK * a - Q_BLOCK + step, base - Q_BLOCK * a + Q_BLOCK - 1
        flags.append(not (hi < 0 or lo >= upper or (lo >= far and hi < upper)))
    runs, start = [], 0
    for a in range(1, len(flags) + 1):
        if a == len(flags) or flags[a] != flags[start]:
            runs.append((start, a, flags[start]))
            start = a
    return tuple(runs)


def _expand_tables(wins, runs):
    G, R, QB = NSA_GROUPS, NSA_REP, Q_BLOCK
    rows = [w.shape[1] * (QB // s) for w, s in zip(wins, TABLE_STEPS)]
    return pl.pallas_call(
        functools.partial(_tables_kernel, runs=runs),
        grid=(G,),
        in_specs=[pl.BlockSpec((R,) + w.shape[1:], lambda g: (g, 0, 0)) for w in wins],
        out_specs=[pl.BlockSpec((None, n, R * QB), lambda g: (g, 0, 0)) for n in rows],
        out_shape=[jax.ShapeDtypeStruct((G, n, R * QB), F32) for n in rows],
        scratch_shapes=[pltpu.VMEM((QB, QB), F32)],
        compiler_params=_cparams(("parallel",)),
        name="nsa_tables",
    )(*wins)


def _nsa_tables(rel_bias, L):
    QB, G, R = Q_BLOCK, NSA_GROUPS, NSA_REP
    ncp = L // CMP_STRIDE
    nsel = L // SEL_BLOCK
    npos = L + QB
    tbl = rel_bias.astype(F32)
    bpos = (tbl[_t5_bucket(jnp.arange(npos, dtype=jnp.int32))] - tbl[REL_BUCKETS - 1][None, :]).T
    f = jnp.concatenate([jnp.full((NSA_HEADS, npos), MASK_VALUE, F32), bpos], axis=-1)
    d = np.arange(-npos, npos)
    fw = jnp.where(jnp.asarray(d < WINDOW), f, MASK_VALUE)
    spec_c = (L - QB - CMP_BLOCK + 1, CMP_STRIDE, 2 * ncp - 8)
    spec_w = (WINDOW, 1, WINDOW + QB)
    spec_s = (SEL_TABLE_FAR, 1, SEL_TABLE_FAR + SEL_TILE)
    assert TABLE_STEPS == (spec_c[1], spec_w[1], spec_s[1])
    wins = [_toeplitz(src, -npos, *spec, QB) for src, spec in ((f, spec_c), (fw, spec_w), (f, spec_s))]
    runs = tuple(_constant_runs(*spec, upper) for spec, upper in ((spec_c, 2 * npos), (spec_w, WINDOW), (spec_s, 2 * npos)))
    tc, tw, ts = _expand_tables(wins, runs)
    c = np.arange(ncp)[None, :]
    j = np.arange(nsel)[:, None]
    ovt = ((c - 4 * j >= -1) & (c - 4 * j <= 3)).astype(np.float32)
    return tc, tw, ts, jnp.asarray(ovt, BF16)


S5_LANES = 128
S5_GT = S5_LANES // SSM_GROUP
S5_SW = S5_GT * SSM_STATE
S5_COLS = 4 * S5_LANES


def _s5_kernel(us_ref, kt_ref, zt_ref, cl_ref, lam_ref, ex_ref, y_ref,
               xs_ref, u8_ref, m8_ref, ws8_ref, wo8_ref, *, nk):
    T, C, P, GT, LT, SW = SSM_CHUNK, SSM_GROUP, SSM_STATE, S5_GT, S5_LANES, S5_SW

    @pl.when(pl.program_id(1) == 0)
    def _build():
        tile = lambda x: jnp.concatenate([x] * GT, axis=0)
        r = lax.broadcasted_iota(jnp.int32, (LT, LT), 0)
        c = lax.broadcasted_iota(jnp.int32, (LT, LT), 1)
        same = (r // C) == (c // C)
        m8_ref[...] = jnp.zeros_like(m8_ref)
        for j in range(T):
            bd = jnp.where(same, tile(kt_ref[j]), 0.0).astype(BF16)
            for b in range(T - j):
                m8_ref[b * LT:(b + 1) * LT, (b + j) * LT:(b + j + 1) * LT] = bd
        r = lax.broadcasted_iota(jnp.int32, (LT, 2 * SW), 0)
        c = lax.broadcasted_iota(jnp.int32, (LT, 2 * SW), 1)
        same = (r // C) == ((c % SW) // P)
        for b in range(T):
            ws8_ref[b * LT:(b + 1) * LT, :] = jnp.where(same, tile(zt_ref[b]), 0.0).astype(BF16)
        r = lax.broadcasted_iota(jnp.int32, (2 * SW, LT), 0)
        c = lax.broadcasted_iota(jnp.int32, (2 * SW, LT), 1)
        same = ((r % SW) // P) == (c // C)
        cl = cl_ref[...]
        for a in range(T):
            blk = jnp.dot(cl, ex_ref[a], preferred_element_type=F32)
            wo8_ref[:, a * LT:(a + 1) * LT] = jnp.where(same, blk, 0.0).astype(BF16)

    xs_ref[...] = us_ref[...].astype(F32)
    for b in range(T):
        u8_ref[:, b * LT:(b + 1) * LT] = xs_ref[pl.ds(b, nk, stride=T), :].astype(BF16)
    u8 = u8_ref[...]
    s = jnp.dot(u8, ws8_ref[...], preferred_element_type=F32)
    hr, hi = s[:, :SW], s[:, SW:]
    kidx = lax.broadcasted_iota(jnp.int32, (nk, SW), 0)
    d, step = 1, 0
    while d < nk:
        lr = lam_ref[step:step + 1, :SW]
        li = lam_ref[step:step + 1, SW:]
        keep = kidx >= d
        sr = jnp.where(keep, pltpu.roll(hr, d, 0), 0.0)
        si = jnp.where(keep, pltpu.roll(hi, d, 0), 0.0)
        hr, hi = hr + lr * sr - li * si, hi + lr * si + li * sr
        d, step = 2 * d, step + 1
    keep = kidx >= 1
    pr = jnp.where(keep, pltpu.roll(hr, 1, 0), 0.0)
    pi = jnp.where(keep, pltpu.roll(hi, 1, 0), 0.0)
    hcat = jnp.concatenate([pr, pi], axis=-1).astype(BF16)
    for q in range(T * LT // S5_COLS):
        kq = (q + 1) * S5_COLS
        yq = (jnp.dot(hcat, wo8_ref[:, q * S5_COLS:kq], preferred_element_type=F32)
              + jnp.dot(u8[:, :kq], m8_ref[0:kq, q * S5_COLS:kq], preferred_element_type=F32))
        for a4 in range(S5_COLS // LT):
            a = q * (S5_COLS // LT) + a4
            y_ref[pl.ds(a, nk, stride=T), :] = yq[:, a4 * LT:(a4 + 1) * LT]


def _s5_scan(proj, kt, zt, clc, lamp, ex, B, L):
    T, LT, SW = SSM_CHUNK, S5_LANES, S5_SW
    nk = L // T
    nt = SSM_WIDTH // LT
    per_tile = lambda a: pl.BlockSpec((None,) + a.shape[1:], lambda t, b: (t,) + (0,) * (a.ndim - 1))
    return pl.pallas_call(
        functools.partial(_s5_kernel, nk=nk),
        grid=(nt, B),
        in_specs=[pl.BlockSpec((L, LT), lambda t, b: (b, COL_US // LT + t)),
                  per_tile(kt), per_tile(zt), per_tile(clc), per_tile(lamp),
                  pl.BlockSpec(ex.shape, lambda t, b: (0, 0, 0))],
        out_specs=pl.BlockSpec((L, LT), lambda t, b: (b, t)),
        out_shape=jax.ShapeDtypeStruct((B * L, SSM_WIDTH), F32),
        scratch_shapes=[pltpu.VMEM((L, LT), F32), pltpu.VMEM((nk, T * LT), BF16),
                        pltpu.VMEM((T * LT, T * LT), BF16), pltpu.VMEM((T * LT, 2 * SW), BF16),
                        pltpu.VMEM((2 * SW, T * LT), BF16)],
        compiler_params=_cparams(("parallel", "arbitrary")),
        name="s5_scan",
    )(proj, kt, zt, clc, lamp, ex)


def _cmul(ar, ai, br, bi):
    return ar * br - ai * bi, ar * bi + ai * br


def _s5_tables(a_re, a_im, log_dt, b_re, b_im, c_re, c_im, nk):
    T, P, C, Gs = SSM_CHUNK, SSM_STATE, SSM_GROUP, SSM_GROUPS
    dt = jnp.exp(log_dt.astype(F32))[:, None]
    ar, ai = a_re.astype(F32), a_im.astype(F32)
    mag = jnp.exp(ar * dt)
    lr, li = mag * jnp.cos(ai * dt), mag * jnp.sin(ai * dt)
    den = ar * ar + ai * ai
    nr, ni = lr - 1.0, li
    fr, fi = (nr * ar + ni * ai) / den, (ni * ar - nr * ai) / den
    br, bim = b_re.astype(F32), b_im.astype(F32)
    bbr = fr[..., None] * br - fi[..., None] * bim
    bbi = fr[..., None] * bim + fi[..., None] * br
    jv = jnp.arange(T + 1, dtype=F32)[:, None, None]
    pmag = jnp.exp(jv * (ar * dt)[None])
    pwr, pwi = pmag * jnp.cos(jv * (ai * dt)[None]), pmag * jnp.sin(jv * (ai * dt)[None])
    zr = pwr[..., None] * bbr[None] - pwi[..., None] * bbi[None]
    zi = pwr[..., None] * bbi[None] + pwi[..., None] * bbr[None]
    cr, ci = c_re.astype(F32), c_im.astype(F32)
    kj = (jnp.einsum('gcp,jgpd->gjcd', cr, zr[:T], precision=HIGHEST)
          - jnp.einsum('gcp,jgpd->gjcd', ci, zi[:T], precision=HIGHEST))
    GT, NT = S5_GT, Gs // S5_GT
    kt = kj.reshape(NT, GT, T, C, C).transpose(0, 2, 4, 1, 3).reshape(NT, T, C, GT * C)
    lay_z = lambda z: z[:T][::-1].reshape(T, NT, GT, P, C).transpose(1, 0, 4, 2, 3).reshape(NT, T, C, GT * P)
    zt = jnp.concatenate([lay_z(zr), lay_z(zi)], axis=-1)
    pa_r, pa_i = pwr[1:].transpose(1, 0, 2)[:, :, None, :], pwi[1:].transpose(1, 0, 2)[:, :, None, :]
    clr = cr[:, None] * pa_r - ci[:, None] * pa_i
    cli = cr[:, None] * pa_i + ci[:, None] * pa_r
    lay_c = lambda x: x.reshape(NT, GT, T, C, P).transpose(0, 1, 4, 2, 3).reshape(NT, GT * P, T * C)
    clc = jnp.concatenate([lay_c(clr), lay_c(-cli)], axis=1).astype(BF16)
    qr, qi = pwr[T], pwi[T]
    steps = []
    d = 1
    while d < nk:
        steps.append(jnp.concatenate([qr.reshape(NT, GT * P), qi.reshape(NT, GT * P)], -1))
        qr, qi = _cmul(qr, qi, qr, qi)
        d *= 2
    lamp = jnp.stack(steps, 1)
    col = np.arange(T * C)[:, None]
    lane = np.arange(GT * C)[None, :]
    ex = np.stack([(col // C == a) & (col % C == lane % C) for a in range(T)]).astype(np.float32)
    return kt, zt, clc, lamp, jnp.asarray(ex, BF16)


def _glu_kernel(y_ref, u_ref, zb_ref, d_ref, w_ref, b_ref, o_ref):
    y = y_ref[...] + d_ref[...] * u_ref[...].astype(F32)
    yg = jax.nn.gelu(y).astype(BF16)
    z = jnp.dot(yg, w_ref[...], preferred_element_type=F32) + b_ref[...]
    zb = zb_ref[...].astype(F32)
    o_ref[...] = (yg.astype(F32) * jax.nn.sigmoid(z) * (zb * jax.nn.sigmoid(zb))).astype(o_ref.dtype)


def _s5_glu(y, proj, d_skip, w_glu, b_glu):
    N, W = y.shape
    tm = min(1024, N)
    return pl.pallas_call(
        _glu_kernel,
        grid=(N // tm,),
        in_specs=[pl.BlockSpec((tm, W), lambda i: (i, 0)),
                  pl.BlockSpec((tm, W), lambda i: (i, COL_US // W)),
                  pl.BlockSpec((tm, W), lambda i: (i, COL_ZB // W)),
                  pl.BlockSpec((1, W), lambda i: (0, 0)),
                  pl.BlockSpec((W, W), lambda i: (0, 0)),
                  pl.BlockSpec((1, W), lambda i: (0, 0))],
        out_specs=pl.BlockSpec((tm, W), lambda i: (i, 0)),
        out_shape=jax.ShapeDtypeStruct((N, W), BF16),
        compiler_params=_cparams(("parallel",)),
        name="s5_glu",
    )(y, proj, proj, d_skip.reshape(1, W), w_glu, b_glu.reshape(1, W))


def _merge_kernel(oa_ref, ob_ref, ga_ref, gb_ref, x_ref, gate_ref, wa_ref, wb_ref, wo_ref, lg_ref, lb_ref, o_ref):
    pa = jnp.dot(oa_ref[...], wa_ref[...], preferred_element_type=F32)
    pb = jnp.dot(ob_ref[...], wb_ref[...], preferred_element_type=F32)
    m = jax.nn.sigmoid(ga_ref[...].astype(F32)) * pa + jax.nn.sigmoid(gb_ref[...].astype(F32)) * pb
    y = jnp.dot(m.astype(BF16), wo_ref[...], preferred_element_type=F32)
    r = DEEPNORM_ALPHA * x_ref[...] + gate_ref[...] * y
    mu = jnp.mean(r, axis=-1, keepdims=True)
    rc = r - mu
    var = jnp.mean(rc * rc, axis=-1, keepdims=True)
    o_ref[...] = rc * lax.rsqrt(var + LN_EPS) * lg_ref[...] + lb_ref[...]


def _merge_out(oa, ob, proj, x2, gate, wa, wb, wo, ln_g, ln_b, L):
    N, D = x2.shape
    W = oa.shape[1]
    B = gate.shape[0]
    tm = min(256, L)
    const = lambda i: (0, 0)
    return pl.pallas_call(
        _merge_kernel,
        grid=(N // tm,),
        in_specs=[pl.BlockSpec((tm, W), lambda i: (i, 0)),
                  pl.BlockSpec((tm, W), lambda i: (i, 0)),
                  pl.BlockSpec((tm, D), lambda i: (i, COL_GA // D)),
                  pl.BlockSpec((tm, D), lambda i: (i, COL_GB // D)),
                  pl.BlockSpec((tm, D), lambda i: (i, 0)),
                  pl.BlockSpec((None, 1, D), lambda i: ((i * tm) // L, 0, 0)),
                  pl.BlockSpec((W, D), const),
                  pl.BlockSpec((W, D), const),
                  pl.BlockSpec((D, D), const),
                  pl.BlockSpec((1, D), const),
                  pl.BlockSpec((1, D), const)],
        out_specs=pl.BlockSpec((tm, D), lambda i: (i, 0)),
        out_shape=jax.ShapeDtypeStruct((N, D), F32),
        compiler_params=_cparams(("parallel",)),
        name="merge_out",
    )(oa, ob, proj, proj, x2, gate.reshape(B, 1, D), wa, wb, wo, ln_g.reshape(1, D), ln_b.reshape(1, D))


def _layer(x, c, w_ada, b_ada, w_in, rel_bias, cmp_pos_k, cmp_pos_v, w_cmp_k1, w_cmp_k2, w_cmp_v1, w_cmp_v2,
           ssm_a_re, ssm_a_im, ssm_log_dt, ssm_b_re, ssm_b_im, ssm_c_re, ssm_c_im, ssm_d, w_glu, b_glu,
           w_branch_nsa, w_branch_ssm, w_out, ln_g, ln_b):
    B, L, D = x.shape
    N = B * L
    G, dh = NSA_GROUPS, NSA_HEAD_DIM
    x2 = x.reshape(N, D)

    mod = _ada_mod(c, w_ada, b_ada)
    shift, scale, gate = mod[:, :D], mod[:, D:2 * D], mod[:, 2 * D:]

    o_q, o_kv, o_ng, o_za, o_us, o_zb, o_ga, o_gb = np.cumsum(
        [0, NSA_WIDTH, 6 * KV_WIDTH, 3 * NSA_HEADS, NSA_WIDTH, SSM_WIDTH, SSM_WIDTH, D_MODEL]).tolist()
    w_re = jnp.concatenate([
        w_in[:, o_q:o_kv], w_in[:, o_za:o_us], w_in[:, o_us:o_zb], w_in[:, o_zb:o_ga], w_in[:, o_ga:o_gb],
        w_in[:, o_gb:], w_in[:, o_kv + 2 * KV_WIDTH:o_ng], w_in[:, o_kv:o_kv + 2 * KV_WIDTH], w_in[:, o_ng:o_za],
        jnp.zeros((D, PROJ_COLS - COL_NG - 3 * NSA_HEADS), w_in.dtype)], axis=1).astype(BF16)
    proj = _in_proj(x2, scale, shift, w_re, L)

    w1s = jnp.stack([w_cmp_k1, w_cmp_v1])
    w2s = jnp.stack([w_cmp_k2, w_cmp_v2])
    poss = jnp.broadcast_to(jnp.stack([cmp_pos_k, cmp_pos_v]).reshape(2, 1, CMP_BLOCK * dh), (2, 8, CMP_BLOCK * dh))
    kcv = _nsa_compress(proj, w1s, w2s, poss, B, L)
    kc = kcv[0].astype(BF16)
    vct = kcv[1].swapaxes(-1, -2).astype(BF16)
    ks, vst, kw, vwt = _kv_prep(proj, B, L)
    tc, tw, ts, ovt = _nsa_tables(rel_bias, L)
    o_a = _nsa_attend(proj, kc, vct, ks, vst, kw, vwt, tc, tw, ts, ovt, B, L)

    s5_tabs = _s5_tables(ssm_a_re, ssm_a_im, ssm_log_dt, ssm_b_re, ssm_b_im, ssm_c_re, ssm_c_im, L // SSM_CHUNK)
    y = _s5_scan(proj, *s5_tabs, B, L)
    o_b = _s5_glu(y, proj, ssm_d, w_glu.astype(BF16), b_glu)

    out = _merge_out(o_a, o_b, proj, x2, gate, w_branch_nsa.astype(BF16), w_branch_ssm.astype(BF16),
                     w_out.astype(BF16), ln_g, ln_b, L)
    return out.reshape(B, L, D)


def kernel(x, c, w_ada, b_ada, w_in, rel_bias, cmp_pos_k, cmp_pos_v, w_cmp_k1, w_cmp_k2, w_cmp_v1, w_cmp_v2,
           ssm_a_re, ssm_a_im, ssm_log_dt, ssm_b_re, ssm_b_im, ssm_c_re, ssm_c_im, ssm_d, w_glu, b_glu,
           w_branch_nsa, w_branch_ssm, w_out, ln_g, ln_b):
    for i in range(w_ada.shape[0]):
        x = _layer(x, c, w_ada[i], b_ada[i], w_in[i], rel_bias, cmp_pos_k[i], cmp_pos_v[i], w_cmp_k1[i],
                   w_cmp_k2[i], w_cmp_v1[i], w_cmp_v2[i], ssm_a_re[i], ssm_a_im[i], ssm_log_dt[i], ssm_b_re[i],
                   ssm_b_im[i], ssm_c_re[i], ssm_c_im[i], ssm_d[i], w_glu[i], b_glu[i], w_branch_nsa[i],
                   w_branch_ssm[i], w_out[i], ln_g[i], ln_b[i])
    return x
```

```python
import functools
import math

import numpy as np
import jax
import jax.numpy as jnp
from jax import lax
from jax.experimental import pallas as pl
from jax.experimental.pallas import tpu as pltpu

F32 = jnp.float32
BF16 = jnp.bfloat16
HIGHEST = lax.Precision.HIGHEST

D_MODEL = 2048
NSA_HEADS = 16
NSA_GROUPS = 4
NSA_HEAD_DIM = 64
NSA_REP = NSA_HEADS // NSA_GROUPS
NSA_WIDTH = NSA_HEADS * NSA_HEAD_DIM
KV_WIDTH = NSA_GROUPS * NSA_HEAD_DIM
CMP_BLOCK = 32
CMP_STRIDE = 16
CMP_HIDDEN = 128
SEL_BLOCK = 64
SEL_TOPK = 8
WINDOW = 512
Q_BLOCK = 128
SSM_WIDTH = 1024
SSM_GROUP = 16
SSM_GROUPS = SSM_WIDTH // SSM_GROUP
SSM_STATE = 64
REL_BUCKETS = 32
REL_MAX_DIST = 128
DEEPNORM_ALPHA = 2.0 ** 0.25
LN_EPS = 1e-5
MASK_VALUE = -1e30
FORCE_VALUE = 1e4
NEVER_VALUE = -3e38
SSM_CHUNK = 16

COL_Q, COL_ZA, COL_US, COL_ZB, COL_GA, COL_GB, COL_KV, COL_KC, COL_NG = 0, 1024, 2048, 3072, 4096, 6144, 8192, 9216, 9728
PROJ_COLS = 9856
VMEM_LIMIT = 56 * 1024 * 1024


def _cparams(sem):
    return pltpu.CompilerParams(dimension_semantics=sem, vmem_limit_bytes=VMEM_LIMIT)


def _ada_kernel(c_ref, w_ref, b_ref, o_ref):
    o_ref[...] = jnp.dot(c_ref[...], w_ref[...], preferred_element_type=F32, precision=HIGHEST) + b_ref[...]


def _ada_mod(c, w_ada, b_ada):
    B, D = c.shape
    n = w_ada.shape[1]
    tn = 1536
    return pl.pallas_call(
        _ada_kernel,
        grid=(n // tn,),
        in_specs=[pl.BlockSpec((B, D), lambda j: (0, 0)),
                  pl.BlockSpec((D, tn), lambda j: (0, j)),
                  pl.BlockSpec((1, tn), lambda j: (0, j))],
        out_specs=pl.BlockSpec((B, tn), lambda j: (0, j)),
        out_shape=jax.ShapeDtypeStruct((B, n), F32),
        compiler_params=_cparams(("parallel",)),
        name="ada_mod",
    )(c, w_ada, b_ada.reshape(1, n))


def _inproj_kernel(x_ref, scale_ref, shift_ref, w_ref, o_ref, h_ref):
    @pl.when(pl.program_id(1) == 0)
    def _():
        def rows(k, carry):
            r0 = pl.multiple_of(k * LN_ROWS, LN_ROWS)
            x = x_ref[pl.ds(r0, LN_ROWS), :]
            mu = jnp.mean(x, axis=-1, keepdims=True)
            xc = x - mu
            var = jnp.mean(xc * xc, axis=-1, keepdims=True)
            hn = xc * lax.rsqrt(var + LN_EPS)
            h_ref[pl.ds(r0, LN_ROWS), :] = (hn * (1.0 + scale_ref[...]) + shift_ref[...]).astype(BF16)
            return carry

        lax.fori_loop(0, x_ref.shape[0] // LN_ROWS, rows, 0)

    o_ref[...] = jnp.dot(h_ref[...], w_ref[...], preferred_element_type=F32).astype(o_ref.dtype)


LN_ROWS = 256


def _in_proj(x2, scale, shift, w, L):
    N, D = x2.shape
    ncol = w.shape[1]
    tm = min(1024, L)
    tn = 1408
    assert N % tm == 0 and L % tm == 0 and ncol % tn == 0
    B = scale.shape[0]
    return pl.pallas_call(
        _inproj_kernel,
        grid=(N // tm, ncol // tn),
        in_specs=[pl.BlockSpec((tm, D), lambda i, j: (i, 0)),
                  pl.BlockSpec((None, 1, D), lambda i, j: ((i * tm) // L, 0, 0)),
                  pl.BlockSpec((None, 1, D), lambda i, j: ((i * tm) // L, 0, 0)),
                  pl.BlockSpec((D, tn), lambda i, j: (0, j))],
        out_specs=pl.BlockSpec((tm, tn), lambda i, j: (i, j)),
        out_shape=jax.ShapeDtypeStruct((N, ncol), BF16),
        scratch_shapes=[pltpu.VMEM((tm, D), BF16)],
        compiler_params=_cparams(("parallel", "arbitrary")),
        name="in_proj",
    )(x2, scale.reshape(B, 1, D), shift.reshape(B, 1, D), w)


def _cmp_kernel(x_ref, w1_ref, w1bd_ref, w2_ref, pos_ref, o_ref, xs_ref, *, ncp):
    S, dh = CMP_STRIDE, NSA_HEAD_DIM
    xs_ref[...] = x_ref[...].astype(F32)
    a = jnp.zeros((ncp, 2 * CMP_HIDDEN), F32)
    b = jnp.zeros((ncp, 2 * CMP_HIDDEN), F32)
    for j in range(S):
        xj = xs_ref[pl.ds(j, ncp, stride=S), :].astype(BF16)
        a = a + jnp.dot(xj, w1bd_ref[0, j], preferred_element_type=F32)
        b = b + jnp.dot(xj, w1bd_ref[1, j], preferred_element_type=F32)
    pw = jnp.dot(pos_ref[...], w1_ref[...], preferred_element_type=F32, precision=HIGHEST)[0:1]
    h = a + pltpu.roll(b, ncp - 1, 0) + jnp.concatenate([pw, pw], axis=1)
    w2 = w2_ref[...].astype(BF16)
    for g in range(2):
        hg = jax.nn.gelu(h[:, g * CMP_HIDDEN:(g + 1) * CMP_HIDDEN]).astype(BF16)
        o_ref[g] = jnp.dot(hg, w2, preferred_element_type=F32)


def _nsa_compress(proj, w1s, w2s, poss, B, L):
    G, dh, S = NSA_GROUPS, NSA_HEAD_DIM, CMP_STRIDE
    ncp = L // S
    w = w1s.reshape(2, 2, S, dh, CMP_HIDDEN).astype(BF16)
    z = jnp.zeros_like(w)
    w1bd = jnp.concatenate([jnp.concatenate([w, z], -1), jnp.concatenate([z, w], -1)], axis=-2)
    return pl.pallas_call(
        functools.partial(_cmp_kernel, ncp=ncp),
        grid=(2, B, G // 2),
        in_specs=[pl.BlockSpec((L, 2 * dh), lambda s, b, t: (b, COL_KC // (2 * dh) + 2 * s + t)),
                  pl.BlockSpec((None,) + w1s.shape[1:], lambda s, b, t: (s, 0, 0)),
                  pl.BlockSpec((None,) + w1bd.shape[1:], lambda s, b, t: (s, 0, 0, 0, 0)),
                  pl.BlockSpec((None,) + w2s.shape[1:], lambda s, b, t: (s, 0, 0)),
                  pl.BlockSpec((None,) + poss.shape[1:], lambda s, b, t: (s, 0, 0))],
        out_specs=pl.BlockSpec((None, None, 2, ncp, dh), lambda s, b, t: (s, b, t, 0, 0)),
        out_shape=jax.ShapeDtypeStruct((2, B, G, ncp, dh), F32),
        scratch_shapes=[pltpu.VMEM((L, 2 * dh), F32)],
        compiler_params=_cparams(("parallel", "parallel", "parallel")),
        name="nsa_compress",
    )(proj, w1s, w1bd, w2s, poss)


V_ROWS = NSA_HEAD_DIM + 16


def _kvprep_kernel(x_ref, ks_ref, vst_ref, kw_ref, vwt_ref, *, TL):
    G, dh = NSA_GROUPS, NSA_HEAD_DIM
    i = pl.program_id(1)
    tile = jnp.maximum(i - 1, 0)
    x = x_ref[...].astype(F32)
    lane = lax.broadcasted_iota(jnp.int32, (TL, 2 * dh), 1)
    row = lax.broadcasted_iota(jnp.int32, (TL, 2 * dh), 0) + tile * TL
    onehot = jnp.where(lane - dh == row // SEL_BLOCK, 1.0, 0.0)
    padrow = jnp.where(lane == dh, 1.0, 0.0)
    tail = jnp.where(lax.broadcasted_iota(jnp.int32, (V_ROWS - dh, TL), 0) == 0, 1.0, 0.0)
    is_pad = i == 0
    for t in range(G // 2):
        xk, xv, xwk, xwv = [x[:, (s * G + 2 * t) * dh:(s * G + 2 * t + 2) * dh] for s in range(4)]
        xvt, xwvt = xv.T, xwv.T
        for h in range(2):
            g = 2 * t + h
            kk = xk if h == 0 else pltpu.roll(xk, dh, 1)
            kwk = xwk if h == 0 else pltpu.roll(xwk, dh, 1)
            ks_ref[g] = jnp.where(lane < dh, kk, onehot).astype(BF16)
            kw_ref[g] = jnp.where(is_pad, padrow, jnp.where(lane < dh, kwk, 0.0)).astype(BF16)
            vst_ref[g] = jnp.concatenate([xvt[h * dh:(h + 1) * dh], tail], axis=0).astype(BF16)
            vw = jnp.concatenate([xwvt[h * dh:(h + 1) * dh], tail], axis=0)
            vwt_ref[g] = jnp.where(is_pad, 0.0, vw).astype(BF16)


def _kv_prep(proj, B, L):
    G, dh, TL = NSA_GROUPS, NSA_HEAD_DIM, WINDOW
    nt = L // TL
    wcols = 4 * G * dh
    assert L // SEL_BLOCK <= dh and COL_KV % wcols == 0 and L % TL == 0
    data = lambda b, i: jnp.maximum(i - 1, 0)
    k_shape = jax.ShapeDtypeStruct((B, G, L, 2 * dh), BF16)
    v_shape = jax.ShapeDtypeStruct((B, G, V_ROWS, L), BF16)
    kw_shape = jax.ShapeDtypeStruct((B, G, L + WINDOW, 2 * dh), BF16)
    vw_shape = jax.ShapeDtypeStruct((B, G, V_ROWS, L + WINDOW), BF16)
    return pl.pallas_call(
        functools.partial(_kvprep_kernel, TL=TL),
        grid=(B, nt + 1),
        in_specs=[pl.BlockSpec((TL, wcols), lambda b, i: (b * nt + data(b, i), COL_KV // wcols))],
        out_specs=[pl.BlockSpec((None, G, TL, 2 * dh), lambda b, i: (b, 0, data(b, i), 0)),
                   pl.BlockSpec((None, G, V_ROWS, TL), lambda b, i: (b, 0, 0, data(b, i))),
                   pl.BlockSpec((None, G, TL, 2 * dh), lambda b, i: (b, 0, i, 0)),
                   pl.BlockSpec((None, G, V_ROWS, TL), lambda b, i: (b, 0, 0, i))],
        out_shape=[k_shape, v_shape, kw_shape, vw_shape],
        compiler_params=_cparams(("parallel", "arbitrary")),
        name="kv_prep",
    )(proj)


SEL_TILE = 512
SEL_TABLE_FAR = 640
NSA_PROBLEMS = 4


def _tree(op, parts):
    while len(parts) > 1:
        parts = [op(parts[i], parts[i + 1]) if i + 1 < len(parts) else parts[i] for i in range(0, len(parts), 2)]
    return parts[0]


def _fold8(op, x):
    return _tree(op, [x[k:k + 8] for k in range(0, x.shape[0], 8)])


def _safe_inv(l):
    return jnp.where(l > 0.0, 1.0 / jnp.where(l > 0.0, l, 1.0), 0.0)


def _nsa_kernel(q_ref, za_ref, ng_ref, kc_ref, vct_ref, ks_ref, vst_ref, kw_ref, vwt_ref,
                tc_ref, tw_ref, ts_ref, ovt_ref, o_ref, s_ref, s2_ref, acc_ref, gt_ref, *, L):
    R, dh, QB = NSA_REP, NSA_HEAD_DIM, Q_BLOCK
    ncp = L // CMP_STRIDE
    nsel = L // SEL_BLOCK
    W = R * QB
    g = pl.program_id(1)
    qb = pl.program_id(2)

    NP = q_ref.shape[0]

    def front(p):
        qt = (q_ref[p].astype(F32) * (dh ** -0.5)).T
        qT = jnp.concatenate([qt[r * dh:(r + 1) * dh] for r in range(R)], axis=1).astype(BF16)

        c_off = pl.multiple_of((ncp - 8) - 8 * qb, 8)
        sc = jnp.dot(kc_ref[p], qT, preferred_element_type=F32) + tc_ref[pl.ds(c_off, ncp), :]
        m = jnp.maximum(jnp.max(_fold8(jnp.maximum, sc), axis=0, keepdims=True), 0.1 * MASK_VALUE)
        e = jnp.exp(sc - m)
        l = jnp.sum(_fold8(jnp.add, e), axis=0, keepdims=True)
        p_c = e * _safe_inv(l)
        oc = jnp.dot(vct_ref[p], p_c.astype(BF16), preferred_element_type=F32)

        psum = _tree(jnp.add, [p_c[:, r * QB:(r + 1) * QB] for r in range(R)])
        p_hi = psum.astype(BF16)
        p_lo = (psum - p_hi.astype(F32)).astype(BF16)
        ovt = ovt_ref[...]
        imp = jnp.dot(ovt, p_hi, preferred_element_type=F32) + jnp.dot(ovt, p_lo, preferred_element_type=F32)
        jj = lax.broadcasted_iota(jnp.int32, (nsel, QB), 0)
        ii = lax.broadcasted_iota(jnp.int32, (nsel, QB), 1)
        cur = 2 * qb + (ii >= SEL_BLOCK).astype(jnp.int32)
        forced = (jj == 0) | (jj == cur) | (jj == cur - 1)
        imp = jnp.where(forced, FORCE_VALUE, imp)
        imp = jnp.where(jj <= cur, imp, MASK_VALUE)
        jf = jj.astype(F32)
        sel = jnp.zeros((nsel, QB), F32)
        for _ in range(min(SEL_TOPK, nsel)):
            mx = jnp.max(_fold8(jnp.maximum, imp), axis=0, keepdims=True)
            idx = jnp.min(_fold8(jnp.minimum, jnp.where(imp == mx, jf, 1e9)), axis=0, keepdims=True)
            hit = jf == idx
            sel = jnp.where(hit & (mx > 0.1 * MASK_VALUE), 1.0, sel)
            imp = jnp.where(hit, -jnp.inf, imp)
        selbias = jnp.where(sel > 0.5, 0.0, MASK_VALUE).astype(BF16)
        sel_rows = [jnp.concatenate([selbias] * R, axis=1)]
        if nsel < dh:
            sel_rows.append(jnp.zeros((dh - nsel, W), BF16))
        q_sel = jnp.concatenate([qT] + sel_rows, axis=0)

        w0 = pl.multiple_of(qb * QB, QB)
        flag = jnp.where(lax.broadcasted_iota(jnp.int32, (dh, W), 0) == 0, MASK_VALUE, 0.0).astype(BF16)
        q_win = jnp.concatenate([qT, flag], axis=0)
        sw = jnp.dot(kw_ref[p, pl.ds(w0, WINDOW + QB), :], q_win, preferred_element_type=F32) + tw_ref[...]
        m_w = jnp.max(_fold8(jnp.maximum, sw), axis=0, keepdims=True)
        pw = jnp.exp((sw - m_w).astype(BF16))
        accw = jnp.dot(vwt_ref[p, :, pl.ds(w0, WINDOW + QB)], pw, preferred_element_type=F32)
        owin = accw[:dh] * _safe_inv(accw[dh:dh + 1])

        gt_ref[p] = jax.nn.sigmoid(ng_ref[p].astype(F32)).T

        def gate_row(branch):
            return jnp.concatenate([gt_ref[p, pl.ds(3 * (g * R + r) + branch, 1), :] for r in range(R)], axis=1)

        return q_sel, oc * gate_row(0) + owin * gate_row(2), gate_row(1)

    fronts = [front(p) for p in range(NP)]

    ntile = qb // (SEL_TILE // QB) + 1
    npair = ntile // 2
    odd = ntile % 2

    slots = (s_ref, s2_ref)

    def scores(t0, p, slot, nt=2):
        mx = None
        for h in range(nt):
            t = t0 + h
            k0 = pl.multiple_of(t * SEL_TILE, SEL_TILE)
            x0 = pl.multiple_of(jnp.maximum(t * SEL_TILE - qb * QB + SEL_TABLE_FAR, 0), QB)
            s = (jnp.dot(ks_ref[p, pl.ds(k0, SEL_TILE), :], fronts[p][0], preferred_element_type=F32)
                 + ts_ref[pl.ds(x0, SEL_TILE), :])
            slots[slot][p, h * SEL_TILE:(h + 1) * SEL_TILE, :] = s
            f = _fold8(jnp.maximum, s)
            mx = f if mx is None else jnp.maximum(mx, f)
        return jnp.max(mx, axis=0, keepdims=True)

    def accumulate(t0, p, slot, m_new, m_old, nt=2):
        part = None
        for h in range(nt):
            k0 = pl.multiple_of((t0 + h) * SEL_TILE, SEL_TILE)
            pr = jnp.exp((slots[slot][p, h * SEL_TILE:(h + 1) * SEL_TILE, :] - m_new).astype(BF16))
            pv = jnp.dot(vst_ref[p, :, pl.ds(k0, SEL_TILE)], pr, preferred_element_type=F32)
            part = pv if part is None else part + pv
        acc_ref[p] = acc_ref[p] * jnp.exp(m_old - m_new) + part

    def phase(t0, slot, m_cur, m_acc, nt_next=2):
        m_next = []
        for p in range(NP):
            accumulate(t0, p, slot, m_cur[p], m_acc[p])
            m_next.append(jnp.maximum(m_cur[p], scores(t0 + 2, p, 1 - slot, nt_next)))
        return tuple(m_next), m_cur

    def finish(t0, slot, m_cur, m_acc):
        @pl.when(odd == 1)
        def _():
            m2, m1 = phase(t0, slot, m_cur, m_acc, nt_next=1)
            for p in range(NP):
                accumulate(t0 + 2, p, 1 - slot, m2[p], m1[p], nt=1)

        @pl.when(odd == 0)
        def _():
            for p in range(NP):
                accumulate(t0, p, slot, m_cur[p], m_acc[p])

    acc_ref[...] = jnp.zeros_like(acc_ref)

    @pl.when(npair == 0)
    def _():
        for p in range(NP):
            m = scores(0, p, 0, nt=1)
            accumulate(0, p, 0, m, m, nt=1)

    @pl.when(npair > 0)
    def _():
        m_first = tuple(scores(0, p, 0) for p in range(NP))

        def trip(v, carry):
            carry = phase(4 * v, 0, *carry)
            return phase(4 * v + 2, 1, *carry)

        n_rest = npair - 1
        m_cur, m_acc = lax.fori_loop(0, n_rest // 2, trip, (m_first, m_first))
        t_last = 4 * (n_rest // 2)

        @pl.when(n_rest % 2 == 1)
        def _():
            finish(t_last + 2, 1, *phase(t_last, 0, m_cur, m_acc))

        @pl.when(n_rest % 2 == 0)
        def _():
            finish(t_last, 0, m_cur, m_acc)

    for p in range(NP):
        acc = acc_ref[p]
        osel = acc[:dh] * _safe_inv(acc[dh:dh + 1])
        ot = fronts[p][1] + osel * fronts[p][2]
        o = jnp.concatenate([ot[:, r * QB:(r + 1) * QB] for r in range(R)], axis=0).T
        za = za_ref[p].astype(F32)
        o_ref[p] = (o * (za * jax.nn.sigmoid(za))).astype(o_ref.dtype)


def _nsa_attend(proj, kc, vct, ks, vst, kw, vwt, tc, tw, ts, ovt, B, L):
    R, dh, G, QB = NSA_REP, NSA_HEAD_DIM, NSA_GROUPS, Q_BLOCK
    assert L % (2 * SEL_TILE) == 0
    nqb = L // QB
    gw = R * dh
    W = R * QB
    NP = NSA_PROBLEMS if B % NSA_PROBLEMS == 0 else 1
    proj4 = proj.reshape(B // NP, NP, L, proj.shape[-1])
    tok = lambda col: pl.BlockSpec((None, NP, QB, col[1]), lambda b, g, i: (b, 0, i, col[0] // col[1] + col[2] * g))
    bg = lambda a: pl.BlockSpec((NP, None) + a.shape[2:], lambda b, g, i: (b, g, 0, 0),
                                pipeline_mode=pl.Buffered(1))
    grp = lambda a: pl.BlockSpec((None,) + a.shape[1:], lambda b, g, i: (g, 0, 0), pipeline_mode=pl.Buffered(1))
    out = pl.pallas_call(
        functools.partial(_nsa_kernel, L=L),
        grid=(B // NP, G, nqb),
        in_specs=[tok((COL_Q, gw, 1)), tok((COL_ZA, gw, 1)), tok((COL_NG, 128, 0)),
                  bg(kc), bg(vct), bg(ks), bg(vst), bg(kw), bg(vwt), grp(tc), grp(tw), grp(ts),
                  pl.BlockSpec(ovt.shape, lambda b, g, i: (0, 0))],
        out_specs=pl.BlockSpec((None, NP, QB, gw), lambda b, g, i: (b, 0, i, g)),
        out_shape=jax.ShapeDtypeStruct((B // NP, NP, L, NSA_WIDTH), BF16),
        scratch_shapes=[pltpu.VMEM((NP, 2 * SEL_TILE, W), F32), pltpu.VMEM((NP, 2 * SEL_TILE, W), F32),
                        pltpu.VMEM((NP, V_ROWS, W), F32),
                        pltpu.VMEM((NP, 128, QB), F32)],
        compiler_params=_cparams(("parallel", "parallel", "arbitrary")),
        name="nsa_attend",
    )(proj4, proj4, proj4, kc, vct, ks, vst, kw, vwt, tc, tw, ts, ovt)
    return out.reshape(B * L, NSA_WIDTH)


def _t5_bucket(dist):
    n = jnp.maximum(dist, 0)
    max_exact = REL_BUCKETS // 2
    nf = jnp.maximum(n, max_exact).astype(F32)
    large = max_exact + (jnp.log(nf / max_exact) / math.log(REL_MAX_DIST / max_exact)
                         * (REL_BUCKETS - max_exact)).astype(jnp.int32)
    large = jnp.minimum(large, REL_BUCKETS - 1)
    return jnp.where(n < max_exact, n, large)


def _toeplitz(f, d0, base, step, n_rows, width):
    rpb = width // step
    nblk = n_rows // rpb
    assert rpb * step == width and nblk * rpb == n_rows
    lo = base - width * nblk - d0
    return f[..., lo:lo + width * (nblk + 1)].reshape(f.shape[:-1] + (nblk + 1, width))


TABLE_STEPS = (CMP_STRIDE, 1, 1)


def _tables_kernel(*refs, runs):
    R, QB = NSA_REP, Q_BLOCK
    n = len(TABLE_STEPS)
    tmp_ref = refs[2 * n]
    for win_ref, out_ref, step, table_runs in zip(refs[:n], refs[n:2 * n], TABLE_STEPS, runs):
        rpb = QB // step

        def block(a, carry, varies, win_ref=win_ref, out_ref=out_ref, step=step, rpb=rpb):
            r0 = pl.multiple_of(a * rpb, rpb)
            top = win_ref.shape[1] - 1 - a
            for r in range(R):
                if varies:
                    win = jnp.concatenate([win_ref[r, pl.ds(top, 1), :], win_ref[r, pl.ds(top - 1, 1), :]], axis=1)
                    x = jnp.broadcast_to(win, (QB, 2 * QB))
                    y = pltpu.roll(x, 0, 1, stride=1, stride_axis=0)[:, :QB]
                    if step > 1:
                        tmp_ref[...] = y
                        y = tmp_ref[pl.ds(0, rpb, stride=step), :]
                else:
                    y = jnp.broadcast_to(win_ref[r, pl.ds(top, 1), :], (rpb, QB))
                out_ref[pl.ds(r0, rpb), r * QB:(r + 1) * QB] = y
            return carry

        for first, end, varies in table_runs:
            lax.fori_loop(first, end, functools.partial(block, varies=varies), 0)


def _constant_runs(base, step, n_rows, upper):
    far = REL_MAX_DIST
    rpb = Q_BLOCK // step
    flags = []
    for a in range(n_rows // rpb):
        lo, hi = base - Q_BLOCK * a - Q_BLOCK + step, base - Q_BLOCK * a + Q_BLOCK - 1
        flags.append(not (hi < 0 or lo >= upper or (lo >= far and hi < upper)))
    runs, start = [], 0
    for a in range(1, len(flags) + 1):
        if a == len(flags) or flags[a] != flags[start]:
            runs.append((start, a, flags[start]))
            start = a
    return tuple(runs)


def _expand_tables(wins, runs):
    G, R, QB = NSA_GROUPS, NSA_REP, Q_BLOCK
    rows = [(w.shape[1] - 1) * (QB // s) for w, s in zip(wins, TABLE_STEPS)]
    return pl.pallas_call(
        functools.partial(_tables_kernel, runs=runs),
        grid=(G,),
        in_specs=[pl.BlockSpec((R,) + w.shape[1:], lambda g: (g, 0, 0)) for w in wins],
        out_specs=[pl.BlockSpec((None, n, R * QB), lambda g: (g, 0, 0)) for n in rows],
        out_shape=[jax.ShapeDtypeStruct((G, n, R * QB), F32) for n in rows],
        scratch_shapes=[pltpu.VMEM((QB, QB), F32)],
        compiler_params=_cparams(("parallel",)),
        name="nsa_tables",
    )(*wins)


def _nsa_tables(rel_bias, L):
    QB, G, R = Q_BLOCK, NSA_GROUPS, NSA_REP
    ncp = L // CMP_STRIDE
    nsel = L // SEL_BLOCK
    npos = L + QB
    tbl = rel_bias.astype(F32)
    bpos = (tbl[_t5_bucket(jnp.arange(npos, dtype=jnp.int32))] - tbl[REL_BUCKETS - 1][None, :]).T
    f = jnp.concatenate([jnp.full((NSA_HEADS, npos), MASK_VALUE, F32), bpos], axis=-1)
    d = np.arange(-npos, npos)
    fw = jnp.where(jnp.asarray(d < WINDOW), f, MASK_VALUE)
    spec_c = (L - QB - CMP_BLOCK + 1, CMP_STRIDE, 2 * ncp - 8)
    spec_w = (WINDOW, 1, WINDOW + QB)
    spec_s = (SEL_TABLE_FAR, 1, SEL_TABLE_FAR + SEL_TILE)
    assert TABLE_STEPS == (spec_c[1], spec_w[1], spec_s[1])
    wins = [_toeplitz(src, -npos, *spec, QB) for src, spec in ((f, spec_c), (fw, spec_w), (f, spec_s))]
    runs = tuple(_constant_runs(*spec, upper) for spec, upper in ((spec_c, 2 * npos), (spec_w, WINDOW), (spec_s, 2 * npos)))
    tc, tw, ts = _expand_tables(wins, runs)
    c = np.arange(ncp)[None, :]
    j = np.arange(nsel)[:, None]
    ovt = ((c - 4 * j >= -1) & (c - 4 * j <= 3)).astype(np.float32)
    return tc, tw, ts, jnp.asarray(ovt, BF16)


S5_LANES = 128
S5_GT = S5_LANES // SSM_GROUP
S5_SW = S5_GT * SSM_STATE
S5_COLS = 4 * S5_LANES


def _s5_kernel(us_ref, kt_ref, zt_ref, cl_ref, lam_ref, ex_ref, y_ref,
               xs_ref, u8_ref, m8_ref, ws8_ref, wo8_ref, *, nk):
    T, C, P, GT, LT, SW = SSM_CHUNK, SSM_GROUP, SSM_STATE, S5_GT, S5_LANES, S5_SW

    @pl.when(pl.program_id(1) == 0)
    def _build():
        tile = lambda x: jnp.concatenate([x] * GT, axis=0)
        r = lax.broadcasted_iota(jnp.int32, (LT, LT), 0)
        c = lax.broadcasted_iota(jnp.int32, (LT, LT), 1)
        same = (r // C) == (c // C)
        m8_ref[...] = jnp.zeros_like(m8_ref)
        for j in range(T):
            bd = jnp.where(same, tile(kt_ref[j]), 0.0).astype(BF16)
            for b in range(T - j):
                m8_ref[b * LT:(b + 1) * LT, (b + j) * LT:(b + j + 1) * LT] = bd
        r = lax.broadcasted_iota(jnp.int32, (LT, 2 * SW), 0)
        c = lax.broadcasted_iota(jnp.int32, (LT, 2 * SW), 1)
        same = (r // C) == ((c % SW) // P)
        for b in range(T):
            ws8_ref[b * LT:(b + 1) * LT, :] = jnp.where(same, tile(zt_ref[b]), 0.0).astype(BF16)
        r = lax.broadcasted_iota(jnp.int32, (2 * SW, LT), 0)
        c = lax.broadcasted_iota(jnp.int32, (2 * SW, LT), 1)
        same = ((r % SW) // P) == (c // C)
        cl = cl_ref[...]
        for a in range(T):
            blk = jnp.dot(cl, ex_ref[a], preferred_element_type=F32)
            wo8_ref[:, a * LT:(a + 1) * LT] = jnp.where(same, blk, 0.0).astype(BF16)

    xs_ref[...] = us_ref[...].astype(F32)
    for b in range(T):
        u8_ref[:, b * LT:(b + 1) * LT] = xs_ref[pl.ds(b, nk, stride=T), :].astype(BF16)
    u8 = u8_ref[...]
    s = jnp.dot(u8, ws8_ref[...], preferred_element_type=F32)
    hr, hi = s[:, :SW], s[:, SW:]
    kidx = lax.broadcasted_iota(jnp.int32, (nk, SW), 0)
    d, step = 1, 0
    while d < nk:
        lr = lam_ref[step:step + 1, :SW]
        li = lam_ref[step:step + 1, SW:]
        keep = kidx >= d
        sr = jnp.where(keep, pltpu.roll(hr, d, 0), 0.0)
        si = jnp.where(keep, pltpu.roll(hi, d, 0), 0.0)
        hr, hi = hr + lr * sr - li * si, hi + lr * si + li * sr
        d, step = 2 * d, step + 1
    keep = kidx >= 1
    pr = jnp.where(keep, pltpu.roll(hr, 1, 0), 0.0)
    pi = jnp.where(keep, pltpu.roll(hi, 1, 0), 0.0)
    hcat = jnp.concatenate([pr, pi], axis=-1).astype(BF16)
    for q in range(T * LT // S5_COLS):
        kq = (q + 1) * S5_COLS
        yq = (jnp.dot(hcat, wo8_ref[:, q * S5_COLS:kq], preferred_element_type=F32)
              + jnp.dot(u8[:, :kq], m8_ref[0:kq, q * S5_COLS:kq], preferred_element_type=F32))
        for a4 in range(S5_COLS // LT):
            a = q * (S5_COLS // LT) + a4
            y_ref[pl.ds(a, nk, stride=T), :] = yq[:, a4 * LT:(a4 + 1) * LT]


def _s5_scan(proj, kt, zt, clc, lamp, ex, B, L):
    T, LT, SW = SSM_CHUNK, S5_LANES, S5_SW
    nk = L // T
    nt = SSM_WIDTH // LT
    per_tile = lambda a: pl.BlockSpec((None,) + a.shape[1:], lambda t, b: (t,) + (0,) * (a.ndim - 1))
    return pl.pallas_call(
        functools.partial(_s5_kernel, nk=nk),
        grid=(nt, B),
        in_specs=[pl.BlockSpec((L, LT), lambda t, b: (b, COL_US // LT + t)),
                  per_tile(kt), per_tile(zt), per_tile(clc), per_tile(lamp),
                  pl.BlockSpec(ex.shape, lambda t, b: (0, 0, 0))],
        out_specs=pl.BlockSpec((L, LT), lambda t, b: (b, t)),
        out_shape=jax.ShapeDtypeStruct((B * L, SSM_WIDTH), F32),
        scratch_shapes=[pltpu.VMEM((L, LT), F32), pltpu.VMEM((nk, T * LT), BF16),
                        pltpu.VMEM((T * LT, T * LT), BF16), pltpu.VMEM((T * LT, 2 * SW), BF16),
                        pltpu.VMEM((2 * SW, T * LT), BF16)],
        compiler_params=_cparams(("parallel", "arbitrary")),
        name="s5_scan",
    )(proj, kt, zt, clc, lamp, ex)


def _cmul(ar, ai, br, bi):
    return ar * br - ai * bi, ar * bi + ai * br


def _s5_tables(a_re, a_im, log_dt, b_re, b_im, c_re, c_im, nk):
    T, P, C, Gs = SSM_CHUNK, SSM_STATE, SSM_GROUP, SSM_GROUPS
    dt = jnp.exp(log_dt.astype(F32))[:, None]
    ar, ai = a_re.astype(F32), a_im.astype(F32)
    mag = jnp.exp(ar * dt)
    lr, li = mag * jnp.cos(ai * dt), mag * jnp.sin(ai * dt)
    den = ar * ar + ai * ai
    nr, ni = lr - 1.0, li
    fr, fi = (nr * ar + ni * ai) / den, (ni * ar - nr * ai) / den
    br, bim = b_re.astype(F32), b_im.astype(F32)
    bbr = fr[..., None] * br - fi[..., None] * bim
    bbi = fr[..., None] * bim + fi[..., None] * br
    jv = jnp.arange(T + 1, dtype=F32)[:, None, None]
    pmag = jnp.exp(jv * (ar * dt)[None])
    pwr, pwi = pmag * jnp.cos(jv * (ai * dt)[None]), pmag * jnp.sin(jv * (ai * dt)[None])
    zr = pwr[..., None] * bbr[None] - pwi[..., None] * bbi[None]
    zi = pwr[..., None] * bbi[None] + pwi[..., None] * bbr[None]
    cr, ci = c_re.astype(F32), c_im.astype(F32)
    kj = (jnp.einsum('gcp,jgpd->gjcd', cr, zr[:T], precision=HIGHEST)
          - jnp.einsum('gcp,jgpd->gjcd', ci, zi[:T], precision=HIGHEST))
    GT, NT = S5_GT, Gs // S5_GT
    kt = kj.reshape(NT, GT, T, C, C).transpose(0, 2, 4, 1, 3).reshape(NT, T, C, GT * C)
    lay_z = lambda z: z[:T][::-1].reshape(T, NT, GT, P, C).transpose(1, 0, 4, 2, 3).reshape(NT, T, C, GT * P)
    zt = jnp.concatenate([lay_z(zr), lay_z(zi)], axis=-1)
    pa_r, pa_i = pwr[1:].transpose(1, 0, 2)[:, :, None, :], pwi[1:].transpose(1, 0, 2)[:, :, None, :]
    clr = cr[:, None] * pa_r - ci[:, None] * pa_i
    cli = cr[:, None] * pa_i + ci[:, None] * pa_r
    lay_c = lambda x: x.reshape(NT, GT, T, C, P).transpose(0, 1, 4, 2, 3).reshape(NT, GT * P, T * C)
    clc = jnp.concatenate([lay_c(clr), lay_c(-cli)], axis=1).astype(BF16)
    qr, qi = pwr[T], pwi[T]
    steps = []
    d = 1
    while d < nk:
        steps.append(jnp.concatenate([qr.reshape(NT, GT * P), qi.reshape(NT, GT * P)], -1))
        qr, qi = _cmul(qr, qi, qr, qi)
        d *= 2
    lamp = jnp.stack(steps, 1)
    col = np.arange(T * C)[:, None]
    lane = np.arange(GT * C)[None, :]
    ex = np.stack([(col // C == a) & (col % C == lane % C) for a in range(T)]).astype(np.float32)
    return kt, zt, clc, lamp, jnp.asarray(ex, BF16)


def _glu_kernel(y_ref, u_ref, zb_ref, d_ref, w_ref, b_ref, o_ref):
    y = y_ref[...] + d_ref[...] * u_ref[...].astype(F32)
    yg = jax.nn.gelu(y).astype(BF16)
    z = jnp.dot(yg, w_ref[...], preferred_element_type=F32) + b_ref[...]
    zb = zb_ref[...].astype(F32)
    o_ref[...] = (yg.astype(F32) * jax.nn.sigmoid(z) * (zb * jax.nn.sigmoid(zb))).astype(o_ref.dtype)


def _s5_glu(y, proj, d_skip, w_glu, b_glu):
    N, W = y.shape
    tm = min(1024, N)
    return pl.pallas_call(
        _glu_kernel,
        grid=(N // tm,),
        in_specs=[pl.BlockSpec((tm, W), lambda i: (i, 0)),
                  pl.BlockSpec((tm, W), lambda i: (i, COL_US // W)),
                  pl.BlockSpec((tm, W), lambda i: (i, COL_ZB // W)),
                  pl.BlockSpec((1, W), lambda i: (0, 0)),
                  pl.BlockSpec((W, W), lambda i: (0, 0)),
                  pl.BlockSpec((1, W), lambda i: (0, 0))],
        out_specs=pl.BlockSpec((tm, W), lambda i: (i, 0)),
        out_shape=jax.ShapeDtypeStruct((N, W), BF16),
        compiler_params=_cparams(("parallel",)),
        name="s5_glu",
    )(y, proj, proj, d_skip.reshape(1, W), w_glu, b_glu.reshape(1, W))


def _merge_kernel(oa_ref, ob_ref, ga_ref, gb_ref, x_ref, gate_ref, wa_ref, wb_ref, wo_ref, lg_ref, lb_ref, o_ref):
    pa = jnp.dot(oa_ref[...], wa_ref[...], preferred_element_type=F32)
    pb = jnp.dot(ob_ref[...], wb_ref[...], preferred_element_type=F32)
    m = jax.nn.sigmoid(ga_ref[...].astype(F32)) * pa + jax.nn.sigmoid(gb_ref[...].astype(F32)) * pb
    y = jnp.dot(m.astype(BF16), wo_ref[...], preferred_element_type=F32)
    r = DEEPNORM_ALPHA * x_ref[...] + gate_ref[...] * y
    mu = jnp.mean(r, axis=-1, keepdims=True)
    rc = r - mu
    var = jnp.mean(rc * rc, axis=-1, keepdims=True)
    o_ref[...] = rc * lax.rsqrt(var + LN_EPS) * lg_ref[...] + lb_ref[...]


def _merge_out(oa, ob, proj, x2, gate, wa, wb, wo, ln_g, ln_b, L):
    N, D = x2.shape
    W = oa.shape[1]
    B = gate.shape[0]
    tm = min(256, L)
    const = lambda i: (0, 0)
    return pl.pallas_call(
        _merge_kernel,
        grid=(N // tm,),
        in_specs=[pl.BlockSpec((tm, W), lambda i: (i, 0)),
                  pl.BlockSpec((tm, W), lambda i: (i, 0)),
                  pl.BlockSpec((tm, D), lambda i: (i, COL_GA // D)),
                  pl.BlockSpec((tm, D), lambda i: (i, COL_GB // D)),
                  pl.BlockSpec((tm, D), lambda i: (i, 0)),
                  pl.BlockSpec((None, 1, D), lambda i: ((i * tm) // L, 0, 0)),
                  pl.BlockSpec((W, D), const),
                  pl.BlockSpec((W, D), const),
                  pl.BlockSpec((D, D), const),
                  pl.BlockSpec((1, D), const),
                  pl.BlockSpec((1, D), const)],
        out_specs=pl.BlockSpec((tm, D), lambda i: (i, 0)),
        out_shape=jax.ShapeDtypeStruct((N, D), F32),
        compiler_params=_cparams(("parallel",)),
        name="merge_out",
    )(oa, ob, proj, proj, x2, gate.reshape(B, 1, D), wa, wb, wo, ln_g.reshape(1, D), ln_b.reshape(1, D))


def _layer(x, c, w_ada, b_ada, w_in, rel_bias, cmp_pos_k, cmp_pos_v, w_cmp_k1, w_cmp_k2, w_cmp_v1, w_cmp_v2,
           ssm_a_re, ssm_a_im, ssm_log_dt, ssm_b_re, ssm_b_im, ssm_c_re, ssm_c_im, ssm_d, w_glu, b_glu,
           w_branch_nsa, w_branch_ssm, w_out, ln_g, ln_b):
    B, L, D = x.shape
    N = B * L
    G, dh = NSA_GROUPS, NSA_HEAD_DIM
    x2 = x.reshape(N, D)

    mod = _ada_mod(c, w_ada, b_ada)
    shift, scale, gate = mod[:, :D], mod[:, D:2 * D], mod[:, 2 * D:]

    o_q, o_kv, o_ng, o_za, o_us, o_zb, o_ga, o_gb = np.cumsum(
        [0, NSA_WIDTH, 6 * KV_WIDTH, 3 * NSA_HEADS, NSA_WIDTH, SSM_WIDTH, SSM_WIDTH, D_MODEL]).tolist()
    w_re = jnp.concatenate([
        w_in[:, o_q:o_kv], w_in[:, o_za:o_us], w_in[:, o_us:o_zb], w_in[:, o_zb:o_ga], w_in[:, o_ga:o_gb],
        w_in[:, o_gb:], w_in[:, o_kv + 2 * KV_WIDTH:o_ng], w_in[:, o_kv:o_kv + 2 * KV_WIDTH], w_in[:, o_ng:o_za],
        jnp.zeros((D, PROJ_COLS - COL_NG - 3 * NSA_HEADS), w_in.dtype)], axis=1).astype(BF16)
    proj = _in_proj(x2, scale, shift, w_re, L)

    w1s = jnp.stack([w_cmp_k1, w_cmp_v1])
    w2s = jnp.stack([w_cmp_k2, w_cmp_v2])
    poss = jnp.broadcast_to(jnp.stack([cmp_pos_k, cmp_pos_v]).reshape(2, 1, CMP_BLOCK * dh), (2, 8, CMP_BLOCK * dh))
    kcv = _nsa_compress(proj, w1s, w2s, poss, B, L)
    kc = kcv[0].astype(BF16)
    vct = kcv[1].swapaxes(-1, -2).astype(BF16)
    ks, vst, kw, vwt = _kv_prep(proj, B, L)
    tc, tw, ts, ovt = _nsa_tables(rel_bias, L)
    o_a = _nsa_attend(proj, kc, vct, ks, vst, kw, vwt, tc, tw, ts, ovt, B, L)

    s5_tabs = _s5_tables(ssm_a_re, ssm_a_im, ssm_log_dt, ssm_b_re, ssm_b_im, ssm_c_re, ssm_c_im, L // SSM_CHUNK)
    y = _s5_scan(proj, *s5_tabs, B, L)
    o_b = _s5_glu(y, proj, ssm_d, w_glu.astype(BF16), b_glu)

    out = _merge_out(o_a, o_b, proj, x2, gate, w_branch_nsa.astype(BF16), w_branch_ssm.astype(BF16),
                     w_out.astype(BF16), ln_g, ln_b, L)
    return out.reshape(B, L, D)


def kernel(x, c, w_ada, b_ada, w_in, rel_bias, cmp_pos_k, cmp_pos_v, w_cmp_k1, w_cmp_k2, w_cmp_v1, w_cmp_v2,
           ssm_a_re, ssm_a_im, ssm_log_dt, ssm_b_re, ssm_b_im, ssm_c_re, ssm_c_im, ssm_d, w_glu, b_glu,
           w_branch_nsa, w_branch_ssm, w_out, ln_g, ln_b):
    for i in range(w_ada.shape[0]):
        x = _layer(x, c, w_ada[i], b_ada[i], w_in[i], rel_bias, cmp_pos_k[i], cmp_pos_v[i], w_cmp_k1[i],
                   w_cmp_k2[i], w_cmp_v1[i], w_cmp_v2[i], ssm_a_re[i], ssm_a_im[i], ssm_log_dt[i], ssm_b_re[i],
                   ssm_b_im[i], ssm_c_re[i], ssm_c_im[i], ssm_d[i], w_glu[i], b_glu[i], w_branch_nsa[i],
                   w_branch_ssm[i], w_out[i], ln_g[i], ln_b[i])
    return x
```

```python
import functools
import math

import numpy as np
import jax
import jax.numpy as jnp
from jax import lax
from jax.experimental import pallas as pl
from jax.experimental.pallas import tpu as pltpu

F32 = jnp.float32
BF16 = jnp.bfloat16
HIGHEST = lax.Precision.HIGHEST
NT_DIMS = (((1,), (1,)), ((), ()))

D_MODEL = 2048
NSA_HEADS = 16
NSA_GROUPS = 4
NSA_HEAD_DIM = 64
NSA_REP = NSA_HEADS // NSA_GROUPS
NSA_WIDTH = NSA_HEADS * NSA_HEAD_DIM
KV_WIDTH = NSA_GROUPS * NSA_HEAD_DIM
CMP_BLOCK = 32
CMP_STRIDE = 16
CMP_HIDDEN = 128
SEL_BLOCK = 64
SEL_TOPK = 8
WINDOW = 512
Q_BLOCK = 128
SSM_WIDTH = 1024
SSM_GROUP = 16
SSM_GROUPS = SSM_WIDTH // SSM_GROUP
SSM_STATE = 64
REL_BUCKETS = 32
REL_MAX_DIST = 128
DEEPNORM_ALPHA = 2.0 ** 0.25
LN_EPS = 1e-5
MASK_VALUE = -1e30
FORCE_VALUE = 1e4
NEVER_VALUE = -3e38
SSM_CHUNK = 16

COL_Q, COL_ZA, COL_US, COL_ZB, COL_GA, COL_GB, COL_KV, COL_KC, COL_NG = 0, 1024, 2048, 3072, 4096, 6144, 8192, 9216, 9728
PROJ_COLS = 9856
VMEM_LIMIT = 56 * 1024 * 1024


def _cparams(sem):
    return pltpu.CompilerParams(dimension_semantics=sem, vmem_limit_bytes=VMEM_LIMIT)


def _ada_kernel(c_ref, w_ref, b_ref, o_ref):
    o_ref[...] = jnp.dot(c_ref[...], w_ref[...], preferred_element_type=F32, precision=HIGHEST) + b_ref[...]


def _ada_mod(c, w_ada, b_ada):
    B, D = c.shape
    n = w_ada.shape[1]
    tn = 1536
    return pl.pallas_call(
        _ada_kernel,
        grid=(n // tn,),
        in_specs=[pl.BlockSpec((B, D), lambda j: (0, 0)),
                  pl.BlockSpec((D, tn), lambda j: (0, j)),
                  pl.BlockSpec((1, tn), lambda j: (0, j))],
        out_specs=pl.BlockSpec((B, tn), lambda j: (0, j)),
        out_shape=jax.ShapeDtypeStruct((B, n), F32),
        compiler_params=_cparams(("parallel",)),
        name="ada_mod",
    )(c, w_ada, b_ada.reshape(1, n))


def _inproj_kernel(x_ref, scale_ref, shift_ref, w_ref, o_ref, h_ref):
    @pl.when(pl.program_id(1) == 0)
    def _():
        def rows(k, carry):
            r0 = pl.multiple_of(k * LN_ROWS, LN_ROWS)
            x = x_ref[pl.ds(r0, LN_ROWS), :]
            mu = jnp.mean(x, axis=-1, keepdims=True)
            xc = x - mu
            var = jnp.mean(xc * xc, axis=-1, keepdims=True)
            hn = xc * lax.rsqrt(var + LN_EPS)
            h_ref[pl.ds(r0, LN_ROWS), :] = (hn * (1.0 + scale_ref[...]) + shift_ref[...]).astype(BF16)
            return carry

        lax.fori_loop(0, x_ref.shape[0] // LN_ROWS, rows, 0)

    o_ref[...] = lax.dot_general(h_ref[...], w_ref[...], NT_DIMS, preferred_element_type=F32).astype(o_ref.dtype)


LN_ROWS = 256


def _in_proj(x2, scale, shift, w, L):
    N, D = x2.shape
    ncol = w.shape[0]
    tm = min(1024, L)
    tn = 1408
    assert N % tm == 0 and L % tm == 0 and ncol % tn == 0
    B = scale.shape[0]
    return pl.pallas_call(
        _inproj_kernel,
        grid=(N // tm, ncol // tn),
        in_specs=[pl.BlockSpec((tm, D), lambda i, j: (i, 0)),
                  pl.BlockSpec((None, 1, D), lambda i, j: ((i * tm) // L, 0, 0)),
                  pl.BlockSpec((None, 1, D), lambda i, j: ((i * tm) // L, 0, 0)),
                  pl.BlockSpec((tn, D), lambda i, j: (j, 0))],
        out_specs=pl.BlockSpec((tm, tn), lambda i, j: (i, j)),
        out_shape=jax.ShapeDtypeStruct((N, ncol), BF16),
        scratch_shapes=[pltpu.VMEM((tm, D), BF16)],
        compiler_params=_cparams(("parallel", "arbitrary")),
        name="in_proj",
    )(x2, scale.reshape(B, 1, D), shift.reshape(B, 1, D), w)


def _cmp_kernel(x_ref, w1_ref, w1bd_ref, w2_ref, pos_ref, o_ref, xs_ref, *, ncp):
    S, dh = CMP_STRIDE, NSA_HEAD_DIM
    xs_ref[...] = x_ref[...].astype(F32)
    a = jnp.zeros((ncp, 2 * CMP_HIDDEN), F32)
    b = jnp.zeros((ncp, 2 * CMP_HIDDEN), F32)
    for j in range(S):
        xj = xs_ref[pl.ds(j, ncp, stride=S), :].astype(BF16)
        a = a + jnp.dot(xj, w1bd_ref[0, j], preferred_element_type=F32)
        b = b + jnp.dot(xj, w1bd_ref[1, j], preferred_element_type=F32)
    pw = jnp.dot(pos_ref[...], w1_ref[...], preferred_element_type=F32, precision=HIGHEST)[0:1]
    h = a + pltpu.roll(b, ncp - 1, 0) + jnp.concatenate([pw, pw], axis=1)
    w2 = w2_ref[...].astype(BF16)
    for g in range(2):
        hg = jax.nn.gelu(h[:, g * CMP_HIDDEN:(g + 1) * CMP_HIDDEN]).astype(BF16)
        o_ref[g] = jnp.dot(hg, w2, preferred_element_type=F32)


def _nsa_compress(proj, w1s, w2s, poss, B, L):
    G, dh, S = NSA_GROUPS, NSA_HEAD_DIM, CMP_STRIDE
    ncp = L // S
    w = w1s.reshape(2, 2, S, dh, CMP_HIDDEN).astype(BF16)
    z = jnp.zeros_like(w)
    w1bd = jnp.concatenate([jnp.concatenate([w, z], -1), jnp.concatenate([z, w], -1)], axis=-2)
    return pl.pallas_call(
        functools.partial(_cmp_kernel, ncp=ncp),
        grid=(2, B, G // 2),
        in_specs=[pl.BlockSpec((L, 2 * dh), lambda s, b, t: (b, COL_KC // (2 * dh) + 2 * s + t)),
                  pl.BlockSpec((None,) + w1s.shape[1:], lambda s, b, t: (s, 0, 0)),
                  pl.BlockSpec((None,) + w1bd.shape[1:], lambda s, b, t: (s, 0, 0, 0, 0)),
                  pl.BlockSpec((None,) + w2s.shape[1:], lambda s, b, t: (s, 0, 0)),
                  pl.BlockSpec((None,) + poss.shape[1:], lambda s, b, t: (s, 0, 0))],
        out_specs=pl.BlockSpec((None, None, 2, ncp, dh), lambda s, b, t: (s, b, t, 0, 0)),
        out_shape=jax.ShapeDtypeStruct((2, B, G, ncp, dh), F32),
        scratch_shapes=[pltpu.VMEM((L, 2 * dh), F32)],
        compiler_params=_cparams(("parallel", "parallel", "parallel")),
        name="nsa_compress",
    )(proj, w1s, w1bd, w2s, poss)


V_ROWS = NSA_HEAD_DIM + 16


def _kvprep_kernel(x_ref, ks_ref, vst_ref, kw_ref, vwt_ref, *, TL):
    G, dh = NSA_GROUPS, NSA_HEAD_DIM
    i = pl.program_id(1)
    tile = jnp.maximum(i - 1, 0)
    x = x_ref[...].astype(F32)
    lane = lax.broadcasted_iota(jnp.int32, (TL, 2 * dh), 1)
    row = lax.broadcasted_iota(jnp.int32, (TL, 2 * dh), 0) + tile * TL
    onehot = jnp.where(lane - dh == row // SEL_BLOCK, 1.0, 0.0)
    padrow = jnp.where(lane == dh, 1.0, 0.0)
    tail = jnp.where(lax.broadcasted_iota(jnp.int32, (V_ROWS - dh, TL), 0) == 0, 1.0, 0.0)
    is_pad = i == 0
    for t in range(G // 2):
        xk, xv, xwk, xwv = [x[:, (s * G + 2 * t) * dh:(s * G + 2 * t + 2) * dh] for s in range(4)]
        xvt, xwvt = xv.T, xwv.T
        for h in range(2):
            g = 2 * t + h
            kk = xk if h == 0 else pltpu.roll(xk, dh, 1)
            kwk = xwk if h == 0 else pltpu.roll(xwk, dh, 1)
            ks_ref[g] = jnp.where(lane < dh, kk, onehot).astype(BF16)
            kw_ref[g] = jnp.where(is_pad, padrow, jnp.where(lane < dh, kwk, 0.0)).astype(BF16)
            vst_ref[g] = jnp.concatenate([xvt[h * dh:(h + 1) * dh], tail], axis=0).astype(BF16)
            vw = jnp.concatenate([xwvt[h * dh:(h + 1) * dh], tail], axis=0)
            vwt_ref[g] = jnp.where(is_pad, 0.0, vw).astype(BF16)


def _kv_prep(proj, B, L):
    G, dh, TL = NSA_GROUPS, NSA_HEAD_DIM, WINDOW
    nt = L // TL
    wcols = 4 * G * dh
    assert L // SEL_BLOCK <= dh and COL_KV % wcols == 0 and L % TL == 0
    data = lambda b, i: jnp.maximum(i - 1, 0)
    k_shape = jax.ShapeDtypeStruct((B, G, L, 2 * dh), BF16)
    v_shape = jax.ShapeDtypeStruct((B, G, V_ROWS, L), BF16)
    kw_shape = jax.ShapeDtypeStruct((B, G, L + WINDOW, 2 * dh), BF16)
    vw_shape = jax.ShapeDtypeStruct((B, G, V_ROWS, L + WINDOW), BF16)
    return pl.pallas_call(
        functools.partial(_kvprep_kernel, TL=TL),
        grid=(B, nt + 1),
        in_specs=[pl.BlockSpec((TL, wcols), lambda b, i: (b * nt + data(b, i), COL_KV // wcols))],
        out_specs=[pl.BlockSpec((None, G, TL, 2 * dh), lambda b, i: (b, 0, data(b, i), 0)),
                   pl.BlockSpec((None, G, V_ROWS, TL), lambda b, i: (b, 0, 0, data(b, i))),
                   pl.BlockSpec((None, G, TL, 2 * dh), lambda b, i: (b, 0, i, 0)),
                   pl.BlockSpec((None, G, V_ROWS, TL), lambda b, i: (b, 0, 0, i))],
        out_shape=[k_shape, v_shape, kw_shape, vw_shape],
        compiler_params=_cparams(("parallel", "arbitrary")),
        name="kv_prep",
    )(proj)


SEL_TILE = 512
SEL_TABLE_FAR = 640
NSA_PROBLEMS = 4


def _tree(op, parts):
    while len(parts) > 1:
        parts = [op(parts[i], parts[i + 1]) if i + 1 < len(parts) else parts[i] for i in range(0, len(parts), 2)]
    return parts[0]


def _fold8(op, x):
    return _tree(op, [x[k:k + 8] for k in range(0, x.shape[0], 8)])


def _safe_inv(l):
    return jnp.where(l > 0.0, 1.0 / jnp.where(l > 0.0, l, 1.0), 0.0)


def _nsa_kernel(q_ref, za_ref, ng_ref, kc_ref, vct_ref, ks_ref, vst_ref, kw_ref, vwt_ref,
                tc_ref, tw_ref, ts_ref, ovt_ref, o_ref, s_ref, s2_ref, acc_ref, gt_ref, *, L):
    R, dh, QB = NSA_REP, NSA_HEAD_DIM, Q_BLOCK
    ncp = L // CMP_STRIDE
    nsel = L // SEL_BLOCK
    W = R * QB
    g = pl.program_id(1)
    qb = pl.program_id(2)

    NP = q_ref.shape[0]

    def front(p):
        qt = (q_ref[p].astype(F32) * (dh ** -0.5)).T
        qT = jnp.concatenate([qt[r * dh:(r + 1) * dh] for r in range(R)], axis=1).astype(BF16)

        c_off = pl.multiple_of((ncp - 8) - 8 * qb, 8)
        sc = jnp.dot(kc_ref[p], qT, preferred_element_type=F32) + tc_ref[pl.ds(c_off, ncp), :]
        m = jnp.maximum(jnp.max(_fold8(jnp.maximum, sc), axis=0, keepdims=True), 0.1 * MASK_VALUE)
        e = jnp.exp(sc - m)
        l = jnp.sum(_fold8(jnp.add, e), axis=0, keepdims=True)
        p_c = e * _safe_inv(l)
        oc = jnp.dot(vct_ref[p], p_c.astype(BF16), preferred_element_type=F32)

        psum = _tree(jnp.add, [p_c[:, r * QB:(r + 1) * QB] for r in range(R)])
        p_hi = psum.astype(BF16)
        p_lo = (psum - p_hi.astype(F32)).astype(BF16)
        ovt = ovt_ref[...]
        imp = jnp.dot(ovt, p_hi, preferred_element_type=F32) + jnp.dot(ovt, p_lo, preferred_element_type=F32)
        jj = lax.broadcasted_iota(jnp.int32, (nsel, QB), 0)
        ii = lax.broadcasted_iota(jnp.int32, (nsel, QB), 1)
        cur = 2 * qb + (ii >= SEL_BLOCK).astype(jnp.int32)
        forced = (jj == 0) | (jj == cur) | (jj == cur - 1)
        imp = jnp.where(forced, FORCE_VALUE, imp)
        imp = jnp.where(jj <= cur, imp, MASK_VALUE)
        jf = jj.astype(F32)
        sel = jnp.zeros((nsel, QB), F32)
        for _ in range(min(SEL_TOPK, nsel)):
            mx = jnp.max(_fold8(jnp.maximum, imp), axis=0, keepdims=True)
            idx = jnp.min(_fold8(jnp.minimum, jnp.where(imp == mx, jf, 1e9)), axis=0, keepdims=True)
            hit = jf == idx
            sel = jnp.where(hit & (mx > 0.1 * MASK_VALUE), 1.0, sel)
            imp = jnp.where(hit, -jnp.inf, imp)
        selbias = jnp.where(sel > 0.5, 0.0, MASK_VALUE).astype(BF16)
        sel_rows = [jnp.concatenate([selbias] * R, axis=1)]
        if nsel < dh:
            sel_rows.append(jnp.zeros((dh - nsel, W), BF16))
        q_sel = jnp.concatenate([qT] + sel_rows, axis=0)

        w0 = pl.multiple_of(qb * QB, QB)
        flag = jnp.where(lax.broadcasted_iota(jnp.int32, (dh, W), 0) == 0, MASK_VALUE, 0.0).astype(BF16)
        q_win = jnp.concatenate([qT, flag], axis=0)
        sw = jnp.dot(kw_ref[p, pl.ds(w0, WINDOW + QB), :], q_win, preferred_element_type=F32) + tw_ref[...]
        m_w = jnp.max(_fold8(jnp.maximum, sw), axis=0, keepdims=True)
        pw = jnp.exp((sw - m_w).astype(BF16))
        accw = jnp.dot(vwt_ref[p, :, pl.ds(w0, WINDOW + QB)], pw, preferred_element_type=F32)
        owin = accw[:dh] * _safe_inv(accw[dh:dh + 1])

        gt_ref[p] = jax.nn.sigmoid(ng_ref[p].astype(F32)).T

        def gate_row(branch):
            return jnp.concatenate([gt_ref[p, pl.ds(3 * (g * R + r) + branch, 1), :] for r in range(R)], axis=1)

        return q_sel, oc * gate_row(0) + owin * gate_row(2), gate_row(1)

    fronts = [front(p) for p in range(NP)]

    ntile = qb // (SEL_TILE // QB) + 1
    npair = ntile // 2
    odd = ntile % 2

    slots = (s_ref, s2_ref)

    def scores(t0, p, slot, nt=2):
        mx = None
        for h in range(nt):
            t = t0 + h
            k0 = pl.multiple_of(t * SEL_TILE, SEL_TILE)
            x0 = pl.multiple_of(jnp.maximum(t * SEL_TILE - qb * QB + SEL_TABLE_FAR, 0), QB)
            s = (jnp.dot(ks_ref[p, pl.ds(k0, SEL_TILE), :], fronts[p][0], preferred_element_type=F32)
                 + ts_ref[pl.ds(x0, SEL_TILE), :])
            slots[slot][p, h * SEL_TILE:(h + 1) * SEL_TILE, :] = s
            f = _fold8(jnp.maximum, s)
            mx = f if mx is None else jnp.maximum(mx, f)
        return jnp.max(mx, axis=0, keepdims=True)

    def accumulate(t0, p, slot, m_new, m_old, nt=2):
        part = None
        for h in range(nt):
            k0 = pl.multiple_of((t0 + h) * SEL_TILE, SEL_TILE)
            pr = jnp.exp((slots[slot][p, h * SEL_TILE:(h + 1) * SEL_TILE, :] - m_new).astype(BF16))
            pv = jnp.dot(vst_ref[p, :, pl.ds(k0, SEL_TILE)], pr, preferred_element_type=F32)
            part = pv if part is None else part + pv
        acc_ref[p] = acc_ref[p] * jnp.exp(m_old - m_new) + part

    def phase(t0, slot, m_cur, m_acc, nt_next=2):
        m_next = []
        for p in range(NP):
            accumulate(t0, p, slot, m_cur[p], m_acc[p])
            m_next.append(jnp.maximum(m_cur[p], scores(t0 + 2, p, 1 - slot, nt_next)))
        return tuple(m_next), m_cur

    def finish(t0, slot, m_cur, m_acc):
        @pl.when(odd == 1)
        def _():
            m2, m1 = phase(t0, slot, m_cur, m_acc, nt_next=1)
            for p in range(NP):
                accumulate(t0 + 2, p, 1 - slot, m2[p], m1[p], nt=1)

        @pl.when(odd == 0)
        def _():
            for p in range(NP):
                accumulate(t0, p, slot, m_cur[p], m_acc[p])

    acc_ref[...] = jnp.zeros_like(acc_ref)

    @pl.when(npair == 0)
    def _():
        for p in range(NP):
            m = scores(0, p, 0, nt=1)
            accumulate(0, p, 0, m, m, nt=1)

    @pl.when(npair > 0)
    def _():
        m_first = tuple(scores(0, p, 0) for p in range(NP))

        def trip(v, carry):
            carry = phase(4 * v, 0, *carry)
            return phase(4 * v + 2, 1, *carry)

        n_rest = npair - 1
        m_cur, m_acc = lax.fori_loop(0, n_rest // 2, trip, (m_first, m_first))
        t_last = 4 * (n_rest // 2)

        @pl.when(n_rest % 2 == 1)
        def _():
            finish(t_last + 2, 1, *phase(t_last, 0, m_cur, m_acc))

        @pl.when(n_rest % 2 == 0)
        def _():
            finish(t_last, 0, m_cur, m_acc)

    for p in range(NP):
        acc = acc_ref[p]
        osel = acc[:dh] * _safe_inv(acc[dh:dh + 1])
        ot = fronts[p][1] + osel * fronts[p][2]
        o = jnp.concatenate([ot[:, r * QB:(r + 1) * QB] for r in range(R)], axis=0).T
        za = za_ref[p].astype(F32)
        o_ref[p] = (o * (za * jax.nn.sigmoid(za))).astype(o_ref.dtype)


def _nsa_attend(proj, kc, vct, ks, vst, kw, vwt, tc, tw, ts, ovt, B, L):
    R, dh, G, QB = NSA_REP, NSA_HEAD_DIM, NSA_GROUPS, Q_BLOCK
    assert L % (2 * SEL_TILE) == 0
    nqb = L // QB
    gw = R * dh
    W = R * QB
    NP = NSA_PROBLEMS if B % NSA_PROBLEMS == 0 else 1
    proj4 = proj.reshape(B // NP, NP, L, proj.shape[-1])
    tok = lambda col: pl.BlockSpec((None, NP, QB, col[1]), lambda b, g, i: (b, 0, i, col[0] // col[1] + col[2] * g))
    bg = lambda a: pl.BlockSpec((NP, None) + a.shape[2:], lambda b, g, i: (b, g, 0, 0),
                                pipeline_mode=pl.Buffered(1))
    grp = lambda a: pl.BlockSpec((None,) + a.shape[1:], lambda b, g, i: (g, 0, 0), pipeline_mode=pl.Buffered(1))
    out = pl.pallas_call(
        functools.partial(_nsa_kernel, L=L),
        grid=(B // NP, G, nqb),
        in_specs=[tok((COL_Q, gw, 1)), tok((COL_ZA, gw, 1)), tok((COL_NG, 128, 0)),
                  bg(kc), bg(vct), bg(ks), bg(vst), bg(kw), bg(vwt), grp(tc), grp(tw), grp(ts),
                  pl.BlockSpec(ovt.shape, lambda b, g, i: (0, 0))],
        out_specs=pl.BlockSpec((None, NP, QB, gw), lambda b, g, i: (b, 0, i, g)),
        out_shape=jax.ShapeDtypeStruct((B // NP, NP, L, NSA_WIDTH), BF16),
        scratch_shapes=[pltpu.VMEM((NP, 2 * SEL_TILE, W), F32), pltpu.VMEM((NP, 2 * SEL_TILE, W), F32),
                        pltpu.VMEM((NP, V_ROWS, W), F32),
                        pltpu.VMEM((NP, 128, QB), F32)],
        compiler_params=_cparams(("parallel", "parallel", "arbitrary")),
        name="nsa_attend",
    )(proj4, proj4, proj4, kc, vct, ks, vst, kw, vwt, tc, tw, ts, ovt)
    return out.reshape(B * L, NSA_WIDTH)


def _t5_bucket(dist):
    n = jnp.maximum(dist, 0)
    max_exact = REL_BUCKETS // 2
    nf = jnp.maximum(n, max_exact).astype(F32)
    large = max_exact + (jnp.log(nf / max_exact) / math.log(REL_MAX_DIST / max_exact)
                         * (REL_BUCKETS - max_exact)).astype(jnp.int32)
    large = jnp.minimum(large, REL_BUCKETS - 1)
    return jnp.where(n < max_exact, n, large)


def _toeplitz(f, d0, base, step, n_rows, width):
    rpb = width // step
    nblk = n_rows // rpb
    assert rpb * step == width and nblk * rpb == n_rows
    lo = base - width * nblk - d0
    return f[..., lo:lo + width * (nblk + 1)].reshape(f.shape[:-1] + (nblk + 1, width))


TABLE_STEPS = (CMP_STRIDE, 1, 1)


def _tables_kernel(*refs, runs):
    R, QB = NSA_REP, Q_BLOCK
    n = len(TABLE_STEPS)
    tmp_ref = refs[2 * n]
    for win_ref, out_ref, step, table_runs in zip(refs[:n], refs[n:2 * n], TABLE_STEPS, runs):
        rpb = QB // step

        def block(a, carry, varies, win_ref=win_ref, out_ref=out_ref, step=step, rpb=rpb):
            r0 = pl.multiple_of(a * rpb, rpb)
            top = win_ref.shape[1] - 1 - a
            for r in range(R):
                if varies:
                    win = jnp.concatenate([win_ref[r, pl.ds(top, 1), :], win_ref[r, pl.ds(top - 1, 1), :]], axis=1)
                    x = jnp.broadcast_to(win, (QB, 2 * QB))
                    y = pltpu.roll(x, 0, 1, stride=1, stride_axis=0)[:, :QB]
                    if step > 1:
                        tmp_ref[...] = y
                        y = tmp_ref[pl.ds(0, rpb, stride=step), :]
                else:
                    y = jnp.broadcast_to(win_ref[r, pl.ds(top, 1), :], (rpb, QB))
                out_ref[pl.ds(r0, rpb), r * QB:(r + 1) * QB] = y
            return carry

        for first, end, varies in table_runs:
            lax.fori_loop(first, end, functools.partial(block, varies=varies), 0)


def _constant_runs(base, step, n_rows, upper):
    far = REL_MAX_DIST
    rpb = Q_BLOCK // step
    flags = []
    for a in range(n_rows // rpb):
        lo, hi = base - Q_BLOCK * a - Q_BLOCK + step, base - Q_BLOCK * a + Q_BLOCK - 1
        flags.append(not (hi < 0 or lo >= upper or (lo >= far and hi < upper)))
    runs, start = [], 0
    for a in range(1, len(flags) + 1):
        if a == len(flags) or flags[a] != flags[start]:
            runs.append((start, a, flags[start]))
            start = a
    return tuple(runs)


def _expand_tables(wins, runs):
    G, R, QB = NSA_GROUPS, NSA_REP, Q_BLOCK
    rows = [(w.shape[1] - 1) * (QB // s) for w, s in zip(wins, TABLE_STEPS)]
    return pl.pallas_call(
        functools.partial(_tables_kernel, runs=runs),
        grid=(G,),
        in_specs=[pl.BlockSpec((R,) + w.shape[1:], lambda g: (g, 0, 0)) for w in wins],
        out_specs=[pl.BlockSpec((None, n, R * QB), lambda g: (g, 0, 0)) for n in rows],
        out_shape=[jax.ShapeDtypeStruct((G, n, R * QB), F32) for n in rows],
        scratch_shapes=[pltpu.VMEM((QB, QB), F32)],
        compiler_params=_cparams(("parallel",)),
        name="nsa_tables",
    )(*wins)


def _nsa_tables(rel_bias, L):
    QB, G, R = Q_BLOCK, NSA_GROUPS, NSA_REP
    ncp = L // CMP_STRIDE
    nsel = L // SEL_BLOCK
    npos = L + QB
    tbl = rel_bias.astype(F32)
    bpos = (tbl[_t5_bucket(jnp.arange(npos, dtype=jnp.int32))] - tbl[REL_BUCKETS - 1][None, :]).T
    f = jnp.concatenate([jnp.full((NSA_HEADS, npos), MASK_VALUE, F32), bpos], axis=-1)
    d = np.arange(-npos, npos)
    fw = jnp.where(jnp.asarray(d < WINDOW), f, MASK_VALUE)
    spec_c = (L - QB - CMP_BLOCK + 1, CMP_STRIDE, 2 * ncp - 8)
    spec_w = (WINDOW, 1, WINDOW + QB)
    spec_s = (SEL_TABLE_FAR, 1, SEL_TABLE_FAR + SEL_TILE)
    assert TABLE_STEPS == (spec_c[1], spec_w[1], spec_s[1])
    wins = [_toeplitz(src, -npos, *spec, QB) for src, spec in ((f, spec_c), (fw, spec_w), (f, spec_s))]
    runs = tuple(_constant_runs(*spec, upper) for spec, upper in ((spec_c, 2 * npos), (spec_w, WINDOW), (spec_s, 2 * npos)))
    tc, tw, ts = _expand_tables(wins, runs)
    c = np.arange(ncp)[None, :]
    j = np.arange(nsel)[:, None]
    ovt = ((c - 4 * j >= -1) & (c - 4 * j <= 3)).astype(np.float32)
    return tc, tw, ts, jnp.asarray(ovt, BF16)


S5_LANES = 128
S5_GT = S5_LANES // SSM_GROUP
S5_SW = S5_GT * SSM_STATE
S5_COLS = 4 * S5_LANES


def _s5_kernel(us_ref, kt_ref, zt_ref, cl_ref, lam_ref, ex_ref, y_ref,
               xs_ref, u8_ref, m8_ref, ws8_ref, wo8_ref, *, nk):
    T, C, P, GT, LT, SW = SSM_CHUNK, SSM_GROUP, SSM_STATE, S5_GT, S5_LANES, S5_SW

    @pl.when(pl.program_id(1) == 0)
    def _build():
        tile = lambda x: jnp.concatenate([x] * GT, axis=0)
        r = lax.broadcasted_iota(jnp.int32, (LT, LT), 0)
        c = lax.broadcasted_iota(jnp.int32, (LT, LT), 1)
        same = (r // C) == (c // C)
        m8_ref[...] = jnp.zeros_like(m8_ref)
        for j in range(T):
            bd = jnp.where(same, tile(kt_ref[j]), 0.0).astype(BF16)
            for b in range(T - j):
                m8_ref[b * LT:(b + 1) * LT, (b + j) * LT:(b + j + 1) * LT] = bd
        r = lax.broadcasted_iota(jnp.int32, (LT, 2 * SW), 0)
        c = lax.broadcasted_iota(jnp.int32, (LT, 2 * SW), 1)
        same = (r // C) == ((c % SW) // P)
        for b in range(T):
            ws8_ref[b * LT:(b + 1) * LT, :] = jnp.where(same, tile(zt_ref[b]), 0.0).astype(BF16)
        r = lax.broadcasted_iota(jnp.int32, (2 * SW, LT), 0)
        c = lax.broadcasted_iota(jnp.int32, (2 * SW, LT), 1)
        same = ((r % SW) // P) == (c // C)
        cl = cl_ref[...]
        for a in range(T):
            blk = jnp.dot(cl, ex_ref[a], preferred_element_type=F32)
            wo8_ref[:, a * LT:(a + 1) * LT] = jnp.where(same, blk, 0.0).astype(BF16)

    xs_ref[...] = us_ref[...].astype(F32)
    for b in range(T):
        u8_ref[:, b * LT:(b + 1) * LT] = xs_ref[pl.ds(b, nk, stride=T), :].astype(BF16)
    u8 = u8_ref[...]
    s = jnp.dot(u8, ws8_ref[...], preferred_element_type=F32)
    hr, hi = s[:, :SW], s[:, SW:]
    kidx = lax.broadcasted_iota(jnp.int32, (nk, SW), 0)
    d, step = 1, 0
    while d < nk:
        lr = lam_ref[step:step + 1, :SW]
        li = lam_ref[step:step + 1, SW:]
        keep = kidx >= d
        sr = jnp.where(keep, pltpu.roll(hr, d, 0), 0.0)
        si = jnp.where(keep, pltpu.roll(hi, d, 0), 0.0)
        hr, hi = hr + lr * sr - li * si, hi + lr * si + li * sr
        d, step = 2 * d, step + 1
    keep = kidx >= 1
    pr = jnp.where(keep, pltpu.roll(hr, 1, 0), 0.0)
    pi = jnp.where(keep, pltpu.roll(hi, 1, 0), 0.0)
    hcat = jnp.concatenate([pr, pi], axis=-1).astype(BF16)
    for q in range(T * LT // S5_COLS):
        kq = (q + 1) * S5_COLS
        yq = (jnp.dot(hcat, wo8_ref[:, q * S5_COLS:kq], preferred_element_type=F32)
              + jnp.dot(u8[:, :kq], m8_ref[0:kq, q * S5_COLS:kq], preferred_element_type=F32))
        for a4 in range(S5_COLS // LT):
            a = q * (S5_COLS // LT) + a4
            y_ref[pl.ds(a, nk, stride=T), :] = yq[:, a4 * LT:(a4 + 1) * LT]


def _s5_scan(proj, kt, zt, clc, lamp, ex, B, L):
    T, LT, SW = SSM_CHUNK, S5_LANES, S5_SW
    nk = L // T
    nt = SSM_WIDTH // LT
    per_tile = lambda a: pl.BlockSpec((None,) + a.shape[1:], lambda t, b: (t,) + (0,) * (a.ndim - 1))
    return pl.pallas_call(
        functools.partial(_s5_kernel, nk=nk),
        grid=(nt, B),
        in_specs=[pl.BlockSpec((L, LT), lambda t, b: (b, COL_US // LT + t)),
                  per_tile(kt), per_tile(zt), per_tile(clc), per_tile(lamp),
                  pl.BlockSpec(ex.shape, lambda t, b: (0, 0, 0))],
        out_specs=pl.BlockSpec((L, LT), lambda t, b: (b, t)),
        out_shape=jax.ShapeDtypeStruct((B * L, SSM_WIDTH), F32),
        scratch_shapes=[pltpu.VMEM((L, LT), F32), pltpu.VMEM((nk, T * LT), BF16),
                        pltpu.VMEM((T * LT, T * LT), BF16), pltpu.VMEM((T * LT, 2 * SW), BF16),
                        pltpu.VMEM((2 * SW, T * LT), BF16)],
        compiler_params=_cparams(("parallel", "arbitrary")),
        name="s5_scan",
    )(proj, kt, zt, clc, lamp, ex)


def _cmul(ar, ai, br, bi):
    return ar * br - ai * bi, ar * bi + ai * br


def _s5_tables(a_re, a_im, log_dt, b_re, b_im, c_re, c_im, nk):
    T, P, C, Gs = SSM_CHUNK, SSM_STATE, SSM_GROUP, SSM_GROUPS
    dt = jnp.exp(log_dt.astype(F32))[:, None]
    ar, ai = a_re.astype(F32), a_im.astype(F32)
    mag = jnp.exp(ar * dt)
    lr, li = mag * jnp.cos(ai * dt), mag * jnp.sin(ai * dt)
    den = ar * ar + ai * ai
    nr, ni = lr - 1.0, li
    fr, fi = (nr * ar + ni * ai) / den, (ni * ar - nr * ai) / den
    br, bim = b_re.astype(F32), b_im.astype(F32)
    bbr = fr[..., None] * br - fi[..., None] * bim
    bbi = fr[..., None] * bim + fi[..., None] * br
    jv = jnp.arange(T + 1, dtype=F32)[:, None, None]
    pmag = jnp.exp(jv * (ar * dt)[None])
    pwr, pwi = pmag * jnp.cos(jv * (ai * dt)[None]), pmag * jnp.sin(jv * (ai * dt)[None])
    zr = pwr[..., None] * bbr[None] - pwi[..., None] * bbi[None]
    zi = pwr[..., None] * bbi[None] + pwi[..., None] * bbr[None]
    cr, ci = c_re.astype(F32), c_im.astype(F32)
    kj = (jnp.einsum('gcp,jgpd->gjcd', cr, zr[:T], precision=HIGHEST)
          - jnp.einsum('gcp,jgpd->gjcd', ci, zi[:T], precision=HIGHEST))
    GT, NT = S5_GT, Gs // S5_GT
    kt = kj.reshape(NT, GT, T, C, C).transpose(0, 2, 4, 1, 3).reshape(NT, T, C, GT * C)
    lay_z = lambda z: z[:T][::-1].reshape(T, NT, GT, P, C).transpose(1, 0, 4, 2, 3).reshape(NT, T, C, GT * P)
    zt = jnp.concatenate([lay_z(zr), lay_z(zi)], axis=-1)
    pa_r, pa_i = pwr[1:].transpose(1, 0, 2)[:, :, None, :], pwi[1:].transpose(1, 0, 2)[:, :, None, :]
    clr = cr[:, None] * pa_r - ci[:, None] * pa_i
    cli = cr[:, None] * pa_i + ci[:, None] * pa_r
    lay_c = lambda x: x.reshape(NT, GT, T, C, P).transpose(0, 1, 4, 2, 3).reshape(NT, GT * P, T * C)
    clc = jnp.concatenate([lay_c(clr), lay_c(-cli)], axis=1).astype(BF16)
    qr, qi = pwr[T], pwi[T]
    steps = []
    d = 1
    while d < nk:
        steps.append(jnp.concatenate([qr.reshape(NT, GT * P), qi.reshape(NT, GT * P)], -1))
        qr, qi = _cmul(qr, qi, qr, qi)
        d *= 2
    lamp = jnp.stack(steps, 1)
    col = np.arange(T * C)[:, None]
    lane = np.arange(GT * C)[None, :]
    ex = np.stack([(col // C == a) & (col % C == lane % C) for a in range(T)]).astype(np.float32)
    return kt, zt, clc, lamp, jnp.asarray(ex, BF16)


def _glu_kernel(y_ref, u_ref, zb_ref, d_ref, w_ref, b_ref, o_ref):
    y = y_ref[...] + d_ref[...] * u_ref[...].astype(F32)
    yg = jax.nn.gelu(y).astype(BF16)
    z = jnp.dot(yg, w_ref[...], preferred_element_type=F32) + b_ref[...]
    zb = zb_ref[...].astype(F32)
    o_ref[...] = (yg.astype(F32) * jax.nn.sigmoid(z) * (zb * jax.nn.sigmoid(zb))).astype(o_ref.dtype)


def _s5_glu(y, proj, d_skip, w_glu, b_glu):
    N, W = y.shape
    tm = min(1024, N)
    return pl.pallas_call(
        _glu_kernel,
        grid=(N // tm,),
        in_specs=[pl.BlockSpec((tm, W), lambda i: (i, 0)),
                  pl.BlockSpec((tm, W), lambda i: (i, COL_US // W)),
                  pl.BlockSpec((tm, W), lambda i: (i, COL_ZB // W)),
                  pl.BlockSpec((1, W), lambda i: (0, 0)),
                  pl.BlockSpec((W, W), lambda i: (0, 0)),
                  pl.BlockSpec((1, W), lambda i: (0, 0))],
        out_specs=pl.BlockSpec((tm, W), lambda i: (i, 0)),
        out_shape=jax.ShapeDtypeStruct((N, W), BF16),
        compiler_params=_cparams(("parallel",)),
        name="s5_glu",
    )(y, proj, proj, d_skip.reshape(1, W), w_glu, b_glu.reshape(1, W))


def _merge_kernel(oa_ref, ob_ref, ga_ref, gb_ref, x_ref, gate_ref, wa_ref, wb_ref, wo_ref, lg_ref, lb_ref, o_ref):
    pa = jnp.dot(oa_ref[...], wa_ref[...], preferred_element_type=F32)
    pb = jnp.dot(ob_ref[...], wb_ref[...], preferred_element_type=F32)
    m = jax.nn.sigmoid(ga_ref[...].astype(F32)) * pa + jax.nn.sigmoid(gb_ref[...].astype(F32)) * pb
    y = jnp.dot(m.astype(BF16), wo_ref[...], preferred_element_type=F32)
    r = DEEPNORM_ALPHA * x_ref[...] + gate_ref[...] * y
    mu = jnp.mean(r, axis=-1, keepdims=True)
    rc = r - mu
    var = jnp.mean(rc * rc, axis=-1, keepdims=True)
    o_ref[...] = rc * lax.rsqrt(var + LN_EPS) * lg_ref[...] + lb_ref[...]


def _merge_out(oa, ob, proj, x2, gate, wa, wb, wo, ln_g, ln_b, L):
    N, D = x2.shape
    W = oa.shape[1]
    B = gate.shape[0]
    tm = min(256, L)
    const = lambda i: (0, 0)
    return pl.pallas_call(
        _merge_kernel,
        grid=(N // tm,),
        in_specs=[pl.BlockSpec((tm, W), lambda i: (i, 0)),
                  pl.BlockSpec((tm, W), lambda i: (i, 0)),
                  pl.BlockSpec((tm, D), lambda i: (i, COL_GA // D)),
                  pl.BlockSpec((tm, D), lambda i: (i, COL_GB // D)),
                  pl.BlockSpec((tm, D), lambda i: (i, 0)),
                  pl.BlockSpec((None, 1, D), lambda i: ((i * tm) // L, 0, 0)),
                  pl.BlockSpec((W, D), const),
                  pl.BlockSpec((W, D), const),
                  pl.BlockSpec((D, D), const),
                  pl.BlockSpec((1, D), const),
                  pl.BlockSpec((1, D), const)],
        out_specs=pl.BlockSpec((tm, D), lambda i: (i, 0)),
        out_shape=jax.ShapeDtypeStruct((N, D), F32),
        compiler_params=_cparams(("parallel",)),
        name="merge_out",
    )(oa, ob, proj, proj, x2, gate.reshape(B, 1, D), wa, wb, wo, ln_g.reshape(1, D), ln_b.reshape(1, D))


def _layer(x, c, w_ada, b_ada, w_in, rel_bias, cmp_pos_k, cmp_pos_v, w_cmp_k1, w_cmp_k2, w_cmp_v1, w_cmp_v2,
           ssm_a_re, ssm_a_im, ssm_log_dt, ssm_b_re, ssm_b_im, ssm_c_re, ssm_c_im, ssm_d, w_glu, b_glu,
           w_branch_nsa, w_branch_ssm, w_out, ln_g, ln_b):
    B, L, D = x.shape
    N = B * L
    G, dh = NSA_GROUPS, NSA_HEAD_DIM
    x2 = x.reshape(N, D)

    mod = _ada_mod(c, w_ada, b_ada)
    shift, scale, gate = mod[:, :D], mod[:, D:2 * D], mod[:, 2 * D:]

    o_q, o_kv, o_ng, o_za, o_us, o_zb, o_ga, o_gb = np.cumsum(
        [0, NSA_WIDTH, 6 * KV_WIDTH, 3 * NSA_HEADS, NSA_WIDTH, SSM_WIDTH, SSM_WIDTH, D_MODEL]).tolist()
    wt = w_in.T
    w_re = jnp.concatenate([
        wt[o_q:o_kv], wt[o_za:o_us], wt[o_us:o_zb], wt[o_zb:o_ga], wt[o_ga:o_gb],
        wt[o_gb:], wt[o_kv + 2 * KV_WIDTH:o_ng], wt[o_kv:o_kv + 2 * KV_WIDTH], wt[o_ng:o_za],
        jnp.zeros((PROJ_COLS - COL_NG - 3 * NSA_HEADS, D), w_in.dtype)], axis=0).astype(BF16)
    proj = _in_proj(x2, scale, shift, w_re, L)

    w1s = jnp.stack([w_cmp_k1, w_cmp_v1])
    w2s = jnp.stack([w_cmp_k2, w_cmp_v2])
    poss = jnp.broadcast_to(jnp.stack([cmp_pos_k, cmp_pos_v]).reshape(2, 1, CMP_BLOCK * dh), (2, 8, CMP_BLOCK * dh))
    kcv = _nsa_compress(proj, w1s, w2s, poss, B, L)
    kc = kcv[0].astype(BF16)
    vct = kcv[1].swapaxes(-1, -2).astype(BF16)
    ks, vst, kw, vwt = _kv_prep(proj, B, L)
    tc, tw, ts, ovt = _nsa_tables(rel_bias, L)
    o_a = _nsa_attend(proj, kc, vct, ks, vst, kw, vwt, tc, tw, ts, ovt, B, L)

    s5_tabs = _s5_tables(ssm_a_re, ssm_a_im, ssm_log_dt, ssm_b_re, ssm_b_im, ssm_c_re, ssm_c_im, L // SSM_CHUNK)
    y = _s5_scan(proj, *s5_tabs, B, L)
    o_b = _s5_glu(y, proj, ssm_d, w_glu.astype(BF16), b_glu)

    out = _merge_out(o_a, o_b, proj, x2, gate, w_branch_nsa.astype(BF16), w_branch_ssm.astype(BF16),
                     w_out.astype(BF16), ln_g, ln_b, L)
    return out.reshape(B, L, D)


def kernel(x, c, w_ada, b_ada, w_in, rel_bias, cmp_pos_k, cmp_pos_v, w_cmp_k1, w_cmp_k2, w_cmp_v1, w_cmp_v2,
           ssm_a_re, ssm_a_im, ssm_log_dt, ssm_b_re, ssm_b_im, ssm_c_re, ssm_c_im, ssm_d, w_glu, b_glu,
           w_branch_nsa, w_branch_ssm, w_out, ln_g, ln_b):
    for i in range(w_ada.shape[0]):
        x = _layer(x, c, w_ada[i], b_ada[i], w_in[i], rel_bias, cmp_pos_k[i], cmp_pos_v[i], w_cmp_k1[i],
                   w_cmp_k2[i], w_cmp_v1[i], w_cmp_v2[i], ssm_a_re[i], ssm_a_im[i], ssm_log_dt[i], ssm_b_re[i],
                   ssm_b_im[i], ssm_c_re[i], ssm_c_im[i], ssm_d[i], w_glu[i], b_glu[i], w_branch_nsa[i],
                   w_branch_ssm[i], w_out[i], ln_g[i], ln_b[i])
    return x
```

```python
import functools
import math

import numpy as np
import jax
import jax.numpy as jnp
from jax import lax
from jax.experimental import pallas as pl
from jax.experimental.pallas import tpu as pltpu

F32 = jnp.float32
BF16 = jnp.bfloat16
HIGHEST = lax.Precision.HIGHEST
NT_DIMS = (((1,), (1,)), ((), ()))

D_MODEL = 2048
NSA_HEADS = 16
NSA_GROUPS = 4
NSA_HEAD_DIM = 64
NSA_REP = NSA_HEADS // NSA_GROUPS
NSA_WIDTH = NSA_HEADS * NSA_HEAD_DIM
KV_WIDTH = NSA_GROUPS * NSA_HEAD_DIM
CMP_BLOCK = 32
CMP_STRIDE = 16
CMP_HIDDEN = 128
SEL_BLOCK = 64
SEL_TOPK = 8
WINDOW = 512
Q_BLOCK = 128
SSM_WIDTH = 1024
SSM_GROUP = 16
SSM_GROUPS = SSM_WIDTH // SSM_GROUP
SSM_STATE = 64
REL_BUCKETS = 32
REL_MAX_DIST = 128
DEEPNORM_ALPHA = 2.0 ** 0.25
LN_EPS = 1e-5
MASK_VALUE = -1e30
FORCE_VALUE = 1e4
NEVER_VALUE = -3e38
SSM_CHUNK = 16

COL_Q, COL_ZA, COL_US, COL_ZB, COL_GA, COL_GB, COL_KV, COL_KC, COL_NG = 0, 1024, 2048, 3072, 4096, 6144, 8192, 9216, 9728
PROJ_COLS = 9856
VMEM_LIMIT = 56 * 1024 * 1024


def _cparams(sem):
    return pltpu.CompilerParams(dimension_semantics=sem, vmem_limit_bytes=VMEM_LIMIT)


def _ada_kernel(c_ref, w_ref, b_ref, o_ref):
    o_ref[...] = jnp.dot(c_ref[...], w_ref[...], preferred_element_type=F32, precision=HIGHEST) + b_ref[...]


def _ada_mod(c, w_ada, b_ada):
    B, D = c.shape
    n = w_ada.shape[1]
    tn = 1536
    return pl.pallas_call(
        _ada_kernel,
        grid=(n // tn,),
        in_specs=[pl.BlockSpec((B, D), lambda j: (0, 0)),
                  pl.BlockSpec((D, tn), lambda j: (0, j)),
                  pl.BlockSpec((1, tn), lambda j: (0, j))],
        out_specs=pl.BlockSpec((B, tn), lambda j: (0, j)),
        out_shape=jax.ShapeDtypeStruct((B, n), F32),
        compiler_params=_cparams(("parallel",)),
        name="ada_mod",
    )(c, w_ada, b_ada.reshape(1, n))


def _inproj_kernel(x_ref, scale_ref, shift_ref, w_ref, o_ref, h_ref):
    @pl.when(pl.program_id(1) == 0)
    def _():
        def rows(k, carry):
            r0 = pl.multiple_of(k * LN_ROWS, LN_ROWS)
            x = x_ref[pl.ds(r0, LN_ROWS), :]
            mu = jnp.mean(x, axis=-1, keepdims=True)
            xc = x - mu
            var = jnp.mean(xc * xc, axis=-1, keepdims=True)
            hn = xc * lax.rsqrt(var + LN_EPS)
            h_ref[pl.ds(r0, LN_ROWS), :] = (hn * (1.0 + scale_ref[...]) + shift_ref[...]).astype(BF16)
            return carry

        lax.fori_loop(0, x_ref.shape[0] // LN_ROWS, rows, 0)

    o_ref[...] = lax.dot_general(h_ref[...], w_ref[...], NT_DIMS, preferred_element_type=F32).astype(o_ref.dtype)


LN_ROWS = 256


def _in_proj(x2, scale, shift, w, L):
    N, D = x2.shape
    ncol = w.shape[0]
    tm = min(1024, L)
    tn = 1408
    assert N % tm == 0 and L % tm == 0 and ncol % tn == 0
    B = scale.shape[0]
    return pl.pallas_call(
        _inproj_kernel,
        grid=(N // tm, ncol // tn),
        in_specs=[pl.BlockSpec((tm, D), lambda i, j: (i, 0)),
                  pl.BlockSpec((None, 1, D), lambda i, j: ((i * tm) // L, 0, 0)),
                  pl.BlockSpec((None, 1, D), lambda i, j: ((i * tm) // L, 0, 0)),
                  pl.BlockSpec((tn, D), lambda i, j: (j, 0))],
        out_specs=pl.BlockSpec((tm, tn), lambda i, j: (i, j)),
        out_shape=jax.ShapeDtypeStruct((N, ncol), BF16),
        scratch_shapes=[pltpu.VMEM((tm, D), BF16)],
        compiler_params=_cparams(("parallel", "arbitrary")),
        name="in_proj",
    )(x2, scale.reshape(B, 1, D), shift.reshape(B, 1, D), w)


def _cmp_kernel(x_ref, w1_ref, w1bd_ref, w2_ref, pos_ref, o_ref, xs_ref, *, ncp):
    S, dh = CMP_STRIDE, NSA_HEAD_DIM
    xs_ref[...] = x_ref[...].astype(F32)
    a = jnp.zeros((ncp, 2 * CMP_HIDDEN), F32)
    b = jnp.zeros((ncp, 2 * CMP_HIDDEN), F32)
    for j in range(S):
        xj = xs_ref[pl.ds(j, ncp, stride=S), :].astype(BF16)
        a = a + jnp.dot(xj, w1bd_ref[0, j], preferred_element_type=F32)
        b = b + jnp.dot(xj, w1bd_ref[1, j], preferred_element_type=F32)
    pw = jnp.dot(pos_ref[...], w1_ref[...], preferred_element_type=F32, precision=HIGHEST)[0:1]
    h = a + pltpu.roll(b, ncp - 1, 0) + jnp.concatenate([pw, pw], axis=1)
    w2 = w2_ref[...].astype(BF16)
    for g in range(2):
        hg = jax.nn.gelu(h[:, g * CMP_HIDDEN:(g + 1) * CMP_HIDDEN]).astype(BF16)
        o_ref[g] = jnp.dot(hg, w2, preferred_element_type=F32)


def _nsa_compress(proj, w1s, w2s, poss, B, L):
    G, dh, S = NSA_GROUPS, NSA_HEAD_DIM, CMP_STRIDE
    ncp = L // S
    w = w1s.reshape(2, 2, S, dh, CMP_HIDDEN).astype(BF16)
    z = jnp.zeros_like(w)
    w1bd = jnp.concatenate([jnp.concatenate([w, z], -1), jnp.concatenate([z, w], -1)], axis=-2)
    return pl.pallas_call(
        functools.partial(_cmp_kernel, ncp=ncp),
        grid=(2, B, G // 2),
        in_specs=[pl.BlockSpec((L, 2 * dh), lambda s, b, t: (b, COL_KC // (2 * dh) + 2 * s + t)),
                  pl.BlockSpec((None,) + w1s.shape[1:], lambda s, b, t: (s, 0, 0)),
                  pl.BlockSpec((None,) + w1bd.shape[1:], lambda s, b, t: (s, 0, 0, 0, 0)),
                  pl.BlockSpec((None,) + w2s.shape[1:], lambda s, b, t: (s, 0, 0)),
                  pl.BlockSpec((None,) + poss.shape[1:], lambda s, b, t: (s, 0, 0))],
        out_specs=pl.BlockSpec((None, None, 2, ncp, dh), lambda s, b, t: (s, b, t, 0, 0)),
        out_shape=jax.ShapeDtypeStruct((2, B, G, ncp, dh), F32),
        scratch_shapes=[pltpu.VMEM((L, 2 * dh), F32)],
        compiler_params=_cparams(("parallel", "parallel", "parallel")),
        name="nsa_compress",
    )(proj, w1s, w1bd, w2s, poss)


V_ROWS = NSA_HEAD_DIM + 16


def _kvprep_kernel(x_ref, ks_ref, vst_ref, kw_ref, vwt_ref, *, TL):
    G, dh = NSA_GROUPS, NSA_HEAD_DIM
    i = pl.program_id(1)
    tile = jnp.maximum(i - 1, 0)
    x = x_ref[...].astype(F32)
    lane = lax.broadcasted_iota(jnp.int32, (TL, 2 * dh), 1)
    row = lax.broadcasted_iota(jnp.int32, (TL, 2 * dh), 0) + tile * TL
    onehot = jnp.where(lane - dh == row // SEL_BLOCK, 1.0, 0.0)
    padrow = jnp.where(lane == dh, 1.0, 0.0)
    tail = jnp.where(lax.broadcasted_iota(jnp.int32, (V_ROWS - dh, TL), 0) == 0, 1.0, 0.0)
    is_pad = i == 0
    for t in range(G // 2):
        xk, xv, xwk, xwv = [x[:, (s * G + 2 * t) * dh:(s * G + 2 * t + 2) * dh] for s in range(4)]
        xvt, xwvt = xv.T, xwv.T
        for h in range(2):
            g = 2 * t + h
            kk = xk if h == 0 else pltpu.roll(xk, dh, 1)
            kwk = xwk if h == 0 else pltpu.roll(xwk, dh, 1)
            ks_ref[g] = jnp.where(lane < dh, kk, onehot).astype(BF16)
            kw_ref[g] = jnp.where(is_pad, padrow, jnp.where(lane < dh, kwk, 0.0)).astype(BF16)
            vst_ref[g] = jnp.concatenate([xvt[h * dh:(h + 1) * dh], tail], axis=0).astype(BF16)
            vw = jnp.concatenate([xwvt[h * dh:(h + 1) * dh], tail], axis=0)
            vwt_ref[g] = jnp.where(is_pad, 0.0, vw).astype(BF16)


def _kv_prep(proj, B, L):
    G, dh, TL = NSA_GROUPS, NSA_HEAD_DIM, WINDOW
    nt = L // TL
    wcols = 4 * G * dh
    assert L // SEL_BLOCK <= dh and COL_KV % wcols == 0 and L % TL == 0
    data = lambda b, i: jnp.maximum(i - 1, 0)
    k_shape = jax.ShapeDtypeStruct((B, G, L, 2 * dh), BF16)
    v_shape = jax.ShapeDtypeStruct((B, G, V_ROWS, L), BF16)
    kw_shape = jax.ShapeDtypeStruct((B, G, L + WINDOW, 2 * dh), BF16)
    vw_shape = jax.ShapeDtypeStruct((B, G, V_ROWS, L + WINDOW), BF16)
    return pl.pallas_call(
        functools.partial(_kvprep_kernel, TL=TL),
        grid=(B, nt + 1),
        in_specs=[pl.BlockSpec((TL, wcols), lambda b, i: (b * nt + data(b, i), COL_KV // wcols))],
        out_specs=[pl.BlockSpec((None, G, TL, 2 * dh), lambda b, i: (b, 0, data(b, i), 0)),
                   pl.BlockSpec((None, G, V_ROWS, TL), lambda b, i: (b, 0, 0, data(b, i))),
                   pl.BlockSpec((None, G, TL, 2 * dh), lambda b, i: (b, 0, i, 0)),
                   pl.BlockSpec((None, G, V_ROWS, TL), lambda b, i: (b, 0, 0, i))],
        out_shape=[k_shape, v_shape, kw_shape, vw_shape],
        compiler_params=_cparams(("parallel", "arbitrary")),
        name="kv_prep",
    )(proj)


SEL_TILE = 512
SEL_TABLE_FAR = 640
NSA_PROBLEMS = 4


def _tree(op, parts):
    while len(parts) > 1:
        parts = [op(parts[i], parts[i + 1]) if i + 1 < len(parts) else parts[i] for i in range(0, len(parts), 2)]
    return parts[0]


def _fold8(op, x):
    return _tree(op, [x[k:k + 8] for k in range(0, x.shape[0], 8)])


def _safe_inv(l):
    return jnp.where(l > 0.0, 1.0 / jnp.where(l > 0.0, l, 1.0), 0.0)


def _nsa_kernel(q_ref, za_ref, ng_ref, kc_ref, vct_ref, ks_ref, vst_ref, kw_ref, vwt_ref,
                tc_ref, tw_ref, ts_ref, ovt_ref, o_ref, s_ref, s2_ref, acc_ref, gt_ref, *, L):
    R, dh, QB = NSA_REP, NSA_HEAD_DIM, Q_BLOCK
    ncp = L // CMP_STRIDE
    nsel = L // SEL_BLOCK
    W = R * QB
    g = pl.program_id(1)
    qb = pl.program_id(2)

    NP = q_ref.shape[0]

    def front(p):
        qt = (q_ref[p].astype(F32) * (dh ** -0.5)).T
        qT = jnp.concatenate([qt[r * dh:(r + 1) * dh] for r in range(R)], axis=1).astype(BF16)

        c_off = pl.multiple_of((ncp - 8) - 8 * qb, 8)
        sc = jnp.dot(kc_ref[p], qT, preferred_element_type=F32) + tc_ref[pl.ds(c_off, ncp), :]
        m = jnp.maximum(jnp.max(_fold8(jnp.maximum, sc), axis=0, keepdims=True), 0.1 * MASK_VALUE)
        e = jnp.exp(sc - m)
        l = jnp.sum(_fold8(jnp.add, e), axis=0, keepdims=True)
        p_c = e * _safe_inv(l)
        oc = jnp.dot(vct_ref[p], p_c.astype(BF16), preferred_element_type=F32)

        psum = _tree(jnp.add, [p_c[:, r * QB:(r + 1) * QB] for r in range(R)])
        p_hi = psum.astype(BF16)
        p_lo = (psum - p_hi.astype(F32)).astype(BF16)
        ovt = ovt_ref[...]
        imp = jnp.dot(ovt, p_hi, preferred_element_type=F32) + jnp.dot(ovt, p_lo, preferred_element_type=F32)
        jj = lax.broadcasted_iota(jnp.int32, (nsel, QB), 0)
        ii = lax.broadcasted_iota(jnp.int32, (nsel, QB), 1)
        cur = 2 * qb + (ii >= SEL_BLOCK).astype(jnp.int32)
        forced = (jj == 0) | (jj == cur) | (jj == cur - 1)
        imp = jnp.where(forced, FORCE_VALUE, imp)
        imp = jnp.where(jj <= cur, imp, MASK_VALUE)
        jf = jj.astype(F32)
        sel = jnp.zeros((nsel, QB), F32)
        for _ in range(min(SEL_TOPK, nsel)):
            mx = jnp.max(_fold8(jnp.maximum, imp), axis=0, keepdims=True)
            idx = jnp.min(_fold8(jnp.minimum, jnp.where(imp == mx, jf, 1e9)), axis=0, keepdims=True)
            hit = jf == idx
            sel = jnp.where(hit & (mx > 0.1 * MASK_VALUE), 1.0, sel)
            imp = jnp.where(hit, -jnp.inf, imp)
        selbias = jnp.where(sel > 0.5, 0.0, MASK_VALUE).astype(BF16)
        sel_rows = [jnp.concatenate([selbias] * R, axis=1)]
        if nsel < dh:
            sel_rows.append(jnp.zeros((dh - nsel, W), BF16))
        q_sel = jnp.concatenate([qT] + sel_rows, axis=0)

        w0 = pl.multiple_of(qb * QB, QB)
        flag = jnp.where(lax.broadcasted_iota(jnp.int32, (dh, W), 0) == 0, MASK_VALUE, 0.0).astype(BF16)
        q_win = jnp.concatenate([qT, flag], axis=0)
        sw = jnp.dot(kw_ref[p, pl.ds(w0, WINDOW + QB), :], q_win, preferred_element_type=F32) + tw_ref[...]
        m_w = jnp.max(_fold8(jnp.maximum, sw), axis=0, keepdims=True)
        pw = jnp.exp((sw - m_w).astype(BF16))
        accw = jnp.dot(vwt_ref[p, :, pl.ds(w0, WINDOW + QB)], pw, preferred_element_type=F32)
        owin = accw[:dh] * _safe_inv(accw[dh:dh + 1])

        gt_ref[p] = jax.nn.sigmoid(ng_ref[p].astype(F32)).T

        def gate_row(branch):
            return jnp.concatenate([gt_ref[p, pl.ds(3 * (g * R + r) + branch, 1), :] for r in range(R)], axis=1)

        return q_sel, oc * gate_row(0) + owin * gate_row(2), gate_row(1)

    fronts = [front(p) for p in range(NP)]

    ntile = qb // (SEL_TILE // QB) + 1
    npair = ntile // 2
    odd = ntile % 2

    slots = (s_ref, s2_ref)

    def scores(t0, p, slot, nt=2):
        mx = None
        for h in range(nt):
            t = t0 + h
            k0 = pl.multiple_of(t * SEL_TILE, SEL_TILE)
            x0 = pl.multiple_of(jnp.maximum(t * SEL_TILE - qb * QB + SEL_TABLE_FAR, 0), QB)
            s = (jnp.dot(ks_ref[p, pl.ds(k0, SEL_TILE), :], fronts[p][0], preferred_element_type=F32)
                 + ts_ref[pl.ds(x0, SEL_TILE), :])
            slots[slot][p, h * SEL_TILE:(h + 1) * SEL_TILE, :] = s
            f = _fold8(jnp.maximum, s)
            mx = f if mx is None else jnp.maximum(mx, f)
        return jnp.max(mx, axis=0, keepdims=True)

    def accumulate(t0, p, slot, m_new, m_old, nt=2):
        part = None
        for h in range(nt):
            k0 = pl.multiple_of((t0 + h) * SEL_TILE, SEL_TILE)
            pr = jnp.exp((slots[slot][p, h * SEL_TILE:(h + 1) * SEL_TILE, :] - m_new).astype(BF16))
            pv = jnp.dot(vst_ref[p, :, pl.ds(k0, SEL_TILE)], pr, preferred_element_type=F32)
            part = pv if part is None else part + pv
        acc_ref[p] = acc_ref[p] * jnp.exp(m_old - m_new) + part

    def phase(t0, slot, m_cur, m_acc, nt_next=2):
        m_next = []
        for p in range(NP):
            accumulate(t0, p, slot, m_cur[p], m_acc[p])
            m_next.append(jnp.maximum(m_cur[p], scores(t0 + 2, p, 1 - slot, nt_next)))
        return tuple(m_next), m_cur

    def finish(t0, slot, m_cur, m_acc):
        @pl.when(odd == 1)
        def _():
            m2, m1 = phase(t0, slot, m_cur, m_acc, nt_next=1)
            for p in range(NP):
                accumulate(t0 + 2, p, 1 - slot, m2[p], m1[p], nt=1)

        @pl.when(odd == 0)
        def _():
            for p in range(NP):
                accumulate(t0, p, slot, m_cur[p], m_acc[p])

    acc_ref[...] = jnp.zeros_like(acc_ref)

    @pl.when(npair == 0)
    def _():
        for p in range(NP):
            m = scores(0, p, 0, nt=1)
            accumulate(0, p, 0, m, m, nt=1)

    @pl.when(npair > 0)
    def _():
        m_first = tuple(scores(0, p, 0) for p in range(NP))

        def trip(v, carry):
            carry = phase(4 * v, 0, *carry)
            return phase(4 * v + 2, 1, *carry)

        n_rest = npair - 1
        m_cur, m_acc = lax.fori_loop(0, n_rest // 2, trip, (m_first, m_first))
        t_last = 4 * (n_rest // 2)

        @pl.when(n_rest % 2 == 1)
        def _():
            finish(t_last + 2, 1, *phase(t_last, 0, m_cur, m_acc))

        @pl.when(n_rest % 2 == 0)
        def _():
            finish(t_last, 0, m_cur, m_acc)

    for p in range(NP):
        acc = acc_ref[p]
        osel = acc[:dh] * _safe_inv(acc[dh:dh + 1])
        ot = fronts[p][1] + osel * fronts[p][2]
        o = jnp.concatenate([ot[:, r * QB:(r + 1) * QB] for r in range(R)], axis=0).T
        za = za_ref[p].astype(F32)
        o_ref[p] = (o * (za * jax.nn.sigmoid(za))).astype(o_ref.dtype)


def _nsa_attend(proj, kc, vct, ks, vst, kw, vwt, tc, tw, ts, ovt, B, L):
    R, dh, G, QB = NSA_REP, NSA_HEAD_DIM, NSA_GROUPS, Q_BLOCK
    assert L % (2 * SEL_TILE) == 0
    nqb = L // QB
    gw = R * dh
    W = R * QB
    NP = NSA_PROBLEMS if B % NSA_PROBLEMS == 0 else 1
    proj4 = proj.reshape(B // NP, NP, L, proj.shape[-1])
    tok = lambda col: pl.BlockSpec((None, NP, QB, col[1]), lambda b, g, i: (b, 0, i, col[0] // col[1] + col[2] * g))
    bg = lambda a: pl.BlockSpec((NP, None) + a.shape[2:], lambda b, g, i: (b, g, 0, 0),
                                pipeline_mode=pl.Buffered(1))
    grp = lambda a: pl.BlockSpec((None,) + a.shape[1:], lambda b, g, i: (g, 0, 0), pipeline_mode=pl.Buffered(1))
    out = pl.pallas_call(
        functools.partial(_nsa_kernel, L=L),
        grid=(B // NP, G, nqb),
        in_specs=[tok((COL_Q, gw, 1)), tok((COL_ZA, gw, 1)), tok((COL_NG, 128, 0)),
                  bg(kc), bg(vct), bg(ks), bg(vst), bg(kw), bg(vwt), grp(tc), grp(tw), grp(ts),
                  pl.BlockSpec(ovt.shape, lambda b, g, i: (0, 0))],
        out_specs=pl.BlockSpec((None, NP, QB, gw), lambda b, g, i: (b, 0, i, g)),
        out_shape=jax.ShapeDtypeStruct((B // NP, NP, L, NSA_WIDTH), BF16),
        scratch_shapes=[pltpu.VMEM((NP, 2 * SEL_TILE, W), F32), pltpu.VMEM((NP, 2 * SEL_TILE, W), F32),
                        pltpu.VMEM((NP, V_ROWS, W), F32),
                        pltpu.VMEM((NP, 128, QB), F32)],
        compiler_params=_cparams(("parallel", "parallel", "arbitrary")),
        name="nsa_attend",
    )(proj4, proj4, proj4, kc, vct, ks, vst, kw, vwt, tc, tw, ts, ovt)
    return out.reshape(B * L, NSA_WIDTH)


def _t5_bucket(dist):
    n = jnp.maximum(dist, 0)
    max_exact = REL_BUCKETS // 2
    nf = jnp.maximum(n, max_exact).astype(F32)
    large = max_exact + (jnp.log(nf / max_exact) / math.log(REL_MAX_DIST / max_exact)
                         * (REL_BUCKETS - max_exact)).astype(jnp.int32)
    large = jnp.minimum(large, REL_BUCKETS - 1)
    return jnp.where(n < max_exact, n, large)


def _toeplitz(f, d0, base, step, n_rows, width):
    rpb = width // step
    nblk = n_rows // rpb
    assert rpb * step == width and nblk * rpb == n_rows
    lo = base - width * nblk - d0
    return f[..., lo:lo + width * (nblk + 1)].reshape(f.shape[:-1] + (nblk + 1, width))


TABLE_STEPS = (CMP_STRIDE, 1, 1)


def _tables_kernel(*refs, runs):
    R, QB = NSA_REP, Q_BLOCK
    n = len(TABLE_STEPS)
    tmp_ref = refs[2 * n]
    for win_ref, out_ref, step, table_runs in zip(refs[:n], refs[n:2 * n], TABLE_STEPS, runs):
        rpb = QB // step

        def block(a, carry, varies, win_ref=win_ref, out_ref=out_ref, step=step, rpb=rpb):
            r0 = pl.multiple_of(a * rpb, rpb)
            top = win_ref.shape[1] - 1 - a
            for r in range(R):
                if varies:
                    win = jnp.concatenate([win_ref[r, pl.ds(top, 1), :], win_ref[r, pl.ds(top - 1, 1), :]], axis=1)
                    x = jnp.broadcast_to(win, (QB, 2 * QB))
                    y = pltpu.roll(x, 0, 1, stride=1, stride_axis=0)[:, :QB]
                    if step > 1:
                        tmp_ref[...] = y
                        y = tmp_ref[pl.ds(0, rpb, stride=step), :]
                else:
                    y = jnp.broadcast_to(win_ref[r, pl.ds(top, 1), :], (rpb, QB))
                out_ref[pl.ds(r0, rpb), r * QB:(r + 1) * QB] = y
            return carry

        for first, end, varies in table_runs:
            lax.fori_loop(first, end, functools.partial(block, varies=varies), 0)


def _constant_runs(base, step, n_rows, upper):
    far = REL_MAX_DIST
    rpb = Q_BLOCK // step
    flags = []
    for a in range(n_rows // rpb):
        lo, hi = base - Q_BLOCK * a - Q_BLOCK + step, base - Q_BLOCK * a + Q_BLOCK - 1
        flags.append(not (hi < 0 or lo >= upper or (lo >= far and hi < upper)))
    runs, start = [], 0
    for a in range(1, len(flags) + 1):
        if a == len(flags) or flags[a] != flags[start]:
            runs.append((start, a, flags[start]))
            start = a
    return tuple(runs)


def _expand_tables(wins, runs):
    G, R, QB = NSA_GROUPS, NSA_REP, Q_BLOCK
    rows = [(w.shape[1] - 1) * (QB // s) for w, s in zip(wins, TABLE_STEPS)]
    return pl.pallas_call(
        functools.partial(_tables_kernel, runs=runs),
        grid=(G,),
        in_specs=[pl.BlockSpec((R,) + w.shape[1:], lambda g: (g, 0, 0)) for w in wins],
        out_specs=[pl.BlockSpec((None, n, R * QB), lambda g: (g, 0, 0)) for n in rows],
        out_shape=[jax.ShapeDtypeStruct((G, n, R * QB), F32) for n in rows],
        scratch_shapes=[pltpu.VMEM((QB, QB), F32)],
        compiler_params=_cparams(("parallel",)),
        name="nsa_tables",
    )(*wins)


def _nsa_tables(rel_bias, L):
    QB, G, R = Q_BLOCK, NSA_GROUPS, NSA_REP
    ncp = L // CMP_STRIDE
    nsel = L // SEL_BLOCK
    npos = L + QB
    tbl = rel_bias.astype(F32)
    bpos = (tbl[_t5_bucket(jnp.arange(npos, dtype=jnp.int32))] - tbl[REL_BUCKETS - 1][None, :]).T
    f = jnp.concatenate([jnp.full((NSA_HEADS, npos), MASK_VALUE, F32), bpos], axis=-1)
    d = np.arange(-npos, npos)
    fw = jnp.where(jnp.asarray(d < WINDOW), f, MASK_VALUE)
    spec_c = (L - QB - CMP_BLOCK + 1, CMP_STRIDE, 2 * ncp - 8)
    spec_w = (WINDOW, 1, WINDOW + QB)
    spec_s = (SEL_TABLE_FAR, 1, SEL_TABLE_FAR + SEL_TILE)
    assert TABLE_STEPS == (spec_c[1], spec_w[1], spec_s[1])
    wins = [_toeplitz(src, -npos, *spec, QB) for src, spec in ((f, spec_c), (fw, spec_w), (f, spec_s))]
    runs = tuple(_constant_runs(*spec, upper) for spec, upper in ((spec_c, 2 * npos), (spec_w, WINDOW), (spec_s, 2 * npos)))
    tc, tw, ts = _expand_tables(wins, runs)
    c = np.arange(ncp)[None, :]
    j = np.arange(nsel)[:, None]
    ovt = ((c - 4 * j >= -1) & (c - 4 * j <= 3)).astype(np.float32)
    return tc, tw, ts, jnp.asarray(ovt, BF16)


S5_LANES = 128
S5_GT = S5_LANES // SSM_GROUP
S5_SW = S5_GT * SSM_STATE
S5_COLS = 4 * S5_LANES


def _s5_kernel(us_ref, kt_ref, zt_ref, cl_ref, lam_ref, ex_ref, y_ref,
               xs_ref, u8_ref, m8_ref, ws8_ref, wo8_ref, *, nk):
    T, C, P, GT, LT, SW = SSM_CHUNK, SSM_GROUP, SSM_STATE, S5_GT, S5_LANES, S5_SW

    @pl.when(pl.program_id(1) == 0)
    def _build():
        tile = lambda x: jnp.concatenate([x] * GT, axis=0)
        r = lax.broadcasted_iota(jnp.int32, (LT, LT), 0)
        c = lax.broadcasted_iota(jnp.int32, (LT, LT), 1)
        same = (r // C) == (c // C)
        m8_ref[...] = jnp.zeros_like(m8_ref)
        for j in range(T):
            bd = jnp.where(same, tile(kt_ref[j]), 0.0).astype(BF16)
            for b in range(T - j):
                m8_ref[b * LT:(b + 1) * LT, (b + j) * LT:(b + j + 1) * LT] = bd
        r = lax.broadcasted_iota(jnp.int32, (LT, 2 * SW), 0)
        c = lax.broadcasted_iota(jnp.int32, (LT, 2 * SW), 1)
        same = (r // C) == ((c % SW) // P)
        for b in range(T):
            ws8_ref[b * LT:(b + 1) * LT, :] = jnp.where(same, tile(zt_ref[b]), 0.0).astype(BF16)
        r = lax.broadcasted_iota(jnp.int32, (2 * SW, LT), 0)
        c = lax.broadcasted_iota(jnp.int32, (2 * SW, LT), 1)
        same = ((r % SW) // P) == (c // C)
        cl = cl_ref[...]
        for a in range(T):
            blk = jnp.dot(cl, ex_ref[a], preferred_element_type=F32)
            wo8_ref[:, a * LT:(a + 1) * LT] = jnp.where(same, blk, 0.0).astype(BF16)

    xs_ref[...] = us_ref[...].astype(F32)
    for b in range(T):
        u8_ref[:, b * LT:(b + 1) * LT] = xs_ref[pl.ds(b, nk, stride=T), :].astype(BF16)
    u8 = u8_ref[...]
    s = jnp.dot(u8, ws8_ref[...], preferred_element_type=F32)
    hr, hi = s[:, :SW], s[:, SW:]
    kidx = lax.broadcasted_iota(jnp.int32, (nk, SW), 0)
    d, step = 1, 0
    while d < nk:
        lr = lam_ref[step:step + 1, :SW]
        li = lam_ref[step:step + 1, SW:]
        keep = kidx >= d
        sr = jnp.where(keep, pltpu.roll(hr, d, 0), 0.0)
        si = jnp.where(keep, pltpu.roll(hi, d, 0), 0.0)
        hr, hi = hr + lr * sr - li * si, hi + lr * si + li * sr
        d, step = 2 * d, step + 1
    keep = kidx >= 1
    pr = jnp.where(keep, pltpu.roll(hr, 1, 0), 0.0)
    pi = jnp.where(keep, pltpu.roll(hi, 1, 0), 0.0)
    hcat = jnp.concatenate([pr, pi], axis=-1).astype(BF16)
    for q in range(T * LT // S5_COLS):
        kq = (q + 1) * S5_COLS
        yq = (jnp.dot(hcat, wo8_ref[:, q * S5_COLS:kq], preferred_element_type=F32)
              + jnp.dot(u8[:, :kq], m8_ref[0:kq, q * S5_COLS:kq], preferred_element_type=F32))
        for a4 in range(S5_COLS // LT):
            a = q * (S5_COLS // LT) + a4
            y_ref[pl.ds(a, nk, stride=T), :] = yq[:, a4 * LT:(a4 + 1) * LT]


def _s5_scan(proj, kt, zt, clc, lamp, ex, B, L):
    T, LT, SW = SSM_CHUNK, S5_LANES, S5_SW
    nk = L // T
    nt = SSM_WIDTH // LT
    per_tile = lambda a: pl.BlockSpec((None,) + a.shape[1:], lambda t, b: (t,) + (0,) * (a.ndim - 1))
    return pl.pallas_call(
        functools.partial(_s5_kernel, nk=nk),
        grid=(nt, B),
        in_specs=[pl.BlockSpec((L, LT), lambda t, b: (b, COL_US // LT + t)),
                  per_tile(kt), per_tile(zt), per_tile(clc), per_tile(lamp),
                  pl.BlockSpec(ex.shape, lambda t, b: (0, 0, 0))],
        out_specs=pl.BlockSpec((L, LT), lambda t, b: (b, t)),
        out_shape=jax.ShapeDtypeStruct((B * L, SSM_WIDTH), F32),
        scratch_shapes=[pltpu.VMEM((L, LT), F32), pltpu.VMEM((nk, T * LT), BF16),
                        pltpu.VMEM((T * LT, T * LT), BF16), pltpu.VMEM((T * LT, 2 * SW), BF16),
                        pltpu.VMEM((2 * SW, T * LT), BF16)],
        compiler_params=_cparams(("parallel", "arbitrary")),
        name="s5_scan",
    )(proj, kt, zt, clc, lamp, ex)


def _cmul(ar, ai, br, bi):
    return ar * br - ai * bi, ar * bi + ai * br


def _s5_tables(a_re, a_im, log_dt, b_re, b_im, c_re, c_im, nk):
    T, P, C, Gs = SSM_CHUNK, SSM_STATE, SSM_GROUP, SSM_GROUPS
    dt = jnp.exp(log_dt.astype(F32))[:, None]
    ar, ai = a_re.astype(F32), a_im.astype(F32)
    mag = jnp.exp(ar * dt)
    lr, li = mag * jnp.cos(ai * dt), mag * jnp.sin(ai * dt)
    den = ar * ar + ai * ai
    nr, ni = lr - 1.0, li
    fr, fi = (nr * ar + ni * ai) / den, (ni * ar - nr * ai) / den
    br, bim = b_re.astype(F32), b_im.astype(F32)
    bbr = fr[..., None] * br - fi[..., None] * bim
    bbi = fr[..., None] * bim + fi[..., None] * br
    jv = jnp.arange(T + 1, dtype=F32)[:, None, None]
    pmag = jnp.exp(jv * (ar * dt)[None])
    pwr, pwi = pmag * jnp.cos(jv * (ai * dt)[None]), pmag * jnp.sin(jv * (ai * dt)[None])
    zr = pwr[..., None] * bbr[None] - pwi[..., None] * bbi[None]
    zi = pwr[..., None] * bbi[None] + pwi[..., None] * bbr[None]
    cr, ci = c_re.astype(F32), c_im.astype(F32)
    kj = (jnp.einsum('gcp,jgpd->gjcd', cr, zr[:T], precision=HIGHEST)
          - jnp.einsum('gcp,jgpd->gjcd', ci, zi[:T], precision=HIGHEST))
    GT, NT = S5_GT, Gs // S5_GT
    kt = kj.reshape(NT, GT, T, C, C).transpose(0, 2, 4, 1, 3).reshape(NT, T, C, GT * C)
    pow_b = lambda pw: pw[T - 1::-1].reshape(T, NT, GT, P).transpose(1, 0, 2, 3)[:, :, None]
    bbar_t = lambda bb: bb.reshape(NT, GT, P, C).transpose(0, 3, 1, 2)[:, None]
    pb_r, pb_i, bt_r, bt_i = pow_b(pwr), pow_b(pwi), bbar_t(bbr), bbar_t(bbi)
    zt = jnp.concatenate([(pb_r * bt_r - pb_i * bt_i).reshape(NT, T, C, GT * P),
                          (pb_r * bt_i + pb_i * bt_r).reshape(NT, T, C, GT * P)], axis=-1)
    pow_a = lambda pw: pw[1:].reshape(T, NT, GT, P).transpose(1, 2, 3, 0)[..., None]
    c_t = lambda cc: cc.reshape(NT, GT, C, P).transpose(0, 1, 3, 2)[:, :, :, None, :]
    pa_r, pa_i, ct_r, ct_i = pow_a(pwr), pow_a(pwi), c_t(cr), c_t(ci)
    clc = jnp.concatenate([(ct_r * pa_r - ct_i * pa_i).reshape(NT, GT * P, T * C),
                           (-(ct_r * pa_i + ct_i * pa_r)).reshape(NT, GT * P, T * C)], axis=1).astype(BF16)
    qr, qi = pwr[T], pwi[T]
    steps = []
    d = 1
    while d < nk:
        steps.append(jnp.concatenate([qr.reshape(NT, GT * P), qi.reshape(NT, GT * P)], -1))
        qr, qi = _cmul(qr, qi, qr, qi)
        d *= 2
    lamp = jnp.stack(steps, 1)
    col = np.arange(T * C)[:, None]
    lane = np.arange(GT * C)[None, :]
    ex = np.stack([(col // C == a) & (col % C == lane % C) for a in range(T)]).astype(np.float32)
    return kt, zt, clc, lamp, jnp.asarray(ex, BF16)


def _glu_kernel(y_ref, u_ref, zb_ref, d_ref, w_ref, b_ref, o_ref):
    y = y_ref[...] + d_ref[...] * u_ref[...].astype(F32)
    yg = jax.nn.gelu(y).astype(BF16)
    z = jnp.dot(yg, w_ref[...], preferred_element_type=F32) + b_ref[...]
    zb = zb_ref[...].astype(F32)
    o_ref[...] = (yg.astype(F32) * jax.nn.sigmoid(z) * (zb * jax.nn.sigmoid(zb))).astype(o_ref.dtype)


def _s5_glu(y, proj, d_skip, w_glu, b_glu):
    N, W = y.shape
    tm = min(1024, N)
    return pl.pallas_call(
        _glu_kernel,
        grid=(N // tm,),
        in_specs=[pl.BlockSpec((tm, W), lambda i: (i, 0)),
                  pl.BlockSpec((tm, W), lambda i: (i, COL_US // W)),
                  pl.BlockSpec((tm, W), lambda i: (i, COL_ZB // W)),
                  pl.BlockSpec((1, W), lambda i: (0, 0)),
                  pl.BlockSpec((W, W), lambda i: (0, 0)),
                  pl.BlockSpec((1, W), lambda i: (0, 0))],
        out_specs=pl.BlockSpec((tm, W), lambda i: (i, 0)),
        out_shape=jax.ShapeDtypeStruct((N, W), BF16),
        compiler_params=_cparams(("parallel",)),
        name="s5_glu",
    )(y, proj, proj, d_skip.reshape(1, W), w_glu, b_glu.reshape(1, W))


def _merge_kernel(oa_ref, ob_ref, ga_ref, gb_ref, x_ref, gate_ref, wa_ref, wb_ref, wo_ref, lg_ref, lb_ref, o_ref):
    pa = jnp.dot(oa_ref[...], wa_ref[...], preferred_element_type=F32)
    pb = jnp.dot(ob_ref[...], wb_ref[...], preferred_element_type=F32)
    m = jax.nn.sigmoid(ga_ref[...].astype(F32)) * pa + jax.nn.sigmoid(gb_ref[...].astype(F32)) * pb
    y = jnp.dot(m.astype(BF16), wo_ref[...], preferred_element_type=F32)
    r = DEEPNORM_ALPHA * x_ref[...] + gate_ref[...] * y
    mu = jnp.mean(r, axis=-1, keepdims=True)
    rc = r - mu
    var = jnp.mean(rc * rc, axis=-1, keepdims=True)
    o_ref[...] = rc * lax.rsqrt(var + LN_EPS) * lg_ref[...] + lb_ref[...]


def _merge_out(oa, ob, proj, x2, gate, wa, wb, wo, ln_g, ln_b, L):
    N, D = x2.shape
    W = oa.shape[1]
    B = gate.shape[0]
    tm = min(256, L)
    const = lambda i: (0, 0)
    return pl.pallas_call(
        _merge_kernel,
        grid=(N // tm,),
        in_specs=[pl.BlockSpec((tm, W), lambda i: (i, 0)),
                  pl.BlockSpec((tm, W), lambda i: (i, 0)),
                  pl.BlockSpec((tm, D), lambda i: (i, COL_GA // D)),
                  pl.BlockSpec((tm, D), lambda i: (i, COL_GB // D)),
                  pl.BlockSpec((tm, D), lambda i: (i, 0)),
                  pl.BlockSpec((None, 1, D), lambda i: ((i * tm) // L, 0, 0)),
                  pl.BlockSpec((W, D), const),
                  pl.BlockSpec((W, D), const),
                  pl.BlockSpec((D, D), const),
                  pl.BlockSpec((1, D), const),
                  pl.BlockSpec((1, D), const)],
        out_specs=pl.BlockSpec((tm, D), lambda i: (i, 0)),
        out_shape=jax.ShapeDtypeStruct((N, D), F32),
        compiler_params=_cparams(("parallel",)),
        name="merge_out",
    )(oa, ob, proj, proj, x2, gate.reshape(B, 1, D), wa, wb, wo, ln_g.reshape(1, D), ln_b.reshape(1, D))


def _layer(x, c, w_ada, b_ada, w_in, rel_bias, cmp_pos_k, cmp_pos_v, w_cmp_k1, w_cmp_k2, w_cmp_v1, w_cmp_v2,
           ssm_a_re, ssm_a_im, ssm_log_dt, ssm_b_re, ssm_b_im, ssm_c_re, ssm_c_im, ssm_d, w_glu, b_glu,
           w_branch_nsa, w_branch_ssm, w_out, ln_g, ln_b):
    B, L, D = x.shape
    N = B * L
    G, dh = NSA_GROUPS, NSA_HEAD_DIM
    x2 = x.reshape(N, D)

    mod = _ada_mod(c, w_ada, b_ada)
    shift, scale, gate = mod[:, :D], mod[:, D:2 * D], mod[:, 2 * D:]

    o_q, o_kv, o_ng, o_za, o_us, o_zb, o_ga, o_gb = np.cumsum(
        [0, NSA_WIDTH, 6 * KV_WIDTH, 3 * NSA_HEADS, NSA_WIDTH, SSM_WIDTH, SSM_WIDTH, D_MODEL]).tolist()
    wt = w_in.T
    w_re = jnp.concatenate([
        wt[o_q:o_kv], wt[o_za:o_us], wt[o_us:o_zb], wt[o_zb:o_ga], wt[o_ga:o_gb],
        wt[o_gb:], wt[o_kv + 2 * KV_WIDTH:o_ng], wt[o_kv:o_kv + 2 * KV_WIDTH], wt[o_ng:o_za],
        jnp.zeros((PROJ_COLS - COL_NG - 3 * NSA_HEADS, D), w_in.dtype)], axis=0).astype(BF16)
    proj = _in_proj(x2, scale, shift, w_re, L)

    w1s = jnp.stack([w_cmp_k1, w_cmp_v1])
    w2s = jnp.stack([w_cmp_k2, w_cmp_v2])
    poss = jnp.broadcast_to(jnp.stack([cmp_pos_k, cmp_pos_v]).reshape(2, 1, CMP_BLOCK * dh), (2, 8, CMP_BLOCK * dh))
    kcv = _nsa_compress(proj, w1s, w2s, poss, B, L)
    kc = kcv[0].astype(BF16)
    vct = kcv[1].swapaxes(-1, -2).astype(BF16)
    ks, vst, kw, vwt = _kv_prep(proj, B, L)
    tc, tw, ts, ovt = _nsa_tables(rel_bias, L)
    o_a = _nsa_attend(proj, kc, vct, ks, vst, kw, vwt, tc, tw, ts, ovt, B, L)

    s5_tabs = _s5_tables(ssm_a_re, ssm_a_im, ssm_log_dt, ssm_b_re, ssm_b_im, ssm_c_re, ssm_c_im, L // SSM_CHUNK)
    y = _s5_scan(proj, *s5_tabs, B, L)
    o_b = _s5_glu(y, proj, ssm_d, w_glu.astype(BF16), b_glu)

    out = _merge_out(o_a, o_b, proj, x2, gate, w_branch_nsa.astype(BF16), w_branch_ssm.astype(BF16),
                     w_out.astype(BF16), ln_g, ln_b, L)
    return out.reshape(B, L, D)


def kernel(x, c, w_ada, b_ada, w_in, rel_bias, cmp_pos_k, cmp_pos_v, w_cmp_k1, w_cmp_k2, w_cmp_v1, w_cmp_v2,
           ssm_a_re, ssm_a_im, ssm_log_dt, ssm_b_re, ssm_b_im, ssm_c_re, ssm_c_im, ssm_d, w_glu, b_glu,
           w_branch_nsa, w_branch_ssm, w_out, ln_g, ln_b):
    for i in range(w_ada.shape[0]):
        x = _layer(x, c, w_ada[i], b_ada[i], w_in[i], rel_bias, cmp_pos_k[i], cmp_pos_v[i], w_cmp_k1[i],
                   w_cmp_k2[i], w_cmp_v1[i], w_cmp_v2[i], ssm_a_re[i], ssm_a_im[i], ssm_log_dt[i], ssm_b_re[i],
                   ssm_b_im[i], ssm_c_re[i], ssm_c_im[i], ssm_d[i], w_glu[i], b_glu[i], w_branch_nsa[i],
                   w_branch_ssm[i], w_out[i], ln_g[i], ln_b[i])
    return x
```

```python
import functools
import math

import numpy as np
import jax
import jax.numpy as jnp
from jax import lax
from jax.experimental import pallas as pl
from jax.experimental.pallas import tpu as pltpu

F32 = jnp.float32
BF16 = jnp.bfloat16
HIGHEST = lax.Precision.HIGHEST
NT_DIMS = (((1,), (1,)), ((), ()))

D_MODEL = 2048
NSA_HEADS = 16
NSA_GROUPS = 4
NSA_HEAD_DIM = 64
NSA_REP = NSA_HEADS // NSA_GROUPS
NSA_WIDTH = NSA_HEADS * NSA_HEAD_DIM
KV_WIDTH = NSA_GROUPS * NSA_HEAD_DIM
CMP_BLOCK = 32
CMP_STRIDE = 16
CMP_HIDDEN = 128
SEL_BLOCK = 64
SEL_TOPK = 8
WINDOW = 512
Q_BLOCK = 128
SSM_WIDTH = 1024
SSM_GROUP = 16
SSM_GROUPS = SSM_WIDTH // SSM_GROUP
SSM_STATE = 64
REL_BUCKETS = 32
REL_MAX_DIST = 128
DEEPNORM_ALPHA = 2.0 ** 0.25
LN_EPS = 1e-5
MASK_VALUE = -1e30
FORCE_VALUE = 1e4
NEVER_VALUE = -3e38
SSM_CHUNK = 16

COL_Q, COL_ZA, COL_US, COL_ZB, COL_GA, COL_GB, COL_KV, COL_KC, COL_NG = 0, 1024, 2048, 3072, 4096, 6144, 8192, 9216, 9728
PROJ_COLS = 9856
VMEM_LIMIT = 56 * 1024 * 1024


def _cparams(sem):
    return pltpu.CompilerParams(dimension_semantics=sem, vmem_limit_bytes=VMEM_LIMIT)


def _ada_kernel(c_ref, w_ref, b_ref, o_ref):
    o_ref[...] = jnp.dot(c_ref[...], w_ref[...], preferred_element_type=F32, precision=HIGHEST) + b_ref[...]


def _ada_mod(c, w_ada, b_ada):
    B, D = c.shape
    n = w_ada.shape[1]
    tn = 1536
    return pl.pallas_call(
        _ada_kernel,
        grid=(n // tn,),
        in_specs=[pl.BlockSpec((B, D), lambda j: (0, 0)),
                  pl.BlockSpec((D, tn), lambda j: (0, j)),
                  pl.BlockSpec((1, tn), lambda j: (0, j))],
        out_specs=pl.BlockSpec((B, tn), lambda j: (0, j)),
        out_shape=jax.ShapeDtypeStruct((B, n), F32),
        compiler_params=_cparams(("parallel",)),
        name="ada_mod",
    )(c, w_ada, b_ada.reshape(1, n))


def _inproj_kernel(x_ref, scale_ref, shift_ref, w_ref, o_ref, h_ref):
    first = pl.program_id(1) == 0

    @pl.when(first)
    def _():
        for k in range(x_ref.shape[0] // LN_ROWS):
            rows = slice(k * LN_ROWS, (k + 1) * LN_ROWS)
            x = x_ref[rows, :]
            mu = jnp.mean(x, axis=-1, keepdims=True)
            xc = x - mu
            var = jnp.mean(xc * xc, axis=-1, keepdims=True)
            hn = xc * lax.rsqrt(var + LN_EPS)
            h = (hn * (1.0 + scale_ref[...]) + shift_ref[...]).astype(BF16)
            h_ref[rows, :] = h
            o_ref[rows, :] = lax.dot_general(h, w_ref[...], NT_DIMS, preferred_element_type=F32).astype(o_ref.dtype)

    @pl.when(jnp.logical_not(first))
    def _():
        o_ref[...] = lax.dot_general(h_ref[...], w_ref[...], NT_DIMS, preferred_element_type=F32).astype(o_ref.dtype)


LN_ROWS = 256


def _in_proj(x2, scale, shift, w, L):
    N, D = x2.shape
    ncol = w.shape[0]
    tm = min(1024, L)
    tn = 1408
    assert N % tm == 0 and L % tm == 0 and ncol % tn == 0
    B = scale.shape[0]
    return pl.pallas_call(
        _inproj_kernel,
        grid=(N // tm, ncol // tn),
        in_specs=[pl.BlockSpec((tm, D), lambda i, j: (i, 0)),
                  pl.BlockSpec((None, 1, D), lambda i, j: ((i * tm) // L, 0, 0)),
                  pl.BlockSpec((None, 1, D), lambda i, j: ((i * tm) // L, 0, 0)),
                  pl.BlockSpec((tn, D), lambda i, j: (j, 0))],
        out_specs=pl.BlockSpec((tm, tn), lambda i, j: (i, j)),
        out_shape=jax.ShapeDtypeStruct((N, ncol), BF16),
        scratch_shapes=[pltpu.VMEM((tm, D), BF16)],
        compiler_params=_cparams(("parallel", "arbitrary")),
        name="in_proj",
    )(x2, scale.reshape(B, 1, D), shift.reshape(B, 1, D), w)


def _cmp_kernel(x_ref, w1_ref, w1bd_ref, w2_ref, pos_ref, o_ref, xs_ref, *, ncp):
    S, dh = CMP_STRIDE, NSA_HEAD_DIM
    xs_ref[...] = x_ref[...].astype(F32)
    a = jnp.zeros((ncp, 2 * CMP_HIDDEN), F32)
    b = jnp.zeros((ncp, 2 * CMP_HIDDEN), F32)
    for j in range(S):
        xj = xs_ref[pl.ds(j, ncp, stride=S), :].astype(BF16)
        a = a + jnp.dot(xj, w1bd_ref[0, j], preferred_element_type=F32)
        b = b + jnp.dot(xj, w1bd_ref[1, j], preferred_element_type=F32)
    pw = jnp.dot(pos_ref[...], w1_ref[...], preferred_element_type=F32, precision=HIGHEST)[0:1]
    h = a + pltpu.roll(b, ncp - 1, 0) + jnp.concatenate([pw, pw], axis=1)
    w2 = w2_ref[...].astype(BF16)
    for g in range(2):
        hg = jax.nn.gelu(h[:, g * CMP_HIDDEN:(g + 1) * CMP_HIDDEN]).astype(BF16)
        o_ref[g] = jnp.dot(hg, w2, preferred_element_type=F32)


def _nsa_compress(proj, w1s, w2s, poss, B, L):
    G, dh, S = NSA_GROUPS, NSA_HEAD_DIM, CMP_STRIDE
    ncp = L // S
    w = w1s.reshape(2, 2, S, dh, CMP_HIDDEN).astype(BF16)
    z = jnp.zeros_like(w)
    w1bd = jnp.concatenate([jnp.concatenate([w, z], -1), jnp.concatenate([z, w], -1)], axis=-2)
    return pl.pallas_call(
        functools.partial(_cmp_kernel, ncp=ncp),
        grid=(2, B, G // 2),
        in_specs=[pl.BlockSpec((L, 2 * dh), lambda s, b, t: (b, COL_KC // (2 * dh) + 2 * s + t)),
                  pl.BlockSpec((None,) + w1s.shape[1:], lambda s, b, t: (s, 0, 0)),
                  pl.BlockSpec((None,) + w1bd.shape[1:], lambda s, b, t: (s, 0, 0, 0, 0)),
                  pl.BlockSpec((None,) + w2s.shape[1:], lambda s, b, t: (s, 0, 0)),
                  pl.BlockSpec((None,) + poss.shape[1:], lambda s, b, t: (s, 0, 0))],
        out_specs=pl.BlockSpec((None, None, 2, ncp, dh), lambda s, b, t: (s, b, t, 0, 0)),
        out_shape=jax.ShapeDtypeStruct((2, B, G, ncp, dh), F32),
        scratch_shapes=[pltpu.VMEM((L, 2 * dh), F32)],
        compiler_params=_cparams(("parallel", "parallel", "parallel")),
        name="nsa_compress",
    )(proj, w1s, w1bd, w2s, poss)


V_ROWS = NSA_HEAD_DIM + 16


def _kvprep_kernel(x_ref, ks_ref, vst_ref, kw_ref, vwt_ref, *, TL):
    G, dh = NSA_GROUPS, NSA_HEAD_DIM
    i = pl.program_id(1)
    tile = jnp.maximum(i - 1, 0)
    x = x_ref[...].astype(F32)
    lane = lax.broadcasted_iota(jnp.int32, (TL, 2 * dh), 1)
    row = lax.broadcasted_iota(jnp.int32, (TL, 2 * dh), 0) + tile * TL
    onehot = jnp.where(lane - dh == row // SEL_BLOCK, 1.0, 0.0)
    padrow = jnp.where(lane == dh, 1.0, 0.0)
    tail = jnp.where(lax.broadcasted_iota(jnp.int32, (V_ROWS - dh, TL), 0) == 0, 1.0, 0.0)
    is_pad = i == 0
    for t in range(G // 2):
        xk, xv, xwk, xwv = [x[:, (s * G + 2 * t) * dh:(s * G + 2 * t + 2) * dh] for s in range(4)]
        xvt, xwvt = xv.T, xwv.T
        for h in range(2):
            g = 2 * t + h
            kk = xk if h == 0 else pltpu.roll(xk, dh, 1)
            kwk = xwk if h == 0 else pltpu.roll(xwk, dh, 1)
            ks_ref[g] = jnp.where(lane < dh, kk, onehot).astype(BF16)
            kw_ref[g] = jnp.where(is_pad, padrow, jnp.where(lane < dh, kwk, 0.0)).astype(BF16)
            vst_ref[g] = jnp.concatenate([xvt[h * dh:(h + 1) * dh], tail], axis=0).astype(BF16)
            vw = jnp.concatenate([xwvt[h * dh:(h + 1) * dh], tail], axis=0)
            vwt_ref[g] = jnp.where(is_pad, 0.0, vw).astype(BF16)


def _kv_prep(proj, B, L):
    G, dh, TL = NSA_GROUPS, NSA_HEAD_DIM, WINDOW
    nt = L // TL
    wcols = 4 * G * dh
    assert L // SEL_BLOCK <= dh and COL_KV % wcols == 0 and L % TL == 0
    data = lambda b, i: jnp.maximum(i - 1, 0)
    k_shape = jax.ShapeDtypeStruct((B, G, L, 2 * dh), BF16)
    v_shape = jax.ShapeDtypeStruct((B, G, V_ROWS, L), BF16)
    kw_shape = jax.ShapeDtypeStruct((B, G, L + WINDOW, 2 * dh), BF16)
    vw_shape = jax.ShapeDtypeStruct((B, G, V_ROWS, L + WINDOW), BF16)
    return pl.pallas_call(
        functools.partial(_kvprep_kernel, TL=TL),
        grid=(B, nt + 1),
        in_specs=[pl.BlockSpec((TL, wcols), lambda b, i: (b * nt + data(b, i), COL_KV // wcols))],
        out_specs=[pl.BlockSpec((None, G, TL, 2 * dh), lambda b, i: (b, 0, data(b, i), 0)),
                   pl.BlockSpec((None, G, V_ROWS, TL), lambda b, i: (b, 0, 0, data(b, i))),
                   pl.BlockSpec((None, G, TL, 2 * dh), lambda b, i: (b, 0, i, 0)),
                   pl.BlockSpec((None, G, V_ROWS, TL), lambda b, i: (b, 0, 0, i))],
        out_shape=[k_shape, v_shape, kw_shape, vw_shape],
        compiler_params=_cparams(("parallel", "arbitrary")),
        name="kv_prep",
    )(proj)


SEL_TILE = 512
SEL_TABLE_FAR = 640
NSA_PROBLEMS = 4


def _tree(op, parts):
    while len(parts) > 1:
        parts = [op(parts[i], parts[i + 1]) if i + 1 < len(parts) else parts[i] for i in range(0, len(parts), 2)]
    return parts[0]


def _fold8(op, x):
    return _tree(op, [x[k:k + 8] for k in range(0, x.shape[0], 8)])


def _safe_inv(l):
    return jnp.where(l > 0.0, 1.0 / jnp.where(l > 0.0, l, 1.0), 0.0)


def _nsa_kernel(q_ref, za_ref, ng_ref, kc_ref, vct_ref, ks_ref, vst_ref, kw_ref, vwt_ref,
                tc_ref, tw_ref, ts_ref, ovt_ref, o_ref, s_ref, s2_ref, acc_ref, gt_ref, *, L):
    R, dh, QB = NSA_REP, NSA_HEAD_DIM, Q_BLOCK
    ncp = L // CMP_STRIDE
    nsel = L // SEL_BLOCK
    W = R * QB
    g = pl.program_id(1)
    qb = pl.program_id(2)

    NP = q_ref.shape[0]

    def front(p):
        qt = (q_ref[p].astype(F32) * (dh ** -0.5)).T
        qT = jnp.concatenate([qt[r * dh:(r + 1) * dh] for r in range(R)], axis=1).astype(BF16)

        c_off = pl.multiple_of((ncp - 8) - 8 * qb, 8)
        sc = jnp.dot(kc_ref[p], qT, preferred_element_type=F32) + tc_ref[pl.ds(c_off, ncp), :]
        m = jnp.maximum(jnp.max(_fold8(jnp.maximum, sc), axis=0, keepdims=True), 0.1 * MASK_VALUE)
        e = jnp.exp(sc - m)
        l = jnp.sum(_fold8(jnp.add, e), axis=0, keepdims=True)
        p_c = e * _safe_inv(l)
        oc = jnp.dot(vct_ref[p], p_c.astype(BF16), preferred_element_type=F32)

        psum = _tree(jnp.add, [p_c[:, r * QB:(r + 1) * QB] for r in range(R)])
        p_hi = psum.astype(BF16)
        p_lo = (psum - p_hi.astype(F32)).astype(BF16)
        ovt = ovt_ref[...]
        imp = jnp.dot(ovt, p_hi, preferred_element_type=F32) + jnp.dot(ovt, p_lo, preferred_element_type=F32)
        jj = lax.broadcasted_iota(jnp.int32, (nsel, QB), 0)
        ii = lax.broadcasted_iota(jnp.int32, (nsel, QB), 1)
        cur = 2 * qb + (ii >= SEL_BLOCK).astype(jnp.int32)
        forced = (jj == 0) | (jj == cur) | (jj == cur - 1)
        imp = jnp.where(forced, FORCE_VALUE, imp)
        imp = jnp.where(jj <= cur, imp, MASK_VALUE)
        jf = jj.astype(F32)
        sel = jnp.zeros((nsel, QB), F32)
        for _ in range(min(SEL_TOPK, nsel)):
            mx = jnp.max(_fold8(jnp.maximum, imp), axis=0, keepdims=True)
            idx = jnp.min(_fold8(jnp.minimum, jnp.where(imp == mx, jf, 1e9)), axis=0, keepdims=True)
            hit = jf == idx
            sel = jnp.where(hit & (mx > 0.1 * MASK_VALUE), 1.0, sel)
            imp = jnp.where(hit, -jnp.inf, imp)
        selbias = jnp.where(sel > 0.5, 0.0, MASK_VALUE).astype(BF16)
        sel_rows = [jnp.concatenate([selbias] * R, axis=1)]
        if nsel < dh:
            sel_rows.append(jnp.zeros((dh - nsel, W), BF16))
        q_sel = jnp.concatenate([qT] + sel_rows, axis=0)

        w0 = pl.multiple_of(qb * QB, QB)
        flag = jnp.where(lax.broadcasted_iota(jnp.int32, (dh, W), 0) == 0, MASK_VALUE, 0.0).astype(BF16)
        q_win = jnp.concatenate([qT, flag], axis=0)
        sw = jnp.dot(kw_ref[p, pl.ds(w0, WINDOW + QB), :], q_win, preferred_element_type=F32) + tw_ref[...]
        m_w = jnp.max(_fold8(jnp.maximum, sw), axis=0, keepdims=True)
        pw = jnp.exp((sw - m_w).astype(BF16))
        accw = jnp.dot(vwt_ref[p, :, pl.ds(w0, WINDOW + QB)], pw, preferred_element_type=F32)
        owin = accw[:dh] * _safe_inv(accw[dh:dh + 1])

        gt_ref[p] = jax.nn.sigmoid(ng_ref[p].astype(F32)).T

        def gate_row(branch):
            return jnp.concatenate([gt_ref[p, pl.ds(3 * (g * R + r) + branch, 1), :] for r in range(R)], axis=1)

        return q_sel, oc * gate_row(0) + owin * gate_row(2), gate_row(1)

    fronts = [front(p) for p in range(NP)]

    ntile = qb // (SEL_TILE // QB) + 1
    npair = ntile // 2
    odd = ntile % 2

    slots = (s_ref, s2_ref)

    def scores(t0, p, slot, nt=2):
        mx = None
        for h in range(nt):
            t = t0 + h
            k0 = pl.multiple_of(t * SEL_TILE, SEL_TILE)
            x0 = pl.multiple_of(jnp.maximum(t * SEL_TILE - qb * QB + SEL_TABLE_FAR, 0), QB)
            s = (jnp.dot(ks_ref[p, pl.ds(k0, SEL_TILE), :], fronts[p][0], preferred_element_type=F32)
                 + ts_ref[pl.ds(x0, SEL_TILE), :])
            slots[slot][p, h * SEL_TILE:(h + 1) * SEL_TILE, :] = s
            f = _fold8(jnp.maximum, s)
            mx = f if mx is None else jnp.maximum(mx, f)
        return jnp.max(mx, axis=0, keepdims=True)

    def accumulate(t0, p, slot, m_new, m_old, nt=2):
        part = None
        for h in range(nt):
            k0 = pl.multiple_of((t0 + h) * SEL_TILE, SEL_TILE)
            pr = jnp.exp((slots[slot][p, h * SEL_TILE:(h + 1) * SEL_TILE, :] - m_new).astype(BF16))
            pv = jnp.dot(vst_ref[p, :, pl.ds(k0, SEL_TILE)], pr, preferred_element_type=F32)
            part = pv if part is None else part + pv
        acc_ref[p] = acc_ref[p] * jnp.exp(m_old - m_new) + part

    def phase(t0, slot, m_cur, m_acc, nt_next=2):
        m_next = []
        for p in range(NP):
            accumulate(t0, p, slot, m_cur[p], m_acc[p])
            m_next.append(jnp.maximum(m_cur[p], scores(t0 + 2, p, 1 - slot, nt_next)))
        return tuple(m_next), m_cur

    def finish(t0, slot, m_cur, m_acc):
        @pl.when(odd == 1)
        def _():
            m2, m1 = phase(t0, slot, m_cur, m_acc, nt_next=1)
            for p in range(NP):
                accumulate(t0 + 2, p, 1 - slot, m2[p], m1[p], nt=1)

        @pl.when(odd == 0)
        def _():
            for p in range(NP):
                accumulate(t0, p, slot, m_cur[p], m_acc[p])

    acc_ref[...] = jnp.zeros_like(acc_ref)

    @pl.when(npair == 0)
    def _():
        for p in range(NP):
            m = scores(0, p, 0, nt=1)
            accumulate(0, p, 0, m, m, nt=1)

    @pl.when(npair > 0)
    def _():
        m_first = tuple(scores(0, p, 0) for p in range(NP))

        def trip(v, carry):
            carry = phase(4 * v, 0, *carry)
            return phase(4 * v + 2, 1, *carry)

        n_rest = npair - 1
        m_cur, m_acc = lax.fori_loop(0, n_rest // 2, trip, (m_first, m_first))
        t_last = 4 * (n_rest // 2)

        @pl.when(n_rest % 2 == 1)
        def _():
            finish(t_last + 2, 1, *phase(t_last, 0, m_cur, m_acc))

        @pl.when(n_rest % 2 == 0)
        def _():
            finish(t_last, 0, m_cur, m_acc)

    for p in range(NP):
        acc = acc_ref[p]
        osel = acc[:dh] * _safe_inv(acc[dh:dh + 1])
        ot = fronts[p][1] + osel * fronts[p][2]
        o = jnp.concatenate([ot[:, r * QB:(r + 1) * QB] for r in range(R)], axis=0).T
        za = za_ref[p].astype(F32)
        o_ref[p] = (o * (za * jax.nn.sigmoid(za))).astype(o_ref.dtype)


def _nsa_attend(proj, kc, vct, ks, vst, kw, vwt, tc, tw, ts, ovt, B, L):
    R, dh, G, QB = NSA_REP, NSA_HEAD_DIM, NSA_GROUPS, Q_BLOCK
    assert L % (2 * SEL_TILE) == 0
    nqb = L // QB
    gw = R * dh
    W = R * QB
    NP = NSA_PROBLEMS if B % NSA_PROBLEMS == 0 else 1
    proj4 = proj.reshape(B // NP, NP, L, proj.shape[-1])
    tok = lambda col: pl.BlockSpec((None, NP, QB, col[1]), lambda b, g, i: (b, 0, i, col[0] // col[1] + col[2] * g))
    bg = lambda a: pl.BlockSpec((NP, None) + a.shape[2:], lambda b, g, i: (b, g, 0, 0),
                                pipeline_mode=pl.Buffered(1))
    grp = lambda a: pl.BlockSpec((None,) + a.shape[1:], lambda b, g, i: (g, 0, 0), pipeline_mode=pl.Buffered(1))
    out = pl.pallas_call(
        functools.partial(_nsa_kernel, L=L),
        grid=(B // NP, G, nqb),
        in_specs=[tok((COL_Q, gw, 1)), tok((COL_ZA, gw, 1)), tok((COL_NG, 128, 0)),
                  bg(kc), bg(vct), bg(ks), bg(vst), bg(kw), bg(vwt), grp(tc), grp(tw), grp(ts),
                  pl.BlockSpec(ovt.shape, lambda b, g, i: (0, 0))],
        out_specs=pl.BlockSpec((None, NP, QB, gw), lambda b, g, i: (b, 0, i, g)),
        out_shape=jax.ShapeDtypeStruct((B // NP, NP, L, NSA_WIDTH), BF16),
        scratch_shapes=[pltpu.VMEM((NP, 2 * SEL_TILE, W), F32), pltpu.VMEM((NP, 2 * SEL_TILE, W), F32),
                        pltpu.VMEM((NP, V_ROWS, W), F32),
                        pltpu.VMEM((NP, 128, QB), F32)],
        compiler_params=_cparams(("parallel", "parallel", "arbitrary")),
        name="nsa_attend",
    )(proj4, proj4, proj4, kc, vct, ks, vst, kw, vwt, tc, tw, ts, ovt)
    return out.reshape(B * L, NSA_WIDTH)


def _t5_bucket(dist):
    n = jnp.maximum(dist, 0)
    max_exact = REL_BUCKETS // 2
    nf = jnp.maximum(n, max_exact).astype(F32)
    large = max_exact + (jnp.log(nf / max_exact) / math.log(REL_MAX_DIST / max_exact)
                         * (REL_BUCKETS - max_exact)).astype(jnp.int32)
    large = jnp.minimum(large, REL_BUCKETS - 1)
    return jnp.where(n < max_exact, n, large)


def _toeplitz(f, d0, base, step, n_rows, width):
    rpb = width // step
    nblk = n_rows // rpb
    assert rpb * step == width and nblk * rpb == n_rows
    lo = base - width * nblk - d0
    return f[..., lo:lo + width * (nblk + 1)].reshape(f.shape[:-1] + (nblk + 1, width))


TABLE_STEPS = (CMP_STRIDE, 1, 1)


def _tables_kernel(*refs, runs):
    R, QB = NSA_REP, Q_BLOCK
    n = len(TABLE_STEPS)
    tmp_ref = refs[2 * n]
    for win_ref, out_ref, step, table_runs in zip(refs[:n], refs[n:2 * n], TABLE_STEPS, runs):
        rpb = QB // step

        def block(a, carry, varies, win_ref=win_ref, out_ref=out_ref, step=step, rpb=rpb):
            r0 = pl.multiple_of(a * rpb, rpb)
            top = win_ref.shape[1] - 1 - a
            for r in range(R):
                if varies:
                    win = jnp.concatenate([win_ref[r, pl.ds(top, 1), :], win_ref[r, pl.ds(top - 1, 1), :]], axis=1)
                    x = jnp.broadcast_to(win, (QB, 2 * QB))
                    y = pltpu.roll(x, 0, 1, stride=1, stride_axis=0)[:, :QB]
                    if step > 1:
                        tmp_ref[...] = y
                        y = tmp_ref[pl.ds(0, rpb, stride=step), :]
                else:
                    y = jnp.broadcast_to(win_ref[r, pl.ds(top, 1), :], (rpb, QB))
                out_ref[pl.ds(r0, rpb), r * QB:(r + 1) * QB] = y
            return carry

        for first, end, varies in table_runs:
            lax.fori_loop(first, end, functools.partial(block, varies=varies), 0)


def _constant_runs(base, step, n_rows, upper):
    far = REL_MAX_DIST
    rpb = Q_BLOCK // step
    flags = []
    for a in range(n_rows // rpb):
        lo, hi = base - Q_BLOCK * a - Q_BLOCK + step, base - Q_BLOCK * a + Q_BLOCK - 1
        flags.append(not (hi < 0 or lo >= upper or (lo >= far and hi < upper)))
    runs, start = [], 0
    for a in range(1, len(flags) + 1):
        if a == len(flags) or flags[a] != flags[start]:
            runs.append((start, a, flags[start]))
            start = a
    return tuple(runs)


def _expand_tables(wins, runs):
    G, R, QB = NSA_GROUPS, NSA_REP, Q_BLOCK
    rows = [(w.shape[1] - 1) * (QB // s) for w, s in zip(wins, TABLE_STEPS)]
    return pl.pallas_call(
        functools.partial(_tables_kernel, runs=runs),
        grid=(G,),
        in_specs=[pl.BlockSpec((R,) + w.shape[1:], lambda g: (g, 0, 0)) for w in wins],
        out_specs=[pl.BlockSpec((None, n, R * QB), lambda g: (g, 0, 0)) for n in rows],
        out_shape=[jax.ShapeDtypeStruct((G, n, R * QB), F32) for n in rows],
        scratch_shapes=[pltpu.VMEM((QB, QB), F32)],
        compiler_params=_cparams(("parallel",)),
        name="nsa_tables",
    )(*wins)


def _nsa_tables(rel_bias, L):
    QB, G, R = Q_BLOCK, NSA_GROUPS, NSA_REP
    ncp = L // CMP_STRIDE
    nsel = L // SEL_BLOCK
    npos = L + QB
    tbl = rel_bias.astype(F32)
    bpos = (tbl[_t5_bucket(jnp.arange(npos, dtype=jnp.int32))] - tbl[REL_BUCKETS - 1][None, :]).T
    f = jnp.concatenate([jnp.full((NSA_HEADS, npos), MASK_VALUE, F32), bpos], axis=-1)
    d = np.arange(-npos, npos)
    fw = jnp.where(jnp.asarray(d < WINDOW), f, MASK_VALUE)
    spec_c = (L - QB - CMP_BLOCK + 1, CMP_STRIDE, 2 * ncp - 8)
    spec_w = (WINDOW, 1, WINDOW + QB)
    spec_s = (SEL_TABLE_FAR, 1, SEL_TABLE_FAR + SEL_TILE)
    assert TABLE_STEPS == (spec_c[1], spec_w[1], spec_s[1])
    wins = [_toeplitz(src, -npos, *spec, QB) for src, spec in ((f, spec_c), (fw, spec_w), (f, spec_s))]
    runs = tuple(_constant_runs(*spec, upper) for spec, upper in ((spec_c, 2 * npos), (spec_w, WINDOW), (spec_s, 2 * npos)))
    tc, tw, ts = _expand_tables(wins, runs)
    c = np.arange(ncp)[None, :]
    j = np.arange(nsel)[:, None]
    ovt = ((c - 4 * j >= -1) & (c - 4 * j <= 3)).astype(np.float32)
    return tc, tw, ts, jnp.asarray(ovt, BF16)


S5_LANES = 128
S5_GT = S5_LANES // SSM_GROUP
S5_SW = S5_GT * SSM_STATE
S5_COLS = 4 * S5_LANES


def _s5_kernel(us_ref, kt_ref, zt_ref, cl_ref, lam_ref, ex_ref, y_ref,
               xs_ref, u8_ref, m8_ref, ws8_ref, wo8_ref, *, nk):
    T, C, P, GT, LT, SW = SSM_CHUNK, SSM_GROUP, SSM_STATE, S5_GT, S5_LANES, S5_SW

    @pl.when(pl.program_id(1) == 0)
    def _build():
        tile = lambda x: jnp.concatenate([x] * GT, axis=0)
        r = lax.broadcasted_iota(jnp.int32, (LT, LT), 0)
        c = lax.broadcasted_iota(jnp.int32, (LT, LT), 1)
        same = (r // C) == (c // C)
        m8_ref[...] = jnp.zeros_like(m8_ref)
        for j in range(T):
            bd = jnp.where(same, tile(kt_ref[j]), 0.0).astype(BF16)
            for b in range(T - j):
                m8_ref[b * LT:(b + 1) * LT, (b + j) * LT:(b + j + 1) * LT] = bd
        r = lax.broadcasted_iota(jnp.int32, (LT, 2 * SW), 0)
        c = lax.broadcasted_iota(jnp.int32, (LT, 2 * SW), 1)
        same = (r // C) == ((c % SW) // P)
        for b in range(T):
            ws8_ref[b * LT:(b + 1) * LT, :] = jnp.where(same, tile(zt_ref[b]), 0.0).astype(BF16)
        r = lax.broadcasted_iota(jnp.int32, (2 * SW, LT), 0)
        c = lax.broadcasted_iota(jnp.int32, (2 * SW, LT), 1)
        same = ((r % SW) // P) == (c // C)
        cl = cl_ref[...]
        for a in range(T):
            blk = jnp.dot(cl, ex_ref[a], preferred_element_type=F32)
            wo8_ref[:, a * LT:(a + 1) * LT] = jnp.where(same, blk, 0.0).astype(BF16)

    xs_ref[...] = us_ref[...].astype(F32)
    for b in range(T):
        u8_ref[:, b * LT:(b + 1) * LT] = xs_ref[pl.ds(b, nk, stride=T), :].astype(BF16)
    u8 = u8_ref[...]
    s = jnp.dot(u8, ws8_ref[...], preferred_element_type=F32)
    hr, hi = s[:, :SW], s[:, SW:]
    kidx = lax.broadcasted_iota(jnp.int32, (nk, SW), 0)
    d, step = 1, 0
    while d < nk:
        lr = lam_ref[step:step + 1, :SW]
        li = lam_ref[step:step + 1, SW:]
        keep = kidx >= d
        sr = jnp.where(keep, pltpu.roll(hr, d, 0), 0.0)
        si = jnp.where(keep, pltpu.roll(hi, d, 0), 0.0)
        hr, hi = hr + lr * sr - li * si, hi + lr * si + li * sr
        d, step = 2 * d, step + 1
    keep = kidx >= 1
    pr = jnp.where(keep, pltpu.roll(hr, 1, 0), 0.0)
    pi = jnp.where(keep, pltpu.roll(hi, 1, 0), 0.0)
    hcat = jnp.concatenate([pr, pi], axis=-1).astype(BF16)
    for q in range(T * LT // S5_COLS):
        kq = (q + 1) * S5_COLS
        yq = (jnp.dot(hcat, wo8_ref[:, q * S5_COLS:kq], preferred_element_type=F32)
              + jnp.dot(u8[:, :kq], m8_ref[0:kq, q * S5_COLS:kq], preferred_element_type=F32))
        for a4 in range(S5_COLS // LT):
            a = q * (S5_COLS // LT) + a4
            y_ref[pl.ds(a, nk, stride=T), :] = yq[:, a4 * LT:(a4 + 1) * LT]


def _s5_scan(proj, kt, zt, clc, lamp, ex, B, L):
    T, LT, SW = SSM_CHUNK, S5_LANES, S5_SW
    nk = L // T
    nt = SSM_WIDTH // LT
    per_tile = lambda a: pl.BlockSpec((None,) + a.shape[1:], lambda t, b: (t,) + (0,) * (a.ndim - 1))
    return pl.pallas_call(
        functools.partial(_s5_kernel, nk=nk),
        grid=(nt, B),
        in_specs=[pl.BlockSpec((L, LT), lambda t, b: (b, COL_US // LT + t)),
                  per_tile(kt), per_tile(zt), per_tile(clc), per_tile(lamp),
                  pl.BlockSpec(ex.shape, lambda t, b: (0, 0, 0))],
        out_specs=pl.BlockSpec((L, LT), lambda t, b: (b, t)),
        out_shape=jax.ShapeDtypeStruct((B * L, SSM_WIDTH), F32),
        scratch_shapes=[pltpu.VMEM((L, LT), F32), pltpu.VMEM((nk, T * LT), BF16),
                        pltpu.VMEM((T * LT, T * LT), BF16), pltpu.VMEM((T * LT, 2 * SW), BF16),
                        pltpu.VMEM((2 * SW, T * LT), BF16)],
        compiler_params=_cparams(("parallel", "arbitrary")),
        name="s5_scan",
    )(proj, kt, zt, clc, lamp, ex)


def _cmul(ar, ai, br, bi):
    return ar * br - ai * bi, ar * bi + ai * br


def _s5_tables(a_re, a_im, log_dt, b_re, b_im, c_re, c_im, nk):
    T, P, C, Gs = SSM_CHUNK, SSM_STATE, SSM_GROUP, SSM_GROUPS
    dt = jnp.exp(log_dt.astype(F32))[:, None]
    ar, ai = a_re.astype(F32), a_im.astype(F32)
    mag = jnp.exp(ar * dt)
    lr, li = mag * jnp.cos(ai * dt), mag * jnp.sin(ai * dt)
    den = ar * ar + ai * ai
    nr, ni = lr - 1.0, li
    fr, fi = (nr * ar + ni * ai) / den, (ni * ar - nr * ai) / den
    br, bim = b_re.astype(F32), b_im.astype(F32)
    bbr = fr[..., None] * br - fi[..., None] * bim
    bbi = fr[..., None] * bim + fi[..., None] * br
    jv = jnp.arange(T + 1, dtype=F32)[:, None, None]
    pmag = jnp.exp(jv * (ar * dt)[None])
    pwr, pwi = pmag * jnp.cos(jv * (ai * dt)[None]), pmag * jnp.sin(jv * (ai * dt)[None])
    zr = pwr[..., None] * bbr[None] - pwi[..., None] * bbi[None]
    zi = pwr[..., None] * bbi[None] + pwi[..., None] * bbr[None]
    cr, ci = c_re.astype(F32), c_im.astype(F32)
    kj = (jnp.einsum('gcp,jgpd->gjcd', cr, zr[:T], precision=HIGHEST)
          - jnp.einsum('gcp,jgpd->gjcd', ci, zi[:T], precision=HIGHEST))
    GT, NT = S5_GT, Gs // S5_GT
    kt = kj.reshape(NT, GT, T, C, C).transpose(0, 2, 4, 1, 3).reshape(NT, T, C, GT * C)
    pow_b = lambda pw: pw[T - 1::-1].reshape(T, NT, GT, P).transpose(1, 0, 2, 3)[:, :, None]
    bbar_t = lambda bb: bb.reshape(NT, GT, P, C).transpose(0, 3, 1, 2)[:, None]
    pb_r, pb_i, bt_r, bt_i = pow_b(pwr), pow_b(pwi), bbar_t(bbr), bbar_t(bbi)
    zt = jnp.concatenate([(pb_r * bt_r - pb_i * bt_i).reshape(NT, T, C, GT * P),
                          (pb_r * bt_i + pb_i * bt_r).reshape(NT, T, C, GT * P)], axis=-1)
    pow_a = lambda pw: pw[1:].reshape(T, NT, GT, P).transpose(1, 2, 3, 0)[..., None]
    c_t = lambda cc: cc.reshape(NT, GT, C, P).transpose(0, 1, 3, 2)[:, :, :, None, :]
    pa_r, pa_i, ct_r, ct_i = pow_a(pwr), pow_a(pwi), c_t(cr), c_t(ci)
    clc = jnp.concatenate([(ct_r * pa_r - ct_i * pa_i).reshape(NT, GT * P, T * C),
                           (-(ct_r * pa_i + ct_i * pa_r)).reshape(NT, GT * P, T * C)], axis=1).astype(BF16)
    qr, qi = pwr[T], pwi[T]
    steps = []
    d = 1
    while d < nk:
        steps.append(jnp.concatenate([qr.reshape(NT, GT * P), qi.reshape(NT, GT * P)], -1))
        qr, qi = _cmul(qr, qi, qr, qi)
        d *= 2
    lamp = jnp.stack(steps, 1)
    col = np.arange(T * C)[:, None]
    lane = np.arange(GT * C)[None, :]
    ex = np.stack([(col // C == a) & (col % C == lane % C) for a in range(T)]).astype(np.float32)
    return kt, zt, clc, lamp, jnp.asarray(ex, BF16)


def _glu_kernel(y_ref, u_ref, zb_ref, d_ref, w_ref, b_ref, o_ref):
    y = y_ref[...] + d_ref[...] * u_ref[...].astype(F32)
    yg = jax.nn.gelu(y).astype(BF16)
    z = jnp.dot(yg, w_ref[...], preferred_element_type=F32) + b_ref[...]
    zb = zb_ref[...].astype(F32)
    o_ref[...] = (yg.astype(F32) * jax.nn.sigmoid(z) * (zb * jax.nn.sigmoid(zb))).astype(o_ref.dtype)


def _s5_glu(y, proj, d_skip, w_glu, b_glu):
    N, W = y.shape
    tm = min(1024, N)
    return pl.pallas_call(
        _glu_kernel,
        grid=(N // tm,),
        in_specs=[pl.BlockSpec((tm, W), lambda i: (i, 0)),
                  pl.BlockSpec((tm, W), lambda i: (i, COL_US // W)),
                  pl.BlockSpec((tm, W), lambda i: (i, COL_ZB // W)),
                  pl.BlockSpec((1, W), lambda i: (0, 0)),
                  pl.BlockSpec((W, W), lambda i: (0, 0)),
                  pl.BlockSpec((1, W), lambda i: (0, 0))],
        out_specs=pl.BlockSpec((tm, W), lambda i: (i, 0)),
        out_shape=jax.ShapeDtypeStruct((N, W), BF16),
        compiler_params=_cparams(("parallel",)),
        name="s5_glu",
    )(y, proj, proj, d_skip.reshape(1, W), w_glu, b_glu.reshape(1, W))


def _merge_kernel(oa_ref, ob_ref, ga_ref, gb_ref, x_ref, gate_ref, wa_ref, wb_ref, wo_ref, lg_ref, lb_ref, o_ref):
    pa = jnp.dot(oa_ref[...], wa_ref[...], preferred_element_type=F32)
    pb = jnp.dot(ob_ref[...], wb_ref[...], preferred_element_type=F32)
    m = jax.nn.sigmoid(ga_ref[...].astype(F32)) * pa + jax.nn.sigmoid(gb_ref[...].astype(F32)) * pb
    y = jnp.dot(m.astype(BF16), wo_ref[...], preferred_element_type=F32)
    r = DEEPNORM_ALPHA * x_ref[...] + gate_ref[...] * y
    mu = jnp.mean(r, axis=-1, keepdims=True)
    rc = r - mu
    var = jnp.mean(rc * rc, axis=-1, keepdims=True)
    o_ref[...] = rc * lax.rsqrt(var + LN_EPS) * lg_ref[...] + lb_ref[...]


def _merge_out(oa, ob, proj, x2, gate, wa, wb, wo, ln_g, ln_b, L):
    N, D = x2.shape
    W = oa.shape[1]
    B = gate.shape[0]
    tm = min(256, L)
    const = lambda i: (0, 0)
    return pl.pallas_call(
        _merge_kernel,
        grid=(N // tm,),
        in_specs=[pl.BlockSpec((tm, W), lambda i: (i, 0)),
                  pl.BlockSpec((tm, W), lambda i: (i, 0)),
                  pl.BlockSpec((tm, D), lambda i: (i, COL_GA // D)),
                  pl.BlockSpec((tm, D), lambda i: (i, COL_GB // D)),
                  pl.BlockSpec((tm, D), lambda i: (i, 0)),
                  pl.BlockSpec((None, 1, D), lambda i: ((i * tm) // L, 0, 0)),
                  pl.BlockSpec((W, D), const),
                  pl.BlockSpec((W, D), const),
                  pl.BlockSpec((D, D), const),
                  pl.BlockSpec((1, D), const),
                  pl.BlockSpec((1, D), const)],
        out_specs=pl.BlockSpec((tm, D), lambda i: (i, 0)),
        out_shape=jax.ShapeDtypeStruct((N, D), F32),
        compiler_params=_cparams(("parallel",)),
        name="merge_out",
    )(oa, ob, proj, proj, x2, gate.reshape(B, 1, D), wa, wb, wo, ln_g.reshape(1, D), ln_b.reshape(1, D))


def _layer(x, c, w_ada, b_ada, w_in, rel_bias, cmp_pos_k, cmp_pos_v, w_cmp_k1, w_cmp_k2, w_cmp_v1, w_cmp_v2,
           ssm_a_re, ssm_a_im, ssm_log_dt, ssm_b_re, ssm_b_im, ssm_c_re, ssm_c_im, ssm_d, w_glu, b_glu,
           w_branch_nsa, w_branch_ssm, w_out, ln_g, ln_b):
    B, L, D = x.shape
    N = B * L
    G, dh = NSA_GROUPS, NSA_HEAD_DIM
    x2 = x.reshape(N, D)

    mod = _ada_mod(c, w_ada, b_ada)
    shift, scale, gate = mod[:, :D], mod[:, D:2 * D], mod[:, 2 * D:]

    o_q, o_kv, o_ng, o_za, o_us, o_zb, o_ga, o_gb = np.cumsum(
        [0, NSA_WIDTH, 6 * KV_WIDTH, 3 * NSA_HEADS, NSA_WIDTH, SSM_WIDTH, SSM_WIDTH, D_MODEL]).tolist()
    wt = w_in.T
    w_re = jnp.concatenate([
        wt[o_q:o_kv], wt[o_za:o_us], wt[o_us:o_zb], wt[o_zb:o_ga], wt[o_ga:o_gb],
        wt[o_gb:], wt[o_kv + 2 * KV_WIDTH:o_ng], wt[o_kv:o_kv + 2 * KV_WIDTH], wt[o_ng:o_za],
        jnp.zeros((PROJ_COLS - COL_NG - 3 * NSA_HEADS, D), w_in.dtype)], axis=0).astype(BF16)
    proj = _in_proj(x2, scale, shift, w_re, L)

    w1s = jnp.stack([w_cmp_k1, w_cmp_v1])
    w2s = jnp.stack([w_cmp_k2, w_cmp_v2])
    poss = jnp.broadcast_to(jnp.stack([cmp_pos_k, cmp_pos_v]).reshape(2, 1, CMP_BLOCK * dh), (2, 8, CMP_BLOCK * dh))
    kcv = _nsa_compress(proj, w1s, w2s, poss, B, L)
    kc = kcv[0].astype(BF16)
    vct = kcv[1].swapaxes(-1, -2).astype(BF16)
    ks, vst, kw, vwt = _kv_prep(proj, B, L)
    tc, tw, ts, ovt = _nsa_tables(rel_bias, L)
    o_a = _nsa_attend(proj, kc, vct, ks, vst, kw, vwt, tc, tw, ts, ovt, B, L)

    s5_tabs = _s5_tables(ssm_a_re, ssm_a_im, ssm_log_dt, ssm_b_re, ssm_b_im, ssm_c_re, ssm_c_im, L // SSM_CHUNK)
    y = _s5_scan(proj, *s5_tabs, B, L)
    o_b = _s5_glu(y, proj, ssm_d, w_glu.astype(BF16), b_glu)

    out = _merge_out(o_a, o_b, proj, x2, gate, w_branch_nsa.astype(BF16), w_branch_ssm.astype(BF16),
                     w_out.astype(BF16), ln_g, ln_b, L)
    return out.reshape(B, L, D)


def kernel(x, c, w_ada, b_ada, w_in, rel_bias, cmp_pos_k, cmp_pos_v, w_cmp_k1, w_cmp_k2, w_cmp_v1, w_cmp_v2,
           ssm_a_re, ssm_a_im, ssm_log_dt, ssm_b_re, ssm_b_im, ssm_c_re, ssm_c_im, ssm_d, w_glu, b_glu,
           w_branch_nsa, w_branch_ssm, w_out, ln_g, ln_b):
    for i in range(w_ada.shape[0]):
        x = _layer(x, c, w_ada[i], b_ada[i], w_in[i], rel_bias, cmp_pos_k[i], cmp_pos_v[i], w_cmp_k1[i],
                   w_cmp_k2[i], w_cmp_v1[i], w_cmp_v2[i], ssm_a_re[i], ssm_a_im[i], ssm_log_dt[i], ssm_b_re[i],
                   ssm_b_im[i], ssm_c_re[i], ssm_c_im[i], ssm_d[i], w_glu[i], b_glu[i], w_branch_nsa[i],
                   w_branch_ssm[i], w_out[i], ln_g[i], ln_b[i])
    return x
```

```python
import functools
import math

import numpy as np
import jax
import jax.numpy as jnp
from jax import lax
from jax.experimental import pallas as pl
from jax.experimental.pallas import tpu as pltpu

F32 = jnp.float32
BF16 = jnp.bfloat16
HIGHEST = lax.Precision.HIGHEST
NT_DIMS = (((1,), (1,)), ((), ()))

D_MODEL = 2048
NSA_HEADS = 16
NSA_GROUPS = 4
NSA_HEAD_DIM = 64
NSA_REP = NSA_HEADS // NSA_GROUPS
NSA_WIDTH = NSA_HEADS * NSA_HEAD_DIM
KV_WIDTH = NSA_GROUPS * NSA_HEAD_DIM
CMP_BLOCK = 32
CMP_STRIDE = 16
CMP_HIDDEN = 128
SEL_BLOCK = 64
SEL_TOPK = 8
WINDOW = 512
Q_BLOCK = 128
SSM_WIDTH = 1024
SSM_GROUP = 16
SSM_GROUPS = SSM_WIDTH // SSM_GROUP
SSM_STATE = 64
REL_BUCKETS = 32
REL_MAX_DIST = 128
DEEPNORM_ALPHA = 2.0 ** 0.25
LN_EPS = 1e-5
MASK_VALUE = -1e30
FORCE_VALUE = 1e4
NEVER_VALUE = -3e38
SSM_CHUNK = 16

COL_Q, COL_ZA, COL_US, COL_ZB, COL_GA, COL_GB, COL_KV, COL_KC, COL_NG = 0, 1024, 2048, 3072, 4096, 6144, 8192, 9216, 9728
PROJ_COLS = 9856
VMEM_LIMIT = 56 * 1024 * 1024


def _cparams(sem):
    return pltpu.CompilerParams(dimension_semantics=sem, vmem_limit_bytes=VMEM_LIMIT)


def _ada_kernel(c_ref, w_ref, b_ref, o_ref):
    o_ref[...] = jnp.dot(c_ref[...], w_ref[...], preferred_element_type=F32, precision=HIGHEST) + b_ref[...]


def _ada_mod(c, w_ada, b_ada):
    B, D = c.shape
    n = w_ada.shape[1]
    tn = 1536
    return pl.pallas_call(
        _ada_kernel,
        grid=(n // tn,),
        in_specs=[pl.BlockSpec((B, D), lambda j: (0, 0)),
                  pl.BlockSpec((D, tn), lambda j: (0, j)),
                  pl.BlockSpec((1, tn), lambda j: (0, j))],
        out_specs=pl.BlockSpec((B, tn), lambda j: (0, j)),
        out_shape=jax.ShapeDtypeStruct((B, n), F32),
        compiler_params=_cparams(("parallel",)),
        name="ada_mod",
    )(c, w_ada, b_ada.reshape(1, n))


def _inproj_kernel(x_ref, scale_ref, shift_ref, w_ref, o_ref, h_ref):
    first = pl.program_id(1) == 0

    @pl.when(first)
    def _():
        for k in range(x_ref.shape[0] // LN_ROWS):
            rows = slice(k * LN_ROWS, (k + 1) * LN_ROWS)
            x = x_ref[rows, :]
            mu = jnp.mean(x, axis=-1, keepdims=True)
            xc = x - mu
            var = jnp.mean(xc * xc, axis=-1, keepdims=True)
            hn = xc * lax.rsqrt(var + LN_EPS)
            h = (hn * (1.0 + scale_ref[...]) + shift_ref[...]).astype(BF16)
            h_ref[rows, :] = h
            o_ref[rows, :] = lax.dot_general(h, w_ref[...], NT_DIMS, preferred_element_type=F32).astype(o_ref.dtype)

    @pl.when(jnp.logical_not(first))
    def _():
        o_ref[...] = lax.dot_general(h_ref[...], w_ref[...], NT_DIMS, preferred_element_type=F32).astype(o_ref.dtype)


LN_ROWS = 256


def _in_proj(x2, scale, shift, w, L):
    N, D = x2.shape
    ncol = w.shape[0]
    tm = min(1024, L)
    tn = 1408
    assert N % tm == 0 and L % tm == 0 and ncol % tn == 0
    B = scale.shape[0]
    return pl.pallas_call(
        _inproj_kernel,
        grid=(N // tm, ncol // tn),
        in_specs=[pl.BlockSpec((tm, D), lambda i, j: (i, 0)),
                  pl.BlockSpec((None, 1, D), lambda i, j: ((i * tm) // L, 0, 0)),
                  pl.BlockSpec((None, 1, D), lambda i, j: ((i * tm) // L, 0, 0)),
                  pl.BlockSpec((tn, D), lambda i, j: (j, 0))],
        out_specs=pl.BlockSpec((tm, tn), lambda i, j: (i, j)),
        out_shape=jax.ShapeDtypeStruct((N, ncol), BF16),
        scratch_shapes=[pltpu.VMEM((tm, D), BF16)],
        compiler_params=_cparams(("parallel", "arbitrary")),
        name="in_proj",
    )(x2, scale.reshape(B, 1, D), shift.reshape(B, 1, D), w)


def _cmp_kernel(x_ref, w1_ref, w1bd_ref, w2_ref, pos_ref, o_ref, xs_ref, *, ncp):
    S, dh = CMP_STRIDE, NSA_HEAD_DIM
    xs_ref[...] = x_ref[...].astype(F32)
    a = jnp.zeros((ncp, 2 * CMP_HIDDEN), F32)
    b = jnp.zeros((ncp, 2 * CMP_HIDDEN), F32)
    for j in range(S):
        xj = xs_ref[pl.ds(j, ncp, stride=S), :].astype(BF16)
        a = a + jnp.dot(xj, w1bd_ref[0, j], preferred_element_type=F32)
        b = b + jnp.dot(xj, w1bd_ref[1, j], preferred_element_type=F32)
    pw = jnp.dot(pos_ref[...], w1_ref[...], preferred_element_type=F32, precision=HIGHEST)[0:1]
    h = a + pltpu.roll(b, ncp - 1, 0) + jnp.concatenate([pw, pw], axis=1)
    w2 = w2_ref[...].astype(BF16)
    for g in range(2):
        hg = jax.nn.gelu(h[:, g * CMP_HIDDEN:(g + 1) * CMP_HIDDEN]).astype(BF16)
        o_ref[g] = jnp.dot(hg, w2, preferred_element_type=F32)


def _nsa_compress(proj, w1s, w2s, poss, B, L):
    G, dh, S = NSA_GROUPS, NSA_HEAD_DIM, CMP_STRIDE
    ncp = L // S
    w = w1s.reshape(2, 2, S, dh, CMP_HIDDEN).astype(BF16)
    z = jnp.zeros_like(w)
    w1bd = jnp.concatenate([jnp.concatenate([w, z], -1), jnp.concatenate([z, w], -1)], axis=-2)
    return pl.pallas_call(
        functools.partial(_cmp_kernel, ncp=ncp),
        grid=(2, B, G // 2),
        in_specs=[pl.BlockSpec((L, 2 * dh), lambda s, b, t: (b, COL_KC // (2 * dh) + 2 * s + t)),
                  pl.BlockSpec((None,) + w1s.shape[1:], lambda s, b, t: (s, 0, 0)),
                  pl.BlockSpec((None,) + w1bd.shape[1:], lambda s, b, t: (s, 0, 0, 0, 0)),
                  pl.BlockSpec((None,) + w2s.shape[1:], lambda s, b, t: (s, 0, 0)),
                  pl.BlockSpec((None,) + poss.shape[1:], lambda s, b, t: (s, 0, 0))],
        out_specs=pl.BlockSpec((None, None, 2, ncp, dh), lambda s, b, t: (s, b, t, 0, 0)),
        out_shape=jax.ShapeDtypeStruct((2, B, G, ncp, dh), F32),
        scratch_shapes=[pltpu.VMEM((L, 2 * dh), F32)],
        compiler_params=_cparams(("parallel", "parallel", "parallel")),
        name="nsa_compress",
    )(proj, w1s, w1bd, w2s, poss)


V_ROWS = NSA_HEAD_DIM + 16


def _kvprep_kernel(x_ref, ks_ref, vst_ref, kw_ref, vwt_ref, *, TL):
    G, dh = NSA_GROUPS, NSA_HEAD_DIM
    i = pl.program_id(1)
    tile = jnp.maximum(i - 1, 0)
    x = x_ref[...].astype(F32)
    lane = lax.broadcasted_iota(jnp.int32, (TL, 2 * dh), 1)
    row = lax.broadcasted_iota(jnp.int32, (TL, 2 * dh), 0) + tile * TL
    onehot = jnp.where(lane - dh == row // SEL_BLOCK, 1.0, 0.0)
    padrow = jnp.where(lane == dh, 1.0, 0.0)
    tail = jnp.where(lax.broadcasted_iota(jnp.int32, (V_ROWS - dh, TL), 0) == 0, 1.0, 0.0)
    is_pad = i == 0
    for t in range(G // 2):
        xk, xv, xwk, xwv = [x[:, (s * G + 2 * t) * dh:(s * G + 2 * t + 2) * dh] for s in range(4)]
        xvt, xwvt = xv.T, xwv.T
        for h in range(2):
            g = 2 * t + h
            kk = xk if h == 0 else pltpu.roll(xk, dh, 1)
            kwk = xwk if h == 0 else pltpu.roll(xwk, dh, 1)
            ks_ref[g] = jnp.where(lane < dh, kk, onehot).astype(BF16)
            kw_ref[g] = jnp.where(is_pad, padrow, jnp.where(lane < dh, kwk, 0.0)).astype(BF16)
            vst_ref[g] = jnp.concatenate([xvt[h * dh:(h + 1) * dh], tail], axis=0).astype(BF16)
            vw = jnp.concatenate([xwvt[h * dh:(h + 1) * dh], tail], axis=0)
            vwt_ref[g] = jnp.where(is_pad, 0.0, vw).astype(BF16)


def _kv_prep(proj, B, L):
    G, dh, TL = NSA_GROUPS, NSA_HEAD_DIM, WINDOW
    nt = L // TL
    wcols = 4 * G * dh
    assert L // SEL_BLOCK <= dh and COL_KV % wcols == 0 and L % TL == 0
    data = lambda b, i: jnp.maximum(i - 1, 0)
    k_shape = jax.ShapeDtypeStruct((B, G, L, 2 * dh), BF16)
    v_shape = jax.ShapeDtypeStruct((B, G, V_ROWS, L), BF16)
    kw_shape = jax.ShapeDtypeStruct((B, G, L + WINDOW, 2 * dh), BF16)
    vw_shape = jax.ShapeDtypeStruct((B, G, V_ROWS, L + WINDOW), BF16)
    return pl.pallas_call(
        functools.partial(_kvprep_kernel, TL=TL),
        grid=(B, nt + 1),
        in_specs=[pl.BlockSpec((TL, wcols), lambda b, i: (b * nt + data(b, i), COL_KV // wcols))],
        out_specs=[pl.BlockSpec((None, G, TL, 2 * dh), lambda b, i: (b, 0, data(b, i), 0)),
                   pl.BlockSpec((None, G, V_ROWS, TL), lambda b, i: (b, 0, 0, data(b, i))),
                   pl.BlockSpec((None, G, TL, 2 * dh), lambda b, i: (b, 0, i, 0)),
                   pl.BlockSpec((None, G, V_ROWS, TL), lambda b, i: (b, 0, 0, i))],
        out_shape=[k_shape, v_shape, kw_shape, vw_shape],
        compiler_params=_cparams(("parallel", "arbitrary")),
        name="kv_prep",
    )(proj)


SEL_TILE = 512
SEL_TABLE_FAR = 640
NSA_PROBLEMS = 4


def _tree(op, parts):
    while len(parts) > 1:
        parts = [op(parts[i], parts[i + 1]) if i + 1 < len(parts) else parts[i] for i in range(0, len(parts), 2)]
    return parts[0]


def _fold8(op, x):
    return _tree(op, [x[k:k + 8] for k in range(0, x.shape[0], 8)])


def _safe_inv(l):
    return jnp.where(l > 0.0, 1.0 / jnp.where(l > 0.0, l, 1.0), 0.0)


def _nsa_kernel(q_ref, za_ref, ng_ref, kc_ref, vct_ref, ks_ref, vst_ref, kw_ref, vwt_ref,
                tc_ref, tw_ref, ts_ref, ovt_ref, o_ref, s_ref, s2_ref, acc_ref, gt_ref, *, L):
    R, dh, QB = NSA_REP, NSA_HEAD_DIM, Q_BLOCK
    ncp = L // CMP_STRIDE
    nsel = L // SEL_BLOCK
    W = R * QB
    g = pl.program_id(1)
    qb = pl.program_id(2)

    NP = q_ref.shape[0]

    def front(p):
        qt = (q_ref[p].astype(F32) * (dh ** -0.5)).T
        qT = jnp.concatenate([qt[r * dh:(r + 1) * dh] for r in range(R)], axis=1).astype(BF16)

        c_off = pl.multiple_of((ncp - 8) - 8 * qb, 8)
        sc = jnp.dot(kc_ref[p], qT, preferred_element_type=F32) + tc_ref[pl.ds(c_off, ncp), :]
        m = jnp.maximum(jnp.max(_fold8(jnp.maximum, sc), axis=0, keepdims=True), 0.1 * MASK_VALUE)
        e = jnp.exp(sc - m)
        l = jnp.sum(_fold8(jnp.add, e), axis=0, keepdims=True)
        p_c = e * _safe_inv(l)
        oc = jnp.dot(vct_ref[p], p_c.astype(BF16), preferred_element_type=F32)

        psum = _tree(jnp.add, [p_c[:, r * QB:(r + 1) * QB] for r in range(R)])
        p_hi = psum.astype(BF16)
        p_lo = (psum - p_hi.astype(F32)).astype(BF16)
        ovt = ovt_ref[...]
        imp = jnp.dot(ovt, p_hi, preferred_element_type=F32) + jnp.dot(ovt, p_lo, preferred_element_type=F32)
        jj = lax.broadcasted_iota(jnp.int32, (nsel, QB), 0)
        ii = lax.broadcasted_iota(jnp.int32, (nsel, QB), 1)
        cur = 2 * qb + (ii >= SEL_BLOCK).astype(jnp.int32)
        forced = (jj == 0) | (jj == cur) | (jj == cur - 1)
        imp = jnp.where(forced, FORCE_VALUE, imp)
        imp = jnp.where(jj <= cur, imp, MASK_VALUE)
        jf = jj.astype(F32)
        sel = jnp.zeros((nsel, QB), F32)
        for _ in range(min(SEL_TOPK, nsel)):
            mx = jnp.max(_fold8(jnp.maximum, imp), axis=0, keepdims=True)
            idx = jnp.min(_fold8(jnp.minimum, jnp.where(imp == mx, jf, 1e9)), axis=0, keepdims=True)
            hit = jf == idx
            sel = jnp.where(hit & (mx > 0.1 * MASK_VALUE), 1.0, sel)
            imp = jnp.where(hit, -jnp.inf, imp)
        selbias = jnp.where(sel > 0.5, 0.0, MASK_VALUE).astype(BF16)
        sel_rows = [jnp.concatenate([selbias] * R, axis=1)]
        if nsel < dh:
            sel_rows.append(jnp.zeros((dh - nsel, W), BF16))
        q_sel = jnp.concatenate([qT] + sel_rows, axis=0)

        w0 = pl.multiple_of(qb * QB, QB)
        flag = jnp.where(lax.broadcasted_iota(jnp.int32, (dh, W), 0) == 0, MASK_VALUE, 0.0).astype(BF16)
        q_win = jnp.concatenate([qT, flag], axis=0)
        sw = jnp.dot(kw_ref[p, pl.ds(w0, WINDOW + QB), :], q_win, preferred_element_type=F32) + tw_ref[...]
        m_w = jnp.max(_fold8(jnp.maximum, sw), axis=0, keepdims=True)
        pw = jnp.exp((sw - m_w).astype(BF16))
        accw = jnp.dot(vwt_ref[p, :, pl.ds(w0, WINDOW + QB)], pw, preferred_element_type=F32)
        owin = accw[:dh] * _safe_inv(accw[dh:dh + 1])

        gt_ref[p] = jax.nn.sigmoid(ng_ref[p].astype(F32)).T

        def gate_row(branch):
            return jnp.concatenate([gt_ref[p, pl.ds(3 * (g * R + r) + branch, 1), :] for r in range(R)], axis=1)

        return q_sel, oc * gate_row(0) + owin * gate_row(2), gate_row(1)

    fronts = [front(p) for p in range(NP)]

    ntile = qb // (SEL_TILE // QB) + 1
    npair = ntile // 2
    odd = ntile % 2

    slots = (s_ref, s2_ref)

    def scores(t0, p, slot, nt=2):
        mx = None
        for h in range(nt):
            t = t0 + h
            k0 = pl.multiple_of(t * SEL_TILE, SEL_TILE)
            x0 = pl.multiple_of(jnp.maximum(t * SEL_TILE - qb * QB + SEL_TABLE_FAR, 0), QB)
            s = (jnp.dot(ks_ref[p, pl.ds(k0, SEL_TILE), :], fronts[p][0], preferred_element_type=F32)
                 + ts_ref[pl.ds(x0, SEL_TILE), :])
            slots[slot][p, h * SEL_TILE:(h + 1) * SEL_TILE, :] = s
            f = _fold8(jnp.maximum, s)
            mx = f if mx is None else jnp.maximum(mx, f)
        return jnp.max(mx, axis=0, keepdims=True)

    def accumulate(t0, p, slot, m_new, m_old, nt=2):
        part = None
        for h in range(nt):
            k0 = pl.multiple_of((t0 + h) * SEL_TILE, SEL_TILE)
            pr = jnp.exp((slots[slot][p, h * SEL_TILE:(h + 1) * SEL_TILE, :] - m_new).astype(BF16))
            pv = jnp.dot(vst_ref[p, :, pl.ds(k0, SEL_TILE)], pr, preferred_element_type=F32)
            part = pv if part is None else part + pv
        acc_ref[p] = acc_ref[p] * jnp.exp(m_old - m_new) + part

    def phase(t0, slot, m_cur, m_acc, nt_next=2):
        m_next = []
        for p in range(NP):
            accumulate(t0, p, slot, m_cur[p], m_acc[p])
            m_next.append(jnp.maximum(m_cur[p], scores(t0 + 2, p, 1 - slot, nt_next)))
        return tuple(m_next), m_cur

    def finish(t0, slot, m_cur, m_acc):
        @pl.when(odd == 1)
        def _():
            m2, m1 = phase(t0, slot, m_cur, m_acc, nt_next=1)
            for p in range(NP):
                accumulate(t0 + 2, p, 1 - slot, m2[p], m1[p], nt=1)

        @pl.when(odd == 0)
        def _():
            for p in range(NP):
                accumulate(t0, p, slot, m_cur[p], m_acc[p])

    acc_ref[...] = jnp.zeros_like(acc_ref)

    @pl.when(npair == 0)
    def _():
        for p in range(NP):
            m = scores(0, p, 0, nt=1)
            accumulate(0, p, 0, m, m, nt=1)

    @pl.when(npair > 0)
    def _():
        m_first = tuple(scores(0, p, 0) for p in range(NP))

        def trip(v, carry):
            carry = phase(4 * v, 0, *carry)
            return phase(4 * v + 2, 1, *carry)

        n_rest = npair - 1
        m_cur, m_acc = lax.fori_loop(0, n_rest // 2, trip, (m_first, m_first))
        t_last = 4 * (n_rest // 2)

        @pl.when(n_rest % 2 == 1)
        def _():
            finish(t_last + 2, 1, *phase(t_last, 0, m_cur, m_acc))

        @pl.when(n_rest % 2 == 0)
        def _():
            finish(t_last, 0, m_cur, m_acc)

    for p in range(NP):
        acc = acc_ref[p]
        osel = acc[:dh] * _safe_inv(acc[dh:dh + 1])
        ot = fronts[p][1] + osel * fronts[p][2]
        o = jnp.concatenate([ot[:, r * QB:(r + 1) * QB] for r in range(R)], axis=0).T
        za = za_ref[p].astype(F32)
        o_ref[p] = (o * (za * jax.nn.sigmoid(za))).astype(o_ref.dtype)


def _nsa_attend(proj, kc, vct, ks, vst, kw, vwt, tc, tw, ts, ovt, B, L):
    R, dh, G, QB = NSA_REP, NSA_HEAD_DIM, NSA_GROUPS, Q_BLOCK
    assert L % (2 * SEL_TILE) == 0
    nqb = L // QB
    gw = R * dh
    W = R * QB
    NP = NSA_PROBLEMS if B % NSA_PROBLEMS == 0 else 1
    proj4 = proj.reshape(B // NP, NP, L, proj.shape[-1])
    tok = lambda col: pl.BlockSpec((None, NP, QB, col[1]), lambda b, g, i: (b, 0, i, col[0] // col[1] + col[2] * g))
    bg = lambda a: pl.BlockSpec((NP, None) + a.shape[2:], lambda b, g, i: (b, g, 0, 0),
                                pipeline_mode=pl.Buffered(1))
    grp = lambda a: pl.BlockSpec((None,) + a.shape[1:], lambda b, g, i: (g, 0, 0), pipeline_mode=pl.Buffered(1))
    out = pl.pallas_call(
        functools.partial(_nsa_kernel, L=L),
        grid=(B // NP, G, nqb),
        in_specs=[tok((COL_Q, gw, 1)), tok((COL_ZA, gw, 1)), tok((COL_NG, 128, 0)),
                  bg(kc), bg(vct), bg(ks), bg(vst), bg(kw), bg(vwt), grp(tc), grp(tw), grp(ts),
                  pl.BlockSpec(ovt.shape, lambda b, g, i: (0, 0))],
        out_specs=pl.BlockSpec((None, NP, QB, gw), lambda b, g, i: (b, 0, i, g)),
        out_shape=jax.ShapeDtypeStruct((B // NP, NP, L, NSA_WIDTH), BF16),
        scratch_shapes=[pltpu.VMEM((NP, 2 * SEL_TILE, W), F32), pltpu.VMEM((NP, 2 * SEL_TILE, W), F32),
                        pltpu.VMEM((NP, V_ROWS, W), F32),
                        pltpu.VMEM((NP, 128, QB), F32)],
        compiler_params=_cparams(("parallel", "parallel", "arbitrary")),
        name="nsa_attend",
    )(proj4, proj4, proj4, kc, vct, ks, vst, kw, vwt, tc, tw, ts, ovt)
    return out.reshape(B * L, NSA_WIDTH)


def _t5_bucket(dist):
    n = jnp.maximum(dist, 0)
    max_exact = REL_BUCKETS // 2
    nf = jnp.maximum(n, max_exact).astype(F32)
    large = max_exact + (jnp.log(nf / max_exact) / math.log(REL_MAX_DIST / max_exact)
                         * (REL_BUCKETS - max_exact)).astype(jnp.int32)
    large = jnp.minimum(large, REL_BUCKETS - 1)
    return jnp.where(n < max_exact, n, large)


def _toeplitz(f, d0, base, step, n_rows, width):
    rpb = width // step
    nblk = n_rows // rpb
    assert rpb * step == width and nblk * rpb == n_rows
    lo = base - width * nblk - d0
    return f[..., lo:lo + width * (nblk + 1)].reshape(f.shape[:-1] + (nblk + 1, width))


TABLE_STEPS = (CMP_STRIDE, 1, 1)


def _tables_kernel(*refs, runs):
    R, QB = NSA_REP, Q_BLOCK
    n = len(TABLE_STEPS)
    tmp_ref = refs[2 * n]
    for win_ref, out_ref, step, table_runs in zip(refs[:n], refs[n:2 * n], TABLE_STEPS, runs):
        rpb = QB // step

        def block(a, carry, varies, win_ref=win_ref, out_ref=out_ref, step=step, rpb=rpb):
            r0 = pl.multiple_of(a * rpb, rpb)
            top = win_ref.shape[1] - 1 - a
            for r in range(R):
                if varies:
                    win = jnp.concatenate([win_ref[r, pl.ds(top, 1), :], win_ref[r, pl.ds(top - 1, 1), :]], axis=1)
                    x = jnp.broadcast_to(win, (QB, 2 * QB))
                    y = pltpu.roll(x, 0, 1, stride=1, stride_axis=0)[:, :QB]
                    if step > 1:
                        tmp_ref[...] = y
                        y = tmp_ref[pl.ds(0, rpb, stride=step), :]
                else:
                    y = jnp.broadcast_to(win_ref[r, pl.ds(top, 1), :], (rpb, QB))
                out_ref[pl.ds(r0, rpb), r * QB:(r + 1) * QB] = y
            return carry

        for first, end, varies in table_runs:
            lax.fori_loop(first, end, functools.partial(block, varies=varies), 0)


def _constant_runs(base, step, n_rows, upper):
    far = REL_MAX_DIST
    rpb = Q_BLOCK // step
    flags = []
    for a in range(n_rows // rpb):
        lo, hi = base - Q_BLOCK * a - Q_BLOCK + step, base - Q_BLOCK * a + Q_BLOCK - 1
        flags.append(not (hi < 0 or lo >= upper or (lo >= far and hi < upper)))
    runs, start = [], 0
    for a in range(1, len(flags) + 1):
        if a == len(flags) or flags[a] != flags[start]:
            runs.append((start, a, flags[start]))
            start = a
    return tuple(runs)


def _expand_tables(wins, runs):
    G, R, QB = NSA_GROUPS, NSA_REP, Q_BLOCK
    rows = [(w.shape[1] - 1) * (QB // s) for w, s in zip(wins, TABLE_STEPS)]
    return pl.pallas_call(
        functools.partial(_tables_kernel, runs=runs),
        grid=(G,),
        in_specs=[pl.BlockSpec((R,) + w.shape[1:], lambda g: (g, 0, 0)) for w in wins],
        out_specs=[pl.BlockSpec((None, n, R * QB), lambda g: (g, 0, 0)) for n in rows],
        out_shape=[jax.ShapeDtypeStruct((G, n, R * QB), F32) for n in rows],
        scratch_shapes=[pltpu.VMEM((QB, QB), F32)],
        compiler_params=_cparams(("parallel",)),
        name="nsa_tables",
    )(*wins)


def _nsa_tables(rel_bias, L):
    QB, G, R = Q_BLOCK, NSA_GROUPS, NSA_REP
    ncp = L // CMP_STRIDE
    nsel = L // SEL_BLOCK
    npos = L + QB
    tbl = rel_bias.astype(F32)
    bpos = (tbl[_t5_bucket(jnp.arange(npos, dtype=jnp.int32))] - tbl[REL_BUCKETS - 1][None, :]).T
    f = jnp.concatenate([jnp.full((NSA_HEADS, npos), MASK_VALUE, F32), bpos], axis=-1)
    d = np.arange(-npos, npos)
    fw = jnp.where(jnp.asarray(d < WINDOW), f, MASK_VALUE)
    spec_c = (L - QB - CMP_BLOCK + 1, CMP_STRIDE, 2 * ncp - 8)
    spec_w = (WINDOW, 1, WINDOW + QB)
    spec_s = (SEL_TABLE_FAR, 1, SEL_TABLE_FAR + SEL_TILE)
    assert TABLE_STEPS == (spec_c[1], spec_w[1], spec_s[1])
    wins = [_toeplitz(src, -npos, *spec, QB) for src, spec in ((f, spec_c), (fw, spec_w), (f, spec_s))]
    runs = tuple(_constant_runs(*spec, upper) for spec, upper in ((spec_c, 2 * npos), (spec_w, WINDOW), (spec_s, 2 * npos)))
    tc, tw, ts = _expand_tables(wins, runs)
    c = np.arange(ncp)[None, :]
    j = np.arange(nsel)[:, None]
    ovt = ((c - 4 * j >= -1) & (c - 4 * j <= 3)).astype(np.float32)
    return tc, tw, ts, jnp.asarray(ovt, BF16)


S5_LANES = 128
S5_GT = S5_LANES // SSM_GROUP
S5_SW = S5_GT * SSM_STATE
S5_COLS = 4 * S5_LANES


def _s5_kernel(us_ref, kt_ref, zt_ref, cl_ref, lam_ref, ex_ref, y_ref,
               xs_ref, u8_ref, m8_ref, ws8_ref, wo8_ref, *, nk):
    T, C, P, GT, LT, SW = SSM_CHUNK, SSM_GROUP, SSM_STATE, S5_GT, S5_LANES, S5_SW

    @pl.when(pl.program_id(1) == 0)
    def _build():
        tile = lambda x: jnp.concatenate([x] * GT, axis=0)
        r = lax.broadcasted_iota(jnp.int32, (LT, LT), 0)
        c = lax.broadcasted_iota(jnp.int32, (LT, LT), 1)
        same = (r // C) == (c // C)
        m8_ref[...] = jnp.zeros_like(m8_ref)
        for j in range(T):
            bd = jnp.where(same, tile(kt_ref[j]), 0.0).astype(BF16)
            for b in range(T - j):
                m8_ref[b * LT:(b + 1) * LT, (b + j) * LT:(b + j + 1) * LT] = bd
        r = lax.broadcasted_iota(jnp.int32, (LT, 2 * SW), 0)
        c = lax.broadcasted_iota(jnp.int32, (LT, 2 * SW), 1)
        same = (r // C) == ((c % SW) // P)
        for b in range(T):
            ws8_ref[b * LT:(b + 1) * LT, :] = jnp.where(same, tile(zt_ref[b]), 0.0).astype(BF16)
        r = lax.broadcasted_iota(jnp.int32, (2 * SW, LT), 0)
        c = lax.broadcasted_iota(jnp.int32, (2 * SW, LT), 1)
        same = ((r % SW) // P) == (c // C)
        cl = cl_ref[...]
        for a in range(T):
            blk = jnp.dot(cl, ex_ref[a], preferred_element_type=F32)
            wo8_ref[:, a * LT:(a + 1) * LT] = jnp.where(same, blk, 0.0).astype(BF16)

    xs_ref[...] = us_ref[...].astype(F32)
    for b in range(T):
        u8_ref[:, b * LT:(b + 1) * LT] = xs_ref[pl.ds(b, nk, stride=T), :].astype(BF16)
    u8 = u8_ref[...]
    s = jnp.dot(u8, ws8_ref[...], preferred_element_type=F32)
    hr, hi = s[:, :SW], s[:, SW:]
    kidx = lax.broadcasted_iota(jnp.int32, (nk, SW), 0)
    d, step = 1, 0
    while d < nk:
        lr = lam_ref[step:step + 1, :SW]
        li = lam_ref[step:step + 1, SW:]
        keep = kidx >= d
        sr = jnp.where(keep, pltpu.roll(hr, d, 0), 0.0)
        si = jnp.where(keep, pltpu.roll(hi, d, 0), 0.0)
        hr, hi = hr + lr * sr - li * si, hi + lr * si + li * sr
        d, step = 2 * d, step + 1
    keep = kidx >= 1
    pr = jnp.where(keep, pltpu.roll(hr, 1, 0), 0.0)
    pi = jnp.where(keep, pltpu.roll(hi, 1, 0), 0.0)
    hcat = jnp.concatenate([pr, pi], axis=-1).astype(BF16)
    for q in range(T * LT // S5_COLS):
        kq = (q + 1) * S5_COLS
        yq = (jnp.dot(hcat, wo8_ref[:, q * S5_COLS:kq], preferred_element_type=F32)
              + jnp.dot(u8[:, :kq], m8_ref[0:kq, q * S5_COLS:kq], preferred_element_type=F32))
        for a4 in range(S5_COLS // LT):
            a = q * (S5_COLS // LT) + a4
            y_ref[pl.ds(a, nk, stride=T), :] = yq[:, a4 * LT:(a4 + 1) * LT]


def _s5_scan(proj, kt, zt, clc, lamp, ex, B, L):
    T, LT, SW = SSM_CHUNK, S5_LANES, S5_SW
    nk = L // T
    nt = SSM_WIDTH // LT
    per_tile = lambda a: pl.BlockSpec((None,) + a.shape[1:], lambda t, b: (t,) + (0,) * (a.ndim - 1))
    return pl.pallas_call(
        functools.partial(_s5_kernel, nk=nk),
        grid=(nt, B),
        in_specs=[pl.BlockSpec((L, LT), lambda t, b: (b, COL_US // LT + t)),
                  per_tile(kt), per_tile(zt), per_tile(clc), per_tile(lamp),
                  pl.BlockSpec(ex.shape, lambda t, b: (0, 0, 0))],
        out_specs=pl.BlockSpec((L, LT), lambda t, b: (b, t)),
        out_shape=jax.ShapeDtypeStruct((B * L, SSM_WIDTH), F32),
        scratch_shapes=[pltpu.VMEM((L, LT), F32), pltpu.VMEM((nk, T * LT), BF16),
                        pltpu.VMEM((T * LT, T * LT), BF16), pltpu.VMEM((T * LT, 2 * SW), BF16),
                        pltpu.VMEM((2 * SW, T * LT), BF16)],
        compiler_params=_cparams(("parallel", "arbitrary")),
        name="s5_scan",
    )(proj, kt, zt, clc, lamp, ex)


def _cmul(ar, ai, br, bi):
    return ar * br - ai * bi, ar * bi + ai * br


def _s5_tables(a_re, a_im, log_dt, b_re, b_im, c_re, c_im, nk):
    T, P, C, Gs = SSM_CHUNK, SSM_STATE, SSM_GROUP, SSM_GROUPS
    dt = jnp.exp(log_dt.astype(F32))[:, None]
    ar, ai = a_re.astype(F32), a_im.astype(F32)
    mag = jnp.exp(ar * dt)
    lr, li = mag * jnp.cos(ai * dt), mag * jnp.sin(ai * dt)
    den = ar * ar + ai * ai
    nr, ni = lr - 1.0, li
    fr, fi = (nr * ar + ni * ai) / den, (ni * ar - nr * ai) / den
    br, bim = b_re.astype(F32), b_im.astype(F32)
    bbr = fr[..., None] * br - fi[..., None] * bim
    bbi = fr[..., None] * bim + fi[..., None] * br
    jv = jnp.arange(T + 1, dtype=F32)[:, None, None]
    pmag = jnp.exp(jv * (ar * dt)[None])
    pwr, pwi = pmag * jnp.cos(jv * (ai * dt)[None]), pmag * jnp.sin(jv * (ai * dt)[None])
    zr = pwr[..., None] * bbr[None] - pwi[..., None] * bbi[None]
    zi = pwr[..., None] * bbi[None] + pwi[..., None] * bbr[None]
    cr, ci = c_re.astype(F32), c_im.astype(F32)
    kj = (jnp.einsum('gcp,jgpd->gjcd', cr, zr[:T], precision=HIGHEST)
          - jnp.einsum('gcp,jgpd->gjcd', ci, zi[:T], precision=HIGHEST))
    GT, NT = S5_GT, Gs // S5_GT
    kt = kj.reshape(NT, GT, T, C, C).transpose(0, 2, 4, 1, 3).reshape(NT, T, C, GT * C)
    pow_b = lambda pw: pw[T - 1::-1].reshape(T, NT, GT, P).transpose(1, 0, 2, 3)[:, :, None]
    bbar_t = lambda bb: bb.reshape(NT, GT, P, C).transpose(0, 3, 1, 2)[:, None]
    pb_r, pb_i, bt_r, bt_i = pow_b(pwr), pow_b(pwi), bbar_t(bbr), bbar_t(bbi)
    zt = jnp.concatenate([(pb_r * bt_r - pb_i * bt_i).reshape(NT, T, C, GT * P),
                          (pb_r * bt_i + pb_i * bt_r).reshape(NT, T, C, GT * P)], axis=-1)
    pow_a = lambda pw: pw[1:].reshape(T, NT, GT, P).transpose(1, 2, 3, 0)[..., None]
    c_t = lambda cc: cc.reshape(NT, GT, C, P).transpose(0, 1, 3, 2)[:, :, :, None, :]
    pa_r, pa_i, ct_r, ct_i = pow_a(pwr), pow_a(pwi), c_t(cr), c_t(ci)
    clc = jnp.concatenate([(ct_r * pa_r - ct_i * pa_i).reshape(NT, GT * P, T * C),
                           (-(ct_r * pa_i + ct_i * pa_r)).reshape(NT, GT * P, T * C)], axis=1).astype(BF16)
    qr, qi = pwr[T], pwi[T]
    steps = []
    d = 1
    while d < nk:
        steps.append(jnp.concatenate([qr.reshape(NT, GT * P), qi.reshape(NT, GT * P)], -1))
        qr, qi = _cmul(qr, qi, qr, qi)
        d *= 2
    lamp = jnp.stack(steps, 1)
    col = np.arange(T * C)[:, None]
    lane = np.arange(GT * C)[None, :]
    ex = np.stack([(col // C == a) & (col % C == lane % C) for a in range(T)]).astype(np.float32)
    return kt, zt, clc, lamp, jnp.asarray(ex, BF16)


def _merge_kernel(oa_ref, ys_ref, u_ref, zb_ref, d_ref, wg_ref, bg_ref, ga_ref, gb_ref, x_ref, gate_ref,
                  wa_ref, wb_ref, wo_ref, lg_ref, lb_ref, o_ref):
    ys = ys_ref[...] + d_ref[...] * u_ref[...].astype(F32)
    yg = jax.nn.gelu(ys).astype(BF16)
    z = jnp.dot(yg, wg_ref[...], preferred_element_type=F32) + bg_ref[...]
    zb = zb_ref[...].astype(F32)
    ob = (yg.astype(F32) * jax.nn.sigmoid(z) * (zb * jax.nn.sigmoid(zb))).astype(BF16)
    pa = jnp.dot(oa_ref[...], wa_ref[...], preferred_element_type=F32)
    pb = jnp.dot(ob, wb_ref[...], preferred_element_type=F32)
    m = jax.nn.sigmoid(ga_ref[...].astype(F32)) * pa + jax.nn.sigmoid(gb_ref[...].astype(F32)) * pb
    y = jnp.dot(m.astype(BF16), wo_ref[...], preferred_element_type=F32)
    r = DEEPNORM_ALPHA * x_ref[...] + gate_ref[...] * y
    mu = jnp.mean(r, axis=-1, keepdims=True)
    rc = r - mu
    var = jnp.mean(rc * rc, axis=-1, keepdims=True)
    o_ref[...] = rc * lax.rsqrt(var + LN_EPS) * lg_ref[...] + lb_ref[...]


def _merge_out(oa, ys, d_skip, w_glu, b_glu, proj, x2, gate, wa, wb, wo, ln_g, ln_b, L):
    N, D = x2.shape
    W = oa.shape[1]
    B = gate.shape[0]
    tm = min(256, L)
    const = lambda shape: pl.BlockSpec(shape, lambda i: (0, 0), pipeline_mode=pl.Buffered(1))
    return pl.pallas_call(
        _merge_kernel,
        grid=(N // tm,),
        in_specs=[pl.BlockSpec((tm, W), lambda i: (i, 0)),
                  pl.BlockSpec((tm, W), lambda i: (i, 0)),
                  pl.BlockSpec((tm, W), lambda i: (i, COL_US // W)),
                  pl.BlockSpec((tm, W), lambda i: (i, COL_ZB // W)),
                  const((1, W)), const((W, W)), const((1, W)),
                  pl.BlockSpec((tm, D), lambda i: (i, COL_GA // D)),
                  pl.BlockSpec((tm, D), lambda i: (i, COL_GB // D)),
                  pl.BlockSpec((tm, D), lambda i: (i, 0)),
                  pl.BlockSpec((None, 1, D), lambda i: ((i * tm) // L, 0, 0)),
                  const((W, D)), const((W, D)), const((D, D)), const((1, D)), const((1, D))],
        out_specs=pl.BlockSpec((tm, D), lambda i: (i, 0)),
        out_shape=jax.ShapeDtypeStruct((N, D), F32),
        compiler_params=_cparams(("parallel",)),
        name="merge_out",
    )(oa, ys, proj, proj, d_skip.reshape(1, W), w_glu, b_glu.reshape(1, W), proj, proj, x2,
      gate.reshape(B, 1, D), wa, wb, wo, ln_g.reshape(1, D), ln_b.reshape(1, D))


def _layer(x, c, w_ada, b_ada, w_in, rel_bias, cmp_pos_k, cmp_pos_v, w_cmp_k1, w_cmp_k2, w_cmp_v1, w_cmp_v2,
           ssm_a_re, ssm_a_im, ssm_log_dt, ssm_b_re, ssm_b_im, ssm_c_re, ssm_c_im, ssm_d, w_glu, b_glu,
           w_branch_nsa, w_branch_ssm, w_out, ln_g, ln_b):
    B, L, D = x.shape
    N = B * L
    G, dh = NSA_GROUPS, NSA_HEAD_DIM
    x2 = x.reshape(N, D)

    mod = _ada_mod(c, w_ada, b_ada)
    shift, scale, gate = mod[:, :D], mod[:, D:2 * D], mod[:, 2 * D:]

    o_q, o_kv, o_ng, o_za, o_us, o_zb, o_ga, o_gb = np.cumsum(
        [0, NSA_WIDTH, 6 * KV_WIDTH, 3 * NSA_HEADS, NSA_WIDTH, SSM_WIDTH, SSM_WIDTH, D_MODEL]).tolist()
    wt = w_in.T
    w_re = jnp.concatenate([
        wt[o_q:o_kv], wt[o_za:o_us], wt[o_us:o_zb], wt[o_zb:o_ga], wt[o_ga:o_gb],
        wt[o_gb:], wt[o_kv + 2 * KV_WIDTH:o_ng], wt[o_kv:o_kv + 2 * KV_WIDTH], wt[o_ng:o_za],
        jnp.zeros((PROJ_COLS - COL_NG - 3 * NSA_HEADS, D), w_in.dtype)], axis=0).astype(BF16)
    proj = _in_proj(x2, scale, shift, w_re, L)

    w1s = jnp.stack([w_cmp_k1, w_cmp_v1])
    w2s = jnp.stack([w_cmp_k2, w_cmp_v2])
    poss = jnp.broadcast_to(jnp.stack([cmp_pos_k, cmp_pos_v]).reshape(2, 1, CMP_BLOCK * dh), (2, 8, CMP_BLOCK * dh))
    kcv = _nsa_compress(proj, w1s, w2s, poss, B, L)
    kc = kcv[0].astype(BF16)
    vct = kcv[1].swapaxes(-1, -2).astype(BF16)
    ks, vst, kw, vwt = _kv_prep(proj, B, L)
    tc, tw, ts, ovt = _nsa_tables(rel_bias, L)
    o_a = _nsa_attend(proj, kc, vct, ks, vst, kw, vwt, tc, tw, ts, ovt, B, L)

    s5_tabs = _s5_tables(ssm_a_re, ssm_a_im, ssm_log_dt, ssm_b_re, ssm_b_im, ssm_c_re, ssm_c_im, L // SSM_CHUNK)
    y = _s5_scan(proj, *s5_tabs, B, L)
    out = _merge_out(o_a, y, ssm_d, w_glu.astype(BF16), b_glu, proj, x2, gate, w_branch_nsa.astype(BF16),
                     w_branch_ssm.astype(BF16), w_out.astype(BF16), ln_g, ln_b, L)
    return out.reshape(B, L, D)


def kernel(x, c, w_ada, b_ada, w_in, rel_bias, cmp_pos_k, cmp_pos_v, w_cmp_k1, w_cmp_k2, w_cmp_v1, w_cmp_v2,
           ssm_a_re, ssm_a_im, ssm_log_dt, ssm_b_re, ssm_b_im, ssm_c_re, ssm_c_im, ssm_d, w_glu, b_glu,
           w_branch_nsa, w_branch_ssm, w_out, ln_g, ln_b):
    for i in range(w_ada.shape[0]):
        x = _layer(x, c, w_ada[i], b_ada[i], w_in[i], rel_bias, cmp_pos_k[i], cmp_pos_v[i], w_cmp_k1[i],
                   w_cmp_k2[i], w_cmp_v1[i], w_cmp_v2[i], ssm_a_re[i], ssm_a_im[i], ssm_log_dt[i], ssm_b_re[i],
                   ssm_b_im[i], ssm_c_re[i], ssm_c_im[i], ssm_d[i], w_glu[i], b_glu[i], w_branch_nsa[i],
                   w_branch_ssm[i], w_out[i], ln_g[i], ln_b[i])
    return x
```

```python
import functools
import math

import numpy as np
import jax
import jax.numpy as jnp
from jax import lax
from jax.experimental import pallas as pl
from jax.experimental.pallas import tpu as pltpu

F32 = jnp.float32
BF16 = jnp.bfloat16
HIGHEST = lax.Precision.HIGHEST
NT_DIMS = (((1,), (1,)), ((), ()))

D_MODEL = 2048
NSA_HEADS = 16
NSA_GROUPS = 4
NSA_HEAD_DIM = 64
NSA_REP = NSA_HEADS // NSA_GROUPS
NSA_WIDTH = NSA_HEADS * NSA_HEAD_DIM
KV_WIDTH = NSA_GROUPS * NSA_HEAD_DIM
CMP_BLOCK = 32
CMP_STRIDE = 16
CMP_HIDDEN = 128
SEL_BLOCK = 64
SEL_TOPK = 8
WINDOW = 512
Q_BLOCK = 128
SSM_WIDTH = 1024
SSM_GROUP = 16
SSM_GROUPS = SSM_WIDTH // SSM_GROUP
SSM_STATE = 64
REL_BUCKETS = 32
REL_MAX_DIST = 128
DEEPNORM_ALPHA = 2.0 ** 0.25
LN_EPS = 1e-5
MASK_VALUE = -1e30
FORCE_VALUE = 1e4
NEVER_VALUE = -3e38
SSM_CHUNK = 16

COL_Q, COL_ZA, COL_US, COL_ZB, COL_GA, COL_GB, COL_KV, COL_KC, COL_NG = 0, 1024, 2048, 3072, 4096, 6144, 8192, 9216, 9728
PROJ_COLS = 9856
VMEM_LIMIT = 56 * 1024 * 1024


def _cparams(sem):
    return pltpu.CompilerParams(dimension_semantics=sem, vmem_limit_bytes=VMEM_LIMIT)


def _ada_kernel(c_ref, w_ref, b_ref, o_ref):
    o_ref[...] = jnp.dot(c_ref[...], w_ref[...], preferred_element_type=F32, precision=HIGHEST) + b_ref[...]


def _ada_mod(c, w_ada, b_ada):
    B, D = c.shape
    n = w_ada.shape[1]
    tn = 1536
    return pl.pallas_call(
        _ada_kernel,
        grid=(n // tn,),
        in_specs=[pl.BlockSpec((B, D), lambda j: (0, 0)),
                  pl.BlockSpec((D, tn), lambda j: (0, j)),
                  pl.BlockSpec((1, tn), lambda j: (0, j))],
        out_specs=pl.BlockSpec((B, tn), lambda j: (0, j)),
        out_shape=jax.ShapeDtypeStruct((B, n), F32),
        compiler_params=_cparams(("parallel",)),
        name="ada_mod",
    )(c, w_ada, b_ada.reshape(1, n))


def _inproj_kernel(x_ref, scale_ref, shift_ref, w_ref, o_ref, h_ref):
    first = pl.program_id(1) == 0

    @pl.when(first)
    def _():
        for k in range(x_ref.shape[0] // LN_ROWS):
            rows = slice(k * LN_ROWS, (k + 1) * LN_ROWS)
            x = x_ref[rows, :]
            mu = jnp.mean(x, axis=-1, keepdims=True)
            xc = x - mu
            var = jnp.mean(xc * xc, axis=-1, keepdims=True)
            hn = xc * lax.rsqrt(var + LN_EPS)
            h = (hn * (1.0 + scale_ref[...]) + shift_ref[...]).astype(BF16)
            h_ref[rows, :] = h
            o_ref[rows, :] = lax.dot_general(h, w_ref[...], NT_DIMS, preferred_element_type=F32).astype(o_ref.dtype)

    @pl.when(jnp.logical_not(first))
    def _():
        o_ref[...] = lax.dot_general(h_ref[...], w_ref[...], NT_DIMS, preferred_element_type=F32).astype(o_ref.dtype)


LN_ROWS = 256


def _in_proj(x2, scale, shift, w, L):
    N, D = x2.shape
    ncol = w.shape[0]
    tm = min(1024, L)
    tn = 1408
    assert N % tm == 0 and L % tm == 0 and ncol % tn == 0
    B = scale.shape[0]
    return pl.pallas_call(
        _inproj_kernel,
        grid=(N // tm, ncol // tn),
        in_specs=[pl.BlockSpec((tm, D), lambda i, j: (i, 0)),
                  pl.BlockSpec((None, 1, D), lambda i, j: ((i * tm) // L, 0, 0)),
                  pl.BlockSpec((None, 1, D), lambda i, j: ((i * tm) // L, 0, 0)),
                  pl.BlockSpec((tn, D), lambda i, j: (j, 0))],
        out_specs=pl.BlockSpec((tm, tn), lambda i, j: (i, j)),
        out_shape=jax.ShapeDtypeStruct((N, ncol), BF16),
        scratch_shapes=[pltpu.VMEM((tm, D), BF16)],
        compiler_params=_cparams(("parallel", "arbitrary")),
        name="in_proj",
    )(x2, scale.reshape(B, 1, D), shift.reshape(B, 1, D), w)


def _cmp_kernel(x_ref, w1_ref, w1bd_ref, w2_ref, pos_ref, o_ref, xs_ref, *, ncp):
    S, dh = CMP_STRIDE, NSA_HEAD_DIM
    xs_ref[...] = x_ref[...].astype(F32)
    a = jnp.zeros((ncp, 2 * CMP_HIDDEN), F32)
    b = jnp.zeros((ncp, 2 * CMP_HIDDEN), F32)
    for j in range(S):
        xj = xs_ref[pl.ds(j, ncp, stride=S), :].astype(BF16)
        a = a + jnp.dot(xj, w1bd_ref[0, j], preferred_element_type=F32)
        b = b + jnp.dot(xj, w1bd_ref[1, j], preferred_element_type=F32)
    pw = jnp.dot(pos_ref[...], w1_ref[...], preferred_element_type=F32, precision=HIGHEST)[0:1]
    h = a + pltpu.roll(b, ncp - 1, 0) + jnp.concatenate([pw, pw], axis=1)
    w2 = w2_ref[...].astype(BF16)
    for g in range(2):
        hg = jax.nn.gelu(h[:, g * CMP_HIDDEN:(g + 1) * CMP_HIDDEN]).astype(BF16)
        o_ref[g] = jnp.dot(hg, w2, preferred_element_type=F32)


def _nsa_compress(proj, w1s, w2s, poss, B, L):
    G, dh, S = NSA_GROUPS, NSA_HEAD_DIM, CMP_STRIDE
    ncp = L // S
    w = w1s.reshape(2, 2, S, dh, CMP_HIDDEN).astype(BF16)
    z = jnp.zeros_like(w)
    w1bd = jnp.concatenate([jnp.concatenate([w, z], -1), jnp.concatenate([z, w], -1)], axis=-2)
    return pl.pallas_call(
        functools.partial(_cmp_kernel, ncp=ncp),
        grid=(2, B, G // 2),
        in_specs=[pl.BlockSpec((L, 2 * dh), lambda s, b, t: (b, COL_KC // (2 * dh) + 2 * s + t)),
                  pl.BlockSpec((None,) + w1s.shape[1:], lambda s, b, t: (s, 0, 0)),
                  pl.BlockSpec((None,) + w1bd.shape[1:], lambda s, b, t: (s, 0, 0, 0, 0)),
                  pl.BlockSpec((None,) + w2s.shape[1:], lambda s, b, t: (s, 0, 0)),
                  pl.BlockSpec((None,) + poss.shape[1:], lambda s, b, t: (s, 0, 0))],
        out_specs=pl.BlockSpec((None, None, 2, ncp, dh), lambda s, b, t: (s, b, t, 0, 0)),
        out_shape=jax.ShapeDtypeStruct((2, B, G, ncp, dh), F32),
        scratch_shapes=[pltpu.VMEM((L, 2 * dh), F32)],
        compiler_params=_cparams(("parallel", "parallel", "parallel")),
        name="nsa_compress",
    )(proj, w1s, w1bd, w2s, poss)


V_ROWS = NSA_HEAD_DIM + 16


def _kvprep_kernel(x_ref, ks_ref, vst_ref, kw_ref, vwt_ref, *, TL):
    G, dh = NSA_GROUPS, NSA_HEAD_DIM
    i = pl.program_id(1)
    tile = jnp.maximum(i - 1, 0)
    x = x_ref[...].astype(F32)
    lane = lax.broadcasted_iota(jnp.int32, (TL, 2 * dh), 1)
    row = lax.broadcasted_iota(jnp.int32, (TL, 2 * dh), 0) + tile * TL
    onehot = jnp.where(lane - dh == row // SEL_BLOCK, 1.0, 0.0)
    padrow = jnp.where(lane == dh, 1.0, 0.0)
    tail = jnp.where(lax.broadcasted_iota(jnp.int32, (V_ROWS - dh, TL), 0) == 0, 1.0, 0.0)
    is_pad = i == 0
    for t in range(G // 2):
        xk, xv, xwk, xwv = [x[:, (s * G + 2 * t) * dh:(s * G + 2 * t + 2) * dh] for s in range(4)]
        xvt, xwvt = xv.T, xwv.T
        for h in range(2):
            g = 2 * t + h
            kk = xk if h == 0 else pltpu.roll(xk, dh, 1)
            kwk = xwk if h == 0 else pltpu.roll(xwk, dh, 1)
            ks_ref[g] = jnp.where(lane < dh, kk, onehot).astype(BF16)
            kw_ref[g] = jnp.where(is_pad, padrow, jnp.where(lane < dh, kwk, 0.0)).astype(BF16)
            vst_ref[g] = jnp.concatenate([xvt[h * dh:(h + 1) * dh], tail], axis=0).astype(BF16)
            vw = jnp.concatenate([xwvt[h * dh:(h + 1) * dh], tail], axis=0)
            vwt_ref[g] = jnp.where(is_pad, 0.0, vw).astype(BF16)


def _kv_prep(proj, B, L):
    G, dh, TL = NSA_GROUPS, NSA_HEAD_DIM, WINDOW
    nt = L // TL
    wcols = 4 * G * dh
    assert L // SEL_BLOCK <= dh and COL_KV % wcols == 0 and L % TL == 0
    data = lambda b, i: jnp.maximum(i - 1, 0)
    k_shape = jax.ShapeDtypeStruct((B, G, L, 2 * dh), BF16)
    v_shape = jax.ShapeDtypeStruct((B, G, V_ROWS, L), BF16)
    kw_shape = jax.ShapeDtypeStruct((B, G, L + WINDOW, 2 * dh), BF16)
    vw_shape = jax.ShapeDtypeStruct((B, G, V_ROWS, L + WINDOW), BF16)
    return pl.pallas_call(
        functools.partial(_kvprep_kernel, TL=TL),
        grid=(B, nt + 1),
        in_specs=[pl.BlockSpec((TL, wcols), lambda b, i: (b * nt + data(b, i), COL_KV // wcols))],
        out_specs=[pl.BlockSpec((None, G, TL, 2 * dh), lambda b, i: (b, 0, data(b, i), 0)),
                   pl.BlockSpec((None, G, V_ROWS, TL), lambda b, i: (b, 0, 0, data(b, i))),
                   pl.BlockSpec((None, G, TL, 2 * dh), lambda b, i: (b, 0, i, 0)),
                   pl.BlockSpec((None, G, V_ROWS, TL), lambda b, i: (b, 0, 0, i))],
        out_shape=[k_shape, v_shape, kw_shape, vw_shape],
        compiler_params=_cparams(("parallel", "arbitrary")),
        name="kv_prep",
    )(proj)


SEL_TILE = 512
SEL_TABLE_FAR = 640
NSA_PROBLEMS = 4


def _tree(op, parts):
    while len(parts) > 1:
        parts = [op(parts[i], parts[i + 1]) if i + 1 < len(parts) else parts[i] for i in range(0, len(parts), 2)]
    return parts[0]


def _fold8(op, x):
    return _tree(op, [x[k:k + 8] for k in range(0, x.shape[0], 8)])


def _safe_inv(l):
    return jnp.where(l > 0.0, 1.0 / jnp.where(l > 0.0, l, 1.0), 0.0)


def _nsa_kernel(q_ref, ng_ref, kc_ref, vct_ref, ks_ref, vst_ref, kw_ref, vwt_ref,
                tc_ref, tw_ref, ts_ref, ovt_ref, o_ref, s_ref, s2_ref, acc_ref, gt_ref, *, L):
    R, dh, QB = NSA_REP, NSA_HEAD_DIM, Q_BLOCK
    ncp = L // CMP_STRIDE
    nsel = L // SEL_BLOCK
    W = R * QB
    g = pl.program_id(1)
    qb = pl.program_id(2)

    NP = q_ref.shape[0]

    def front(p):
        qt = (q_ref[p].astype(F32) * (dh ** -0.5)).T
        qT = jnp.concatenate([qt[r * dh:(r + 1) * dh] for r in range(R)], axis=1).astype(BF16)

        c_off = pl.multiple_of((ncp - 8) - 8 * qb, 8)
        sc = jnp.dot(kc_ref[p], qT, preferred_element_type=F32) + tc_ref[pl.ds(c_off, ncp), :]
        m = jnp.maximum(jnp.max(_fold8(jnp.maximum, sc), axis=0, keepdims=True), 0.1 * MASK_VALUE)
        e = jnp.exp(sc - m)
        l = jnp.sum(_fold8(jnp.add, e), axis=0, keepdims=True)
        p_c = e * _safe_inv(l)
        oc = jnp.dot(vct_ref[p], p_c.astype(BF16), preferred_element_type=F32)

        psum = _tree(jnp.add, [p_c[:, r * QB:(r + 1) * QB] for r in range(R)])
        p_hi = psum.astype(BF16)
        p_lo = (psum - p_hi.astype(F32)).astype(BF16)
        ovt = ovt_ref[...]
        imp = jnp.dot(ovt, p_hi, preferred_element_type=F32) + jnp.dot(ovt, p_lo, preferred_element_type=F32)
        jj = lax.broadcasted_iota(jnp.int32, (nsel, QB), 0)
        ii = lax.broadcasted_iota(jnp.int32, (nsel, QB), 1)
        cur = 2 * qb + (ii >= SEL_BLOCK).astype(jnp.int32)
        forced = (jj == 0) | (jj == cur) | (jj == cur - 1)
        imp = jnp.where(forced, FORCE_VALUE, imp)
        imp = jnp.where(jj <= cur, imp, MASK_VALUE)
        jf = jj.astype(F32)
        sel = jnp.zeros((nsel, QB), F32)
        for _ in range(min(SEL_TOPK, nsel)):
            mx = jnp.max(_fold8(jnp.maximum, imp), axis=0, keepdims=True)
            idx = jnp.min(_fold8(jnp.minimum, jnp.where(imp == mx, jf, 1e9)), axis=0, keepdims=True)
            hit = jf == idx
            sel = jnp.where(hit & (mx > 0.1 * MASK_VALUE), 1.0, sel)
            imp = jnp.where(hit, -jnp.inf, imp)
        selbias = jnp.where(sel > 0.5, 0.0, MASK_VALUE).astype(BF16)
        sel_rows = [jnp.concatenate([selbias] * R, axis=1)]
        if nsel < dh:
            sel_rows.append(jnp.zeros((dh - nsel, W), BF16))
        q_sel = jnp.concatenate([qT] + sel_rows, axis=0)

        w0 = pl.multiple_of(qb * QB, QB)
        flag = jnp.where(lax.broadcasted_iota(jnp.int32, (dh, W), 0) == 0, MASK_VALUE, 0.0).astype(BF16)
        q_win = jnp.concatenate([qT, flag], axis=0)
        sw = jnp.dot(kw_ref[p, pl.ds(w0, WINDOW + QB), :], q_win, preferred_element_type=F32) + tw_ref[...]
        m_w = jnp.max(_fold8(jnp.maximum, sw), axis=0, keepdims=True)
        pw = jnp.exp((sw - m_w).astype(BF16))
        accw = jnp.dot(vwt_ref[p, :, pl.ds(w0, WINDOW + QB)], pw, preferred_element_type=F32)
        owin = accw[:dh] * _safe_inv(accw[dh:dh + 1])

        gt_ref[p] = jax.nn.sigmoid(ng_ref[p].astype(F32)).T

        def gate_row(branch):
            return jnp.concatenate([gt_ref[p, pl.ds(3 * (g * R + r) + branch, 1), :] for r in range(R)], axis=1)

        return q_sel, oc * gate_row(0) + owin * gate_row(2), gate_row(1)

    fronts = [front(p) for p in range(NP)]

    ntile = qb // (SEL_TILE // QB) + 1
    npair = ntile // 2
    odd = ntile % 2

    slots = (s_ref, s2_ref)

    def scores(t0, p, slot, nt=2):
        mx = None
        for h in range(nt):
            t = t0 + h
            k0 = pl.multiple_of(t * SEL_TILE, SEL_TILE)
            x0 = pl.multiple_of(jnp.maximum(t * SEL_TILE - qb * QB + SEL_TABLE_FAR, 0), QB)
            s = (jnp.dot(ks_ref[p, pl.ds(k0, SEL_TILE), :], fronts[p][0], preferred_element_type=F32)
                 + ts_ref[pl.ds(x0, SEL_TILE), :])
            slots[slot][p, h * SEL_TILE:(h + 1) * SEL_TILE, :] = s
            f = _fold8(jnp.maximum, s)
            mx = f if mx is None else jnp.maximum(mx, f)
        return jnp.max(mx, axis=0, keepdims=True)

    def accumulate(t0, p, slot, m_new, m_old, nt=2):
        part = None
        for h in range(nt):
            k0 = pl.multiple_of((t0 + h) * SEL_TILE, SEL_TILE)
            pr = jnp.exp((slots[slot][p, h * SEL_TILE:(h + 1) * SEL_TILE, :] - m_new).astype(BF16))
            pv = jnp.dot(vst_ref[p, :, pl.ds(k0, SEL_TILE)], pr, preferred_element_type=F32)
            part = pv if part is None else part + pv
        acc_ref[p] = acc_ref[p] * jnp.exp(m_old - m_new) + part

    def phase(t0, slot, m_cur, m_acc, nt_next=2):
        m_next = []
        for p in range(NP):
            accumulate(t0, p, slot, m_cur[p], m_acc[p])
            m_next.append(jnp.maximum(m_cur[p], scores(t0 + 2, p, 1 - slot, nt_next)))
        return tuple(m_next), m_cur

    def finish(t0, slot, m_cur, m_acc):
        @pl.when(odd == 1)
        def _():
            m2, m1 = phase(t0, slot, m_cur, m_acc, nt_next=1)
            for p in range(NP):
                accumulate(t0 + 2, p, 1 - slot, m2[p], m1[p], nt=1)

        @pl.when(odd == 0)
        def _():
            for p in range(NP):
                accumulate(t0, p, slot, m_cur[p], m_acc[p])

    acc_ref[...] = jnp.zeros_like(acc_ref)

    @pl.when(npair == 0)
    def _():
        for p in range(NP):
            m = scores(0, p, 0, nt=1)
            accumulate(0, p, 0, m, m, nt=1)

    @pl.when(npair > 0)
    def _():
        m_first = tuple(scores(0, p, 0) for p in range(NP))

        def trip(v, carry):
            carry = phase(4 * v, 0, *carry)
            return phase(4 * v + 2, 1, *carry)

        n_rest = npair - 1
        m_cur, m_acc = lax.fori_loop(0, n_rest // 2, trip, (m_first, m_first))
        t_last = 4 * (n_rest // 2)

        @pl.when(n_rest % 2 == 1)
        def _():
            finish(t_last + 2, 1, *phase(t_last, 0, m_cur, m_acc))

        @pl.when(n_rest % 2 == 0)
        def _():
            finish(t_last, 0, m_cur, m_acc)

    for p in range(NP):
        acc = acc_ref[p]
        osel = acc[:dh] * _safe_inv(acc[dh:dh + 1])
        ot = fronts[p][1] + osel * fronts[p][2]
        o = jnp.concatenate([ot[:, r * QB:(r + 1) * QB] for r in range(R)], axis=0).T
        o_ref[p] = o.astype(o_ref.dtype)


def _nsa_attend(proj, kc, vct, ks, vst, kw, vwt, tc, tw, ts, ovt, B, L):
    R, dh, G, QB = NSA_REP, NSA_HEAD_DIM, NSA_GROUPS, Q_BLOCK
    assert L % (2 * SEL_TILE) == 0
    nqb = L // QB
    gw = R * dh
    W = R * QB
    NP = NSA_PROBLEMS if B % NSA_PROBLEMS == 0 else 1
    proj4 = proj.reshape(B // NP, NP, L, proj.shape[-1])
    tok = lambda col: pl.BlockSpec((None, NP, QB, col[1]), lambda b, g, i: (b, 0, i, col[0] // col[1] + col[2] * g))
    bg = lambda a: pl.BlockSpec((NP, None) + a.shape[2:], lambda b, g, i: (b, g, 0, 0),
                                pipeline_mode=pl.Buffered(1))
    grp = lambda a: pl.BlockSpec((None,) + a.shape[1:], lambda b, g, i: (g, 0, 0), pipeline_mode=pl.Buffered(1))
    out = pl.pallas_call(
        functools.partial(_nsa_kernel, L=L),
        grid=(B // NP, G, nqb),
        in_specs=[tok((COL_Q, gw, 1)), tok((COL_NG, 128, 0)),
                  bg(kc), bg(vct), bg(ks), bg(vst), bg(kw), bg(vwt), grp(tc), grp(tw), grp(ts),
                  pl.BlockSpec(ovt.shape, lambda b, g, i: (0, 0))],
        out_specs=pl.BlockSpec((None, NP, QB, gw), lambda b, g, i: (b, 0, i, g)),
        out_shape=jax.ShapeDtypeStruct((B // NP, NP, L, NSA_WIDTH), BF16),
        scratch_shapes=[pltpu.VMEM((NP, 2 * SEL_TILE, W), F32), pltpu.VMEM((NP, 2 * SEL_TILE, W), F32),
                        pltpu.VMEM((NP, V_ROWS, W), F32),
                        pltpu.VMEM((NP, 128, QB), F32)],
        compiler_params=_cparams(("parallel", "parallel", "arbitrary")),
        name="nsa_attend",
    )(proj4, proj4, kc, vct, ks, vst, kw, vwt, tc, tw, ts, ovt)
    return out.reshape(B * L, NSA_WIDTH)


def _t5_bucket(dist):
    n = jnp.maximum(dist, 0)
    max_exact = REL_BUCKETS // 2
    nf = jnp.maximum(n, max_exact).astype(F32)
    large = max_exact + (jnp.log(nf / max_exact) / math.log(REL_MAX_DIST / max_exact)
                         * (REL_BUCKETS - max_exact)).astype(jnp.int32)
    large = jnp.minimum(large, REL_BUCKETS - 1)
    return jnp.where(n < max_exact, n, large)


def _toeplitz(f, d0, base, step, n_rows, width):
    rpb = width // step
    nblk = n_rows // rpb
    assert rpb * step == width and nblk * rpb == n_rows
    lo = base - width * nblk - d0
    return f[..., lo:lo + width * (nblk + 1)].reshape(f.shape[:-1] + (nblk + 1, width))


TABLE_STEPS = (CMP_STRIDE, 1, 1)


def _tables_kernel(*refs, runs):
    R, QB = NSA_REP, Q_BLOCK
    n = len(TABLE_STEPS)
    tmp_ref = refs[2 * n]
    for win_ref, out_ref, step, table_runs in zip(refs[:n], refs[n:2 * n], TABLE_STEPS, runs):
        rpb = QB // step

        def block(a, carry, varies, win_ref=win_ref, out_ref=out_ref, step=step, rpb=rpb):
            r0 = pl.multiple_of(a * rpb, rpb)
            top = win_ref.shape[1] - 1 - a
            for r in range(R):
                if varies:
                    win = jnp.concatenate([win_ref[r, pl.ds(top, 1), :], win_ref[r, pl.ds(top - 1, 1), :]], axis=1)
                    x = jnp.broadcast_to(win, (QB, 2 * QB))
                    y = pltpu.roll(x, 0, 1, stride=1, stride_axis=0)[:, :QB]
                    if step > 1:
                        tmp_ref[...] = y
                        y = tmp_ref[pl.ds(0, rpb, stride=step), :]
                else:
                    y = jnp.broadcast_to(win_ref[r, pl.ds(top, 1), :], (rpb, QB))
                out_ref[pl.ds(r0, rpb), r * QB:(r + 1) * QB] = y
            return carry

        for first, end, varies in table_runs:
            lax.fori_loop(first, end, functools.partial(block, varies=varies), 0)


def _constant_runs(base, step, n_rows, upper):
    far = REL_MAX_DIST
    rpb = Q_BLOCK // step
    flags = []
    for a in range(n_rows // rpb):
        lo, hi = base - Q_BLOCK * a - Q_BLOCK + step, base - Q_BLOCK * a + Q_BLOCK - 1
        flags.append(not (hi < 0 or lo >= upper or (lo >= far and hi < upper)))
    runs, start = [], 0
    for a in range(1, len(flags) + 1):
        if a == len(flags) or flags[a] != flags[start]:
            runs.append((start, a, flags[start]))
            start = a
    return tuple(runs)


def _expand_tables(wins, runs):
    G, R, QB = NSA_GROUPS, NSA_REP, Q_BLOCK
    rows = [(w.shape[1] - 1) * (QB // s) for w, s in zip(wins, TABLE_STEPS)]
    return pl.pallas_call(
        functools.partial(_tables_kernel, runs=runs),
        grid=(G,),
        in_specs=[pl.BlockSpec((R,) + w.shape[1:], lambda g: (g, 0, 0)) for w in wins],
        out_specs=[pl.BlockSpec((None, n, R * QB), lambda g: (g, 0, 0)) for n in rows],
        out_shape=[jax.ShapeDtypeStruct((G, n, R * QB), F32) for n in rows],
        scratch_shapes=[pltpu.VMEM((QB, QB), F32)],
        compiler_params=_cparams(("parallel",)),
        name="nsa_tables",
    )(*wins)


def _nsa_tables(rel_bias, L):
    QB, G, R = Q_BLOCK, NSA_GROUPS, NSA_REP
    ncp = L // CMP_STRIDE
    nsel = L // SEL_BLOCK
    npos = L + QB
    tbl = rel_bias.astype(F32)
    bpos = (tbl[_t5_bucket(jnp.arange(npos, dtype=jnp.int32))] - tbl[REL_BUCKETS - 1][None, :]).T
    f = jnp.concatenate([jnp.full((NSA_HEADS, npos), MASK_VALUE, F32), bpos], axis=-1)
    d = np.arange(-npos, npos)
    fw = jnp.where(jnp.asarray(d < WINDOW), f, MASK_VALUE)
    spec_c = (L - QB - CMP_BLOCK + 1, CMP_STRIDE, 2 * ncp - 8)
    spec_w = (WINDOW, 1, WINDOW + QB)
    spec_s = (SEL_TABLE_FAR, 1, SEL_TABLE_FAR + SEL_TILE)
    assert TABLE_STEPS == (spec_c[1], spec_w[1], spec_s[1])
    wins = [_toeplitz(src, -npos, *spec, QB) for src, spec in ((f, spec_c), (fw, spec_w), (f, spec_s))]
    runs = tuple(_constant_runs(*spec, upper) for spec, upper in ((spec_c, 2 * npos), (spec_w, WINDOW), (spec_s, 2 * npos)))
    tc, tw, ts = _expand_tables(wins, runs)
    c = np.arange(ncp)[None, :]
    j = np.arange(nsel)[:, None]
    ovt = ((c - 4 * j >= -1) & (c - 4 * j <= 3)).astype(np.float32)
    return tc, tw, ts, jnp.asarray(ovt, BF16)


S5_LANES = 128
S5_GT = S5_LANES // SSM_GROUP
S5_SW = S5_GT * SSM_STATE
S5_COLS = 4 * S5_LANES


def _s5_kernel(us_ref, kt_ref, zt_ref, cl_ref, lam_ref, ex_ref, y_ref,
               xs_ref, u8_ref, m8_ref, ws8_ref, wo8_ref, *, nk):
    T, C, P, GT, LT, SW = SSM_CHUNK, SSM_GROUP, SSM_STATE, S5_GT, S5_LANES, S5_SW

    @pl.when(pl.program_id(1) == 0)
    def _build():
        tile = lambda x: jnp.concatenate([x] * GT, axis=0)
        r = lax.broadcasted_iota(jnp.int32, (LT, LT), 0)
        c = lax.broadcasted_iota(jnp.int32, (LT, LT), 1)
        same = (r // C) == (c // C)
        m8_ref[...] = jnp.zeros_like(m8_ref)
        for j in range(T):
            bd = jnp.where(same, tile(kt_ref[j]), 0.0).astype(BF16)
            for b in range(T - j):
                m8_ref[b * LT:(b + 1) * LT, (b + j) * LT:(b + j + 1) * LT] = bd
        r = lax.broadcasted_iota(jnp.int32, (LT, 2 * SW), 0)
        c = lax.broadcasted_iota(jnp.int32, (LT, 2 * SW), 1)
        same = (r // C) == ((c % SW) // P)
        for b in range(T):
            ws8_ref[b * LT:(b + 1) * LT, :] = jnp.where(same, tile(zt_ref[b]), 0.0).astype(BF16)
        r = lax.broadcasted_iota(jnp.int32, (2 * SW, LT), 0)
        c = lax.broadcasted_iota(jnp.int32, (2 * SW, LT), 1)
        same = ((r % SW) // P) == (c // C)
        cl = cl_ref[...]
        for a in range(T):
            blk = jnp.dot(cl, ex_ref[a], preferred_element_type=F32)
            wo8_ref[:, a * LT:(a + 1) * LT] = jnp.where(same, blk, 0.0).astype(BF16)

    xs_ref[...] = us_ref[...].astype(F32)
    for b in range(T):
        u8_ref[:, b * LT:(b + 1) * LT] = xs_ref[pl.ds(b, nk, stride=T), :].astype(BF16)
    u8 = u8_ref[...]
    s = jnp.dot(u8, ws8_ref[...], preferred_element_type=F32)
    hr, hi = s[:, :SW], s[:, SW:]
    kidx = lax.broadcasted_iota(jnp.int32, (nk, SW), 0)
    d, step = 1, 0
    while d < nk:
        lr = lam_ref[step:step + 1, :SW]
        li = lam_ref[step:step + 1, SW:]
        keep = kidx >= d
        sr = jnp.where(keep, pltpu.roll(hr, d, 0), 0.0)
        si = jnp.where(keep, pltpu.roll(hi, d, 0), 0.0)
        hr, hi = hr + lr * sr - li * si, hi + lr * si + li * sr
        d, step = 2 * d, step + 1
    keep = kidx >= 1
    pr = jnp.where(keep, pltpu.roll(hr, 1, 0), 0.0)
    pi = jnp.where(keep, pltpu.roll(hi, 1, 0), 0.0)
    hcat = jnp.concatenate([pr, pi], axis=-1).astype(BF16)
    for q in range(T * LT // S5_COLS):
        kq = (q + 1) * S5_COLS
        yq = (jnp.dot(hcat, wo8_ref[:, q * S5_COLS:kq], preferred_element_type=F32)
              + jnp.dot(u8[:, :kq], m8_ref[0:kq, q * S5_COLS:kq], preferred_element_type=F32))
        for a4 in range(S5_COLS // LT):
            a = q * (S5_COLS // LT) + a4
            y_ref[pl.ds(a, nk, stride=T), :] = yq[:, a4 * LT:(a4 + 1) * LT]


def _s5_scan(proj, kt, zt, clc, lamp, ex, B, L):
    T, LT, SW = SSM_CHUNK, S5_LANES, S5_SW
    nk = L // T
    nt = SSM_WIDTH // LT
    per_tile = lambda a: pl.BlockSpec((None,) + a.shape[1:], lambda t, b: (t,) + (0,) * (a.ndim - 1))
    return pl.pallas_call(
        functools.partial(_s5_kernel, nk=nk),
        grid=(nt, B),
        in_specs=[pl.BlockSpec((L, LT), lambda t, b: (b, COL_US // LT + t)),
                  per_tile(kt), per_tile(zt), per_tile(clc), per_tile(lamp),
                  pl.BlockSpec(ex.shape, lambda t, b: (0, 0, 0))],
        out_specs=pl.BlockSpec((L, LT), lambda t, b: (b, t)),
        out_shape=jax.ShapeDtypeStruct((B * L, SSM_WIDTH), F32),
        scratch_shapes=[pltpu.VMEM((L, LT), F32), pltpu.VMEM((nk, T * LT), BF16),
                        pltpu.VMEM((T * LT, T * LT), BF16), pltpu.VMEM((T * LT, 2 * SW), BF16),
                        pltpu.VMEM((2 * SW, T * LT), BF16)],
        compiler_params=_cparams(("parallel", "arbitrary")),
        name="s5_scan",
    )(proj, kt, zt, clc, lamp, ex)


def _cmul(ar, ai, br, bi):
    return ar * br - ai * bi, ar * bi + ai * br


def _s5_tables(a_re, a_im, log_dt, b_re, b_im, c_re, c_im, nk):
    T, P, C, Gs = SSM_CHUNK, SSM_STATE, SSM_GROUP, SSM_GROUPS
    dt = jnp.exp(log_dt.astype(F32))[:, None]
    ar, ai = a_re.astype(F32), a_im.astype(F32)
    mag = jnp.exp(ar * dt)
    lr, li = mag * jnp.cos(ai * dt), mag * jnp.sin(ai * dt)
    den = ar * ar + ai * ai
    nr, ni = lr - 1.0, li
    fr, fi = (nr * ar + ni * ai) / den, (ni * ar - nr * ai) / den
    br, bim = b_re.astype(F32), b_im.astype(F32)
    bbr = fr[..., None] * br - fi[..., None] * bim
    bbi = fr[..., None] * bim + fi[..., None] * br
    jv = jnp.arange(T + 1, dtype=F32)[:, None, None]
    pmag = jnp.exp(jv * (ar * dt)[None])
    pwr, pwi = pmag * jnp.cos(jv * (ai * dt)[None]), pmag * jnp.sin(jv * (ai * dt)[None])
    zr = pwr[..., None] * bbr[None] - pwi[..., None] * bbi[None]
    zi = pwr[..., None] * bbi[None] + pwi[..., None] * bbr[None]
    cr, ci = c_re.astype(F32), c_im.astype(F32)
    kj = (jnp.einsum('gcp,jgpd->gjcd', cr, zr[:T], precision=HIGHEST)
          - jnp.einsum('gcp,jgpd->gjcd', ci, zi[:T], precision=HIGHEST))
    GT, NT = S5_GT, Gs // S5_GT
    kt = kj.reshape(NT, GT, T, C, C).transpose(0, 2, 4, 1, 3).reshape(NT, T, C, GT * C)
    pow_b = lambda pw: pw[T - 1::-1].reshape(T, NT, GT, P).transpose(1, 0, 2, 3)[:, :, None]
    bbar_t = lambda bb: bb.reshape(NT, GT, P, C).transpose(0, 3, 1, 2)[:, None]
    pb_r, pb_i, bt_r, bt_i = pow_b(pwr), pow_b(pwi), bbar_t(bbr), bbar_t(bbi)
    zt = jnp.concatenate([(pb_r * bt_r - pb_i * bt_i).reshape(NT, T, C, GT * P),
                          (pb_r * bt_i + pb_i * bt_r).reshape(NT, T, C, GT * P)], axis=-1)
    pow_a = lambda pw: pw[1:].reshape(T, NT, GT, P).transpose(1, 2, 3, 0)[..., None]
    c_t = lambda cc: cc.reshape(NT, GT, C, P).transpose(0, 1, 3, 2)[:, :, :, None, :]
    pa_r, pa_i, ct_r, ct_i = pow_a(pwr), pow_a(pwi), c_t(cr), c_t(ci)
    clc = jnp.concatenate([(ct_r * pa_r - ct_i * pa_i).reshape(NT, GT * P, T * C),
                           (-(ct_r * pa_i + ct_i * pa_r)).reshape(NT, GT * P, T * C)], axis=1).astype(BF16)
    qr, qi = pwr[T], pwi[T]
    steps = []
    d = 1
    while d < nk:
        steps.append(jnp.concatenate([qr.reshape(NT, GT * P), qi.reshape(NT, GT * P)], -1))
        qr, qi = _cmul(qr, qi, qr, qi)
        d *= 2
    lamp = jnp.stack(steps, 1)
    col = np.arange(T * C)[:, None]
    lane = np.arange(GT * C)[None, :]
    ex = np.stack([(col // C == a) & (col % C == lane % C) for a in range(T)]).astype(np.float32)
    return kt, zt, clc, lamp, jnp.asarray(ex, BF16)


def _merge_kernel(oa_ref, za_ref, ys_ref, u_ref, zb_ref, d_ref, wg_ref, bg_ref, ga_ref, gb_ref, x_ref, gate_ref,
                  wa_ref, wb_ref, wo_ref, lg_ref, lb_ref, o_ref):
    za = za_ref[...].astype(F32)
    oa = (oa_ref[...].astype(F32) * (za * jax.nn.sigmoid(za))).astype(BF16)
    ys = ys_ref[...] + d_ref[...] * u_ref[...].astype(F32)
    yg = jax.nn.gelu(ys).astype(BF16)
    z = jnp.dot(yg, wg_ref[...], preferred_element_type=F32) + bg_ref[...]
    zb = zb_ref[...].astype(F32)
    ob = (yg.astype(F32) * jax.nn.sigmoid(z) * (zb * jax.nn.sigmoid(zb))).astype(BF16)
    pa = jnp.dot(oa, wa_ref[...], preferred_element_type=F32)
    pb = jnp.dot(ob, wb_ref[...], preferred_element_type=F32)
    m = jax.nn.sigmoid(ga_ref[...].astype(F32)) * pa + jax.nn.sigmoid(gb_ref[...].astype(F32)) * pb
    y = jnp.dot(m.astype(BF16), wo_ref[...], preferred_element_type=F32)
    r = DEEPNORM_ALPHA * x_ref[...] + gate_ref[...] * y
    mu = jnp.mean(r, axis=-1, keepdims=True)
    rc = r - mu
    var = jnp.mean(rc * rc, axis=-1, keepdims=True)
    o_ref[...] = rc * lax.rsqrt(var + LN_EPS) * lg_ref[...] + lb_ref[...]


def _merge_out(oa, ys, d_skip, w_glu, b_glu, proj, x2, gate, wa, wb, wo, ln_g, ln_b, L):
    N, D = x2.shape
    W = oa.shape[1]
    B = gate.shape[0]
    tm = min(256, L)
    const = lambda shape: pl.BlockSpec(shape, lambda i: (0, 0), pipeline_mode=pl.Buffered(1))
    return pl.pallas_call(
        _merge_kernel,
        grid=(N // tm,),
        in_specs=[pl.BlockSpec((tm, W), lambda i: (i, 0)),
                  pl.BlockSpec((tm, W), lambda i: (i, COL_ZA // W)),
                  pl.BlockSpec((tm, W), lambda i: (i, 0)),
                  pl.BlockSpec((tm, W), lambda i: (i, COL_US // W)),
                  pl.BlockSpec((tm, W), lambda i: (i, COL_ZB // W)),
                  const((1, W)), const((W, W)), const((1, W)),
                  pl.BlockSpec((tm, D), lambda i: (i, COL_GA // D)),
                  pl.BlockSpec((tm, D), lambda i: (i, COL_GB // D)),
                  pl.BlockSpec((tm, D), lambda i: (i, 0)),
                  pl.BlockSpec((None, 1, D), lambda i: ((i * tm) // L, 0, 0)),
                  const((W, D)), const((W, D)), const((D, D)), const((1, D)), const((1, D))],
        out_specs=pl.BlockSpec((tm, D), lambda i: (i, 0)),
        out_shape=jax.ShapeDtypeStruct((N, D), F32),
        compiler_params=_cparams(("parallel",)),
        name="merge_out",
    )(oa, proj, ys, proj, proj, d_skip.reshape(1, W), w_glu, b_glu.reshape(1, W), proj, proj, x2,
      gate.reshape(B, 1, D), wa, wb, wo, ln_g.reshape(1, D), ln_b.reshape(1, D))


def _layer(x, c, w_ada, b_ada, w_in, rel_bias, cmp_pos_k, cmp_pos_v, w_cmp_k1, w_cmp_k2, w_cmp_v1, w_cmp_v2,
           ssm_a_re, ssm_a_im, ssm_log_dt, ssm_b_re, ssm_b_im, ssm_c_re, ssm_c_im, ssm_d, w_glu, b_glu,
           w_branch_nsa, w_branch_ssm, w_out, ln_g, ln_b):
    B, L, D = x.shape
    N = B * L
    G, dh = NSA_GROUPS, NSA_HEAD_DIM
    x2 = x.reshape(N, D)

    mod = _ada_mod(c, w_ada, b_ada)
    shift, scale, gate = mod[:, :D], mod[:, D:2 * D], mod[:, 2 * D:]

    o_q, o_kv, o_ng, o_za, o_us, o_zb, o_ga, o_gb = np.cumsum(
        [0, NSA_WIDTH, 6 * KV_WIDTH, 3 * NSA_HEADS, NSA_WIDTH, SSM_WIDTH, SSM_WIDTH, D_MODEL]).tolist()
    wt = w_in.T
    w_re = jnp.concatenate([
        wt[o_q:o_kv], wt[o_za:o_us], wt[o_us:o_zb], wt[o_zb:o_ga], wt[o_ga:o_gb],
        wt[o_gb:], wt[o_kv + 2 * KV_WIDTH:o_ng], wt[o_kv:o_kv + 2 * KV_WIDTH], wt[o_ng:o_za],
        jnp.zeros((PROJ_COLS - COL_NG - 3 * NSA_HEADS, D), w_in.dtype)], axis=0).astype(BF16)
    proj = _in_proj(x2, scale, shift, w_re, L)

    w1s = jnp.stack([w_cmp_k1, w_cmp_v1])
    w2s = jnp.stack([w_cmp_k2, w_cmp_v2])
    poss = jnp.broadcast_to(jnp.stack([cmp_pos_k, cmp_pos_v]).reshape(2, 1, CMP_BLOCK * dh), (2, 8, CMP_BLOCK * dh))
    kcv = _nsa_compress(proj, w1s, w2s, poss, B, L)
    kc = kcv[0].astype(BF16)
    vct = kcv[1].swapaxes(-1, -2).astype(BF16)
    ks, vst, kw, vwt = _kv_prep(proj, B, L)
    tc, tw, ts, ovt = _nsa_tables(rel_bias, L)
    o_a = _nsa_attend(proj, kc, vct, ks, vst, kw, vwt, tc, tw, ts, ovt, B, L)

    s5_tabs = _s5_tables(ssm_a_re, ssm_a_im, ssm_log_dt, ssm_b_re, ssm_b_im, ssm_c_re, ssm_c_im, L // SSM_CHUNK)
    y = _s5_scan(proj, *s5_tabs, B, L)
    out = _merge_out(o_a, y, ssm_d, w_glu.astype(BF16), b_glu, proj, x2, gate, w_branch_nsa.astype(BF16),
                     w_branch_ssm.astype(BF16), w_out.astype(BF16), ln_g, ln_b, L)
    return out.reshape(B, L, D)


def kernel(x, c, w_ada, b_ada, w_in, rel_bias, cmp_pos_k, cmp_pos_v, w_cmp_k1, w_cmp_k2, w_cmp_v1, w_cmp_v2,
           ssm_a_re, ssm_a_im, ssm_log_dt, ssm_b_re, ssm_b_im, ssm_c_re, ssm_c_im, ssm_d, w_glu, b_glu,
           w_branch_nsa, w_branch_ssm, w_out, ln_g, ln_b):
    for i in range(w_ada.shape[0]):
        x = _layer(x, c, w_ada[i], b_ada[i], w_in[i], rel_bias, cmp_pos_k[i], cmp_pos_v[i], w_cmp_k1[i],
                   w_cmp_k2[i], w_cmp_v1[i], w_cmp_v2[i], ssm_a_re[i], ssm_a_im[i], ssm_log_dt[i], ssm_b_re[i],
                   ssm_b_im[i], ssm_c_re[i], ssm_c_im[i], ssm_d[i], w_glu[i], b_glu[i], w_branch_nsa[i],
                   w_branch_ssm[i], w_out[i], ln_g[i], ln_b[i])
    return x
```
